```python
import math
import jax, jax.numpy as jnp
from jax import lax
import numpy as np

D_MODEL = 1024
BATCH = 8
SEQ = 2048
DEPTH = 2
DEC_BATCH = 128
DEC_SEQ = 4
PAST_LEN = 2048
PAGE_SIZE = 128

D_CONV = D_MODEL // 2
CONV_K = 31
N_HEADS = 8
HEAD_DIM = 64
N_KV = 2
HPG = N_HEADS // N_KV
KV_COLS = 2 * N_KV * HEAD_DIM
CMP_BLOCK = 32
CMP_STRIDE = 16
SEL_BLOCK = 64
N_SEL = 16
WINDOW = 512
D_FF = 4 * D_MODEL
ROPE_THETA = 10000.0
EPS = 1e-6
Q_BLOCK = 128
IN_COLS = 2 * D_CONV + N_HEADS * HEAD_DIM + 3 * KV_COLS + 3 * N_HEADS + 2 * D_MODEL

kernel_name = 'hybrid_conformer_conv_nsa_decoder_step'


def rms_norm(x, g):
    xf = x.astype(jnp.float32)
    y = xf * lax.rsqrt(jnp.mean(xf * xf, axis=-1, keepdims=True) + EPS)
    return (y * g.astype(jnp.float32)).astype(x.dtype)


def layer_norm(x, g, b):
    xf = x.astype(jnp.float32)
    mu = jnp.mean(xf, axis=-1, keepdims=True)
    var = jnp.mean(jnp.square(xf - mu), axis=-1, keepdims=True)
    y = (xf - mu) * lax.rsqrt(var + EPS)
    return (y * g.astype(jnp.float32) + b.astype(jnp.float32)).astype(x.dtype)


def rope(x, pos):
    half = HEAD_DIM // 2
    inv = jnp.power(ROPE_THETA, -jnp.arange(half, dtype=jnp.float32) / half)
    ang = pos.astype(jnp.float32)[:, None] * inv[None, :]
    cos = jnp.cos(ang)[:, None, :]
    sin = jnp.sin(ang)[:, None, :]
    xf = x.astype(jnp.float32)
    x1, x2 = xf[..., :half], xf[..., half:]
    return jnp.concatenate([x1 * cos - x2 * sin, x2 * cos + x1 * sin], axis=-1).astype(x.dtype)


def rope_kv(kv, pos):
    return jnp.stack([rope(kv[:, :, 0], pos), kv[:, :, 1]], axis=2)


def masked_softmax(s, mask):
    s = jnp.where(mask, s, -1e30)
    m = jnp.max(s, axis=-1, keepdims=True)
    p = jnp.where(mask, jnp.exp(s - m), 0.0)
    return p / jnp.maximum(jnp.sum(p, axis=-1, keepdims=True), 1e-30)


def causal_depthwise_conv(u_full, w, b):
    out = lax.conv_general_dilated(u_full, w[:, None, :], window_strides=(1,), padding='VALID',
                                   dimension_numbers=('NWC', 'WIO', 'NWC'), feature_group_count=D_CONV)
    return out + b


def compress(rows, pos_emb, w):
    t = rows.shape[1]
    n_cmp = (t - CMP_BLOCK) // CMP_STRIDE + 1
    idx = jnp.arange(n_cmp)[:, None] * CMP_STRIDE + jnp.arange(CMP_BLOCK)[None, :]
    blocks = rows[:, idx] + pos_emb[None, None, :, None, :]
    return jnp.einsum('bnlgd,lde->bnge', blocks, w)


def nsa_attention(q, kc_raw, vc_raw, ks, vs, kw, vw, gates, q0, cmp_pos, w_cmp):
    bsz, tq = q.shape[0], q.shape[1]
    t = kc_raw.shape[1]
    scale = HEAD_DIM ** -0.5
    n_cmp = (t - CMP_BLOCK) // CMP_STRIDE + 1
    n_sel = -(-t // SEL_BLOCK)
    k_sel = min(N_SEL, n_sel)
    cmp_end = jnp.arange(n_cmp) * CMP_STRIDE + CMP_BLOCK - 1
    kc = rope(compress(kc_raw, cmp_pos[0], w_cmp[0]), cmp_end)
    vc = compress(vc_raw, cmp_pos[1], w_cmp[1])
    pad = n_sel * SEL_BLOCK - t
    ks_b = jnp.pad(ks, ((0, 0), (0, pad), (0, 0), (0, 0))).reshape(bsz, n_sel, SEL_BLOCK, N_KV, HEAD_DIM).transpose(0, 3, 1, 2, 4)
    vs_b = jnp.pad(vs, ((0, 0), (0, pad), (0, 0), (0, 0))).reshape(bsz, n_sel, SEL_BLOCK, N_KV, HEAD_DIM).transpose(0, 3, 1, 2, 4)
    ci = jnp.arange(n_cmp)[:, None] * CMP_STRIDE
    sj = jnp.arange(n_sel)[None, :] * SEL_BLOCK
    overlap = ((ci < sj + SEL_BLOCK) & (ci + CMP_BLOCK > sj)).astype(jnp.float32)
    blk = jnp.arange(n_sel)
    bi = jnp.arange(bsz)[:, None, None, None]
    gi = jnp.arange(N_KV)[None, :, None, None]
    qb = Q_BLOCK if tq % Q_BLOCK == 0 else tq
    n_blk = tq // qb
    q_blocks = jnp.moveaxis(q.reshape(bsz, n_blk, qb, N_KV, HPG, HEAD_DIM), 1, 0)
    g_blocks = jnp.moveaxis(gates.reshape(bsz, n_blk, qb, 3, N_KV, HPG), 1, 0)

    def block_fn(args):
        i, qblk, gblk = args
        qp = q0 + i * qb + jnp.arange(qb)
        s_c = jnp.einsum('bqghd,bngd->bghqn', qblk, kc).astype(jnp.float32) * scale
        p_c = masked_softmax(s_c, cmp_end[None, :] <= qp[:, None])
        o_c = jnp.einsum('bghqn,bngd->bqghd', p_c.astype(vc.dtype), vc)
        imp = jnp.einsum('bghqn,nj->bgqj', p_c, overlap)
        cur = qp // SEL_BLOCK
        forced = (blk[None, :] == 0) | (blk[None, :] == cur[:, None]) | (blk[None, :] == cur[:, None] - 1)
        valid = blk[None, :] * SEL_BLOCK <= qp[:, None]
        imp = jnp.where(valid, jnp.where(forced, jnp.inf, imp), -jnp.inf)
        _, idx = lax.top_k(imp, k_sel)
        ksg = ks_b[bi, gi, idx]
        vsg = vs_b[bi, gi, idx]
        kpos = idx[..., None] * SEL_BLOCK + jnp.arange(SEL_BLOCK)
        mask_s = (kpos <= qp[None, None, :, None, None]).reshape(bsz, N_KV, 1, qb, k_sel * SEL_BLOCK)
        s_s = jnp.einsum('bqghd,bgqkld->bghqkl', qblk, ksg).astype(jnp.float32) * scale
        p_s = masked_softmax(s_s.reshape(bsz, N_KV, HPG, qb, k_sel * SEL_BLOCK), mask_s)
        p_s = p_s.reshape(bsz, N_KV, HPG, qb, k_sel, SEL_BLOCK).astype(vsg.dtype)
        o_s = jnp.einsum('bghqkl,bgqkld->bqghd', p_s, vsg)
        kwb = lax.dynamic_slice_in_dim(kw, i * qb, WINDOW + qb, axis=1)
        vwb = lax.dynamic_slice_in_dim(vw, i * qb, WINDOW + qb, axis=1)
        kpos_w = q0 - WINDOW + i * qb + jnp.arange(WINDOW + qb)
        mask_w = (kpos_w[None, :] <= qp[:, None]) & (kpos_w[None, :] > qp[:, None] - WINDOW) & (kpos_w[None, :] >= 0)
        s_w = jnp.einsum('bqghd,bkgd->bghqk', qblk, kwb).astype(jnp.float32) * scale
        p_w = masked_softmax(s_w, mask_w).astype(vwb.dtype)
        o_w = jnp.einsum('bghqk,bkgd->bqghd', p_w, vwb)
        return gblk[:, :, 0, :, :, None] * o_c + gblk[:, :, 1, :, :, None] * o_s + gblk[:, :, 2, :, :, None] * o_w

    out = lax.map(block_fn, (jnp.arange(n_blk), q_blocks, g_blocks))
    return jnp.moveaxis(out, 0, 1).reshape(bsz, tq, N_HEADS * HEAD_DIM)


def layer(x, past_ckv, past_skv, past_wkv, past_conv, q0, win_keep,
          norm_g, w_in, dw_w, dw_b, ln_g, ln_b, w_conv_out, cmp_pos, w_cmp, w_nsa_out, w_out, mlp_g, w_up, w_down):
    bsz, tq, _ = x.shape
    h = rms_norm(x, norm_g)
    proj = h @ w_in
    o0 = 2 * D_CONV
    o1 = o0 + N_HEADS * HEAD_DIM
    o2 = o1 + KV_COLS
    o3 = o2 + KV_COLS
    o4 = o3 + KV_COLS
    o5 = o4 + 3 * N_HEADS
    glu_in, q, ckv, skv, wkv, g_nsa, g_merge = jnp.split(proj, [o0, o1, o2, o3, o4, o5], axis=-1)
    u = glu_in[..., :D_CONV] * jax.nn.sigmoid(glu_in[..., D_CONV:])
    u_full = jnp.concatenate([past_conv, u], axis=1)
    c = layer_norm(causal_depthwise_conv(u_full, dw_w, dw_b), ln_g, ln_b)
    conv_out = jax.nn.silu(c) @ w_conv_out
    new_conv = u_full[:, -(CONV_K - 1):]
    pos = q0 + jnp.arange(tq)
    q = rope(q.reshape(bsz, tq, N_HEADS, HEAD_DIM), pos)
    ckv = ckv.reshape(bsz, tq, 2, N_KV, HEAD_DIM)
    skv = rope_kv(skv.reshape(bsz, tq, 2, N_KV, HEAD_DIM), pos)
    wkv = rope_kv(wkv.reshape(bsz, tq, 2, N_KV, HEAD_DIM), pos)
    ckv_all = jnp.concatenate([past_ckv, ckv], axis=1)
    skv_all = jnp.concatenate([past_skv, skv], axis=1)
    wkv_all = jnp.concatenate([past_wkv, wkv], axis=1)
    gates = jax.nn.sigmoid(g_nsa).reshape(bsz, tq, 3, N_KV, HPG)
    o = nsa_attention(q, ckv_all[:, :, 0], ckv_all[:, :, 1], skv_all[:, :, 0], skv_all[:, :, 1],
                      wkv_all[:, :, 0], wkv_all[:, :, 1], gates, q0, cmp_pos, w_cmp)
    nsa_out = o @ w_nsa_out
    gm = jax.nn.sigmoid(g_merge)
    merged = gm[..., :D_MODEL] * conv_out + gm[..., D_MODEL:] * nsa_out
    x = x + merged @ w_out
    h2 = rms_norm(x, mlp_g)
    x = x + jnp.square(jax.nn.relu(h2 @ w_up)) @ w_down
    return x, ckv, skv, wkv_all[:, -win_keep:], new_conv


def setup_inputs(seed: int = 0) -> dict:
    key = jax.random.key(seed)
    ks = jax.random.split(key, 24)
    n_pages = PAST_LEN // PAGE_SIZE
    n_pool = (DEC_BATCH * n_pages * 5) // 4
    win_buf = min(WINDOW, PAST_LEN)
    f32 = jnp.float32
    nrm = lambda k, shape, s: jax.random.normal(k, shape, f32) * s
    page_table = jax.random.permutation(ks[6], n_pool)[:DEC_BATCH * n_pages].reshape(DEC_BATCH, n_pages).astype(jnp.int32)
    return {
        'x_prompt': nrm(ks[0], (BATCH, SEQ, D_MODEL), 1.0),
        'x_sample': nrm(ks[1], (DEC_BATCH, DEC_SEQ, D_MODEL), 1.0),
        'cache_cmp_kv': nrm(ks[2], (DEPTH, n_pool, PAGE_SIZE, 2, N_KV, HEAD_DIM), 1.0),
        'cache_sel_kv': nrm(ks[3], (DEPTH, n_pool, PAGE_SIZE, 2, N_KV, HEAD_DIM), 1.0),
        'state_win_kv': nrm(ks[4], (DEPTH, DEC_BATCH, win_buf, 2, N_KV, HEAD_DIM), 1.0),
        'state_conv': nrm(ks[5], (DEPTH, DEC_BATCH, CONV_K - 1, D_CONV), 0.5),
        'page_table': page_table,
        'norm_mix_g': 1.0 + nrm(ks[7], (DEPTH, D_MODEL), 0.01),
        'w_in': nrm(ks[8], (DEPTH, D_MODEL, IN_COLS), D_MODEL ** -0.5),
        'conv_dw_w': nrm(ks[9], (DEPTH, CONV_K, D_CONV), CONV_K ** -0.5),
        'conv_dw_b': nrm(ks[10], (DEPTH, D_CONV), 0.01),
        'conv_ln_g': 1.0 + nrm(ks[11], (DEPTH, D_CONV), 0.01),
        'conv_ln_b': nrm(ks[12], (DEPTH, D_CONV), 0.01),
        'w_conv_out': nrm(ks[13], (DEPTH, D_CONV, D_MODEL), D_CONV ** -0.5),
        'cmp_pos': nrm(ks[14], (DEPTH, 2, CMP_BLOCK, HEAD_DIM), 0.1),
        'w_cmp': nrm(ks[15], (DEPTH, 2, CMP_BLOCK, HEAD_DIM, HEAD_DIM), (CMP_BLOCK * HEAD_DIM) ** -0.5),
        'w_nsa_out': nrm(ks[16], (DEPTH, N_HEADS * HEAD_DIM, D_MODEL), (N_HEADS * HEAD_DIM) ** -0.5),
        'w_out': nrm(ks[17], (DEPTH, D_MODEL, D_MODEL), D_MODEL ** -0.5),
        'norm_mlp_g': 1.0 + nrm(ks[18], (DEPTH, D_MODEL), 0.01),
        'w_up': nrm(ks[19], (DEPTH, D_MODEL, D_FF), D_MODEL ** -0.5),
        'w_down': nrm(ks[20], (DEPTH, D_FF, D_MODEL), 0.5 * D_FF ** -0.5),
        'norm_final_g': 1.0 + nrm(ks[21], (D_MODEL,), 0.01),
    }


def reference(x_prompt, x_sample, cache_cmp_kv, cache_sel_kv, state_win_kv, state_conv, page_table,
              norm_mix_g, w_in, conv_dw_w, conv_dw_b, conv_ln_g, conv_ln_b, w_conv_out, cmp_pos, w_cmp,
              w_nsa_out, w_out, norm_mlp_g, w_up, w_down, norm_final_g):
    dtype = x_prompt.dtype
    bp, tp = x_prompt.shape[0], x_prompt.shape[1]
    bs, ts = x_sample.shape[0], x_sample.shape[1]
    n_pages = page_table.shape[1]
    past_len = n_pages * PAGE_SIZE
    win_buf = state_win_kv.shape[2]
    xp, xs = x_prompt, x_sample
    ckp, cks, skp, sks, wkp, wks, cvp, cvs = [], [], [], [], [], [], [], []
    for l in range(DEPTH):
        weights = (norm_mix_g[l], w_in[l], conv_dw_w[l], conv_dw_b[l], conv_ln_g[l], conv_ln_b[l], w_conv_out[l],
                   cmp_pos[l], w_cmp[l], w_nsa_out[l], w_out[l], norm_mlp_g[l], w_up[l], w_down[l])
        empty = jnp.zeros((bp, 0, 2, N_KV, HEAD_DIM), dtype)
        xp, ck, sk, wk, cv = layer(xp, empty, empty, jnp.zeros((bp, WINDOW, 2, N_KV, HEAD_DIM), dtype),
                                   jnp.zeros((bp, CONV_K - 1, D_CONV), dtype), 0, min(WINDOW, tp), *weights)
        ckp.append(ck); skp.append(sk); wkp.append(wk); cvp.append(cv)
        past_ckv = cache_cmp_kv[l][page_table].reshape(bs, past_len, 2, N_KV, HEAD_DIM)
        past_skv = cache_sel_kv[l][page_table].reshape(bs, past_len, 2, N_KV, HEAD_DIM)
        past_wkv = jnp.pad(state_win_kv[l], ((0, 0), (WINDOW - win_buf, 0), (0, 0), (0, 0), (0, 0)))
        xs, ck, sk, wk, cv = layer(xs, past_ckv, past_skv, past_wkv, state_conv[l], past_len, win_buf, *weights)
        cks.append(ck); sks.append(sk); wks.append(wk); cvs.append(cv)
    y_prompt = rms_norm(xp, norm_final_g)
    y_sample = rms_norm(xs, norm_final_g)
    new_cmp_kv_prompt = jnp.stack(ckp)
    new_cmp_kv_sample = jnp.stack(cks)
    new_sel_kv_prompt = jnp.stack(skp)
    new_sel_kv_sample = jnp.stack(sks)
    new_win_kv_prompt = jnp.stack(wkp)
    new_win_kv_sample = jnp.stack(wks)
    new_conv_prompt = jnp.stack(cvp)
    new_conv_sample = jnp.stack(cvs)
    return (y_prompt, y_sample, new_cmp_kv_prompt, new_cmp_kv_sample, new_sel_kv_prompt, new_sel_kv_sample,
            new_win_kv_prompt, new_win_kv_sample, new_conv_prompt, new_conv_sample)
```

```python
import functools

import jax
import jax.numpy as jnp
from jax import lax
from jax.experimental import pallas as pl
from jax.experimental.pallas import tpu as pltpu

D_MODEL = 1024
D_CONV = D_MODEL // 2
CONV_K = 31
N_HEADS = 8
HEAD_DIM = 64
N_KV = 2
HPG = N_HEADS // N_KV
KV_COLS = 2 * N_KV * HEAD_DIM
CMP_BLOCK = 32
CMP_STRIDE = 16
SEL_BLOCK = 64
N_SEL = 16
WINDOW = 512
D_FF = 4 * D_MODEL
ROPE_THETA = 10000.0
EPS = 1e-6
PAGE_SIZE = 128

LANES = 128
Q_EXP = N_HEADS * LANES
N_GATES = 3 * N_HEADS
CHUNKS_PER_PAGE = PAGE_SIZE // CMP_STRIDE
CHUNK_COLS = CMP_STRIDE * KV_COLS
NEG = -1e30

O_GLU = 0
O_Q = O_GLU + 2 * D_CONV
O_CKV = O_Q + Q_EXP
O_SKV = O_CKV + KV_COLS
O_WKV = O_SKV + KV_COLS
O_GN = O_WKV + KV_COLS
O_GM = O_GN + LANES
IN_COLS_PACKED = O_GM + 2 * D_MODEL

VMEM_LIMIT = 56 * 1024 * 1024

F32 = jnp.float32
BF16 = jnp.bfloat16


def _params(n_axes, vmem=VMEM_LIMIT):
    return pltpu.CompilerParams(dimension_semantics=("arbitrary",) * n_axes, vmem_limit_bytes=vmem)


def _sigmoid(x):
    return 1.0 / (1.0 + jnp.exp(-x))


def _dot(a, b):
    return jnp.dot(a, b, preferred_element_type=F32)


def _dot_nt(a, b):
    return lax.dot_general(a, b, (((1,), (1,)), ((), ())), preferred_element_type=F32)


def _rope_slab(xs, cos, sin_signed):
    lane = lax.broadcasted_iota(jnp.int32, xs.shape, 1)
    first = (lane % HEAD_DIM) < (HEAD_DIM // 2)
    rot = jnp.where(first, pltpu.roll(xs, LANES - HEAD_DIM // 2, 1), pltpu.roll(xs, HEAD_DIM // 2, 1))
    return xs * cos + rot * sin_signed


def _pick_tile(n, cap, mult=8):
    t = min(n, cap)
    while n % t or t % mult:
        t -= 1
    return t


def _inproj_kernel(x_ref, g_ref, w_ref, cos_ref, sin_ref,
                   u_ref, q_ref, ckv_ref, skv_ref, wkv_ref, gn_ref, gm_ref):
    x = x_ref[...]
    ms = jnp.mean(x * x, axis=-1, keepdims=True)
    h = (x * lax.rsqrt(ms + EPS) * g_ref[...]).astype(BF16)
    cos = cos_ref[...]
    sin = sin_ref[...]

    def proj(a, n):
        return _dot(h, w_ref[:, a:a + n])

    u_ref[...] = proj(O_GLU, D_CONV) * _sigmoid(proj(O_GLU + D_CONV, D_CONV))
    for s in range(N_HEADS):
        q_ref[:, s * LANES:(s + 1) * LANES] = _rope_slab(proj(O_Q + s * LANES, LANES), cos, sin).astype(BF16)
    ckv_ref[...] = proj(O_CKV, KV_COLS)
    skv_ref[:, 0:LANES] = _rope_slab(proj(O_SKV, LANES), cos, sin)
    skv_ref[:, LANES:KV_COLS] = proj(O_SKV + LANES, LANES)
    wkv_ref[:, 0:LANES] = _rope_slab(proj(O_WKV, LANES), cos, sin)
    wkv_ref[:, LANES:KV_COLS] = proj(O_WKV + LANES, LANES)
    gn_ref[...] = _sigmoid(proj(O_GN, LANES))
    gm_ref[...] = _sigmoid(proj(O_GM, 2 * D_MODEL)).astype(BF16)


def _inproj(x2, g, w_packed, cos_t, sin_t, tm):
    m = x2.shape[0]
    n_tab = cos_t.shape[0] // tm
    row = lambda i: (i, 0)
    const = lambda i: (0, 0)
    tab = lambda i: (i % n_tab, 0)
    out_shapes = (
        jax.ShapeDtypeStruct((m, D_CONV), F32),
        jax.ShapeDtypeStruct((m, Q_EXP), BF16),
        jax.ShapeDtypeStruct((m, KV_COLS), F32),
        jax.ShapeDtypeStruct((m, KV_COLS), F32),
        jax.ShapeDtypeStruct((m, KV_COLS), F32),
        jax.ShapeDtypeStruct((m, LANES), F32),
        jax.ShapeDtypeStruct((m, 2 * D_MODEL), BF16),
    )
    return pl.pallas_call(
        _inproj_kernel,
        grid=(m // tm,),
        in_specs=[
            pl.BlockSpec((tm, D_MODEL), row),
            pl.BlockSpec((1, D_MODEL), const),
            pl.BlockSpec((D_MODEL, IN_COLS_PACKED), const),
            pl.BlockSpec((tm, LANES), tab),
            pl.BlockSpec((tm, LANES), tab),
        ],
        out_specs=tuple(pl.BlockSpec((tm, s.shape[1]), row) for s in out_shapes),
        out_shape=out_shapes,
        compiler_params=_params(1),
    )(x2, g, w_packed, cos_t, sin_t)


def _ln_swish(c, lg, lb):
    mu = jnp.mean(c, axis=-1, keepdims=True)
    d = c - mu
    var = jnp.mean(d * d, axis=-1, keepdims=True)
    y = d * lax.rsqrt(var + EPS) * lg + lb
    return y * _sigmoid(y)


CONV_PAD = 32
CONV_CHUNK = 64


def _conv_prompt_kernel(u_ref, past_ref, w_ref, b_ref, lg_ref, lb_ref, o_ref, uf_ref, *, t_len):
    off = CONV_PAD - (CONV_K - 1)
    uf_ref[0:8, :] = jnp.zeros((8, D_CONV), F32)
    uf_ref[off:CONV_PAD, :] = past_ref[0]
    uf_ref[CONV_PAD:CONV_PAD + t_len, :] = u_ref[0]
    bias = b_ref[...]
    lg = lg_ref[...]
    lb = lb_ref[...]
    ct = CONV_CHUNK

    def body(i, carry):
        base = pl.multiple_of(i * ct, ct)
        xw = uf_ref[pl.ds(base, ct + CONV_PAD), :]
        acc = jnp.zeros((ct, D_CONV), F32)
        for r in range(8):
            yr = xw if r == 0 else xw[r:r + ct + CONV_PAD - 8, :]
            for a in range((CONV_PAD // 8) + 1):
                k = 8 * a + r - off
                if 0 <= k < CONV_K and 8 * a + ct <= yr.shape[0]:
                    acc = acc + w_ref[k:k + 1, :] * yr[8 * a:8 * a + ct, :]
        o_ref[0, pl.ds(base, ct), :] = _ln_swish(acc + bias, lg, lb).astype(BF16)
        return carry

    lax.fori_loop(0, t_len // ct, body, 0)


def _conv_prompt(u3, past3, dw_w, dw_b, ln_g, ln_b):
    b, t_len, _ = u3.shape
    const2 = lambda i: (0, 0)
    return pl.pallas_call(
        functools.partial(_conv_prompt_kernel, t_len=t_len),
        grid=(b,),
        in_specs=[
            pl.BlockSpec((1, t_len, D_CONV), lambda i: (i, 0, 0)),
            pl.BlockSpec((1, CONV_K - 1, D_CONV), lambda i: (i, 0, 0)),
            pl.BlockSpec((CONV_K, D_CONV), const2),
            pl.BlockSpec((1, D_CONV), const2),
            pl.BlockSpec((1, D_CONV), const2),
            pl.BlockSpec((1, D_CONV), const2),
        ],
        out_specs=pl.BlockSpec((1, t_len, D_CONV), lambda i: (i, 0, 0)),
        out_shape=jax.ShapeDtypeStruct((b, t_len, D_CONV), BF16),
        scratch_shapes=[pltpu.VMEM((t_len + CONV_PAD, D_CONV), F32)],
        compiler_params=_params(1),
    )(u3, past3, dw_w, dw_b, ln_g, ln_b)


def _conv_sample_kernel(past_ref, u_ref, w_ref, b_ref, lg_ref, lb_ref, o_ref, *, tq):
    bias = b_ref[...]
    lg = lg_ref[...]
    lb = lb_ref[...]
    n_past = CONV_K - 1
    for t in range(tq):
        acc = jnp.zeros(o_ref.shape[1:], F32)
        for j in range(t, n_past):
            acc = acc + w_ref[j - t:j - t + 1, :] * past_ref[j]
        for i in range(t + 1):
            k = n_past - t + i
            acc = acc + w_ref[k:k + 1, :] * u_ref[i]
        o_ref[t] = _ln_swish(acc + bias, lg, lb).astype(BF16)


def _conv_sample(past_t, u_t, dw_w, dw_b, ln_g, ln_b):
    n_past, bd, _ = past_t.shape
    tq = u_t.shape[0]
    bt = _pick_tile(bd, 32)
    const2 = lambda i: (0, 0)
    return pl.pallas_call(
        functools.partial(_conv_sample_kernel, tq=tq),
        grid=(bd // bt,),
        in_specs=[
            pl.BlockSpec((n_past, bt, D_CONV), lambda i: (0, i, 0)),
            pl.BlockSpec((tq, bt, D_CONV), lambda i: (0, i, 0)),
            pl.BlockSpec((CONV_K, D_CONV), const2),
            pl.BlockSpec((1, D_CONV), const2),
            pl.BlockSpec((1, D_CONV), const2),
            pl.BlockSpec((1, D_CONV), const2),
        ],
        out_specs=pl.BlockSpec((tq, bt, D_CONV), lambda i: (0, i, 0)),
        out_shape=jax.ShapeDtypeStruct((tq, bd, D_CONV), BF16),
        compiler_params=_params(1),
    )(past_t, u_t, dw_w, dw_b, ln_g, ln_b)


def _chunkproj_kernel(x_ref, wk_ref, wv_ref, pk_ref, pv_ref, ab_ref):
    half = KV_COLS // 2
    for kv, (w_ref, p_ref) in enumerate(((wk_ref, pk_ref), (wv_ref, pv_ref))):
        xs = jnp.concatenate(
            [x_ref[:, l * KV_COLS + kv * half:l * KV_COLS + (kv + 1) * half] for l in range(CMP_STRIDE)],
            axis=1).astype(BF16)
        w = w_ref[...]
        part = _dot(xs, w)
        posb = _dot(p_ref[...], w)
        c0 = kv * 2 * LANES
        ab_ref[:, c0:c0 + LANES] = part[:, 0:LANES] + posb[0:1, 0:LANES]
        ab_ref[:, c0 + LANES:c0 + 2 * LANES] = part[:, LANES:2 * LANES] + posb[1:2, LANES:2 * LANES]


def _chunkproj(x2, wk2, wv2, pk2, pv2, row_off, n_rows):
    r = _pick_tile(n_rows, 256)
    off = row_off // r
    const2 = lambda i: (0, 0)
    return pl.pallas_call(
        _chunkproj_kernel,
        grid=(n_rows // r,),
        in_specs=[
            pl.BlockSpec((r, CHUNK_COLS), lambda i: (i + off, 0)),
            pl.BlockSpec(wk2.shape, const2),
            pl.BlockSpec(wv2.shape, const2),
            pl.BlockSpec(pk2.shape, const2),
            pl.BlockSpec(pv2.shape, const2),
        ],
        out_specs=pl.BlockSpec((r, 4 * LANES), lambda i: (i, 0)),
        out_shape=jax.ShapeDtypeStruct((n_rows, 4 * LANES), F32),
        compiler_params=_params(1),
    )(x2, wk2, wv2, pk2, pv2)


def _compressed_kv(ab, cosc, sinc):
    n = ab.shape[0]
    kc = ab[:, 0:LANES] + pltpu.roll(ab[:, LANES:2 * LANES], n - 1, 0)
    vc = ab[:, 2 * LANES:3 * LANES] + pltpu.roll(ab[:, 3 * LANES:4 * LANES], n - 1, 0)
    return _rope_slab(kc, cosc, sinc).astype(BF16), vc.astype(BF16)


def _softmax_parts(s, mask):
    sm = jnp.where(mask, s, NEG)
    m = jnp.max(sm, axis=-1, keepdims=True)
    p = jnp.where(mask, jnp.exp(sm - m), 0.0)
    l = jnp.maximum(jnp.sum(p, axis=-1, keepdims=True), 1e-30)
    return p, l


def _select_blocks(imp, qpos, n_sel):
    jcol = lax.broadcasted_iota(jnp.int32, imp.shape, 1)
    cur = lax.shift_right_logical(qpos, 6)
    forced = (jcol == 0) | (jcol == cur) | (jcol == cur - 1)
    valid = (jcol * SEL_BLOCK <= qpos) & (jcol < n_sel)
    v = jnp.where(valid, jnp.where(forced, jnp.inf, imp), -jnp.inf)
    rank = jnp.zeros(imp.shape, jnp.int32)
    for k in range(n_sel):
        vk = v[:, k:k + 1]
        ahead = (vk > v) | ((vk == v) & (jcol > k))
        rank = rank + ahead.astype(jnp.int32)
    return (rank < min(N_SEL, n_sel)) & valid


def _overlap_matrix(n_cmp):
    i = lax.broadcasted_iota(jnp.int32, (LANES, LANES), 0)
    j = lax.broadcasted_iota(jnp.int32, (LANES, LANES), 1)
    hit = (i * CMP_STRIDE < (j + 1) * SEL_BLOCK) & (i * CMP_STRIDE + CMP_BLOCK > j * SEL_BLOCK) & (i < n_cmp)
    return hit.astype(F32)


def _importance(hs, n_cmp):
    return jnp.dot(hs, _overlap_matrix(n_cmp), preferred_element_type=F32, precision=lax.Precision.HIGHEST)


Q_TILE = 256
K_TILE = 256


def _attn_prompt_kernel(q_ref, gn_ref, skv_ref, wkv_ref, ab_ref, cosc_ref, sinc_ref, o_ref, kc_ref, vc_ref,
                        *, t_len):
    qb = pl.program_id(1)
    n_chunk = t_len // CMP_STRIDE
    n_cmp = (t_len - CMP_BLOCK) // CMP_STRIDE + 1
    n_sel = -(-t_len // SEL_BLOCK)
    qt, kt_sz = Q_TILE, K_TILE

    @pl.when(qb == 0)
    def _():
        kc, vc = _compressed_kv(ab_ref[...], cosc_ref[...], sinc_ref[...])
        kc_ref[...] = kc
        vc_ref[...] = vc

    q0 = qb * qt
    qpos = q0 + lax.broadcasted_iota(jnp.int32, (qt, 1), 0)
    gn = gn_ref[...]

    for g in range(N_KV):
        qz = jnp.concatenate([q_ref[:, (g * HPG + h) * LANES:(g * HPG + h + 1) * LANES] for h in range(HPG)], axis=0)

        ncol = lax.broadcasted_iota(jnp.int32, (qt, n_chunk), 1)
        mask_c = (ncol * CMP_STRIDE + CMP_BLOCK - 1 <= qpos) & (ncol < n_cmp)
        s_c = _dot_nt(qz, kc_ref[...]).reshape(HPG, qt, n_chunk)
        p_c, l_c = _softmax_parts(s_c, mask_c[None])
        p_c = p_c / l_c
        o_c = _dot(p_c.reshape(HPG * qt, n_chunk).astype(BF16), vc_ref[...])
        hs = p_c[0]
        for h in range(1, HPG):
            hs = hs + p_c[h]
        sel = _select_blocks(_importance(hs, n_cmp), qpos, n_sel).astype(BF16)

        def tile_step(kv_ref, k_idx, carry, mask_fn):
            m, l, acc = carry
            k0 = pl.multiple_of(k_idx * kt_sz, kt_sz)
            kslab = kv_ref[pl.ds(k0, kt_sz), 0:LANES].astype(BF16)
            vslab = kv_ref[pl.ds(k0, kt_sz), LANES:KV_COLS].astype(BF16)
            s = _dot_nt(qz, kslab).reshape(HPG, qt, kt_sz)
            kpos = k0 + lax.broadcasted_iota(jnp.int32, (qt, kt_sz), 1)
            mask = mask_fn(k_idx, kpos)[None]
            sm = jnp.where(mask, s, NEG)
            m_new = jnp.maximum(m, jnp.max(sm, axis=-1, keepdims=True))
            alpha = jnp.exp(m - m_new)
            p = jnp.where(mask, jnp.exp(sm - m_new), 0.0)
            l = alpha * l + jnp.sum(p, axis=-1, keepdims=True)
            pv = _dot(p.reshape(HPG * qt, kt_sz).astype(BF16), vslab)
            acc = alpha.reshape(HPG * qt, 1) * acc + pv
            return m_new, l, acc

        def sel_mask(k_idx, kpos):
            jrow = lax.broadcasted_iota(jnp.int32, (LANES, kt_sz), 0)
            kcol = lax.broadcasted_iota(jnp.int32, (LANES, kt_sz), 1)
            expand = (jrow == k_idx * (kt_sz // SEL_BLOCK) + lax.shift_right_logical(kcol, 6)).astype(BF16)
            return (_dot(sel, expand) > 0.5) & (kpos <= qpos)

        def win_mask(k_idx, kpos):
            return (kpos <= qpos) & (kpos > qpos - WINDOW)

        init = (jnp.full((HPG, qt, 1), NEG, F32), jnp.zeros((HPG, qt, 1), F32), jnp.zeros((HPG * qt, LANES), F32))
        _, l_s, acc_s = lax.fori_loop(0, qb + 1, lambda i, c: tile_step(skv_ref, i, c, sel_mask), init)
        lo = jnp.maximum(qb - WINDOW // kt_sz, 0)
        _, l_w, acc_w = lax.fori_loop(lo, qb + 1, lambda i, c: tile_step(wkv_ref, i, c, win_mask), init)
        o_s = acc_s / jnp.maximum(l_s, 1e-30).reshape(HPG * qt, 1)
        o_w = acc_w / jnp.maximum(l_w, 1e-30).reshape(HPG * qt, 1)

        for h in range(HPG):
            hh = g * HPG + h
            rows = slice(h * qt, (h + 1) * qt)
            out = (gn[:, hh:hh + 1] * o_c[rows]
                   + gn[:, N_HEADS + hh:N_HEADS + hh + 1] * o_s[rows]
                   + gn[:, 2 * N_HEADS + hh:2 * N_HEADS + hh + 1] * o_w[rows])
            o_ref[:, hh * LANES:(hh + 1) * LANES] = out.astype(BF16)


def _attn_prompt(q2, gn2, skv2, wkv2, ab2, cosc, sinc, b, t_len):
    n_chunk = t_len // CMP_STRIDE
    nqb = t_len // Q_TILE
    row = lambda i, j: (i * nqb + j, 0)
    per_b = lambda i, j: (i, 0)
    const2 = lambda i, j: (0, 0)
    return pl.pallas_call(
        functools.partial(_attn_prompt_kernel, t_len=t_len),
        grid=(b, nqb),
        in_specs=[
            pl.BlockSpec((Q_TILE, Q_EXP), row),
            pl.BlockSpec((Q_TILE, LANES), row),
            pl.BlockSpec((t_len, KV_COLS), per_b),
            pl.BlockSpec((t_len, KV_COLS), per_b),
            pl.BlockSpec((n_chunk, 4 * LANES), per_b),
            pl.BlockSpec((n_chunk, LANES), const2),
            pl.BlockSpec((n_chunk, LANES), const2),
        ],
        out_specs=pl.BlockSpec((Q_TILE, Q_EXP), row),
        out_shape=jax.ShapeDtypeStruct((b * t_len, Q_EXP), BF16),
        scratch_shapes=[pltpu.VMEM((n_chunk, LANES), BF16), pltpu.VMEM((n_chunk, LANES), BF16)],
        compiler_params=_params(2),
    )(q2, gn2, skv2, wkv2, ab2, cosc, sinc)


Q_ROWS = 8


def _attn_sample_kernel(pt_ref, q_ref, gn_ref, sknew_ref, wknew_ref, win_ref, cosc_ref, sinc_ref, *rest,
                        past_len, tq, n_pages):
    del pt_ref
    sel_pages = rest[:n_pages]
    ab_pages = rest[n_pages:2 * n_pages]
    o_ref = rest[2 * n_pages]
    t_all = past_len + tq
    n_chunk = past_len // CMP_STRIDE
    n_cmp = (t_all - CMP_BLOCK) // CMP_STRIDE + 1
    n_sel = -(-t_all // SEL_BLOCK)
    rows = N_HEADS * Q_ROWS
    win_buf = win_ref.shape[1]

    qbd = jnp.concatenate([q_ref[0, :, hh * LANES:(hh + 1) * LANES] for hh in range(N_HEADS)], axis=0)
    qpos = past_len + lax.broadcasted_iota(jnp.int32, (rows, 1), 0) % Q_ROWS

    ab = jnp.concatenate([r[0] for r in ab_pages], axis=0)
    kc, vc = _compressed_kv(ab, cosc_ref[...], sinc_ref[...])
    ncol = lax.broadcasted_iota(jnp.int32, (rows, n_chunk), 1)
    mask_c = (ncol * CMP_STRIDE + CMP_BLOCK - 1 <= qpos) & (ncol < n_cmp)
    p_c, l_c = _softmax_parts(_dot_nt(qbd, kc), mask_c)
    p_c = p_c / l_c
    o_c = _dot(p_c.astype(BF16), vc)

    hs = []
    for g in range(N_KV):
        acc = p_c[g * HPG * Q_ROWS:(g * HPG + 1) * Q_ROWS]
        for h in range(1, HPG):
            acc = acc + p_c[(g * HPG + h) * Q_ROWS:(g * HPG + h + 1) * Q_ROWS]
        hs.append(acc)
    hs = jnp.concatenate(hs, axis=0)
    sel_g = _select_blocks(_importance(hs, n_cmp), qpos[0:N_KV * Q_ROWS], n_sel).astype(BF16)
    sel_rows = jnp.concatenate([sel_g[g * Q_ROWS:(g + 1) * Q_ROWS] for g in range(N_KV) for _ in range(HPG)], axis=0)

    def new_rows_tile(ref, c0):
        return jnp.concatenate([ref[0, :, c0:c0 + LANES], jnp.zeros((LANES - Q_ROWS, LANES), F32)], axis=0).astype(BF16)

    n_keys = (n_pages + 1) * PAGE_SIZE
    k_tiles = [r[0, :, 0:LANES].astype(BF16) for r in sel_pages] + [new_rows_tile(sknew_ref, 0)]
    v_tiles = [r[0, :, LANES:KV_COLS].astype(BF16) for r in sel_pages] + [new_rows_tile(sknew_ref, LANES)]
    s_s = jnp.concatenate([_dot_nt(qbd, kt) for kt in k_tiles], axis=1)
    jrow = lax.broadcasted_iota(jnp.int32, (LANES, n_keys), 0)
    kcol = lax.broadcasted_iota(jnp.int32, (LANES, n_keys), 1)
    expand = (jrow == lax.shift_right_logical(kcol, 6)).astype(BF16)
    kpos = lax.broadcasted_iota(jnp.int32, (rows, n_keys), 1)
    mask_s = (_dot(sel_rows, expand) > 0.5) & (kpos <= qpos)
    p_s, l_s = _softmax_parts(s_s, mask_s)
    p_s = p_s.astype(BF16)
    o_s = _dot(p_s[:, 0:PAGE_SIZE], v_tiles[0])
    for i in range(1, n_pages + 1):
        o_s = o_s + _dot(p_s[:, i * PAGE_SIZE:(i + 1) * PAGE_SIZE], v_tiles[i])
    o_s = o_s / l_s

    kw = win_ref[0, :, 0:LANES].astype(BF16)
    vw = win_ref[0, :, LANES:KV_COLS].astype(BF16)
    s_w = jnp.concatenate([_dot_nt(qbd, kw), _dot_nt(qbd, new_rows_tile(wknew_ref, 0))], axis=1)
    wcol = lax.broadcasted_iota(jnp.int32, (rows, win_buf + LANES), 1)
    kpos_w = past_len - win_buf + wcol
    mask_w = (kpos_w <= qpos) & (kpos_w > qpos - WINDOW)
    p_w, l_w = _softmax_parts(s_w, mask_w)
    p_w = p_w.astype(BF16)
    o_w = (_dot(p_w[:, 0:win_buf], vw) + _dot(p_w[:, win_buf:], new_rows_tile(wknew_ref, LANES))) / l_w

    gn = gn_ref[0]
    for hh in range(N_HEADS):
        r = slice(hh * Q_ROWS, (hh + 1) * Q_ROWS)
        out = (gn[:, hh:hh + 1] * o_c[r]
               + gn[:, N_HEADS + hh:N_HEADS + hh + 1] * o_s[r]
               + gn[:, 2 * N_HEADS + hh:2 * N_HEADS + hh + 1] * o_w[r])
        o_ref[0, :, hh * LANES:(hh + 1) * LANES] = out.astype(BF16)


def _attn_sample(pt_flat, q3, gn3, sknew3, wknew3, win3, cosc, sinc, sel_pool3, ab_pool3,
                 layer, n_pool, bd, past_len, tq):
    n_pages = past_len // PAGE_SIZE
    win_buf = win3.shape[1]
    n_chunk = past_len // CMP_STRIDE
    per_b = lambda i, pt: (i, 0, 0)
    const2 = lambda i, pt: (0, 0)
    sel_specs = [pl.BlockSpec((1, PAGE_SIZE, KV_COLS),
                              lambda i, pt, p=p: (layer * n_pool + pt[i * n_pages + p], 0, 0)) for p in range(n_pages)]
    ab_specs = [pl.BlockSpec((1, CHUNKS_PER_PAGE, 4 * LANES),
                             lambda i, pt, p=p: (pt[i * n_pages + p], 0, 0)) for p in range(n_pages)]
    grid_spec = pltpu.PrefetchScalarGridSpec(
        num_scalar_prefetch=1,
        grid=(bd,),
        in_specs=[
            pl.BlockSpec((1, Q_ROWS, Q_EXP), per_b),
            pl.BlockSpec((1, Q_ROWS, LANES), per_b),
            pl.BlockSpec((1, Q_ROWS, KV_COLS), per_b),
            pl.BlockSpec((1, Q_ROWS, KV_COLS), per_b),
            pl.BlockSpec((1, win_buf, KV_COLS), lambda i, pt: (layer * bd + i, 0, 0)),
            pl.BlockSpec((n_chunk, LANES), const2),
            pl.BlockSpec((n_chunk, LANES), const2),
        ] + sel_specs + ab_specs,
        out_specs=pl.BlockSpec((1, Q_ROWS, Q_EXP), per_b),
    )
    return pl.pallas_call(
        functools.partial(_attn_sample_kernel, past_len=past_len, tq=tq, n_pages=n_pages),
        grid_spec=grid_spec,
        out_shape=jax.ShapeDtypeStruct((bd, Q_ROWS, Q_EXP), BF16),
        compiler_params=_params(1),
    )(pt_flat, q3, gn3, sknew3, wknew3, win3, cosc, sinc, *([sel_pool3] * n_pages), *([ab_pool3] * n_pages))


def _merge_kernel(c_ref, o_ref, gm_ref, x_ref, wc_ref, wn_ref, wo_ref, y_ref):
    conv_out = _dot(c_ref[...], wc_ref[...])
    nsa_out = _dot(o_ref[...], wn_ref[...])
    gm = gm_ref[...].astype(F32)
    merged = gm[:, 0:D_MODEL] * conv_out + gm[:, D_MODEL:2 * D_MODEL] * nsa_out
    y_ref[...] = x_ref[...] + _dot(merged.astype(BF16), wo_ref[...])


def _merge(c2, o2, gm2, x2, wc, wn, wo, tm):
    m = x2.shape[0]
    row = lambda i: (i, 0)
    const = lambda i: (0, 0)
    return pl.pallas_call(
        _merge_kernel,
        grid=(m // tm,),
        in_specs=[
            pl.BlockSpec((tm, D_CONV), row),
            pl.BlockSpec((tm, Q_EXP), row),
            pl.BlockSpec((tm, 2 * D_MODEL), row),
            pl.BlockSpec((tm, D_MODEL), row),
            pl.BlockSpec(wc.shape, const),
            pl.BlockSpec(wn.shape, const),
            pl.BlockSpec(wo.shape, const),
        ],
        out_specs=pl.BlockSpec((tm, D_MODEL), row),
        out_shape=jax.ShapeDtypeStruct((m, D_MODEL), F32),
        compiler_params=_params(1),
    )(c2, o2, gm2, x2, wc, wn, wo)


FF_TILE = 1024


def _mlp_kernel(x_ref, g_ref, wu_ref, wd_ref, gf_ref, y_ref, *, final):
    x = x_ref[...]
    ms = jnp.mean(x * x, axis=-1, keepdims=True)
    h = (x * lax.rsqrt(ms + EPS) * g_ref[...]).astype(BF16)
    acc = x
    for f in range(D_FF // FF_TILE):
        up = jnp.maximum(_dot(h, wu_ref[:, f * FF_TILE:(f + 1) * FF_TILE]), 0.0)
        acc = acc + _dot((up * up).astype(BF16), wd_ref[f * FF_TILE:(f + 1) * FF_TILE, :])
    if final:
        ms2 = jnp.mean(acc * acc, axis=-1, keepdims=True)
        acc = acc * lax.rsqrt(ms2 + EPS) * gf_ref[...]
    y_ref[...] = acc


def _mlp(x2, g, wu, wd, gf, tm, final):
    m = x2.shape[0]
    row = lambda i: (i, 0)
    const = lambda i: (0, 0)
    return pl.pallas_call(
        functools.partial(_mlp_kernel, final=final),
        grid=(m // tm,),
        in_specs=[
            pl.BlockSpec((tm, D_MODEL), row),
            pl.BlockSpec((1, D_MODEL), const),
            pl.BlockSpec(wu.shape, const),
            pl.BlockSpec(wd.shape, const),
            pl.BlockSpec((1, D_MODEL), const),
        ],
        out_specs=pl.BlockSpec((tm, D_MODEL), row),
        out_shape=jax.ShapeDtypeStruct((m, D_MODEL), F32),
        compiler_params=_params(1),
    )(x2, g, wu, wd, gf)


def _rope_tables(pos):
    half = HEAD_DIM // 2
    inv = jnp.power(ROPE_THETA, -jnp.arange(half, dtype=F32) / half)
    ang = pos.astype(F32)[:, None] * inv[None, :]
    cos = jnp.cos(ang)
    sin = jnp.sin(ang)
    return jnp.concatenate([cos, cos, cos, cos], axis=1), jnp.concatenate([-sin, sin, -sin, sin], axis=1)


def _pack_w_in(w):
    o0 = 2 * D_CONV
    o1 = o0 + N_HEADS * HEAD_DIM
    o4 = o1 + 3 * KV_COLS
    o5 = o4 + N_GATES
    wq = w[:, o0:o1].reshape(D_MODEL, N_HEADS, HEAD_DIM) * (HEAD_DIM ** -0.5)
    zero = jnp.zeros_like(wq)
    in_g0 = (jnp.arange(N_HEADS) < HPG)[None, :, None]
    wq_exp = jnp.concatenate([jnp.where(in_g0, wq, zero), jnp.where(in_g0, zero, wq)], axis=2)
    wgn = jnp.pad(w[:, o4:o5], ((0, 0), (0, LANES - N_GATES)))
    packed = jnp.concatenate([w[:, :o0], wq_exp.reshape(D_MODEL, Q_EXP), w[:, o1:o4], wgn, w[:, o5:]], axis=1)
    return packed.astype(BF16)


def _pack_w_nsa_out(w):
    wh = w.reshape(N_HEADS, HEAD_DIM, D_MODEL)
    zero = jnp.zeros_like(wh)
    in_g0 = (jnp.arange(N_HEADS) < HPG)[:, None, None]
    return jnp.concatenate([jnp.where(in_g0, wh, zero), jnp.where(in_g0, zero, wh)], axis=1).reshape(Q_EXP, D_MODEL).astype(BF16)


def _pack_w_cmp(w_kv, pos_kv):
    lo, hi = w_kv[:CMP_STRIDE], w_kv[CMP_STRIDE:]
    eye = jnp.eye(N_KV, dtype=w_kv.dtype)
    blk = lambda part: jnp.einsum('lde,gh->lgdhe', part, eye).reshape(CMP_STRIDE * N_KV * HEAD_DIM, N_KV * HEAD_DIM)
    w2 = jnp.concatenate([blk(lo), blk(hi)], axis=1).astype(BF16)
    tile = lambda p: jnp.broadcast_to(p[:, None, :], (CMP_STRIDE, N_KV, HEAD_DIM)).reshape(1, -1)
    p2 = jnp.concatenate([tile(pos_kv[:CMP_STRIDE]), tile(pos_kv[CMP_STRIDE:]),
                          jnp.zeros((6, CMP_STRIDE * N_KV * HEAD_DIM), pos_kv.dtype)], axis=0).astype(BF16)
    return w2, p2


def kernel(x_prompt, x_sample, cache_cmp_kv, cache_sel_kv, state_win_kv, state_conv, page_table, norm_mix_g, w_in, conv_dw_w, conv_dw_b, conv_ln_g, conv_ln_b, w_conv_out, cmp_pos, w_cmp, w_nsa_out, w_out, norm_mlp_g, w_up, w_down, norm_final_g):
    depth = w_in.shape[0]
    bp, tp, _ = x_prompt.shape
    bd, tq, _ = x_sample.shape
    n_pool = cache_cmp_kv.shape[1]
    n_pages = page_table.shape[1]
    past_len = n_pages * PAGE_SIZE
    win_buf = state_win_kv.shape[2]
    assert tp % Q_TILE == 0 and tp // CMP_STRIDE == LANES and tp >= WINDOW
    assert past_len // CMP_STRIDE == LANES and win_buf == WINDOW and tq <= Q_ROWS

    mp, ms = bp * tp, bd * tq
    tm_p = _pick_tile(mp, 256)
    tm_s = _pick_tile(ms, 256)
    assert tp % tm_p == 0 and tm_s % tq == 0

    cos_p, sin_p = _rope_tables(jnp.arange(tp))
    cos_s, sin_s = _rope_tables(past_len + jnp.arange(tm_s) % tq)
    n_chunk = tp // CMP_STRIDE
    cos_c, sin_c = _rope_tables(jnp.arange(n_chunk) * CMP_STRIDE + CMP_BLOCK - 1)

    pt_flat = page_table.reshape(-1).astype(jnp.int32)
    cmp_pool2 = cache_cmp_kv.reshape(depth * n_pool * CHUNKS_PER_PAGE, CHUNK_COLS)
    sel_pool3 = cache_sel_kv.reshape(depth * n_pool, PAGE_SIZE, KV_COLS)
    win3 = state_win_kv.reshape(depth * bd, win_buf, KV_COLS)
    zeros_conv = jnp.zeros((bp, CONV_K - 1, D_CONV), F32)

    xp = x_prompt.reshape(mp, D_MODEL)
    xs = x_sample.reshape(ms, D_MODEL)
    outs = [[] for _ in range(8)]
    for l in range(depth):
        w_packed = _pack_w_in(w_in[l])
        wn = _pack_w_nsa_out(w_nsa_out[l])
        wc, wo = w_conv_out[l].astype(BF16), w_out[l].astype(BF16)
        wu, wd = w_up[l].astype(BF16), w_down[l].astype(BF16)
        wk2, pk2 = _pack_w_cmp(w_cmp[l, 0], cmp_pos[l, 0])
        wv2, pv2 = _pack_w_cmp(w_cmp[l, 1], cmp_pos[l, 1])
        g_mix, g_mlp = norm_mix_g[l][None], norm_mlp_g[l][None]
        dw_b, ln_g, ln_b = conv_dw_b[l][None], conv_ln_g[l][None], conv_ln_b[l][None]
        gf = norm_final_g[None]
        final = l == depth - 1

        u, q, ckv, skv, wkv, gn, gm = _inproj(xp, g_mix, w_packed, cos_p, sin_p, tm_p)
        c_act = _conv_prompt(u.reshape(bp, tp, D_CONV), zeros_conv, conv_dw_w[l], dw_b, ln_g, ln_b)
        ab = _chunkproj(ckv.reshape(mp // CMP_STRIDE, CHUNK_COLS), wk2, wv2, pk2, pv2, 0, mp // CMP_STRIDE)
        o = _attn_prompt(q, gn, skv, wkv, ab, cos_c, sin_c, bp, tp)
        xp = _merge(c_act.reshape(mp, D_CONV), o, gm, xp, wc, wn, wo, tm_p)
        xp = _mlp(xp, g_mlp, wu, wd, gf, _pick_tile(mp, 512), final)
        outs[0].append(ckv.reshape(bp, tp, 2, N_KV, HEAD_DIM))
        outs[2].append(skv.reshape(bp, tp, 2, N_KV, HEAD_DIM))
        outs[4].append(wkv.reshape(bp, tp, 2, N_KV, HEAD_DIM)[:, tp - min(WINDOW, tp):])
        outs[6].append(u.reshape(bp, tp, D_CONV)[:, tp - (CONV_K - 1):])

        u, q, ckv, skv, wkv, gn, gm = _inproj(xs, g_mix, w_packed, cos_s, sin_s, tm_s)
        u3 = u.reshape(bd, tq, D_CONV)
        c_t = _conv_sample(jnp.swapaxes(state_conv[l], 0, 1), jnp.swapaxes(u3, 0, 1), conv_dw_w[l], dw_b, ln_g, ln_b)
        c_act = jnp.swapaxes(c_t, 0, 1).reshape(ms, D_CONV)
        ab_pool = _chunkproj(cmp_pool2, wk2, wv2, pk2, pv2, l * n_pool * CHUNKS_PER_PAGE, n_pool * CHUNKS_PER_PAGE)
        pad_q = lambda a: jnp.pad(a.reshape(bd, tq, -1), ((0, 0), (0, Q_ROWS - tq), (0, 0)))
        o = _attn_sample(pt_flat, pad_q(q), pad_q(gn), pad_q(skv), pad_q(wkv), win3, cos_c, sin_c, sel_pool3,
                         ab_pool.reshape(n_pool, CHUNKS_PER_PAGE, 4 * LANES), l, n_pool, bd, past_len, tq)
        xs = _merge(c_act, o[:, :tq].reshape(ms, Q_EXP), gm, xs, wc, wn, wo, tm_s)
        xs = _mlp(xs, g_mlp, wu, wd, gf, _pick_tile(ms, 512), final)
        wkv5 = wkv.reshape(bd, tq, 2, N_KV, HEAD_DIM)
        outs[1].append(ckv.reshape(bd, tq, 2, N_KV, HEAD_DIM))
        outs[3].append(skv.reshape(bd, tq, 2, N_KV, HEAD_DIM))
        outs[5].append(jnp.concatenate([state_win_kv[l], wkv5], axis=1)[:, -win_buf:])
        outs[7].append(jnp.concatenate([state_conv[l], u3], axis=1)[:, -(CONV_K - 1):])

    y_prompt = xp.reshape(bp, tp, D_MODEL)
    y_sample = xs.reshape(bd, tq, D_MODEL)
    return (y_prompt, y_sample) + tuple(jnp.stack(o) for o in outs)
```

```python
import functools

import jax
import jax.numpy as jnp
from jax import lax
from jax.experimental import pallas as pl
from jax.experimental.pallas import tpu as pltpu

D_MODEL = 1024
D_CONV = D_MODEL // 2
CONV_K = 31
N_HEADS = 8
HEAD_DIM = 64
N_KV = 2
HPG = N_HEADS // N_KV
KV_COLS = 2 * N_KV * HEAD_DIM
CMP_BLOCK = 32
CMP_STRIDE = 16
SEL_BLOCK = 64
N_SEL = 16
WINDOW = 512
D_FF = 4 * D_MODEL
ROPE_THETA = 10000.0
EPS = 1e-6
PAGE_SIZE = 128

LANES = 128
SUBLANES = 8
Q_EXP = N_HEADS * LANES
N_GATES = 3 * N_HEADS
CHUNKS_PER_PAGE = PAGE_SIZE // CMP_STRIDE
NEG = -1e30

O_GLU = 0
O_Q = O_GLU + 2 * D_CONV
O_CKV = O_Q + Q_EXP
O_SKV = O_CKV + KV_COLS
O_WKV = O_SKV + KV_COLS
O_GN = O_WKV + KV_COLS
O_GM = O_GN + LANES
IN_COLS_PACKED = O_GM + 2 * D_MODEL

VMEM_LIMIT = 56 * 1024 * 1024

F32 = jnp.float32
BF16 = jnp.bfloat16


def _params(n_axes, vmem=VMEM_LIMIT):
    return pltpu.CompilerParams(dimension_semantics=("arbitrary",) * n_axes, vmem_limit_bytes=vmem)


def _sigmoid(x):
    return 1.0 / (1.0 + jnp.exp(-x))


def _dot(a, b):
    return jnp.dot(a, b, preferred_element_type=F32)


def _dot_nt(a, b):
    return lax.dot_general(a, b, (((1,), (1,)), ((), ())), preferred_element_type=F32)


def _rope_slab(xs, cos, sin_signed):
    lane = lax.broadcasted_iota(jnp.int32, xs.shape, 1)
    first = (lane % HEAD_DIM) < (HEAD_DIM // 2)
    rot = jnp.where(first, pltpu.roll(xs, LANES - HEAD_DIM // 2, 1), pltpu.roll(xs, HEAD_DIM // 2, 1))
    return xs * cos + rot * sin_signed


def _rope_rows(xt, cos_t, sin_t):
    half = HEAD_DIM // 2
    rot = jnp.concatenate([xt[half:2 * half], xt[0:half], xt[3 * half:4 * half], xt[2 * half:3 * half]], axis=0)
    return xt * cos_t + rot * sin_t


def _pick_tile(n, cap, mult=8):
    t = min(n, cap)
    while n % t or t % mult:
        t -= 1
    return t


def _rms_bf16(x, g):
    ms = jnp.mean(x * x, axis=-1, keepdims=True)
    return (x * lax.rsqrt(ms + EPS) * g).astype(BF16)


def _inproj_common(h, w_ref, cos, sin, u_ref, q_ref, gn_ref, gm_ref):
    def proj(a, n):
        return _dot(h, w_ref[:, a:a + n])

    u_ref[...] = proj(O_GLU, D_CONV) * _sigmoid(proj(O_GLU + D_CONV, D_CONV))
    for s in range(N_HEADS):
        q_ref[:, s * LANES:(s + 1) * LANES] = _rope_slab(proj(O_Q + s * LANES, LANES), cos, sin).astype(BF16)
    gn_ref[...] = _sigmoid(proj(O_GN, LANES))
    gm_ref[...] = _sigmoid(proj(O_GM, 2 * D_MODEL)).astype(BF16)
    return proj


def _inproj_rows_kernel(x_ref, g_ref, w_ref, cos_ref, sin_ref,
                        u_ref, q_ref, ckv_ref, skv_ref, wkv_ref, gn_ref, gm_ref):
    h = _rms_bf16(x_ref[...], g_ref[...])
    cos = cos_ref[...]
    sin = sin_ref[...]
    proj = _inproj_common(h, w_ref, cos, sin, u_ref, q_ref, gn_ref, gm_ref)
    ckv_ref[...] = proj(O_CKV, KV_COLS)
    skv_ref[:, 0:LANES] = _rope_slab(proj(O_SKV, LANES), cos, sin)
    skv_ref[:, LANES:KV_COLS] = proj(O_SKV + LANES, LANES)
    wkv_ref[:, 0:LANES] = _rope_slab(proj(O_WKV, LANES), cos, sin)
    wkv_ref[:, LANES:KV_COLS] = proj(O_WKV + LANES, LANES)


def _inproj_cols_kernel(x_ref, g_ref, w_ref, wkvt_ref, cos_ref, sin_ref, cost_ref, sint_ref,
                        u_ref, q_ref, ckv_ref, skv_ref, wkv_ref, gn_ref, gm_ref):
    h = _rms_bf16(x_ref[...], g_ref[...])
    _inproj_common(h, w_ref, cos_ref[...], sin_ref[...], u_ref, q_ref, gn_ref, gm_ref)
    cos_t = cost_ref[...]
    sin_t = sint_ref[...]
    kvt = _dot_nt(wkvt_ref[...], h)
    ckv_ref[0] = kvt[0:KV_COLS]
    skv_ref[0, 0:LANES] = _rope_rows(kvt[KV_COLS:KV_COLS + LANES], cos_t, sin_t)
    skv_ref[0, LANES:KV_COLS] = kvt[KV_COLS + LANES:2 * KV_COLS]
    wkv_ref[0, 0:LANES] = _rope_rows(kvt[2 * KV_COLS:2 * KV_COLS + LANES], cos_t, sin_t)
    wkv_ref[0, LANES:KV_COLS] = kvt[2 * KV_COLS + LANES:3 * KV_COLS]


def _inproj_rows(x2, g, w_packed, cos_t, sin_t, tm):
    m = x2.shape[0]
    row = lambda i: (i, 0)
    const = lambda i: (0, 0)
    out_shapes = (
        jax.ShapeDtypeStruct((m, D_CONV), F32),
        jax.ShapeDtypeStruct((m, Q_EXP), BF16),
        jax.ShapeDtypeStruct((m, KV_COLS), F32),
        jax.ShapeDtypeStruct((m, KV_COLS), F32),
        jax.ShapeDtypeStruct((m, KV_COLS), F32),
        jax.ShapeDtypeStruct((m, LANES), F32),
        jax.ShapeDtypeStruct((m, 2 * D_MODEL), BF16),
    )
    return pl.pallas_call(
        _inproj_rows_kernel,
        grid=(m // tm,),
        in_specs=[
            pl.BlockSpec((tm, D_MODEL), row),
            pl.BlockSpec((1, D_MODEL), const),
            pl.BlockSpec(w_packed.shape, const),
            pl.BlockSpec((tm, LANES), const),
            pl.BlockSpec((tm, LANES), const),
        ],
        out_specs=tuple(pl.BlockSpec((tm, s.shape[1]), row) for s in out_shapes),
        out_shape=out_shapes,
        compiler_params=_params(1),
    )(x2, g, w_packed, cos_t, sin_t)


def _inproj_cols(x2, g, w_packed, wkv_t, cos_t, sin_t, cos_tt, sin_tt, b, t_len, tm):
    m = x2.shape[0]
    n_tab = t_len // tm
    row = lambda i: (i, 0)
    const = lambda i: (0, 0)
    kv_map = lambda i: (i // n_tab, 0, i % n_tab)
    kv_shape = jax.ShapeDtypeStruct((b, KV_COLS, t_len), F32)
    out_shapes = (
        jax.ShapeDtypeStruct((m, D_CONV), F32),
        jax.ShapeDtypeStruct((m, Q_EXP), BF16),
        kv_shape, kv_shape, kv_shape,
        jax.ShapeDtypeStruct((m, LANES), F32),
        jax.ShapeDtypeStruct((m, 2 * D_MODEL), BF16),
    )
    out_specs = (
        pl.BlockSpec((tm, D_CONV), row),
        pl.BlockSpec((tm, Q_EXP), row),
        pl.BlockSpec((1, KV_COLS, tm), kv_map),
        pl.BlockSpec((1, KV_COLS, tm), kv_map),
        pl.BlockSpec((1, KV_COLS, tm), kv_map),
        pl.BlockSpec((tm, LANES), row),
        pl.BlockSpec((tm, 2 * D_MODEL), row),
    )
    return pl.pallas_call(
        _inproj_cols_kernel,
        grid=(m // tm,),
        in_specs=[
            pl.BlockSpec((tm, D_MODEL), row),
            pl.BlockSpec((1, D_MODEL), const),
            pl.BlockSpec(w_packed.shape, const),
            pl.BlockSpec(wkv_t.shape, const),
            pl.BlockSpec((tm, LANES), lambda i: (i % n_tab, 0)),
            pl.BlockSpec((tm, LANES), lambda i: (i % n_tab, 0)),
            pl.BlockSpec((LANES, tm), lambda i: (0, i % n_tab)),
            pl.BlockSpec((LANES, tm), lambda i: (0, i % n_tab)),
        ],
        out_specs=out_specs,
        out_shape=out_shapes,
        compiler_params=_params(1),
    )(x2, g, w_packed, wkv_t, cos_t, sin_t, cos_tt, sin_tt)


def _ln_swish(c, lg, lb):
    mu = jnp.mean(c, axis=-1, keepdims=True)
    d = c - mu
    var = jnp.mean(d * d, axis=-1, keepdims=True)
    y = d * lax.rsqrt(var + EPS) * lg + lb
    return y * _sigmoid(y)


CONV_PAD = 32
CONV_CHUNK = 64


def _conv_prompt_kernel(u_ref, past_ref, w_ref, b_ref, lg_ref, lb_ref, o_ref, uf_ref, *, t_len):
    off = CONV_PAD - (CONV_K - 1)
    uf_ref[0:SUBLANES, :] = jnp.zeros((SUBLANES, D_CONV), F32)
    uf_ref[off:CONV_PAD, :] = past_ref[0]
    uf_ref[CONV_PAD:CONV_PAD + t_len, :] = u_ref[0]
    bias = b_ref[...]
    lg = lg_ref[...]
    lb = lb_ref[...]
    ct = CONV_CHUNK

    def body(i, carry):
        base = pl.multiple_of(i * ct, ct)
        xw = uf_ref[pl.ds(base, ct + CONV_PAD), :]
        acc = jnp.zeros((ct, D_CONV), F32)
        for r in range(SUBLANES):
            yr = xw if r == 0 else xw[r:r + ct + CONV_PAD - SUBLANES, :]
            for a in range((CONV_PAD // SUBLANES) + 1):
                k = SUBLANES * a + r - off
                if 0 <= k < CONV_K and SUBLANES * a + ct <= yr.shape[0]:
                    acc = acc + w_ref[k:k + 1, :] * yr[SUBLANES * a:SUBLANES * a + ct, :]
        o_ref[0, pl.ds(base, ct), :] = _ln_swish(acc + bias, lg, lb).astype(BF16)
        return carry

    lax.fori_loop(0, t_len // ct, body, 0)


def _conv_prompt(u3, past3, dw_w, dw_b, ln_g, ln_b):
    b, t_len, _ = u3.shape
    const2 = lambda i: (0, 0)
    return pl.pallas_call(
        functools.partial(_conv_prompt_kernel, t_len=t_len),
        grid=(b,),
        in_specs=[
            pl.BlockSpec((1, t_len, D_CONV), lambda i: (i, 0, 0)),
            pl.BlockSpec((1, CONV_K - 1, D_CONV), lambda i: (i, 0, 0)),
            pl.BlockSpec((CONV_K, D_CONV), const2),
            pl.BlockSpec((1, D_CONV), const2),
            pl.BlockSpec((1, D_CONV), const2),
            pl.BlockSpec((1, D_CONV), const2),
        ],
        out_specs=pl.BlockSpec((1, t_len, D_CONV), lambda i: (i, 0, 0)),
        out_shape=jax.ShapeDtypeStruct((b, t_len, D_CONV), BF16),
        scratch_shapes=[pltpu.VMEM((t_len + CONV_PAD, D_CONV), F32)],
        compiler_params=_params(1),
    )(u3, past3, dw_w, dw_b, ln_g, ln_b)


def _conv_sample_kernel(past_ref, u_ref, w_ref, b_ref, lg_ref, lb_ref, o_ref, *, tq):
    bias = b_ref[...]
    lg = lg_ref[...]
    lb = lb_ref[...]
    n_past = CONV_K - 1
    for t in range(tq):
        acc = jnp.zeros(o_ref.shape[1:], F32)
        for j in range(t, n_past):
            acc = acc + w_ref[j - t:j - t + 1, :] * past_ref[j]
        for i in range(t + 1):
            k = n_past - t + i
            acc = acc + w_ref[k:k + 1, :] * u_ref[i]
        o_ref[t] = _ln_swish(acc + bias, lg, lb).astype(BF16)


def _conv_sample(past_t, u_t, dw_w, dw_b, ln_g, ln_b):
    n_past, bd, _ = past_t.shape
    tq = u_t.shape[0]
    bt = _pick_tile(bd, 32)
    const2 = lambda i: (0, 0)
    return pl.pallas_call(
        functools.partial(_conv_sample_kernel, tq=tq),
        grid=(bd // bt,),
        in_specs=[
            pl.BlockSpec((n_past, bt, D_CONV), lambda i: (0, i, 0)),
            pl.BlockSpec((tq, bt, D_CONV), lambda i: (0, i, 0)),
            pl.BlockSpec((CONV_K, D_CONV), const2),
            pl.BlockSpec((1, D_CONV), const2),
            pl.BlockSpec((1, D_CONV), const2),
            pl.BlockSpec((1, D_CONV), const2),
        ],
        out_specs=pl.BlockSpec((tq, bt, D_CONV), lambda i: (0, i, 0)),
        out_shape=jax.ShapeDtypeStruct((tq, bd, D_CONV), BF16),
        compiler_params=_params(1),
    )(past_t, u_t, dw_w, dw_b, ln_g, ln_b)


def _chunkproj_kernel(x_ref, wk_ref, wv_ref, pk_ref, pv_ref, ab_ref, t_ref, *, n_pages, pages_on_lanes):
    n_rows = n_pages * CHUNKS_PER_PAGE
    for kv, (w_ref, p_ref) in enumerate(((wk_ref, pk_ref), (wv_ref, pv_ref))):
        for p in range(n_pages):
            page = x_ref[0, kv, :, p * PAGE_SIZE:(p + 1) * PAGE_SIZE] if pages_on_lanes else x_ref[p, kv]
            t_ref[p * PAGE_SIZE:(p + 1) * PAGE_SIZE, :] = page.T
        xs = jnp.concatenate([t_ref[pl.ds(l, n_rows, stride=CMP_STRIDE), :] for l in range(CMP_STRIDE)],
                             axis=1).astype(BF16)
        w = w_ref[...]
        part = _dot(xs, w)
        posb = _dot(p_ref[...], w)
        c0 = kv * 2 * LANES
        ab_ref[:, c0:c0 + LANES] = part[:, 0:LANES] + posb[0:1, 0:LANES]
        ab_ref[:, c0 + LANES:c0 + 2 * LANES] = part[:, LANES:2 * LANES] + posb[1:2, LANES:2 * LANES]


def _chunkproj(x4, wk2, wv2, pk2, pv2, page_off, n_total, pages_on_lanes):
    if pages_on_lanes:
        n_pages = x4.shape[3] // PAGE_SIZE
        n_steps = x4.shape[0]
        x_spec = pl.BlockSpec((1, 2, LANES, x4.shape[3]), lambda i: (i, 0, 0, 0))
    else:
        n_pages = _pick_tile(n_total, 64, 1)
        n_steps = n_total // n_pages
        off = page_off // n_pages
        x_spec = pl.BlockSpec((n_pages, 2, LANES, PAGE_SIZE), lambda i: (i + off, 0, 0, 0))
    const2 = lambda i: (0, 0)
    rows = n_pages * CHUNKS_PER_PAGE
    return pl.pallas_call(
        functools.partial(_chunkproj_kernel, n_pages=n_pages, pages_on_lanes=pages_on_lanes),
        grid=(n_steps,),
        in_specs=[
            x_spec,
            pl.BlockSpec(wk2.shape, const2),
            pl.BlockSpec(wv2.shape, const2),
            pl.BlockSpec(pk2.shape, const2),
            pl.BlockSpec(pv2.shape, const2),
        ],
        out_specs=pl.BlockSpec((rows, 4 * LANES), lambda i: (i, 0)),
        out_shape=jax.ShapeDtypeStruct((n_steps * rows, 4 * LANES), F32),
        scratch_shapes=[pltpu.VMEM((n_pages * PAGE_SIZE, LANES), F32)],
        compiler_params=_params(1),
    )(x4, wk2, wv2, pk2, pv2)


def _compressed_kv(ab, cosc, sinc):
    n = ab.shape[0]
    kc = ab[:, 0:LANES] + pltpu.roll(ab[:, LANES:2 * LANES], n - 1, 0)
    vc = ab[:, 2 * LANES:3 * LANES] + pltpu.roll(ab[:, 3 * LANES:4 * LANES], n - 1, 0)
    return _rope_slab(kc, cosc, sinc).astype(BF16), vc.astype(BF16)


def _softmax_parts(s, mask):
    sm = jnp.where(mask, s, NEG)
    m = jnp.max(sm, axis=-1, keepdims=True)
    p = jnp.where(mask, jnp.exp(sm - m), 0.0)
    l = jnp.maximum(jnp.sum(p, axis=-1, keepdims=True), 1e-30)
    return p, l


def _select_blocks(imp, qpos, n_sel, axis):
    jidx = lax.broadcasted_iota(jnp.int32, imp.shape, axis)
    cur = lax.shift_right_logical(qpos, 6)
    forced = (jidx == 0) | (jidx == cur) | (jidx == cur - 1)
    valid = (jidx * SEL_BLOCK <= qpos) & (jidx < n_sel)
    v = jnp.where(valid, jnp.where(forced, jnp.inf, imp), -jnp.inf)
    rank = jnp.zeros(imp.shape, jnp.int32)
    for k in range(n_sel):
        vk = v[k:k + 1, :] if axis == 0 else v[:, k:k + 1]
        ahead = (vk > v) | ((vk == v) & (jidx > k))
        rank = rank + ahead.astype(jnp.int32)
    return (rank < min(N_SEL, n_sel)) & valid


def _overlap(n_cmp, cmp_axis):
    i = lax.broadcasted_iota(jnp.int32, (LANES, LANES), cmp_axis)
    j = lax.broadcasted_iota(jnp.int32, (LANES, LANES), 1 - cmp_axis)
    hit = (i * CMP_STRIDE < (j + 1) * SEL_BLOCK) & (i * CMP_STRIDE + CMP_BLOCK > j * SEL_BLOCK) & (i < n_cmp)
    return hit.astype(F32)


Q_TILE = 256
K_TILE = 256


def _attn_prompt_kernel(q_ref, gn_ref, skv_ref, wkv_ref, ab_ref, cosc_ref, sinc_ref, o_ref, kc_ref, vc_ref,
                        *, t_len):
    qb = pl.program_id(1)
    n_chunk = t_len // CMP_STRIDE
    n_cmp = (t_len - CMP_BLOCK) // CMP_STRIDE + 1
    n_sel = -(-t_len // SEL_BLOCK)
    sel_rows = -(-n_sel // SUBLANES) * SUBLANES
    qt, kt_sz = Q_TILE, K_TILE

    @pl.when(qb == 0)
    def _():
        kc, vc = _compressed_kv(ab_ref[...], cosc_ref[...], sinc_ref[...])
        kc_ref[...] = kc
        vc_ref[...] = vc

    q0 = qb * qt
    qpos = q0 + lax.broadcasted_iota(jnp.int32, (qt, 1), 0)
    qpos_row = q0 + lax.broadcasted_iota(jnp.int32, (1, qt), 1)
    gn = gn_ref[...]

    for g in range(N_KV):
        qz = jnp.concatenate([q_ref[:, (g * HPG + h) * LANES:(g * HPG + h + 1) * LANES] for h in range(HPG)], axis=0)

        ncol = lax.broadcasted_iota(jnp.int32, (qt, n_chunk), 1)
        mask_c = (ncol * CMP_STRIDE + CMP_BLOCK - 1 <= qpos) & (ncol < n_cmp)
        s_c = _dot_nt(qz, kc_ref[...]).reshape(HPG, qt, n_chunk)
        p_c, l_c = _softmax_parts(s_c, mask_c[None])
        p_c = p_c / l_c
        o_c = _dot(p_c.reshape(HPG * qt, n_chunk).astype(BF16), vc_ref[...])
        hs = p_c[0]
        for h in range(1, HPG):
            hs = hs + p_c[h]
        imp_t = lax.dot_general(_overlap(n_cmp, 1), hs, (((1,), (1,)), ((), ())),
                                preferred_element_type=F32, precision=lax.Precision.HIGHEST)
        sel_t = _select_blocks(imp_t[0:sel_rows], qpos_row, n_sel, 0).astype(F32)
        sel_t = jnp.concatenate([sel_t, jnp.zeros((LANES - sel_rows, qt), F32)], axis=0)
        sel = sel_t.T.astype(BF16)

        def tile_step(kv_ref, k_idx, carry, mask_fn):
            m, l, acc = carry
            k0 = pl.multiple_of(k_idx * kt_sz, kt_sz)
            kslab = kv_ref[0, 0:LANES, pl.ds(k0, kt_sz)].astype(BF16)
            vslab = kv_ref[0, LANES:KV_COLS, pl.ds(k0, kt_sz)].astype(BF16)
            s = _dot(qz, kslab).reshape(HPG, qt, kt_sz)
            kpos = k0 + lax.broadcasted_iota(jnp.int32, (qt, kt_sz), 1)
            mask = mask_fn(k_idx, kpos)[None]
            sm = jnp.where(mask, s, NEG)
            m_new = jnp.maximum(m, jnp.max(sm, axis=-1, keepdims=True))
            alpha = jnp.exp(m - m_new)
            p = jnp.where(mask, jnp.exp(sm - m_new), 0.0)
            l = alpha * l + jnp.sum(p, axis=-1, keepdims=True)
            pv = _dot_nt(p.reshape(HPG * qt, kt_sz).astype(BF16), vslab)
            acc = alpha.reshape(HPG * qt, 1) * acc + pv
            return m_new, l, acc

        def sel_mask(k_idx, kpos):
            jrow = lax.broadcasted_iota(jnp.int32, (LANES, kt_sz), 0)
            kcol = lax.broadcasted_iota(jnp.int32, (LANES, kt_sz), 1)
            expand = (jrow == k_idx * (kt_sz // SEL_BLOCK) + lax.shift_right_logical(kcol, 6)).astype(BF16)
            return (_dot(sel, expand) > 0.5) & (kpos <= qpos)

        def win_mask(k_idx, kpos):
            return (kpos <= qpos) & (kpos > qpos - WINDOW)

        init = (jnp.full((HPG, qt, 1), NEG, F32), jnp.zeros((HPG, qt, 1), F32), jnp.zeros((HPG * qt, LANES), F32))
        _, l_s, acc_s = lax.fori_loop(0, qb + 1, lambda i, c: tile_step(skv_ref, i, c, sel_mask), init)
        lo = jnp.maximum(qb - WINDOW // kt_sz, 0)
        _, l_w, acc_w = lax.fori_loop(lo, qb + 1, lambda i, c: tile_step(wkv_ref, i, c, win_mask), init)
        o_s = acc_s / jnp.maximum(l_s, 1e-30).reshape(HPG * qt, 1)
        o_w = acc_w / jnp.maximum(l_w, 1e-30).reshape(HPG * qt, 1)

        for h in range(HPG):
            hh = g * HPG + h
            rows = slice(h * qt, (h + 1) * qt)
            out = (gn[:, hh:hh + 1] * o_c[rows]
                   + gn[:, N_HEADS + hh:N_HEADS + hh + 1] * o_s[rows]
                   + gn[:, 2 * N_HEADS + hh:2 * N_HEADS + hh + 1] * o_w[rows])
            o_ref[:, hh * LANES:(hh + 1) * LANES] = out.astype(BF16)


def _attn_prompt(q2, gn2, skv_t, wkv_t, ab2, cosc, sinc, b, t_len):
    n_chunk = t_len // CMP_STRIDE
    nqb = t_len // Q_TILE
    row = lambda i, j: (i * nqb + j, 0)
    per_b3 = lambda i, j: (i, 0, 0)
    const2 = lambda i, j: (0, 0)
    return pl.pallas_call(
        functools.partial(_attn_prompt_kernel, t_len=t_len),
        grid=(b, nqb),
        in_specs=[
            pl.BlockSpec((Q_TILE, Q_EXP), row),
            pl.BlockSpec((Q_TILE, LANES), row),
            pl.BlockSpec((1, KV_COLS, t_len), per_b3),
            pl.BlockSpec((1, KV_COLS, t_len), per_b3),
            pl.BlockSpec((n_chunk, 4 * LANES), lambda i, j: (i, 0)),
            pl.BlockSpec((n_chunk, LANES), const2),
            pl.BlockSpec((n_chunk, LANES), const2),
        ],
        out_specs=pl.BlockSpec((Q_TILE, Q_EXP), row),
        out_shape=jax.ShapeDtypeStruct((b * t_len, Q_EXP), BF16),
        scratch_shapes=[pltpu.VMEM((n_chunk, LANES), BF16), pltpu.VMEM((n_chunk, LANES), BF16)],
        compiler_params=_params(2),
    )(q2, gn2, skv_t, wkv_t, ab2, cosc, sinc)


Q_ROWS = 8


def _attn_sample_kernel(pt_ref, q_ref, gn_ref, sknew_ref, wknew_ref, win_ref, cosc_ref, sinc_ref, *rest,
                        past_len, tq, n_pages):
    del pt_ref
    sel_pages = rest[:n_pages]
    ab_pages = rest[n_pages:2 * n_pages]
    o_ref = rest[2 * n_pages]
    t_all = past_len + tq
    n_chunk = past_len // CMP_STRIDE
    n_cmp = (t_all - CMP_BLOCK) // CMP_STRIDE + 1
    n_sel = -(-t_all // SEL_BLOCK)
    rows = N_HEADS * Q_ROWS
    win_buf = win_ref.shape[3]

    qbd = jnp.concatenate([q_ref[0, :, hh * LANES:(hh + 1) * LANES] for hh in range(N_HEADS)], axis=0)
    qpos = past_len + lax.broadcasted_iota(jnp.int32, (rows, 1), 0) % Q_ROWS

    ab = jnp.concatenate([r[0] for r in ab_pages], axis=0)
    kc, vc = _compressed_kv(ab, cosc_ref[...], sinc_ref[...])
    ncol = lax.broadcasted_iota(jnp.int32, (rows, n_chunk), 1)
    mask_c = (ncol * CMP_STRIDE + CMP_BLOCK - 1 <= qpos) & (ncol < n_cmp)
    p_c, l_c = _softmax_parts(_dot_nt(qbd, kc), mask_c)
    p_c = p_c / l_c
    o_c = _dot(p_c.astype(BF16), vc)

    hs = []
    for g in range(N_KV):
        acc = p_c[g * HPG * Q_ROWS:(g * HPG + 1) * Q_ROWS]
        for h in range(1, HPG):
            acc = acc + p_c[(g * HPG + h) * Q_ROWS:(g * HPG + h + 1) * Q_ROWS]
        hs.append(acc)
    hs = jnp.concatenate(hs, axis=0)
    imp = jnp.dot(hs, _overlap(n_cmp, 0), preferred_element_type=F32, precision=lax.Precision.HIGHEST)
    sel_g = _select_blocks(imp, qpos[0:N_KV * Q_ROWS], n_sel, 1).astype(BF16)
    sel_rows = jnp.concatenate([sel_g[g * Q_ROWS:(g + 1) * Q_ROWS] for g in range(N_KV) for _ in range(HPG)], axis=0)

    def new_rows_tile(ref, c0):
        return jnp.concatenate([ref[0, :, c0:c0 + LANES], jnp.zeros((LANES - Q_ROWS, LANES), F32)], axis=0).astype(BF16)

    n_keys = (n_pages + 1) * PAGE_SIZE
    s_s = jnp.concatenate([_dot(qbd, r[0, 0].astype(BF16)) for r in sel_pages]
                          + [_dot_nt(qbd, new_rows_tile(sknew_ref, 0))], axis=1)
    jrow = lax.broadcasted_iota(jnp.int32, (LANES, n_keys), 0)
    kcol = lax.broadcasted_iota(jnp.int32, (LANES, n_keys), 1)
    expand = (jrow == lax.shift_right_logical(kcol, 6)).astype(BF16)
    kpos = lax.broadcasted_iota(jnp.int32, (rows, n_keys), 1)
    mask_s = (_dot(sel_rows, expand) > 0.5) & (kpos <= qpos)
    p_s, l_s = _softmax_parts(s_s, mask_s)
    p_s = p_s.astype(BF16)
    o_s = _dot(p_s[:, n_pages * PAGE_SIZE:], new_rows_tile(sknew_ref, LANES))
    for i in range(n_pages):
        o_s = o_s + _dot_nt(p_s[:, i * PAGE_SIZE:(i + 1) * PAGE_SIZE], sel_pages[i][0, 1].astype(BF16))
    o_s = o_s / l_s

    s_w = jnp.concatenate([_dot(qbd, win_ref[0, 0].astype(BF16)), _dot_nt(qbd, new_rows_tile(wknew_ref, 0))], axis=1)
    wcol = lax.broadcasted_iota(jnp.int32, (rows, win_buf + LANES), 1)
    kpos_w = past_len - win_buf + wcol
    mask_w = (kpos_w <= qpos) & (kpos_w > qpos - WINDOW)
    p_w, l_w = _softmax_parts(s_w, mask_w)
    p_w = p_w.astype(BF16)
    o_w = (_dot_nt(p_w[:, 0:win_buf], win_ref[0, 1].astype(BF16))
           + _dot(p_w[:, win_buf:], new_rows_tile(wknew_ref, LANES))) / l_w

    gn = gn_ref[0]
    for hh in range(N_HEADS):
        r = slice(hh * Q_ROWS, (hh + 1) * Q_ROWS)
        out = (gn[:, hh:hh + 1] * o_c[r]
               + gn[:, N_HEADS + hh:N_HEADS + hh + 1] * o_s[r]
               + gn[:, 2 * N_HEADS + hh:2 * N_HEADS + hh + 1] * o_w[r])
        o_ref[0, :, hh * LANES:(hh + 1) * LANES] = out.astype(BF16)


def _attn_sample(pt_flat, q3, gn3, sknew3, wknew3, win4, cosc, sinc, sel_pool4, ab_pool3,
                 layer, n_pool, bd, past_len, tq):
    n_pages = past_len // PAGE_SIZE
    win_buf = win4.shape[3]
    n_chunk = past_len // CMP_STRIDE
    per_b = lambda i, pt: (i, 0, 0)
    const2 = lambda i, pt: (0, 0)
    sel_specs = [pl.BlockSpec((1, 2, LANES, PAGE_SIZE),
                              lambda i, pt, p=p: (layer * n_pool + pt[i * n_pages + p], 0, 0, 0))
                 for p in range(n_pages)]
    ab_specs = [pl.BlockSpec((1, CHUNKS_PER_PAGE, 4 * LANES),
                             lambda i, pt, p=p: (pt[i * n_pages + p], 0, 0)) for p in range(n_pages)]
    grid_spec = pltpu.PrefetchScalarGridSpec(
        num_scalar_prefetch=1,
        grid=(bd,),
        in_specs=[
            pl.BlockSpec((1, Q_ROWS, Q_EXP), per_b),
            pl.BlockSpec((1, Q_ROWS, LANES), per_b),
            pl.BlockSpec((1, Q_ROWS, KV_COLS), per_b),
            pl.BlockSpec((1, Q_ROWS, KV_COLS), per_b),
            pl.BlockSpec((1, 2, LANES, win_buf), lambda i, pt: (layer * bd + i, 0, 0, 0)),
            pl.BlockSpec((n_chunk, LANES), const2),
            pl.BlockSpec((n_chunk, LANES), const2),
        ] + sel_specs + ab_specs,
        out_specs=pl.BlockSpec((1, Q_ROWS, Q_EXP), per_b),
    )
    return pl.pallas_call(
        functools.partial(_attn_sample_kernel, past_len=past_len, tq=tq, n_pages=n_pages),
        grid_spec=grid_spec,
        out_shape=jax.ShapeDtypeStruct((bd, Q_ROWS, Q_EXP), BF16),
        compiler_params=_params(1),
    )(pt_flat, q3, gn3, sknew3, wknew3, win4, cosc, sinc, *([sel_pool4] * n_pages), *([ab_pool3] * n_pages))


def _merge_kernel(c_ref, o_ref, gm_ref, x_ref, wc_ref, wn_ref, wo_ref, y_ref):
    conv_out = _dot(c_ref[...], wc_ref[...])
    nsa_out = _dot(o_ref[...], wn_ref[...])
    gm = gm_ref[...].astype(F32)
    merged = gm[:, 0:D_MODEL] * conv_out + gm[:, D_MODEL:2 * D_MODEL] * nsa_out
    y_ref[...] = x_ref[...] + _dot(merged.astype(BF16), wo_ref[...])


def _merge(c2, o2, gm2, x2, wc, wn, wo, tm):
    m = x2.shape[0]
    row = lambda i: (i, 0)
    const = lambda i: (0, 0)
    return pl.pallas_call(
        _merge_kernel,
        grid=(m // tm,),
        in_specs=[
            pl.BlockSpec((tm, D_CONV), row),
            pl.BlockSpec((tm, Q_EXP), row),
            pl.BlockSpec((tm, 2 * D_MODEL), row),
            pl.BlockSpec((tm, D_MODEL), row),
            pl.BlockSpec(wc.shape, const),
            pl.BlockSpec(wn.shape, const),
            pl.BlockSpec(wo.shape, const),
        ],
        out_specs=pl.BlockSpec((tm, D_MODEL), row),
        out_shape=jax.ShapeDtypeStruct((m, D_MODEL), F32),
        compiler_params=_params(1),
    )(c2, o2, gm2, x2, wc, wn, wo)


FF_TILE = 1024


def _mlp_kernel(x_ref, g_ref, wu_ref, wd_ref, gf_ref, y_ref, *, final):
    x = x_ref[...]
    h = _rms_bf16(x, g_ref[...])
    acc = x
    for f in range(D_FF // FF_TILE):
        up = jnp.maximum(_dot(h, wu_ref[:, f * FF_TILE:(f + 1) * FF_TILE]), 0.0)
        acc = acc + _dot((up * up).astype(BF16), wd_ref[f * FF_TILE:(f + 1) * FF_TILE, :])
    if final:
        ms2 = jnp.mean(acc * acc, axis=-1, keepdims=True)
        acc = acc * lax.rsqrt(ms2 + EPS) * gf_ref[...]
    y_ref[...] = acc


def _mlp(x2, g, wu, wd, gf, tm, final):
    m = x2.shape[0]
    row = lambda i: (i, 0)
    const = lambda i: (0, 0)
    return pl.pallas_call(
        functools.partial(_mlp_kernel, final=final),
        grid=(m // tm,),
        in_specs=[
            pl.BlockSpec((tm, D_MODEL), row),
            pl.BlockSpec((1, D_MODEL), const),
            pl.BlockSpec(wu.shape, const),
            pl.BlockSpec(wd.shape, const),
            pl.BlockSpec((1, D_MODEL), const),
        ],
        out_specs=pl.BlockSpec((tm, D_MODEL), row),
        out_shape=jax.ShapeDtypeStruct((m, D_MODEL), F32),
        compiler_params=_params(1),
    )(x2, g, wu, wd, gf)


def _rope_tables(pos):
    half = HEAD_DIM // 2
    inv = jnp.power(ROPE_THETA, -jnp.arange(half, dtype=F32) / half)
    ang = pos.astype(F32)[:, None] * inv[None, :]
    cos = jnp.cos(ang)
    sin = jnp.sin(ang)
    return jnp.concatenate([cos, cos, cos, cos], axis=1), jnp.concatenate([-sin, sin, -sin, sin], axis=1)


def _pack_w_in(w):
    o0 = 2 * D_CONV
    o1 = o0 + N_HEADS * HEAD_DIM
    o4 = o1 + 3 * KV_COLS
    o5 = o4 + N_GATES
    wq = w[:, o0:o1].reshape(D_MODEL, N_HEADS, HEAD_DIM) * (HEAD_DIM ** -0.5)
    zero = jnp.zeros_like(wq)
    in_g0 = (jnp.arange(N_HEADS) < HPG)[None, :, None]
    wq_exp = jnp.concatenate([jnp.where(in_g0, wq, zero), jnp.where(in_g0, zero, wq)], axis=2)
    wgn = jnp.pad(w[:, o4:o5], ((0, 0), (0, LANES - N_GATES)))
    packed = jnp.concatenate([w[:, :o0], wq_exp.reshape(D_MODEL, Q_EXP), w[:, o1:o4], wgn, w[:, o5:]], axis=1)
    return packed.astype(BF16), w[:, o1:o4].T.astype(BF16)


def _pack_w_nsa_out(w):
    wh = w.reshape(N_HEADS, HEAD_DIM, D_MODEL)
    zero = jnp.zeros_like(wh)
    in_g0 = (jnp.arange(N_HEADS) < HPG)[:, None, None]
    return jnp.concatenate([jnp.where(in_g0, wh, zero), jnp.where(in_g0, zero, wh)], axis=1).reshape(Q_EXP, D_MODEL).astype(BF16)


def _pack_w_cmp(w_kv, pos_kv):
    lo, hi = w_kv[:CMP_STRIDE], w_kv[CMP_STRIDE:]
    eye = jnp.eye(N_KV, dtype=w_kv.dtype)
    blk = lambda part: jnp.einsum('lde,gh->lgdhe', part, eye).reshape(CMP_STRIDE * N_KV * HEAD_DIM, N_KV * HEAD_DIM)
    w2 = jnp.concatenate([blk(lo), blk(hi)], axis=1).astype(BF16)
    tile = lambda p: jnp.broadcast_to(p[:, None, :], (CMP_STRIDE, N_KV, HEAD_DIM)).reshape(1, -1)
    p2 = jnp.concatenate([tile(pos_kv[:CMP_STRIDE]), tile(pos_kv[CMP_STRIDE:]),
                          jnp.zeros((SUBLANES - 2, CMP_STRIDE * N_KV * HEAD_DIM), pos_kv.dtype)], axis=0).astype(BF16)
    return w2, p2


def _feature_major(a):
    lead = a.shape[:-4]
    n = len(lead)
    a = jnp.transpose(a, tuple(range(n)) + (n + 1, n + 2, n + 3, n))
    return a.reshape(lead + (2, N_KV * HEAD_DIM, a.shape[-1]))


def _position_major(a_t, rows):
    b = a_t.shape[0]
    return jnp.transpose(a_t.reshape(b, 2, N_KV, HEAD_DIM, rows), (0, 4, 1, 2, 3))


def kernel(x_prompt, x_sample, cache_cmp_kv, cache_sel_kv, state_win_kv, state_conv, page_table, norm_mix_g, w_in, conv_dw_w, conv_dw_b, conv_ln_g, conv_ln_b, w_conv_out, cmp_pos, w_cmp, w_nsa_out, w_out, norm_mlp_g, w_up, w_down, norm_final_g):
    depth = w_in.shape[0]
    bp, tp, _ = x_prompt.shape
    bd, tq, _ = x_sample.shape
    n_pool = cache_cmp_kv.shape[1]
    n_pages = page_table.shape[1]
    past_len = n_pages * PAGE_SIZE
    win_buf = state_win_kv.shape[2]
    assert tp % Q_TILE == 0 and tp // CMP_STRIDE == LANES and tp >= WINDOW
    assert past_len // CMP_STRIDE == LANES and win_buf == WINDOW and tq <= Q_ROWS

    mp, ms = bp * tp, bd * tq
    tm_p = _pick_tile(mp, 256)
    tm_s = _pick_tile(ms, 256)
    assert tp % tm_p == 0 and tm_s % tq == 0

    cos_p, sin_p = _rope_tables(jnp.arange(tp))
    cos_pt, sin_pt = cos_p.T, sin_p.T
    cos_s, sin_s = _rope_tables(past_len + jnp.arange(tm_s) % tq)
    n_chunk = tp // CMP_STRIDE
    cos_c, sin_c = _rope_tables(jnp.arange(n_chunk) * CMP_STRIDE + CMP_BLOCK - 1)

    pt_flat = page_table.reshape(-1).astype(jnp.int32)
    cmp_pool4 = _feature_major(cache_cmp_kv).reshape(depth * n_pool, 2, LANES, PAGE_SIZE)
    sel_pool4 = _feature_major(cache_sel_kv).reshape(depth * n_pool, 2, LANES, PAGE_SIZE)
    win4 = _feature_major(state_win_kv).reshape(depth * bd, 2, LANES, win_buf)
    zeros_conv = jnp.zeros((bp, CONV_K - 1, D_CONV), F32)

    xp = x_prompt.reshape(mp, D_MODEL)
    xs = x_sample.reshape(ms, D_MODEL)
    outs = [[] for _ in range(8)]
    for l in range(depth):
        w_packed, wkv_t = _pack_w_in(w_in[l])
        wn = _pack_w_nsa_out(w_nsa_out[l])
        wc, wo = w_conv_out[l].astype(BF16), w_out[l].astype(BF16)
        wu, wd = w_up[l].astype(BF16), w_down[l].astype(BF16)
        wk2, pk2 = _pack_w_cmp(w_cmp[l, 0], cmp_pos[l, 0])
        wv2, pv2 = _pack_w_cmp(w_cmp[l, 1], cmp_pos[l, 1])
        g_mix, g_mlp = norm_mix_g[l][None], norm_mlp_g[l][None]
        dw_b, ln_g, ln_b = conv_dw_b[l][None], conv_ln_g[l][None], conv_ln_b[l][None]
        gf = norm_final_g[None]
        final = l == depth - 1

        u, q, ckv_t, skv_t, wkv_tm, gn, gm = _inproj_cols(xp, g_mix, w_packed, wkv_t, cos_p, sin_p, cos_pt, sin_pt,
                                                          bp, tp, tm_p)
        c_act = _conv_prompt(u.reshape(bp, tp, D_CONV), zeros_conv, conv_dw_w[l], dw_b, ln_g, ln_b)
        ab = _chunkproj(ckv_t.reshape(bp, 2, LANES, tp), wk2, wv2, pk2, pv2, 0, 0, True)
        o = _attn_prompt(q, gn, skv_t, wkv_tm, ab, cos_c, sin_c, bp, tp)
        xp = _merge(c_act.reshape(mp, D_CONV), o, gm, xp, wc, wn, wo, tm_p)
        xp = _mlp(xp, g_mlp, wu, wd, gf, _pick_tile(mp, 512), final)
        keep = min(WINDOW, tp)
        outs[0].append(_position_major(ckv_t, tp))
        outs[2].append(_position_major(skv_t, tp))
        outs[4].append(_position_major(wkv_tm[:, :, tp - keep:], keep))
        outs[6].append(u.reshape(bp, tp, D_CONV)[:, tp - (CONV_K - 1):])

        u, q, ckv, skv, wkv, gn, gm = _inproj_rows(xs, g_mix, w_packed, cos_s, sin_s, tm_s)
        u3 = u.reshape(bd, tq, D_CONV)
        c_t = _conv_sample(jnp.swapaxes(state_conv[l], 0, 1), jnp.swapaxes(u3, 0, 1), conv_dw_w[l], dw_b, ln_g, ln_b)
        c_act = jnp.swapaxes(c_t, 0, 1).reshape(ms, D_CONV)
        ab_pool = _chunkproj(cmp_pool4, wk2, wv2, pk2, pv2, l * n_pool, n_pool, False)
        pad_q = lambda a: jnp.pad(a.reshape(bd, tq, -1), ((0, 0), (0, Q_ROWS - tq), (0, 0)))
        o = _attn_sample(pt_flat, pad_q(q), pad_q(gn), pad_q(skv), pad_q(wkv), win4, cos_c, sin_c, sel_pool4,
                         ab_pool.reshape(n_pool, CHUNKS_PER_PAGE, 4 * LANES), l, n_pool, bd, past_len, tq)
        xs = _merge(c_act, o[:, :tq].reshape(ms, Q_EXP), gm, xs, wc, wn, wo, tm_s)
        xs = _mlp(xs, g_mlp, wu, wd, gf, _pick_tile(ms, 512), final)
        wkv5 = wkv.reshape(bd, tq, 2, N_KV, HEAD_DIM)
        outs[1].append(ckv.reshape(bd, tq, 2, N_KV, HEAD_DIM))
        outs[3].append(skv.reshape(bd, tq, 2, N_KV, HEAD_DIM))
        outs[5].append(jnp.concatenate([state_win_kv[l], wkv5], axis=1)[:, -win_buf:])
        outs[7].append(jnp.concatenate([state_conv[l], u3], axis=1)[:, -(CONV_K - 1):])

    y_prompt = xp.reshape(bp, tp, D_MODEL)
    y_sample = xs.reshape(bd, tq, D_MODEL)
    return (y_prompt, y_sample) + tuple(jnp.stack(o) for o in outs)
```

```python
import functools

import jax
import jax.numpy as jnp
from jax import lax
from jax.experimental import pallas as pl
from jax.experimental.pallas import tpu as pltpu

D_MODEL = 1024
D_CONV = D_MODEL // 2
CONV_K = 31
N_HEADS = 8
HEAD_DIM = 64
N_KV = 2
HPG = N_HEADS // N_KV
KV_COLS = 2 * N_KV * HEAD_DIM
CMP_BLOCK = 32
CMP_STRIDE = 16
SEL_BLOCK = 64
N_SEL = 16
WINDOW = 512
D_FF = 4 * D_MODEL
ROPE_THETA = 10000.0
EPS = 1e-6
PAGE_SIZE = 128

LANES = 128
SUBLANES = 8
Q_EXP = N_HEADS * LANES
N_GATES = 3 * N_HEADS
CHUNKS_PER_PAGE = PAGE_SIZE // CMP_STRIDE
NEG = -1e30

O_GLU = 0
O_Q = O_GLU + 2 * D_CONV
O_CKV = O_Q + Q_EXP
O_SKV = O_CKV + KV_COLS
O_WKV = O_SKV + KV_COLS
O_GN = O_WKV + KV_COLS
O_GM = O_GN + LANES
IN_COLS_PACKED = O_GM + 2 * D_MODEL

VMEM_LIMIT = 56 * 1024 * 1024

F32 = jnp.float32
BF16 = jnp.bfloat16


def _params(n_axes, vmem=VMEM_LIMIT):
    return pltpu.CompilerParams(dimension_semantics=("arbitrary",) * n_axes, vmem_limit_bytes=vmem)


def _sigmoid(x):
    return 1.0 / (1.0 + jnp.exp(-x))


def _dot(a, b):
    return jnp.dot(a, b, preferred_element_type=F32)


def _dot_nt(a, b):
    return lax.dot_general(a, b, (((1,), (1,)), ((), ())), preferred_element_type=F32)


def _rope_slab(xs, cos, sin_signed):
    lane = lax.broadcasted_iota(jnp.int32, xs.shape, 1)
    first = (lane % HEAD_DIM) < (HEAD_DIM // 2)
    rot = jnp.where(first, pltpu.roll(xs, LANES - HEAD_DIM // 2, 1), pltpu.roll(xs, HEAD_DIM // 2, 1))
    return xs * cos + rot * sin_signed


def _rope_rows(xt, cos_t, sin_t):
    half = HEAD_DIM // 2
    rot = jnp.concatenate([xt[half:2 * half], xt[0:half], xt[3 * half:4 * half], xt[2 * half:3 * half]], axis=0)
    return xt * cos_t + rot * sin_t


def _pick_tile(n, cap, mult=8):
    t = min(n, cap)
    while n % t or t % mult:
        t -= 1
    return t


def _rms_bf16(x, g):
    ms = jnp.mean(x * x, axis=-1, keepdims=True)
    return (x * lax.rsqrt(ms + EPS) * g).astype(BF16)


def _inproj_common(h, w_ref, cos, sin, u_ref, q_ref, gn_ref, gm_ref):
    def proj(a, n):
        return _dot(h, w_ref[:, a:a + n])

    u_ref[...] = proj(O_GLU, D_CONV) * _sigmoid(proj(O_GLU + D_CONV, D_CONV))
    for s in range(N_HEADS):
        q_ref[:, s * LANES:(s + 1) * LANES] = _rope_slab(proj(O_Q + s * LANES, LANES), cos, sin).astype(BF16)
    gn_ref[...] = _sigmoid(proj(O_GN, LANES))
    gm_ref[...] = _sigmoid(proj(O_GM, 2 * D_MODEL)).astype(BF16)
    return proj


def _inproj_rows_kernel(x_ref, g_ref, w_ref, cos_ref, sin_ref,
                        u_ref, q_ref, ckv_ref, skv_ref, wkv_ref, gn_ref, gm_ref):
    h = _rms_bf16(x_ref[...], g_ref[...])
    cos = cos_ref[...]
    sin = sin_ref[...]
    proj = _inproj_common(h, w_ref, cos, sin, u_ref, q_ref, gn_ref, gm_ref)
    ckv_ref[...] = proj(O_CKV, KV_COLS)
    skv_ref[:, 0:LANES] = _rope_slab(proj(O_SKV, LANES), cos, sin)
    skv_ref[:, LANES:KV_COLS] = proj(O_SKV + LANES, LANES)
    wkv_ref[:, 0:LANES] = _rope_slab(proj(O_WKV, LANES), cos, sin)
    wkv_ref[:, LANES:KV_COLS] = proj(O_WKV + LANES, LANES)


def _inproj_cols_kernel(x_ref, g_ref, w_ref, wkvt_ref, cos_ref, sin_ref, cost_ref, sint_ref,
                        u_ref, q_ref, ckv_ref, skv_ref, wkv_ref, gn_ref, gm_ref):
    h = _rms_bf16(x_ref[...], g_ref[...])
    _inproj_common(h, w_ref, cos_ref[...], sin_ref[...], u_ref, q_ref, gn_ref, gm_ref)
    cos_t = cost_ref[...]
    sin_t = sint_ref[...]
    kvt = _dot_nt(wkvt_ref[...], h)
    ckv_ref[0] = kvt[0:KV_COLS]
    skv_ref[0, 0:LANES] = _rope_rows(kvt[KV_COLS:KV_COLS + LANES], cos_t, sin_t)
    skv_ref[0, LANES:KV_COLS] = kvt[KV_COLS + LANES:2 * KV_COLS]
    wkv_ref[0, 0:LANES] = _rope_rows(kvt[2 * KV_COLS:2 * KV_COLS + LANES], cos_t, sin_t)
    wkv_ref[0, LANES:KV_COLS] = kvt[2 * KV_COLS + LANES:3 * KV_COLS]


def _inproj_rows(x2, g, w_packed, cos_t, sin_t, tm):
    m = x2.shape[0]
    row = lambda i: (i, 0)
    const = lambda i: (0, 0)
    out_shapes = (
        jax.ShapeDtypeStruct((m, D_CONV), F32),
        jax.ShapeDtypeStruct((m, Q_EXP), BF16),
        jax.ShapeDtypeStruct((m, KV_COLS), F32),
        jax.ShapeDtypeStruct((m, KV_COLS), F32),
        jax.ShapeDtypeStruct((m, KV_COLS), F32),
        jax.ShapeDtypeStruct((m, LANES), F32),
        jax.ShapeDtypeStruct((m, 2 * D_MODEL), BF16),
    )
    return pl.pallas_call(
        _inproj_rows_kernel,
        grid=(m // tm,),
        in_specs=[
            pl.BlockSpec((tm, D_MODEL), row),
            pl.BlockSpec((1, D_MODEL), const),
            pl.BlockSpec(w_packed.shape, const),
            pl.BlockSpec((tm, LANES), const),
            pl.BlockSpec((tm, LANES), const),
        ],
        out_specs=tuple(pl.BlockSpec((tm, s.shape[1]), row) for s in out_shapes),
        out_shape=out_shapes,
        compiler_params=_params(1),
    )(x2, g, w_packed, cos_t, sin_t)


def _inproj_cols(x2, g, w_packed, wkv_t, cos_t, sin_t, cos_tt, sin_tt, b, t_len, tm):
    m = x2.shape[0]
    n_tab = t_len // tm
    row = lambda i: (i, 0)
    const = lambda i: (0, 0)
    kv_map = lambda i: (i // n_tab, 0, i % n_tab)
    kv_shape = jax.ShapeDtypeStruct((b, KV_COLS, t_len), F32)
    out_shapes = (
        jax.ShapeDtypeStruct((m, D_CONV), F32),
        jax.ShapeDtypeStruct((m, Q_EXP), BF16),
        kv_shape, kv_shape, kv_shape,
        jax.ShapeDtypeStruct((m, LANES), F32),
        jax.ShapeDtypeStruct((m, 2 * D_MODEL), BF16),
    )
    out_specs = (
        pl.BlockSpec((tm, D_CONV), row),
        pl.BlockSpec((tm, Q_EXP), row),
        pl.BlockSpec((1, KV_COLS, tm), kv_map),
        pl.BlockSpec((1, KV_COLS, tm), kv_map),
        pl.BlockSpec((1, KV_COLS, tm), kv_map),
        pl.BlockSpec((tm, LANES), row),
        pl.BlockSpec((tm, 2 * D_MODEL), row),
    )
    return pl.pallas_call(
        _inproj_cols_kernel,
        grid=(m // tm,),
        in_specs=[
            pl.BlockSpec((tm, D_MODEL), row),
            pl.BlockSpec((1, D_MODEL), const),
            pl.BlockSpec(w_packed.shape, const),
            pl.BlockSpec(wkv_t.shape, const),
            pl.BlockSpec((tm, LANES), lambda i: (i % n_tab, 0)),
            pl.BlockSpec((tm, LANES), lambda i: (i % n_tab, 0)),
            pl.BlockSpec((LANES, tm), lambda i: (0, i % n_tab)),
            pl.BlockSpec((LANES, tm), lambda i: (0, i % n_tab)),
        ],
        out_specs=out_specs,
        out_shape=out_shapes,
        compiler_params=_params(1),
    )(x2, g, w_packed, wkv_t, cos_t, sin_t, cos_tt, sin_tt)


def _ln_swish(c, lg, lb):
    mu = jnp.mean(c, axis=-1, keepdims=True)
    d = c - mu
    var = jnp.mean(d * d, axis=-1, keepdims=True)
    y = d * lax.rsqrt(var + EPS) * lg + lb
    return y * _sigmoid(y)


CONV_PAD = 32
CONV_CHUNK = 64


def _conv_prompt_kernel(u_ref, past_ref, w_ref, b_ref, lg_ref, lb_ref, o_ref, uf_ref, *, t_len):
    off = CONV_PAD - (CONV_K - 1)
    uf_ref[0:SUBLANES, :] = jnp.zeros((SUBLANES, D_CONV), F32)
    uf_ref[off:CONV_PAD, :] = past_ref[0]
    uf_ref[CONV_PAD:CONV_PAD + t_len, :] = u_ref[0]
    bias = b_ref[...]
    lg = lg_ref[...]
    lb = lb_ref[...]
    ct = CONV_CHUNK

    def body(i, carry):
        base = pl.multiple_of(i * ct, ct)
        xw = uf_ref[pl.ds(base, ct + CONV_PAD), :]
        acc = jnp.zeros((ct, D_CONV), F32)
        n_win = ct + CONV_PAD
        for r in range(SUBLANES):
            yr = xw if r == 0 else pltpu.roll(xw, n_win - r, 0)
            for a in range((CONV_PAD // SUBLANES) + 1):
                k = SUBLANES * a + r - off
                if 0 <= k < CONV_K:
                    acc = acc + w_ref[k:k + 1, :] * yr[SUBLANES * a:SUBLANES * a + ct, :]
        o_ref[0, pl.ds(base, ct), :] = _ln_swish(acc + bias, lg, lb).astype(BF16)
        return carry

    lax.fori_loop(0, t_len // ct, body, 0)


def _conv_prompt(u3, past3, dw_w, dw_b, ln_g, ln_b):
    b, t_len, _ = u3.shape
    const2 = lambda i: (0, 0)
    return pl.pallas_call(
        functools.partial(_conv_prompt_kernel, t_len=t_len),
        grid=(b,),
        in_specs=[
            pl.BlockSpec((1, t_len, D_CONV), lambda i: (i, 0, 0)),
            pl.BlockSpec((1, CONV_K - 1, D_CONV), lambda i: (i, 0, 0)),
            pl.BlockSpec((CONV_K, D_CONV), const2),
            pl.BlockSpec((1, D_CONV), const2),
            pl.BlockSpec((1, D_CONV), const2),
            pl.BlockSpec((1, D_CONV), const2),
        ],
        out_specs=pl.BlockSpec((1, t_len, D_CONV), lambda i: (i, 0, 0)),
        out_shape=jax.ShapeDtypeStruct((b, t_len, D_CONV), BF16),
        scratch_shapes=[pltpu.VMEM((t_len + CONV_PAD, D_CONV), F32)],
        compiler_params=_params(1),
    )(u3, past3, dw_w, dw_b, ln_g, ln_b)


def _conv_sample_kernel(past_ref, u_ref, w_ref, b_ref, lg_ref, lb_ref, o_ref, *, tq):
    bias = b_ref[...]
    lg = lg_ref[...]
    lb = lb_ref[...]
    n_past = CONV_K - 1
    for t in range(tq):
        acc = jnp.zeros(o_ref.shape[1:], F32)
        for j in range(t, n_past):
            acc = acc + w_ref[j - t:j - t + 1, :] * past_ref[j]
        for i in range(t + 1):
            k = n_past - t + i
            acc = acc + w_ref[k:k + 1, :] * u_ref[i]
        o_ref[t] = _ln_swish(acc + bias, lg, lb).astype(BF16)


def _conv_sample(past_t, u_t, dw_w, dw_b, ln_g, ln_b):
    n_past, bd, _ = past_t.shape
    tq = u_t.shape[0]
    bt = _pick_tile(bd, 32)
    const2 = lambda i: (0, 0)
    return pl.pallas_call(
        functools.partial(_conv_sample_kernel, tq=tq),
        grid=(bd // bt,),
        in_specs=[
            pl.BlockSpec((n_past, bt, D_CONV), lambda i: (0, i, 0)),
            pl.BlockSpec((tq, bt, D_CONV), lambda i: (0, i, 0)),
            pl.BlockSpec((CONV_K, D_CONV), const2),
            pl.BlockSpec((1, D_CONV), const2),
            pl.BlockSpec((1, D_CONV), const2),
            pl.BlockSpec((1, D_CONV), const2),
        ],
        out_specs=pl.BlockSpec((tq, bt, D_CONV), lambda i: (0, i, 0)),
        out_shape=jax.ShapeDtypeStruct((tq, bd, D_CONV), BF16),
        compiler_params=_params(1),
    )(past_t, u_t, dw_w, dw_b, ln_g, ln_b)


def _chunkproj_kernel(x_ref, wk_ref, wv_ref, pk_ref, pv_ref, ab_ref, t_ref, *, n_pages, pages_on_lanes):
    n_rows = n_pages * CHUNKS_PER_PAGE
    for kv, (w_ref, p_ref) in enumerate(((wk_ref, pk_ref), (wv_ref, pv_ref))):
        for p in range(n_pages):
            page = x_ref[0, kv, :, p * PAGE_SIZE:(p + 1) * PAGE_SIZE] if pages_on_lanes else x_ref[p, kv]
            t_ref[p * PAGE_SIZE:(p + 1) * PAGE_SIZE, :] = page.T
        xs = jnp.concatenate([t_ref[pl.ds(l, n_rows, stride=CMP_STRIDE), :] for l in range(CMP_STRIDE)],
                             axis=1).astype(BF16)
        w = w_ref[...]
        part = _dot(xs, w)
        posb = _dot(p_ref[...], w)
        c0 = kv * 2 * LANES
        ab_ref[:, c0:c0 + LANES] = part[:, 0:LANES] + posb[0:1, 0:LANES]
        ab_ref[:, c0 + LANES:c0 + 2 * LANES] = part[:, LANES:2 * LANES] + posb[1:2, LANES:2 * LANES]


def _chunkproj(x4, wk2, wv2, pk2, pv2, page_off, n_total, pages_on_lanes):
    if pages_on_lanes:
        n_pages = x4.shape[3] // PAGE_SIZE
        n_steps = x4.shape[0]
        x_spec = pl.BlockSpec((1, 2, LANES, x4.shape[3]), lambda i: (i, 0, 0, 0))
    else:
        n_pages = _pick_tile(n_total, 64, 1)
        n_steps = n_total // n_pages
        off = page_off // n_pages
        x_spec = pl.BlockSpec((n_pages, 2, LANES, PAGE_SIZE), lambda i: (i + off, 0, 0, 0))
    const2 = lambda i: (0, 0)
    rows = n_pages * CHUNKS_PER_PAGE
    return pl.pallas_call(
        functools.partial(_chunkproj_kernel, n_pages=n_pages, pages_on_lanes=pages_on_lanes),
        grid=(n_steps,),
        in_specs=[
            x_spec,
            pl.BlockSpec(wk2.shape, const2),
            pl.BlockSpec(wv2.shape, const2),
            pl.BlockSpec(pk2.shape, const2),
            pl.BlockSpec(pv2.shape, const2),
        ],
        out_specs=pl.BlockSpec((rows, 4 * LANES), lambda i: (i, 0)),
        out_shape=jax.ShapeDtypeStruct((n_steps * rows, 4 * LANES), F32),
        scratch_shapes=[pltpu.VMEM((n_pages * PAGE_SIZE, LANES), F32)],
        compiler_params=_params(1),
    )(x4, wk2, wv2, pk2, pv2)


def _compressed_kv(ab, cosc, sinc):
    n = ab.shape[0]
    kc = ab[:, 0:LANES] + pltpu.roll(ab[:, LANES:2 * LANES], n - 1, 0)
    vc = ab[:, 2 * LANES:3 * LANES] + pltpu.roll(ab[:, 3 * LANES:4 * LANES], n - 1, 0)
    return _rope_slab(kc, cosc, sinc).astype(BF16), vc.astype(BF16)


def _softmax_parts(s, mask):
    sm = jnp.where(mask, s, NEG)
    m = jnp.max(sm, axis=-1, keepdims=True)
    p = jnp.where(mask, jnp.exp(sm - m), 0.0)
    l = jnp.maximum(jnp.sum(p, axis=-1, keepdims=True), 1e-30)
    return p, l


def _select_blocks(imp, qpos, n_sel, axis):
    jidx = lax.broadcasted_iota(jnp.int32, imp.shape, axis)
    cur = lax.shift_right_logical(qpos, 6)
    forced = (jidx == 0) | (jidx == cur) | (jidx == cur - 1)
    valid = (jidx * SEL_BLOCK <= qpos) & (jidx < n_sel)
    v = jnp.where(valid, jnp.where(forced, jnp.inf, imp), -jnp.inf)
    rank = jnp.zeros(imp.shape, jnp.int32)
    for k in range(n_sel):
        vk = v[k:k + 1, :] if axis == 0 else v[:, k:k + 1]
        ahead = (vk > v) | ((vk == v) & (jidx > k))
        rank = rank + ahead.astype(jnp.int32)
    return (rank < min(N_SEL, n_sel)) & valid


def _overlap(n_cmp, cmp_axis):
    i = lax.broadcasted_iota(jnp.int32, (LANES, LANES), cmp_axis)
    j = lax.broadcasted_iota(jnp.int32, (LANES, LANES), 1 - cmp_axis)
    hit = (i * CMP_STRIDE < (j + 1) * SEL_BLOCK) & (i * CMP_STRIDE + CMP_BLOCK > j * SEL_BLOCK) & (i < n_cmp)
    return hit.astype(F32)


Q_TILE = 256
K_TILE = 256
ROW_CHUNK = 128


def _attn_prompt_kernel(q_ref, gn_ref, skv_ref, wkv_ref, ab_ref, cosc_ref, sinc_ref, o_ref,
                        kc_ref, vc_ref, bias_ref, m_ref, acc_ref, out_ref, hs_ref, *, t_len):
    qb = pl.program_id(1)
    n_chunk = t_len // CMP_STRIDE
    n_cmp = (t_len - CMP_BLOCK) // CMP_STRIDE + 1
    n_sel = -(-t_len // SEL_BLOCK)
    sel_rows = -(-n_sel // SUBLANES) * SUBLANES
    qt, kt_sz, rc = Q_TILE, K_TILE, ROW_CHUNK
    n_rc = qt // rc

    @pl.when(qb == 0)
    def _():
        kc, vc = _compressed_kv(ab_ref[...], cosc_ref[...], sinc_ref[...])
        kc_ref[...] = kc
        vc_ref[...] = vc

    q0 = qb * qt
    qpos = q0 + lax.broadcasted_iota(jnp.int32, (qt, 1), 0)
    qpos_row = q0 + lax.broadcasted_iota(jnp.int32, (1, qt), 1)
    lane = lax.broadcasted_iota(jnp.int32, (rc, LANES), 1)

    for g in range(N_KV):
        in_group = (lane >= g * HEAD_DIM) & (lane < (g + 1) * HEAD_DIM)

        def q_chunk(h, c):
            return q_ref[c * rc:(c + 1) * rc, (g * HPG + h) * LANES:(g * HPG + h + 1) * LANES]

        def rows(h, c):
            return slice(h * qt + c * rc, h * qt + (c + 1) * rc)

        def gate(branch, h, c):
            col = branch * N_HEADS + g * HPG + h
            return gn_ref[c * rc:(c + 1) * rc, col:col + 1]

        qz = jnp.concatenate([q_ref[:, (g * HPG + h) * LANES:(g * HPG + h + 1) * LANES] for h in range(HPG)], axis=0)
        acc_ref[...] = _dot_nt(qz, kc_ref[...])
        for c in range(n_rc):
            qpos_c = q0 + c * rc + lax.broadcasted_iota(jnp.int32, (rc, 1), 0)
            ncol = lax.broadcasted_iota(jnp.int32, (rc, n_chunk), 1)
            mask_c = (ncol * CMP_STRIDE + CMP_BLOCK - 1 <= qpos_c) & (ncol < n_cmp)
            hs = jnp.zeros((rc, n_chunk), F32)
            for h in range(HPG):
                p_c, l_c = _softmax_parts(acc_ref[rows(h, c), :], mask_c)
                p_c = p_c / l_c
                hs = hs + p_c
                m_ref[rows(h, c), :] = p_c
            hs_ref[c * rc:(c + 1) * rc, :] = hs
        out_ref[...] = _dot(m_ref[...].astype(BF16), vc_ref[...])
        for h in range(HPG):
            for c in range(n_rc):
                out_ref[rows(h, c), :] = gate(0, h, c) * out_ref[rows(h, c), :]

        imp_t = lax.dot_general(_overlap(n_cmp, 1), hs_ref[...], (((1,), (1,)), ((), ())),
                                preferred_element_type=F32, precision=lax.Precision.HIGHEST)
        sel_t = _select_blocks(imp_t[0:sel_rows], qpos_row, n_sel, 0).astype(F32)
        sel_t = jnp.concatenate([sel_t, jnp.zeros((LANES - sel_rows, qt), F32)], axis=0)
        sel = sel_t.T.astype(BF16)

        def sel_bias(k_idx, kpos):
            jrow = lax.broadcasted_iota(jnp.int32, (LANES, kt_sz), 0)
            kcol = lax.broadcasted_iota(jnp.int32, (LANES, kt_sz), 1)
            expand = (jrow == k_idx * (kt_sz // SEL_BLOCK) + lax.shift_right_logical(kcol, 6)).astype(BF16)
            return jnp.where((_dot(sel, expand) > 0.5) & (kpos <= qpos), 0.0, NEG)

        def win_bias(k_idx, kpos):
            return jnp.where((kpos <= qpos) & (kpos > qpos - WINDOW), 0.0, NEG)

        def run_branch(kv_ref, n_tiles, tile_of, bias_fn, branch):
            m_ref[...] = jnp.full(m_ref.shape, NEG, F32)
            acc_ref[...] = jnp.zeros(acc_ref.shape, F32)

            def body(i, carry):
                k_idx = tile_of(i)
                k0 = pl.multiple_of(k_idx * kt_sz, kt_sz)
                kslab = kv_ref[0, 0:LANES, pl.ds(k0, kt_sz)].astype(BF16)
                vrow = lax.broadcasted_iota(jnp.int32, (LANES, kt_sz), 0)
                v_own = (vrow >= g * HEAD_DIM) & (vrow < (g + 1) * HEAD_DIM)
                vaug = jnp.where(v_own, kv_ref[0, LANES:KV_COLS, pl.ds(k0, kt_sz)], 1.0).astype(BF16)
                kpos = k0 + lax.broadcasted_iota(jnp.int32, (qt, kt_sz), 1)
                bias_ref[...] = bias_fn(k_idx, kpos)
                for h in range(HPG):
                    for c in range(n_rc):
                        r = rows(h, c)
                        sb = _dot(q_chunk(h, c), kslab) + bias_ref[c * rc:(c + 1) * rc, :]
                        m_old = m_ref[r, :]
                        halves = [sb[:, j * LANES:(j + 1) * LANES] for j in range(kt_sz // LANES)]
                        top = functools.reduce(jnp.maximum, halves)
                        m_new = jnp.maximum(m_old, jnp.max(top, axis=-1, keepdims=True))
                        p = jnp.concatenate([jnp.exp(s_j - m_new) for s_j in halves], axis=1)
                        acc_ref[r, :] = jnp.exp(m_old - m_new) * acc_ref[r, :] + _dot_nt(p.astype(BF16), vaug)
                        m_ref[r, :] = m_new
                return carry

            lax.fori_loop(0, n_tiles, body, 0)
            for h in range(HPG):
                for c in range(n_rc):
                    r = rows(h, c)
                    acc = acc_ref[r, :]
                    o_b = jnp.where(in_group, acc / pltpu.roll(acc, HEAD_DIM, 1), 0.0)
                    out_ref[r, :] = out_ref[r, :] + gate(branch, h, c) * o_b

        run_branch(skv_ref, qb + 1, lambda i: i, sel_bias, 1)
        run_branch(wkv_ref, jnp.minimum(qb, WINDOW // kt_sz) + 1, lambda i: qb - i, win_bias, 2)

        for h in range(HPG):
            hh = g * HPG + h
            o_ref[:, hh * LANES:(hh + 1) * LANES] = out_ref[h * qt:(h + 1) * qt, :].astype(BF16)


def _attn_prompt(q2, gn2, skv_t, wkv_t, ab2, cosc, sinc, b, t_len):
    n_chunk = t_len // CMP_STRIDE
    nqb = t_len // Q_TILE
    row = lambda i, j: (i * nqb + j, 0)
    per_b3 = lambda i, j: (i, 0, 0)
    const2 = lambda i, j: (0, 0)
    return pl.pallas_call(
        functools.partial(_attn_prompt_kernel, t_len=t_len),
        grid=(b, nqb),
        in_specs=[
            pl.BlockSpec((Q_TILE, Q_EXP), row),
            pl.BlockSpec((Q_TILE, LANES), row),
            pl.BlockSpec((1, KV_COLS, t_len), per_b3),
            pl.BlockSpec((1, KV_COLS, t_len), per_b3),
            pl.BlockSpec((n_chunk, 4 * LANES), lambda i, j: (i, 0)),
            pl.BlockSpec((n_chunk, LANES), const2),
            pl.BlockSpec((n_chunk, LANES), const2),
        ],
        out_specs=pl.BlockSpec((Q_TILE, Q_EXP), row),
        out_shape=jax.ShapeDtypeStruct((b * t_len, Q_EXP), BF16),
        scratch_shapes=[
            pltpu.VMEM((n_chunk, LANES), BF16),
            pltpu.VMEM((n_chunk, LANES), BF16),
            pltpu.VMEM((Q_TILE, K_TILE), F32),
            pltpu.VMEM((HPG * Q_TILE, LANES), F32),
            pltpu.VMEM((HPG * Q_TILE, LANES), F32),
            pltpu.VMEM((HPG * Q_TILE, LANES), F32),
            pltpu.VMEM((Q_TILE, n_chunk), F32),
        ],
        compiler_params=_params(2),
    )(q2, gn2, skv_t, wkv_t, ab2, cosc, sinc)


Q_ROWS = 8


def _attn_sample_kernel(pt_ref, q_ref, gn_ref, sknew_ref, wknew_ref, win_ref, cosc_ref, sinc_ref, *rest,
                        past_len, tq, n_pages):
    del pt_ref
    sel_pages = rest[:n_pages]
    ab_pages = rest[n_pages:2 * n_pages]
    o_ref = rest[2 * n_pages]
    t_all = past_len + tq
    n_chunk = past_len // CMP_STRIDE
    n_cmp = (t_all - CMP_BLOCK) // CMP_STRIDE + 1
    n_sel = -(-t_all // SEL_BLOCK)
    rows = N_HEADS * Q_ROWS
    win_buf = win_ref.shape[3]

    qbd = jnp.concatenate([q_ref[0, :, hh * LANES:(hh + 1) * LANES] for hh in range(N_HEADS)], axis=0)
    qpos = past_len + lax.broadcasted_iota(jnp.int32, (rows, 1), 0) % Q_ROWS

    ab = jnp.concatenate([r[0] for r in ab_pages], axis=0)
    kc, vc = _compressed_kv(ab, cosc_ref[...], sinc_ref[...])
    ncol = lax.broadcasted_iota(jnp.int32, (rows, n_chunk), 1)
    mask_c = (ncol * CMP_STRIDE + CMP_BLOCK - 1 <= qpos) & (ncol < n_cmp)
    p_c, l_c = _softmax_parts(_dot_nt(qbd, kc), mask_c)
    p_c = p_c / l_c
    o_c = _dot(p_c.astype(BF16), vc)

    hs = []
    for g in range(N_KV):
        acc = p_c[g * HPG * Q_ROWS:(g * HPG + 1) * Q_ROWS]
        for h in range(1, HPG):
            acc = acc + p_c[(g * HPG + h) * Q_ROWS:(g * HPG + h + 1) * Q_ROWS]
        hs.append(acc)
    hs = jnp.concatenate(hs, axis=0)
    imp = jnp.dot(hs, _overlap(n_cmp, 0), preferred_element_type=F32, precision=lax.Precision.HIGHEST)
    sel_g = _select_blocks(imp, qpos[0:N_KV * Q_ROWS], n_sel, 1).astype(BF16)
    sel_rows = jnp.concatenate([sel_g[g * Q_ROWS:(g + 1) * Q_ROWS] for g in range(N_KV) for _ in range(HPG)], axis=0)

    def new_rows_tile(ref, c0):
        return jnp.concatenate([ref[0, :, c0:c0 + LANES], jnp.zeros((LANES - Q_ROWS, LANES), F32)], axis=0).astype(BF16)

    n_keys = (n_pages + 1) * PAGE_SIZE
    s_s = jnp.concatenate([_dot(qbd, r[0, 0].astype(BF16)) for r in sel_pages]
                          + [_dot_nt(qbd, new_rows_tile(sknew_ref, 0))], axis=1)
    jrow = lax.broadcasted_iota(jnp.int32, (LANES, n_keys), 0)
    kcol = lax.broadcasted_iota(jnp.int32, (LANES, n_keys), 1)
    expand = (jrow == lax.shift_right_logical(kcol, 6)).astype(BF16)
    kpos = lax.broadcasted_iota(jnp.int32, (rows, n_keys), 1)
    mask_s = (_dot(sel_rows, expand) > 0.5) & (kpos <= qpos)
    p_s, l_s = _softmax_parts(s_s, mask_s)
    p_s = p_s.astype(BF16)
    o_s = _dot(p_s[:, n_pages * PAGE_SIZE:], new_rows_tile(sknew_ref, LANES))
    for i in range(n_pages):
        o_s = o_s + _dot_nt(p_s[:, i * PAGE_SIZE:(i + 1) * PAGE_SIZE], sel_pages[i][0, 1].astype(BF16))
    o_s = o_s / l_s

    s_w = jnp.concatenate([_dot(qbd, win_ref[0, 0].astype(BF16)), _dot_nt(qbd, new_rows_tile(wknew_ref, 0))], axis=1)
    wcol = lax.broadcasted_iota(jnp.int32, (rows, win_buf + LANES), 1)
    kpos_w = past_len - win_buf + wcol
    mask_w = (kpos_w <= qpos) & (kpos_w > qpos - WINDOW)
    p_w, l_w = _softmax_parts(s_w, mask_w)
    p_w = p_w.astype(BF16)
    o_w = (_dot_nt(p_w[:, 0:win_buf], win_ref[0, 1].astype(BF16))
           + _dot(p_w[:, win_buf:], new_rows_tile(wknew_ref, LANES))) / l_w

    gn = gn_ref[0]
    for hh in range(N_HEADS):
        r = slice(hh * Q_ROWS, (hh + 1) * Q_ROWS)
        out = (gn[:, hh:hh + 1] * o_c[r]
               + gn[:, N_HEADS + hh:N_HEADS + hh + 1] * o_s[r]
               + gn[:, 2 * N_HEADS + hh:2 * N_HEADS + hh + 1] * o_w[r])
        o_ref[0, :, hh * LANES:(hh + 1) * LANES] = out.astype(BF16)


def _attn_sample(pt_flat, q3, gn3, sknew3, wknew3, win4, cosc, sinc, sel_pool4, ab_pool3,
                 layer, n_pool, bd, past_len, tq):
    n_pages = past_len // PAGE_SIZE
    win_buf = win4.shape[3]
    n_chunk = past_len // CMP_STRIDE
    per_b = lambda i, pt: (i, 0, 0)
    const2 = lambda i, pt: (0, 0)
    sel_specs = [pl.BlockSpec((1, 2, LANES, PAGE_SIZE),
                              lambda i, pt, p=p: (layer * n_pool + pt[i * n_pages + p], 0, 0, 0))
                 for p in range(n_pages)]
    ab_specs = [pl.BlockSpec((1, CHUNKS_PER_PAGE, 4 * LANES),
                             lambda i, pt, p=p: (pt[i * n_pages + p], 0, 0)) for p in range(n_pages)]
    grid_spec = pltpu.PrefetchScalarGridSpec(
        num_scalar_prefetch=1,
        grid=(bd,),
        in_specs=[
            pl.BlockSpec((1, Q_ROWS, Q_EXP), per_b),
            pl.BlockSpec((1, Q_ROWS, LANES), per_b),
            pl.BlockSpec((1, Q_ROWS, KV_COLS), per_b),
            pl.BlockSpec((1, Q_ROWS, KV_COLS), per_b),
            pl.BlockSpec((1, 2, LANES, win_buf), lambda i, pt: (layer * bd + i, 0, 0, 0)),
            pl.BlockSpec((n_chunk, LANES), const2),
            pl.BlockSpec((n_chunk, LANES), const2),
        ] + sel_specs + ab_specs,
        out_specs=pl.BlockSpec((1, Q_ROWS, Q_EXP), per_b),
    )
    return pl.pallas_call(
        functools.partial(_attn_sample_kernel, past_len=past_len, tq=tq, n_pages=n_pages),
        grid_spec=grid_spec,
        out_shape=jax.ShapeDtypeStruct((bd, Q_ROWS, Q_EXP), BF16),
        compiler_params=_params(1),
    )(pt_flat, q3, gn3, sknew3, wknew3, win4, cosc, sinc, *([sel_pool4] * n_pages), *([ab_pool3] * n_pages))


def _merge_kernel(c_ref, o_ref, gm_ref, x_ref, wc_ref, wn_ref, wo_ref, y_ref):
    conv_out = _dot(c_ref[...], wc_ref[...])
    nsa_out = _dot(o_ref[...], wn_ref[...])
    gm = gm_ref[...].astype(F32)
    merged = gm[:, 0:D_MODEL] * conv_out + gm[:, D_MODEL:2 * D_MODEL] * nsa_out
    y_ref[...] = x_ref[...] + _dot(merged.astype(BF16), wo_ref[...])


def _merge(c2, o2, gm2, x2, wc, wn, wo, tm):
    m = x2.shape[0]
    row = lambda i: (i, 0)
    const = lambda i: (0, 0)
    return pl.pallas_call(
        _merge_kernel,
        grid=(m // tm,),
        in_specs=[
            pl.BlockSpec((tm, D_CONV), row),
            pl.BlockSpec((tm, Q_EXP), row),
            pl.BlockSpec((tm, 2 * D_MODEL), row),
            pl.BlockSpec((tm, D_MODEL), row),
            pl.BlockSpec(wc.shape, const),
            pl.BlockSpec(wn.shape, const),
            pl.BlockSpec(wo.shape, const),
        ],
        out_specs=pl.BlockSpec((tm, D_MODEL), row),
        out_shape=jax.ShapeDtypeStruct((m, D_MODEL), F32),
        compiler_params=_params(1),
    )(c2, o2, gm2, x2, wc, wn, wo)


FF_TILE = 1024


def _mlp_kernel(x_ref, g_ref, wu_ref, wd_ref, gf_ref, y_ref, *, final):
    x = x_ref[...]
    h = _rms_bf16(x, g_ref[...])
    acc = x
    for f in range(D_FF // FF_TILE):
        up = jnp.maximum(_dot(h, wu_ref[:, f * FF_TILE:(f + 1) * FF_TILE]), 0.0)
        acc = acc + _dot((up * up).astype(BF16), wd_ref[f * FF_TILE:(f + 1) * FF_TILE, :])
    if final:
        ms2 = jnp.mean(acc * acc, axis=-1, keepdims=True)
        acc = acc * lax.rsqrt(ms2 + EPS) * gf_ref[...]
    y_ref[...] = acc


def _mlp(x2, g, wu, wd, gf, tm, final):
    m = x2.shape[0]
    row = lambda i: (i, 0)
    const = lambda i: (0, 0)
    return pl.pallas_call(
        functools.partial(_mlp_kernel, final=final),
        grid=(m // tm,),
        in_specs=[
            pl.BlockSpec((tm, D_MODEL), row),
            pl.BlockSpec((1, D_MODEL), const),
            pl.BlockSpec(wu.shape, const),
            pl.BlockSpec(wd.shape, const),
            pl.BlockSpec((1, D_MODEL), const),
        ],
        out_specs=pl.BlockSpec((tm, D_MODEL), row),
        out_shape=jax.ShapeDtypeStruct((m, D_MODEL), F32),
        compiler_params=_params(1),
    )(x2, g, wu, wd, gf)


def _rope_tables(pos):
    half = HEAD_DIM // 2
    inv = jnp.power(ROPE_THETA, -jnp.arange(half, dtype=F32) / half)
    ang = pos.astype(F32)[:, None] * inv[None, :]
    cos = jnp.cos(ang)
    sin = jnp.sin(ang)
    return jnp.concatenate([cos, cos, cos, cos], axis=1), jnp.concatenate([-sin, sin, -sin, sin], axis=1)


def _pack_w_in(w):
    o0 = 2 * D_CONV
    o1 = o0 + N_HEADS * HEAD_DIM
    o4 = o1 + 3 * KV_COLS
    o5 = o4 + N_GATES
    wq = w[:, o0:o1].reshape(D_MODEL, N_HEADS, HEAD_DIM) * (HEAD_DIM ** -0.5)
    zero = jnp.zeros_like(wq)
    in_g0 = (jnp.arange(N_HEADS) < HPG)[None, :, None]
    wq_exp = jnp.concatenate([jnp.where(in_g0, wq, zero), jnp.where(in_g0, zero, wq)], axis=2)
    wgn = jnp.pad(w[:, o4:o5], ((0, 0), (0, LANES - N_GATES)))
    packed = jnp.concatenate([w[:, :o0], wq_exp.reshape(D_MODEL, Q_EXP), w[:, o1:o4], wgn, w[:, o5:]], axis=1)
    return packed.astype(BF16), w[:, o1:o4].T.astype(BF16)


def _pack_w_nsa_out(w):
    wh = w.reshape(N_HEADS, HEAD_DIM, D_MODEL)
    zero = jnp.zeros_like(wh)
    in_g0 = (jnp.arange(N_HEADS) < HPG)[:, None, None]
    return jnp.concatenate([jnp.where(in_g0, wh, zero), jnp.where(in_g0, zero, wh)], axis=1).reshape(Q_EXP, D_MODEL).astype(BF16)


def _pack_w_cmp(w_kv, pos_kv):
    lo, hi = w_kv[:CMP_STRIDE], w_kv[CMP_STRIDE:]
    eye = jnp.eye(N_KV, dtype=w_kv.dtype)
    blk = lambda part: jnp.einsum('lde,gh->lgdhe', part, eye).reshape(CMP_STRIDE * N_KV * HEAD_DIM, N_KV * HEAD_DIM)
    w2 = jnp.concatenate([blk(lo), blk(hi)], axis=1).astype(BF16)
    tile = lambda p: jnp.broadcast_to(p[:, None, :], (CMP_STRIDE, N_KV, HEAD_DIM)).reshape(1, -1)
    p2 = jnp.concatenate([tile(pos_kv[:CMP_STRIDE]), tile(pos_kv[CMP_STRIDE:]),
                          jnp.zeros((SUBLANES - 2, CMP_STRIDE * N_KV * HEAD_DIM), pos_kv.dtype)], axis=0).astype(BF16)
    return w2, p2


def _feature_major(a):
    lead = a.shape[:-4]
    n = len(lead)
    a = jnp.transpose(a, tuple(range(n)) + (n + 1, n + 2, n + 3, n))
    return a.reshape(lead + (2, N_KV * HEAD_DIM, a.shape[-1]))


def _position_major(a_t, rows):
    b = a_t.shape[0]
    return jnp.transpose(a_t.reshape(b, 2, N_KV, HEAD_DIM, rows), (0, 4, 1, 2, 3))


def kernel(x_prompt, x_sample, cache_cmp_kv, cache_sel_kv, state_win_kv, state_conv, page_table, norm_mix_g, w_in, conv_dw_w, conv_dw_b, conv_ln_g, conv_ln_b, w_conv_out, cmp_pos, w_cmp, w_nsa_out, w_out, norm_mlp_g, w_up, w_down, norm_final_g):
    depth = w_in.shape[0]
    bp, tp, _ = x_prompt.shape
    bd, tq, _ = x_sample.shape
    n_pool = cache_cmp_kv.shape[1]
    n_pages = page_table.shape[1]
    past_len = n_pages * PAGE_SIZE
    win_buf = state_win_kv.shape[2]
    assert tp % Q_TILE == 0 and tp // CMP_STRIDE == LANES and tp >= WINDOW
    assert past_len // CMP_STRIDE == LANES and win_buf == WINDOW and tq <= Q_ROWS

    mp, ms = bp * tp, bd * tq
    tm_p = _pick_tile(mp, 256)
    tm_s = _pick_tile(ms, 256)
    assert tp % tm_p == 0 and tm_s % tq == 0

    cos_p, sin_p = _rope_tables(jnp.arange(tp))
    cos_pt, sin_pt = cos_p.T, sin_p.T
    cos_s, sin_s = _rope_tables(past_len + jnp.arange(tm_s) % tq)
    n_chunk = tp // CMP_STRIDE
    cos_c, sin_c = _rope_tables(jnp.arange(n_chunk) * CMP_STRIDE + CMP_BLOCK - 1)

    pt_flat = page_table.reshape(-1).astype(jnp.int32)
    cmp_pool4 = _feature_major(cache_cmp_kv).reshape(depth * n_pool, 2, LANES, PAGE_SIZE)
    sel_pool4 = _feature_major(cache_sel_kv).reshape(depth * n_pool, 2, LANES, PAGE_SIZE)
    win4 = _feature_major(state_win_kv).reshape(depth * bd, 2, LANES, win_buf)
    zeros_conv = jnp.zeros((bp, CONV_K - 1, D_CONV), F32)

    xp = x_prompt.reshape(mp, D_MODEL)
    xs = x_sample.reshape(ms, D_MODEL)
    outs = [[] for _ in range(8)]
    for l in range(depth):
        w_packed, wkv_t = _pack_w_in(w_in[l])
        wn = _pack_w_nsa_out(w_nsa_out[l])
        wc, wo = w_conv_out[l].astype(BF16), w_out[l].astype(BF16)
        wu, wd = w_up[l].astype(BF16), w_down[l].astype(BF16)
        wk2, pk2 = _pack_w_cmp(w_cmp[l, 0], cmp_pos[l, 0])
        wv2, pv2 = _pack_w_cmp(w_cmp[l, 1], cmp_pos[l, 1])
        g_mix, g_mlp = norm_mix_g[l][None], norm_mlp_g[l][None]
        dw_b, ln_g, ln_b = conv_dw_b[l][None], conv_ln_g[l][None], conv_ln_b[l][None]
        gf = norm_final_g[None]
        final = l == depth - 1

        u, q, ckv_t, skv_t, wkv_tm, gn, gm = _inproj_cols(xp, g_mix, w_packed, wkv_t, cos_p, sin_p, cos_pt, sin_pt,
                                                          bp, tp, tm_p)
        c_act = _conv_prompt(u.reshape(bp, tp, D_CONV), zeros_conv, conv_dw_w[l], dw_b, ln_g, ln_b)
        ab = _chunkproj(ckv_t.reshape(bp, 2, LANES, tp), wk2, wv2, pk2, pv2, 0, 0, True)
        o = _attn_prompt(q, gn, skv_t, wkv_tm, ab, cos_c, sin_c, bp, tp)
        xp = _merge(c_act.reshape(mp, D_CONV), o, gm, xp, wc, wn, wo, tm_p)
        xp = _mlp(xp, g_mlp, wu, wd, gf, _pick_tile(mp, 512), final)
        keep = min(WINDOW, tp)
        outs[0].append(_position_major(ckv_t, tp))
        outs[2].append(_position_major(skv_t, tp))
        outs[4].append(_position_major(wkv_tm[:, :, tp - keep:], keep))
        outs[6].append(u.reshape(bp, tp, D_CONV)[:, tp - (CONV_K - 1):])

        u, q, ckv, skv, wkv, gn, gm = _inproj_rows(xs, g_mix, w_packed, cos_s, sin_s, tm_s)
        u3 = u.reshape(bd, tq, D_CONV)
        c_t = _conv_sample(jnp.swapaxes(state_conv[l], 0, 1), jnp.swapaxes(u3, 0, 1), conv_dw_w[l], dw_b, ln_g, ln_b)
        c_act = jnp.swapaxes(c_t, 0, 1).reshape(ms, D_CONV)
        ab_pool = _chunkproj(cmp_pool4, wk2, wv2, pk2, pv2, l * n_pool, n_pool, False)
        pad_q = lambda a: jnp.pad(a.reshape(bd, tq, -1), ((0, 0), (0, Q_ROWS - tq), (0, 0)))
        o = _attn_sample(pt_flat, pad_q(q), pad_q(gn), pad_q(skv), pad_q(wkv), win4, cos_c, sin_c, sel_pool4,
                         ab_pool.reshape(n_pool, CHUNKS_PER_PAGE, 4 * LANES), l, n_pool, bd, past_len, tq)
        xs = _merge(c_act, o[:, :tq].reshape(ms, Q_EXP), gm, xs, wc, wn, wo, tm_s)
        xs = _mlp(xs, g_mlp, wu, wd, gf, _pick_tile(ms, 512), final)
        wkv5 = wkv.reshape(bd, tq, 2, N_KV, HEAD_DIM)
        outs[1].append(ckv.reshape(bd, tq, 2, N_KV, HEAD_DIM))
        outs[3].append(skv.reshape(bd, tq, 2, N_KV, HEAD_DIM))
        outs[5].append(wkv5)
        outs[7].append(u3)

    y_prompt = xp.reshape(bp, tp, D_MODEL)
    y_sample = xs.reshape(bd, tq, D_MODEL)
    outs = [jnp.stack(o) for o in outs]
    outs[5] = jnp.concatenate([state_win_kv[:, :, tq:], outs[5]], axis=2)
    outs[7] = jnp.concatenate([state_conv[:, :, tq:], outs[7]], axis=2)
    return (y_prompt, y_sample) + tuple(outs)
```

```python
import functools

import jax
import jax.numpy as jnp
from jax import lax
from jax.experimental import pallas as pl
from jax.experimental.pallas import tpu as pltpu

D_MODEL = 1024
D_CONV = D_MODEL // 2
CONV_K = 31
N_HEADS = 8
HEAD_DIM = 64
N_KV = 2
HPG = N_HEADS // N_KV
KV_COLS = 2 * N_KV * HEAD_DIM
CMP_BLOCK = 32
CMP_STRIDE = 16
SEL_BLOCK = 64
N_SEL = 16
WINDOW = 512
D_FF = 4 * D_MODEL
ROPE_THETA = 10000.0
EPS = 1e-6
PAGE_SIZE = 128

LANES = 128
SUBLANES = 8
Q_EXP = N_HEADS * LANES
N_GATES = 3 * N_HEADS
CHUNKS_PER_PAGE = PAGE_SIZE // CMP_STRIDE
NEG = -1e30

O_GLU = 0
O_Q = O_GLU + 2 * D_CONV
O_CKV = O_Q + Q_EXP
O_SKV = O_CKV + KV_COLS
O_WKV = O_SKV + KV_COLS
O_GN = O_WKV + KV_COLS
O_GM = O_GN + LANES
IN_COLS_PACKED = O_GM + 2 * D_MODEL

VMEM_LIMIT = 56 * 1024 * 1024

F32 = jnp.float32
BF16 = jnp.bfloat16


def _params(n_axes, vmem=VMEM_LIMIT):
    return pltpu.CompilerParams(dimension_semantics=("arbitrary",) * n_axes, vmem_limit_bytes=vmem)


def _sigmoid(x):
    return 1.0 / (1.0 + jnp.exp(-x))


def _dot(a, b):
    return jnp.dot(a, b, preferred_element_type=F32)


def _dot_nt(a, b):
    return lax.dot_general(a, b, (((1,), (1,)), ((), ())), preferred_element_type=F32)


def _rope_slab(xs, cos, sin_signed):
    lane = lax.broadcasted_iota(jnp.int32, xs.shape, 1)
    first = (lane % HEAD_DIM) < (HEAD_DIM // 2)
    rot = jnp.where(first, pltpu.roll(xs, LANES - HEAD_DIM // 2, 1), pltpu.roll(xs, HEAD_DIM // 2, 1))
    return xs * cos + rot * sin_signed


def _rope_rows(xt, cos_t, sin_t):
    half = HEAD_DIM // 2
    rot = jnp.concatenate([xt[half:2 * half], xt[0:half], xt[3 * half:4 * half], xt[2 * half:3 * half]], axis=0)
    return xt * cos_t + rot * sin_t


def _pick_tile(n, cap, mult=8):
    t = min(n, cap)
    while n % t or t % mult:
        t -= 1
    return t


def _rms_bf16(x, g):
    ms = jnp.mean(x * x, axis=-1, keepdims=True)
    return (x * lax.rsqrt(ms + EPS) * g).astype(BF16)


def _inproj_common(h, w_ref, cos, sin, u_ref, q_ref, gn_ref, gm_ref):
    def proj(a, n):
        return _dot(h, w_ref[:, a:a + n])

    u_ref[...] = proj(O_GLU, D_CONV) * _sigmoid(proj(O_GLU + D_CONV, D_CONV))
    for s in range(N_HEADS):
        q_ref[:, s * LANES:(s + 1) * LANES] = _rope_slab(proj(O_Q + s * LANES, LANES), cos, sin).astype(BF16)
    gn_ref[...] = _sigmoid(proj(O_GN, LANES))
    gm_ref[...] = _sigmoid(proj(O_GM, 2 * D_MODEL)).astype(BF16)
    return proj


def _inproj_rows_kernel(x_ref, g_ref, w_ref, cos_ref, sin_ref,
                        u_ref, q_ref, ckv_ref, skv_ref, wkv_ref, gn_ref, gm_ref):
    h = _rms_bf16(x_ref[...], g_ref[...])
    cos = cos_ref[...]
    sin = sin_ref[...]
    proj = _inproj_common(h, w_ref, cos, sin, u_ref, q_ref, gn_ref, gm_ref)
    ckv_ref[...] = proj(O_CKV, KV_COLS)
    skv_ref[:, 0:LANES] = _rope_slab(proj(O_SKV, LANES), cos, sin)
    skv_ref[:, LANES:KV_COLS] = proj(O_SKV + LANES, LANES)
    wkv_ref[:, 0:LANES] = _rope_slab(proj(O_WKV, LANES), cos, sin)
    wkv_ref[:, LANES:KV_COLS] = proj(O_WKV + LANES, LANES)


def _inproj_cols_kernel(x_ref, g_ref, w_ref, wkvt_ref, cos_ref, sin_ref, cost_ref, sint_ref,
                        u_ref, q_ref, ckv_ref, skv_ref, wkv_ref, gn_ref, gm_ref):
    h = _rms_bf16(x_ref[...], g_ref[...])
    _inproj_common(h, w_ref, cos_ref[...], sin_ref[...], u_ref, q_ref, gn_ref, gm_ref)
    cos_t = cost_ref[...]
    sin_t = sint_ref[...]
    kvt = _dot_nt(wkvt_ref[...], h)
    ckv_ref[0] = kvt[0:KV_COLS]
    skv_ref[0, 0:LANES] = _rope_rows(kvt[KV_COLS:KV_COLS + LANES], cos_t, sin_t)
    skv_ref[0, LANES:KV_COLS] = kvt[KV_COLS + LANES:2 * KV_COLS]
    wkv_ref[0, 0:LANES] = _rope_rows(kvt[2 * KV_COLS:2 * KV_COLS + LANES], cos_t, sin_t)
    wkv_ref[0, LANES:KV_COLS] = kvt[2 * KV_COLS + LANES:3 * KV_COLS]


def _inproj_rows(x2, g, w_packed, cos_t, sin_t, tm):
    m = x2.shape[0]
    row = lambda i: (i, 0)
    const = lambda i: (0, 0)
    out_shapes = (
        jax.ShapeDtypeStruct((m, D_CONV), F32),
        jax.ShapeDtypeStruct((m, Q_EXP), BF16),
        jax.ShapeDtypeStruct((m, KV_COLS), F32),
        jax.ShapeDtypeStruct((m, KV_COLS), F32),
        jax.ShapeDtypeStruct((m, KV_COLS), F32),
        jax.ShapeDtypeStruct((m, LANES), F32),
        jax.ShapeDtypeStruct((m, 2 * D_MODEL), BF16),
    )
    return pl.pallas_call(
        _inproj_rows_kernel,
        grid=(m // tm,),
        in_specs=[
            pl.BlockSpec((tm, D_MODEL), row),
            pl.BlockSpec((1, D_MODEL), const),
            pl.BlockSpec(w_packed.shape, const),
            pl.BlockSpec((tm, LANES), const),
            pl.BlockSpec((tm, LANES), const),
        ],
        out_specs=tuple(pl.BlockSpec((tm, s.shape[1]), row) for s in out_shapes),
        out_shape=out_shapes,
        compiler_params=_params(1),
    )(x2, g, w_packed, cos_t, sin_t)


def _inproj_cols(x2, g, w_packed, wkv_t, cos_t, sin_t, cos_tt, sin_tt, b, t_len, tm):
    m = x2.shape[0]
    n_tab = t_len // tm
    row = lambda i: (i, 0)
    const = lambda i: (0, 0)
    kv_map = lambda i: (i // n_tab, 0, i % n_tab)
    kv_shape = jax.ShapeDtypeStruct((b, KV_COLS, t_len), F32)
    out_shapes = (
        jax.ShapeDtypeStruct((m, D_CONV), F32),
        jax.ShapeDtypeStruct((m, Q_EXP), BF16),
        kv_shape, kv_shape, kv_shape,
        jax.ShapeDtypeStruct((m, LANES), F32),
        jax.ShapeDtypeStruct((m, 2 * D_MODEL), BF16),
    )
    out_specs = (
        pl.BlockSpec((tm, D_CONV), row),
        pl.BlockSpec((tm, Q_EXP), row),
        pl.BlockSpec((1, KV_COLS, tm), kv_map),
        pl.BlockSpec((1, KV_COLS, tm), kv_map),
        pl.BlockSpec((1, KV_COLS, tm), kv_map),
        pl.BlockSpec((tm, LANES), row),
        pl.BlockSpec((tm, 2 * D_MODEL), row),
    )
    return pl.pallas_call(
        _inproj_cols_kernel,
        grid=(m // tm,),
        in_specs=[
            pl.BlockSpec((tm, D_MODEL), row),
            pl.BlockSpec((1, D_MODEL), const),
            pl.BlockSpec(w_packed.shape, const),
            pl.BlockSpec(wkv_t.shape, const),
            pl.BlockSpec((tm, LANES), lambda i: (i % n_tab, 0)),
            pl.BlockSpec((tm, LANES), lambda i: (i % n_tab, 0)),
            pl.BlockSpec((LANES, tm), lambda i: (0, i % n_tab)),
            pl.BlockSpec((LANES, tm), lambda i: (0, i % n_tab)),
        ],
        out_specs=out_specs,
        out_shape=out_shapes,
        compiler_params=_params(1),
    )(x2, g, w_packed, wkv_t, cos_t, sin_t, cos_tt, sin_tt)


def _ln_swish(c, lg, lb):
    mu = jnp.mean(c, axis=-1, keepdims=True)
    d = c - mu
    var = jnp.mean(d * d, axis=-1, keepdims=True)
    y = d * lax.rsqrt(var + EPS) * lg + lb
    return y * _sigmoid(y)


CONV_PAD = 32
CONV_CHUNK = 64


def _conv_prompt_kernel(u_ref, past_ref, w_ref, b_ref, lg_ref, lb_ref, o_ref, uf_ref, *, t_len):
    off = CONV_PAD - (CONV_K - 1)
    uf_ref[0:SUBLANES, :] = jnp.zeros((SUBLANES, D_CONV), F32)
    uf_ref[off:CONV_PAD, :] = past_ref[0]
    uf_ref[CONV_PAD:CONV_PAD + t_len, :] = u_ref[0]
    bias = b_ref[...]
    lg = lg_ref[...]
    lb = lb_ref[...]
    ct = CONV_CHUNK

    def body(i, carry):
        base = pl.multiple_of(i * ct, ct)
        xw = uf_ref[pl.ds(base, ct + CONV_PAD), :]
        acc = jnp.zeros((ct, D_CONV), F32)
        n_win = ct + CONV_PAD
        for r in range(SUBLANES):
            yr = xw if r == 0 else pltpu.roll(xw, n_win - r, 0)
            for a in range((CONV_PAD // SUBLANES) + 1):
                k = SUBLANES * a + r - off
                if 0 <= k < CONV_K:
                    acc = acc + w_ref[k:k + 1, :] * yr[SUBLANES * a:SUBLANES * a + ct, :]
        o_ref[0, pl.ds(base, ct), :] = _ln_swish(acc + bias, lg, lb).astype(BF16)
        return carry

    lax.fori_loop(0, t_len // ct, body, 0)


def _conv_prompt(u3, past3, dw_w, dw_b, ln_g, ln_b):
    b, t_len, _ = u3.shape
    const2 = lambda i: (0, 0)
    return pl.pallas_call(
        functools.partial(_conv_prompt_kernel, t_len=t_len),
        grid=(b,),
        in_specs=[
            pl.BlockSpec((1, t_len, D_CONV), lambda i: (i, 0, 0)),
            pl.BlockSpec((1, CONV_K - 1, D_CONV), lambda i: (i, 0, 0)),
            pl.BlockSpec((CONV_K, D_CONV), const2),
            pl.BlockSpec((1, D_CONV), const2),
            pl.BlockSpec((1, D_CONV), const2),
            pl.BlockSpec((1, D_CONV), const2),
        ],
        out_specs=pl.BlockSpec((1, t_len, D_CONV), lambda i: (i, 0, 0)),
        out_shape=jax.ShapeDtypeStruct((b, t_len, D_CONV), BF16),
        scratch_shapes=[pltpu.VMEM((t_len + CONV_PAD, D_CONV), F32)],
        compiler_params=_params(1),
    )(u3, past3, dw_w, dw_b, ln_g, ln_b)


def _conv_sample_kernel(past_ref, u_ref, w_ref, b_ref, lg_ref, lb_ref, o_ref, *, tq):
    bias = b_ref[...]
    lg = lg_ref[...]
    lb = lb_ref[...]
    n_past = CONV_K - 1
    for t in range(tq):
        acc = jnp.zeros(o_ref.shape[1:], F32)
        for j in range(t, n_past):
            acc = acc + w_ref[j - t:j - t + 1, :] * past_ref[j]
        for i in range(t + 1):
            k = n_past - t + i
            acc = acc + w_ref[k:k + 1, :] * u_ref[i]
        o_ref[t] = _ln_swish(acc + bias, lg, lb).astype(BF16)


def _conv_sample(past_t, u_t, dw_w, dw_b, ln_g, ln_b):
    n_past, bd, _ = past_t.shape
    tq = u_t.shape[0]
    bt = _pick_tile(bd, 32)
    const2 = lambda i: (0, 0)
    return pl.pallas_call(
        functools.partial(_conv_sample_kernel, tq=tq),
        grid=(bd // bt,),
        in_specs=[
            pl.BlockSpec((n_past, bt, D_CONV), lambda i: (0, i, 0)),
            pl.BlockSpec((tq, bt, D_CONV), lambda i: (0, i, 0)),
            pl.BlockSpec((CONV_K, D_CONV), const2),
            pl.BlockSpec((1, D_CONV), const2),
            pl.BlockSpec((1, D_CONV), const2),
            pl.BlockSpec((1, D_CONV), const2),
        ],
        out_specs=pl.BlockSpec((tq, bt, D_CONV), lambda i: (0, i, 0)),
        out_shape=jax.ShapeDtypeStruct((tq, bd, D_CONV), BF16),
        compiler_params=_params(1),
    )(past_t, u_t, dw_w, dw_b, ln_g, ln_b)


def _chunkproj_kernel(x_ref, wk_ref, wv_ref, pk_ref, pv_ref, ab_ref, t_ref, *, n_pages, pages_on_lanes):
    n_rows = n_pages * CHUNKS_PER_PAGE
    for kv, (w_ref, p_ref) in enumerate(((wk_ref, pk_ref), (wv_ref, pv_ref))):
        for p in range(n_pages):
            page = x_ref[0, kv, :, p * PAGE_SIZE:(p + 1) * PAGE_SIZE] if pages_on_lanes else x_ref[p, kv]
            t_ref[p * PAGE_SIZE:(p + 1) * PAGE_SIZE, :] = page.T
        xs = jnp.concatenate([t_ref[pl.ds(l, n_rows, stride=CMP_STRIDE), :] for l in range(CMP_STRIDE)],
                             axis=1).astype(BF16)
        w = w_ref[...]
        part = _dot(xs, w)
        posb = _dot(p_ref[...], w)
        c0 = kv * 2 * LANES
        ab_ref[:, c0:c0 + LANES] = part[:, 0:LANES] + posb[0:1, 0:LANES]
        ab_ref[:, c0 + LANES:c0 + 2 * LANES] = part[:, LANES:2 * LANES] + posb[1:2, LANES:2 * LANES]


def _chunkproj(x4, wk2, wv2, pk2, pv2, page_off, n_total, pages_on_lanes):
    if pages_on_lanes:
        n_pages = x4.shape[3] // PAGE_SIZE
        n_steps = x4.shape[0]
        x_spec = pl.BlockSpec((1, 2, LANES, x4.shape[3]), lambda i: (i, 0, 0, 0))
    else:
        n_pages = _pick_tile(n_total, 64, 1)
        n_steps = n_total // n_pages
        off = page_off // n_pages
        x_spec = pl.BlockSpec((n_pages, 2, LANES, PAGE_SIZE), lambda i: (i + off, 0, 0, 0))
    const2 = lambda i: (0, 0)
    rows = n_pages * CHUNKS_PER_PAGE
    return pl.pallas_call(
        functools.partial(_chunkproj_kernel, n_pages=n_pages, pages_on_lanes=pages_on_lanes),
        grid=(n_steps,),
        in_specs=[
            x_spec,
            pl.BlockSpec(wk2.shape, const2),
            pl.BlockSpec(wv2.shape, const2),
            pl.BlockSpec(pk2.shape, const2),
            pl.BlockSpec(pv2.shape, const2),
        ],
        out_specs=pl.BlockSpec((rows, 4 * LANES), lambda i: (i, 0)),
        out_shape=jax.ShapeDtypeStruct((n_steps * rows, 4 * LANES), F32),
        scratch_shapes=[pltpu.VMEM((n_pages * PAGE_SIZE, LANES), F32)],
        compiler_params=_params(1),
    )(x4, wk2, wv2, pk2, pv2)


def _compressed_kv(ab, cosc, sinc):
    n = ab.shape[0]
    kc = ab[:, 0:LANES] + pltpu.roll(ab[:, LANES:2 * LANES], n - 1, 0)
    vc = ab[:, 2 * LANES:3 * LANES] + pltpu.roll(ab[:, 3 * LANES:4 * LANES], n - 1, 0)
    return _rope_slab(kc, cosc, sinc).astype(BF16), vc.astype(BF16)


def _softmax_parts(s, mask):
    sm = jnp.where(mask, s, NEG)
    m = jnp.max(sm, axis=-1, keepdims=True)
    p = jnp.where(mask, jnp.exp(sm - m), 0.0)
    l = jnp.maximum(jnp.sum(p, axis=-1, keepdims=True), 1e-30)
    return p, l


def _select_blocks(imp, qpos, n_sel, axis):
    jidx = lax.broadcasted_iota(jnp.int32, imp.shape, axis)
    cur = lax.shift_right_logical(qpos, 6)
    forced = (jidx == 0) | (jidx == cur) | (jidx == cur - 1)
    valid = (jidx * SEL_BLOCK <= qpos) & (jidx < n_sel)
    v = jnp.where(valid, jnp.where(forced, jnp.inf, imp), -jnp.inf)
    rank = jnp.zeros(imp.shape, jnp.int32)
    for k in range(n_sel):
        vk = v[k:k + 1, :] if axis == 0 else v[:, k:k + 1]
        ahead = (vk > v) | ((vk == v) & (jidx > k))
        rank = rank + ahead.astype(jnp.int32)
    return (rank < min(N_SEL, n_sel)) & valid


def _overlap(n_cmp, cmp_axis):
    i = lax.broadcasted_iota(jnp.int32, (LANES, LANES), cmp_axis)
    j = lax.broadcasted_iota(jnp.int32, (LANES, LANES), 1 - cmp_axis)
    hit = (i * CMP_STRIDE < (j + 1) * SEL_BLOCK) & (i * CMP_STRIDE + CMP_BLOCK > j * SEL_BLOCK) & (i < n_cmp)
    return hit.astype(F32)


Q_TILE = 256
K_TILE = 256
ROW_CHUNK = 128


def _attn_prompt_kernel(q_ref, gn_ref, skv_ref, wkv_ref, ab_ref, cosc_ref, sinc_ref, o_ref,
                        kc_ref, vc_ref, bias_ref, m_ref, acc_ref, out_ref, hs_ref, *, t_len):
    qb = pl.program_id(1)
    n_chunk = t_len // CMP_STRIDE
    n_cmp = (t_len - CMP_BLOCK) // CMP_STRIDE + 1
    n_sel = -(-t_len // SEL_BLOCK)
    sel_rows = -(-n_sel // SUBLANES) * SUBLANES
    qt, kt_sz, rc = Q_TILE, K_TILE, ROW_CHUNK
    n_rc = qt // rc

    @pl.when(qb == 0)
    def _():
        kc, vc = _compressed_kv(ab_ref[...], cosc_ref[...], sinc_ref[...])
        kc_ref[...] = kc
        vc_ref[...] = vc

    q0 = qb * qt
    qpos = q0 + lax.broadcasted_iota(jnp.int32, (qt, 1), 0)
    qpos_row = q0 + lax.broadcasted_iota(jnp.int32, (1, qt), 1)
    lane = lax.broadcasted_iota(jnp.int32, (rc, LANES), 1)

    for g in range(N_KV):
        in_group = (lane >= g * HEAD_DIM) & (lane < (g + 1) * HEAD_DIM)

        def q_chunk(h, c):
            return q_ref[c * rc:(c + 1) * rc, (g * HPG + h) * LANES:(g * HPG + h + 1) * LANES]

        def rows(h, c):
            return slice(h * qt + c * rc, h * qt + (c + 1) * rc)

        def gate(branch, h, c):
            col = branch * N_HEADS + g * HPG + h
            return gn_ref[c * rc:(c + 1) * rc, col:col + 1]

        qz = jnp.concatenate([q_ref[:, (g * HPG + h) * LANES:(g * HPG + h + 1) * LANES] for h in range(HPG)], axis=0)
        acc_ref[...] = _dot_nt(qz, kc_ref[...])
        for c in range(n_rc):
            qpos_c = q0 + c * rc + lax.broadcasted_iota(jnp.int32, (rc, 1), 0)
            ncol = lax.broadcasted_iota(jnp.int32, (rc, n_chunk), 1)
            mask_c = (ncol * CMP_STRIDE + CMP_BLOCK - 1 <= qpos_c) & (ncol < n_cmp)
            hs = jnp.zeros((rc, n_chunk), F32)
            for h in range(HPG):
                p_c, l_c = _softmax_parts(acc_ref[rows(h, c), :], mask_c)
                p_c = p_c / l_c
                hs = hs + p_c
                m_ref[rows(h, c), :] = p_c
            hs_ref[c * rc:(c + 1) * rc, :] = hs
        out_ref[...] = _dot(m_ref[...].astype(BF16), vc_ref[...])
        for h in range(HPG):
            for c in range(n_rc):
                out_ref[rows(h, c), :] = gate(0, h, c) * out_ref[rows(h, c), :]

        imp_t = lax.dot_general(_overlap(n_cmp, 1), hs_ref[...], (((1,), (1,)), ((), ())),
                                preferred_element_type=F32, precision=lax.Precision.HIGHEST)
        sel_t = _select_blocks(imp_t[0:sel_rows], qpos_row, n_sel, 0).astype(F32)
        sel_t = jnp.concatenate([sel_t, jnp.zeros((LANES - sel_rows, qt), F32)], axis=0)
        sel = sel_t.T.astype(BF16)

        def sel_bias(k_idx, kpos):
            jrow = lax.broadcasted_iota(jnp.int32, (LANES, kt_sz), 0)
            kcol = lax.broadcasted_iota(jnp.int32, (LANES, kt_sz), 1)
            expand = (jrow == k_idx * (kt_sz // SEL_BLOCK) + lax.shift_right_logical(kcol, 6)).astype(BF16)
            return jnp.where((_dot(sel, expand) > 0.5) & (kpos <= qpos), 0.0, NEG)

        def win_bias(k_idx, kpos):
            return jnp.where((kpos <= qpos) & (kpos > qpos - WINDOW), 0.0, NEG)

        def run_branch(kv_ref, n_tiles, tile_of, bias_fn, branch):
            m_ref[...] = jnp.full(m_ref.shape, NEG, F32)
            acc_ref[...] = jnp.zeros(acc_ref.shape, F32)

            def body(i, carry):
                k_idx = tile_of(i)
                k0 = pl.multiple_of(k_idx * kt_sz, kt_sz)
                kslab = kv_ref[0, 0:LANES, pl.ds(k0, kt_sz)].astype(BF16)
                vrow = lax.broadcasted_iota(jnp.int32, (LANES, kt_sz), 0)
                v_own = (vrow >= g * HEAD_DIM) & (vrow < (g + 1) * HEAD_DIM)
                vaug = jnp.where(v_own, kv_ref[0, LANES:KV_COLS, pl.ds(k0, kt_sz)], 1.0).astype(BF16)
                kpos = k0 + lax.broadcasted_iota(jnp.int32, (qt, kt_sz), 1)
                bias_ref[...] = bias_fn(k_idx, kpos)
                for h in range(HPG):
                    for c in range(n_rc):
                        r = rows(h, c)
                        sb = _dot(q_chunk(h, c), kslab) + bias_ref[c * rc:(c + 1) * rc, :]
                        m_old = m_ref[r, :]
                        halves = [sb[:, j * LANES:(j + 1) * LANES] for j in range(kt_sz // LANES)]
                        top = functools.reduce(jnp.maximum, halves)
                        m_new = jnp.maximum(m_old, jnp.max(top, axis=-1, keepdims=True))
                        p = jnp.concatenate([jnp.exp(s_j - m_new) for s_j in halves], axis=1)
                        acc_ref[r, :] = jnp.exp(m_old - m_new) * acc_ref[r, :] + _dot_nt(p.astype(BF16), vaug)
                        m_ref[r, :] = m_new
                return carry

            lax.fori_loop(0, n_tiles, body, 0)
            for h in range(HPG):
                for c in range(n_rc):
                    r = rows(h, c)
                    acc = acc_ref[r, :]
                    o_b = jnp.where(in_group, acc / pltpu.roll(acc, HEAD_DIM, 1), 0.0)
                    out_ref[r, :] = out_ref[r, :] + gate(branch, h, c) * o_b

        run_branch(skv_ref, qb + 1, lambda i: i, sel_bias, 1)
        run_branch(wkv_ref, jnp.minimum(qb, WINDOW // kt_sz) + 1, lambda i: qb - i, win_bias, 2)

        for h in range(HPG):
            hh = g * HPG + h
            o_ref[:, hh * LANES:(hh + 1) * LANES] = out_ref[h * qt:(h + 1) * qt, :].astype(BF16)


def _attn_prompt(q2, gn2, skv_t, wkv_t, ab2, cosc, sinc, b, t_len):
    n_chunk = t_len // CMP_STRIDE
    nqb = t_len // Q_TILE
    row = lambda i, j: (i * nqb + j, 0)
    per_b3 = lambda i, j: (i, 0, 0)
    const2 = lambda i, j: (0, 0)
    return pl.pallas_call(
        functools.partial(_attn_prompt_kernel, t_len=t_len),
        grid=(b, nqb),
        in_specs=[
            pl.BlockSpec((Q_TILE, Q_EXP), row),
            pl.BlockSpec((Q_TILE, LANES), row),
            pl.BlockSpec((1, KV_COLS, t_len), per_b3),
            pl.BlockSpec((1, KV_COLS, t_len), per_b3),
            pl.BlockSpec((n_chunk, 4 * LANES), lambda i, j: (i, 0)),
            pl.BlockSpec((n_chunk, LANES), const2),
            pl.BlockSpec((n_chunk, LANES), const2),
        ],
        out_specs=pl.BlockSpec((Q_TILE, Q_EXP), row),
        out_shape=jax.ShapeDtypeStruct((b * t_len, Q_EXP), BF16),
        scratch_shapes=[
            pltpu.VMEM((n_chunk, LANES), BF16),
            pltpu.VMEM((n_chunk, LANES), BF16),
            pltpu.VMEM((Q_TILE, K_TILE), F32),
            pltpu.VMEM((HPG * Q_TILE, LANES), F32),
            pltpu.VMEM((HPG * Q_TILE, LANES), F32),
            pltpu.VMEM((HPG * Q_TILE, LANES), F32),
            pltpu.VMEM((Q_TILE, n_chunk), F32),
        ],
        compiler_params=_params(2),
    )(q2, gn2, skv_t, wkv_t, ab2, cosc, sinc)


Q_ROWS = 8
SEQ_PER_STEP = 4


def _attn_sample_kernel(pt_ref, q_ref, gn_ref, sknew_ref, wknew_ref, win_ref, cosc_ref, sinc_ref, *rest,
                        past_len, tq, n_pages, n_seq):
    del pt_ref
    sel_pages = [rest[e * n_pages:(e + 1) * n_pages] for e in range(n_seq)]
    ab_pages = [rest[(n_seq + e) * n_pages:(n_seq + e + 1) * n_pages] for e in range(n_seq)]
    o_ref = rest[2 * n_seq * n_pages]
    seqs = range(n_seq)
    t_all = past_len + tq
    n_chunk = past_len // CMP_STRIDE
    n_cmp = (t_all - CMP_BLOCK) // CMP_STRIDE + 1
    n_sel = -(-t_all // SEL_BLOCK)
    rows = N_HEADS * Q_ROWS
    all_rows = n_seq * rows
    win_buf = win_ref.shape[3]

    def stack(parts):
        return jnp.concatenate(parts, axis=0)

    def per_seq(a, e):
        return a[e * rows:(e + 1) * rows]

    qbd = [stack([q_ref[e, :, hh * LANES:(hh + 1) * LANES] for hh in range(N_HEADS)]) for e in seqs]
    qpos = past_len + lax.broadcasted_iota(jnp.int32, (all_rows, 1), 0) % Q_ROWS

    cosc = cosc_ref[...]
    sinc = sinc_ref[...]
    kvc = [_compressed_kv(stack([r[0] for r in ab_pages[e]]), cosc, sinc) for e in seqs]
    ncol = lax.broadcasted_iota(jnp.int32, (all_rows, n_chunk), 1)
    mask_c = (ncol * CMP_STRIDE + CMP_BLOCK - 1 <= qpos) & (ncol < n_cmp)
    p_c, l_c = _softmax_parts(stack([_dot_nt(qbd[e], kvc[e][0]) for e in seqs]), mask_c)
    p_c = p_c / l_c
    p_c16 = p_c.astype(BF16)
    o_c = stack([_dot(per_seq(p_c16, e), kvc[e][1]) for e in seqs])

    hs = []
    for e in seqs:
        for g in range(N_KV):
            r0 = e * rows + g * HPG * Q_ROWS
            acc = p_c[r0:r0 + Q_ROWS]
            for h in range(1, HPG):
                acc = acc + p_c[r0 + h * Q_ROWS:r0 + (h + 1) * Q_ROWS]
            hs.append(acc)
    imp = jnp.dot(stack(hs), _overlap(n_cmp, 0), preferred_element_type=F32, precision=lax.Precision.HIGHEST)
    sel_g = _select_blocks(imp, qpos[0:n_seq * N_KV * Q_ROWS], n_sel, 1).astype(BF16)
    sel_rows = stack([sel_g[(e * N_KV + g) * Q_ROWS:(e * N_KV + g + 1) * Q_ROWS]
                      for e in seqs for g in range(N_KV) for _ in range(HPG)])

    def new_rows_tile(ref, e, c0):
        return stack([ref[e, :, c0:c0 + LANES], jnp.zeros((LANES - Q_ROWS, LANES), F32)]).astype(BF16)

    n_keys = (n_pages + 1) * PAGE_SIZE
    s_s = stack([jnp.concatenate([_dot(qbd[e], r[0, 0].astype(BF16)) for r in sel_pages[e]]
                                 + [_dot_nt(qbd[e], new_rows_tile(sknew_ref, e, 0))], axis=1) for e in seqs])
    jrow = lax.broadcasted_iota(jnp.int32, (LANES, n_keys), 0)
    kcol = lax.broadcasted_iota(jnp.int32, (LANES, n_keys), 1)
    expand = (jrow == lax.shift_right_logical(kcol, 6)).astype(BF16)
    kpos = lax.broadcasted_iota(jnp.int32, (all_rows, n_keys), 1)
    mask_s = (_dot(sel_rows, expand) > 0.5) & (kpos <= qpos)
    p_s, l_s = _softmax_parts(s_s, mask_s)
    p_s = p_s.astype(BF16)
    o_s = []
    for e in seqs:
        p_e = per_seq(p_s, e)
        acc = _dot(p_e[:, n_pages * PAGE_SIZE:], new_rows_tile(sknew_ref, e, LANES))
        for i in range(n_pages):
            acc = acc + _dot_nt(p_e[:, i * PAGE_SIZE:(i + 1) * PAGE_SIZE], sel_pages[e][i][0, 1].astype(BF16))
        o_s.append(acc)
    o_s = stack(o_s) / l_s

    s_w = stack([jnp.concatenate([_dot(qbd[e], win_ref[e, 0].astype(BF16)),
                                  _dot_nt(qbd[e], new_rows_tile(wknew_ref, e, 0))], axis=1) for e in seqs])
    wcol = lax.broadcasted_iota(jnp.int32, (all_rows, win_buf + LANES), 1)
    kpos_w = past_len - win_buf + wcol
    mask_w = (kpos_w <= qpos) & (kpos_w > qpos - WINDOW)
    p_w, l_w = _softmax_parts(s_w, mask_w)
    p_w = p_w.astype(BF16)
    o_w = stack([_dot_nt(per_seq(p_w, e)[:, 0:win_buf], win_ref[e, 1].astype(BF16))
                 + _dot(per_seq(p_w, e)[:, win_buf:], new_rows_tile(wknew_ref, e, LANES)) for e in seqs]) / l_w

    for e in seqs:
        gn = gn_ref[e]
        for hh in range(N_HEADS):
            r = slice(e * rows + hh * Q_ROWS, e * rows + (hh + 1) * Q_ROWS)
            out = (gn[:, hh:hh + 1] * o_c[r]
                   + gn[:, N_HEADS + hh:N_HEADS + hh + 1] * o_s[r]
                   + gn[:, 2 * N_HEADS + hh:2 * N_HEADS + hh + 1] * o_w[r])
            o_ref[e, :, hh * LANES:(hh + 1) * LANES] = out.astype(BF16)


def _attn_sample(pt_flat, q3, gn3, sknew3, wknew3, win4, cosc, sinc, sel_pool4, ab_pool3,
                 layer, n_pool, bd, past_len, tq):
    n_pages = past_len // PAGE_SIZE
    win_buf = win4.shape[3]
    n_chunk = past_len // CMP_STRIDE
    n_seq = _pick_tile(bd, SEQ_PER_STEP, 1)
    per_b = lambda i, pt: (i, 0, 0)
    const2 = lambda i, pt: (0, 0)
    sel_specs = [pl.BlockSpec((1, 2, LANES, PAGE_SIZE),
                              lambda i, pt, e=e, p=p: (layer * n_pool + pt[(i * n_seq + e) * n_pages + p], 0, 0, 0))
                 for e in range(n_seq) for p in range(n_pages)]
    ab_specs = [pl.BlockSpec((1, CHUNKS_PER_PAGE, 4 * LANES),
                             lambda i, pt, e=e, p=p: (pt[(i * n_seq + e) * n_pages + p], 0, 0))
                for e in range(n_seq) for p in range(n_pages)]
    grid_spec = pltpu.PrefetchScalarGridSpec(
        num_scalar_prefetch=1,
        grid=(bd // n_seq,),
        in_specs=[
            pl.BlockSpec((n_seq, Q_ROWS, Q_EXP), per_b),
            pl.BlockSpec((n_seq, Q_ROWS, LANES), per_b),
            pl.BlockSpec((n_seq, Q_ROWS, KV_COLS), per_b),
            pl.BlockSpec((n_seq, Q_ROWS, KV_COLS), per_b),
            pl.BlockSpec((n_seq, 2, LANES, win_buf), lambda i, pt: (layer * (bd // n_seq) + i, 0, 0, 0)),
            pl.BlockSpec((n_chunk, LANES), const2),
            pl.BlockSpec((n_chunk, LANES), const2),
        ] + sel_specs + ab_specs,
        out_specs=pl.BlockSpec((n_seq, Q_ROWS, Q_EXP), per_b),
    )
    n_refs = n_seq * n_pages
    return pl.pallas_call(
        functools.partial(_attn_sample_kernel, past_len=past_len, tq=tq, n_pages=n_pages, n_seq=n_seq),
        grid_spec=grid_spec,
        out_shape=jax.ShapeDtypeStruct((bd, Q_ROWS, Q_EXP), BF16),
        compiler_params=_params(1),
    )(pt_flat, q3, gn3, sknew3, wknew3, win4, cosc, sinc, *([sel_pool4] * n_refs), *([ab_pool3] * n_refs))


def _merge_kernel(c_ref, o_ref, gm_ref, x_ref, wc_ref, wn_ref, wo_ref, y_ref):
    conv_out = _dot(c_ref[...], wc_ref[...])
    nsa_out = _dot(o_ref[...], wn_ref[...])
    gm = gm_ref[...].astype(F32)
    merged = gm[:, 0:D_MODEL] * conv_out + gm[:, D_MODEL:2 * D_MODEL] * nsa_out
    y_ref[...] = x_ref[...] + _dot(merged.astype(BF16), wo_ref[...])


def _merge(c2, o2, gm2, x2, wc, wn, wo, tm):
    m = x2.shape[0]
    row = lambda i: (i, 0)
    const = lambda i: (0, 0)
    return pl.pallas_call(
        _merge_kernel,
        grid=(m // tm,),
        in_specs=[
            pl.BlockSpec((tm, D_CONV), row),
            pl.BlockSpec((tm, Q_EXP), row),
            pl.BlockSpec((tm, 2 * D_MODEL), row),
            pl.BlockSpec((tm, D_MODEL), row),
            pl.BlockSpec(wc.shape, const),
            pl.BlockSpec(wn.shape, const),
            pl.BlockSpec(wo.shape, const),
        ],
        out_specs=pl.BlockSpec((tm, D_MODEL), row),
        out_shape=jax.ShapeDtypeStruct((m, D_MODEL), F32),
        compiler_params=_params(1),
    )(c2, o2, gm2, x2, wc, wn, wo)


FF_TILE = 1024


def _mlp_kernel(x_ref, g_ref, wu_ref, wd_ref, gf_ref, y_ref, *, final):
    x = x_ref[...]
    h = _rms_bf16(x, g_ref[...])
    acc = x
    for f in range(D_FF // FF_TILE):
        up = jnp.maximum(_dot(h, wu_ref[:, f * FF_TILE:(f + 1) * FF_TILE]), 0.0)
        acc = acc + _dot((up * up).astype(BF16), wd_ref[f * FF_TILE:(f + 1) * FF_TILE, :])
    if final:
        ms2 = jnp.mean(acc * acc, axis=-1, keepdims=True)
        acc = acc * lax.rsqrt(ms2 + EPS) * gf_ref[...]
    y_ref[...] = acc


def _mlp(x2, g, wu, wd, gf, tm, final):
    m = x2.shape[0]
    row = lambda i: (i, 0)
    const = lambda i: (0, 0)
    return pl.pallas_call(
        functools.partial(_mlp_kernel, final=final),
        grid=(m // tm,),
        in_specs=[
            pl.BlockSpec((tm, D_MODEL), row),
            pl.BlockSpec((1, D_MODEL), const),
            pl.BlockSpec(wu.shape, const),
            pl.BlockSpec(wd.shape, const),
            pl.BlockSpec((1, D_MODEL), const),
        ],
        out_specs=pl.BlockSpec((tm, D_MODEL), row),
        out_shape=jax.ShapeDtypeStruct((m, D_MODEL), F32),
        compiler_params=_params(1),
    )(x2, g, wu, wd, gf)


def _rope_tables(pos):
    half = HEAD_DIM // 2
    inv = jnp.power(ROPE_THETA, -jnp.arange(half, dtype=F32) / half)
    ang = pos.astype(F32)[:, None] * inv[None, :]
    cos = jnp.cos(ang)
    sin = jnp.sin(ang)
    return jnp.concatenate([cos, cos, cos, cos], axis=1), jnp.concatenate([-sin, sin, -sin, sin], axis=1)


def _pack_w_in(w):
    o0 = 2 * D_CONV
    o1 = o0 + N_HEADS * HEAD_DIM
    o4 = o1 + 3 * KV_COLS
    o5 = o4 + N_GATES
    wq = w[:, o0:o1].reshape(D_MODEL, N_HEADS, HEAD_DIM) * (HEAD_DIM ** -0.5)
    zero = jnp.zeros_like(wq)
    in_g0 = (jnp.arange(N_HEADS) < HPG)[None, :, None]
    wq_exp = jnp.concatenate([jnp.where(in_g0, wq, zero), jnp.where(in_g0, zero, wq)], axis=2)
    wgn = jnp.pad(w[:, o4:o5], ((0, 0), (0, LANES - N_GATES)))
    packed = jnp.concatenate([w[:, :o0], wq_exp.reshape(D_MODEL, Q_EXP), w[:, o1:o4], wgn, w[:, o5:]], axis=1)
    return packed.astype(BF16), w[:, o1:o4].T.astype(BF16)


def _pack_w_nsa_out(w):
    wh = w.reshape(N_HEADS, HEAD_DIM, D_MODEL)
    zero = jnp.zeros_like(wh)
    in_g0 = (jnp.arange(N_HEADS) < HPG)[:, None, None]
    return jnp.concatenate([jnp.where(in_g0, wh, zero), jnp.where(in_g0, zero, wh)], axis=1).reshape(Q_EXP, D_MODEL).astype(BF16)


def _pack_w_cmp(w_kv, pos_kv):
    lo, hi = w_kv[:CMP_STRIDE], w_kv[CMP_STRIDE:]
    eye = jnp.eye(N_KV, dtype=w_kv.dtype)
    blk = lambda part: jnp.einsum('lde,gh->lgdhe', part, eye).reshape(CMP_STRIDE * N_KV * HEAD_DIM, N_KV * HEAD_DIM)
    w2 = jnp.concatenate([blk(lo), blk(hi)], axis=1).astype(BF16)
    tile = lambda p: jnp.broadcast_to(p[:, None, :], (CMP_STRIDE, N_KV, HEAD_DIM)).reshape(1, -1)
    p2 = jnp.concatenate([tile(pos_kv[:CMP_STRIDE]), tile(pos_kv[CMP_STRIDE:]),
                          jnp.zeros((SUBLANES - 2, CMP_STRIDE * N_KV * HEAD_DIM), pos_kv.dtype)], axis=0).astype(BF16)
    return w2, p2


def _feature_major(a):
    lead = a.shape[:-4]
    n = len(lead)
    a = jnp.transpose(a, tuple(range(n)) + (n + 1, n + 2, n + 3, n))
    return a.reshape(lead + (2, N_KV * HEAD_DIM, a.shape[-1]))


def _position_major(a_t, rows):
    b = a_t.shape[0]
    return jnp.transpose(a_t.reshape(b, 2, N_KV, HEAD_DIM, rows), (0, 4, 1, 2, 3))


def kernel(x_prompt, x_sample, cache_cmp_kv, cache_sel_kv, state_win_kv, state_conv, page_table, norm_mix_g, w_in, conv_dw_w, conv_dw_b, conv_ln_g, conv_ln_b, w_conv_out, cmp_pos, w_cmp, w_nsa_out, w_out, norm_mlp_g, w_up, w_down, norm_final_g):
    depth = w_in.shape[0]
    bp, tp, _ = x_prompt.shape
    bd, tq, _ = x_sample.shape
    n_pool = cache_cmp_kv.shape[1]
    n_pages = page_table.shape[1]
    past_len = n_pages * PAGE_SIZE
    win_buf = state_win_kv.shape[2]
    assert tp % Q_TILE == 0 and tp // CMP_STRIDE == LANES and tp >= WINDOW
    assert past_len // CMP_STRIDE == LANES and win_buf == WINDOW and tq <= Q_ROWS

    mp, ms = bp * tp, bd * tq
    tm_p = _pick_tile(tp, 512)
    tm_s = _pick_tile(ms, 256)
    assert tp % tm_p == 0 and tm_s % tq == 0

    cos_p, sin_p = _rope_tables(jnp.arange(tp))
    cos_pt, sin_pt = cos_p.T, sin_p.T
    cos_s, sin_s = _rope_tables(past_len + jnp.arange(tm_s) % tq)
    n_chunk = tp // CMP_STRIDE
    cos_c, sin_c = _rope_tables(jnp.arange(n_chunk) * CMP_STRIDE + CMP_BLOCK - 1)

    pt_flat = page_table.reshape(-1).astype(jnp.int32)
    cmp_pool4 = _feature_major(cache_cmp_kv).reshape(depth * n_pool, 2, LANES, PAGE_SIZE)
    sel_pool4 = _feature_major(cache_sel_kv).reshape(depth * n_pool, 2, LANES, PAGE_SIZE)
    win4 = _feature_major(state_win_kv).reshape(depth * bd, 2, LANES, win_buf)
    zeros_conv = jnp.zeros((bp, CONV_K - 1, D_CONV), F32)

    xp = x_prompt.reshape(mp, D_MODEL)
    xs = x_sample.reshape(ms, D_MODEL)
    outs = [[] for _ in range(8)]
    for l in range(depth):
        w_packed, wkv_t = _pack_w_in(w_in[l])
        wn = _pack_w_nsa_out(w_nsa_out[l])
        wc, wo = w_conv_out[l].astype(BF16), w_out[l].astype(BF16)
        wu, wd = w_up[l].astype(BF16), w_down[l].astype(BF16)
        wk2, pk2 = _pack_w_cmp(w_cmp[l, 0], cmp_pos[l, 0])
        wv2, pv2 = _pack_w_cmp(w_cmp[l, 1], cmp_pos[l, 1])
        g_mix, g_mlp = norm_mix_g[l][None], norm_mlp_g[l][None]
        dw_b, ln_g, ln_b = conv_dw_b[l][None], conv_ln_g[l][None], conv_ln_b[l][None]
        gf = norm_final_g[None]
        final = l == depth - 1

        u, q, ckv_t, skv_t, wkv_tm, gn, gm = _inproj_cols(xp, g_mix, w_packed, wkv_t, cos_p, sin_p, cos_pt, sin_pt,
                                                          bp, tp, tm_p)
        c_act = _conv_prompt(u.reshape(bp, tp, D_CONV), zeros_conv, conv_dw_w[l], dw_b, ln_g, ln_b)
        ab = _chunkproj(ckv_t.reshape(bp, 2, LANES, tp), wk2, wv2, pk2, pv2, 0, 0, True)
        o = _attn_prompt(q, gn, skv_t, wkv_tm, ab, cos_c, sin_c, bp, tp)
        xp = _merge(c_act.reshape(mp, D_CONV), o, gm, xp, wc, wn, wo, tm_p)
        xp = _mlp(xp, g_mlp, wu, wd, gf, _pick_tile(mp, 512), final)
        keep = min(WINDOW, tp)
        outs[0].append(_position_major(ckv_t, tp))
        outs[2].append(_position_major(skv_t, tp))
        outs[4].append(_position_major(wkv_tm[:, :, tp - keep:], keep))
        outs[6].append(u.reshape(bp, tp, D_CONV)[:, tp - (CONV_K - 1):])

        u, q, ckv, skv, wkv, gn, gm = _inproj_rows(xs, g_mix, w_packed, cos_s, sin_s, tm_s)
        u3 = u.reshape(bd, tq, D_CONV)
        c_t = _conv_sample(jnp.swapaxes(state_conv[l], 0, 1), jnp.swapaxes(u3, 0, 1), conv_dw_w[l], dw_b, ln_g, ln_b)
        c_act = jnp.swapaxes(c_t, 0, 1).reshape(ms, D_CONV)
        ab_pool = _chunkproj(cmp_pool4, wk2, wv2, pk2, pv2, l * n_pool, n_pool, False)
        pad_q = lambda a: jnp.pad(a.reshape(bd, tq, -1), ((0, 0), (0, Q_ROWS - tq), (0, 0)))
        o = _attn_sample(pt_flat, pad_q(q), pad_q(gn), pad_q(skv), pad_q(wkv), win4, cos_c, sin_c, sel_pool4,
                         ab_pool.reshape(n_pool, CHUNKS_PER_PAGE, 4 * LANES), l, n_pool, bd, past_len, tq)
        xs = _merge(c_act, o[:, :tq].reshape(ms, Q_EXP), gm, xs, wc, wn, wo, tm_s)
        xs = _mlp(xs, g_mlp, wu, wd, gf, _pick_tile(ms, 512), final)
        wkv5 = wkv.reshape(bd, tq, 2, N_KV, HEAD_DIM)
        outs[1].append(ckv.reshape(bd, tq, 2, N_KV, HEAD_DIM))
        outs[3].append(skv.reshape(bd, tq, 2, N_KV, HEAD_DIM))
        outs[5].append(wkv5)
        outs[7].append(u3)

    y_prompt = xp.reshape(bp, tp, D_MODEL)
    y_sample = xs.reshape(bd, tq, D_MODEL)
    outs = [jnp.stack(o) for o in outs]
    outs[5] = jnp.concatenate([state_win_kv[:, :, tq:], outs[5]], axis=2)
    outs[7] = jnp.concatenate([state_conv[:, :, tq:], outs[7]], axis=2)
    return (y_prompt, y_sample) + tuple(outs)
```

```python
import functools

import jax
import jax.numpy as jnp
from jax import lax
from jax.experimental import pallas as pl
from jax.experimental.pallas import tpu as pltpu

D_MODEL = 1024
D_CONV = D_MODEL // 2
CONV_K = 31
N_HEADS = 8
HEAD_DIM = 64
N_KV = 2
HPG = N_HEADS // N_KV
KV_COLS = 2 * N_KV * HEAD_DIM
CMP_BLOCK = 32
CMP_STRIDE = 16
SEL_BLOCK = 64
N_SEL = 16
WINDOW = 512
D_FF = 4 * D_MODEL
ROPE_THETA = 10000.0
EPS = 1e-6
PAGE_SIZE = 128

LANES = 128
SUBLANES = 8
Q_EXP = N_HEADS * LANES
N_GATES = 3 * N_HEADS
CHUNKS_PER_PAGE = PAGE_SIZE // CMP_STRIDE
NEG = -1e30

O_GLU = 0
O_Q = O_GLU + 2 * D_CONV
O_CKV = O_Q + Q_EXP
O_SKV = O_CKV + KV_COLS
O_WKV = O_SKV + KV_COLS
O_GN = O_WKV + KV_COLS
O_GM = O_GN + LANES
IN_COLS_PACKED = O_GM + 2 * D_MODEL

VMEM_LIMIT = 56 * 1024 * 1024

F32 = jnp.float32
BF16 = jnp.bfloat16


def _params(n_axes, vmem=VMEM_LIMIT):
    return pltpu.CompilerParams(dimension_semantics=("arbitrary",) * n_axes, vmem_limit_bytes=vmem)


def _sigmoid(x):
    return 1.0 / (1.0 + jnp.exp(-x))


def _dot(a, b):
    return jnp.dot(a, b, preferred_element_type=F32)


def _dot_nt(a, b):
    return lax.dot_general(a, b, (((1,), (1,)), ((), ())), preferred_element_type=F32)


def _rope_slab(xs, cos, sin_signed):
    lane = lax.broadcasted_iota(jnp.int32, xs.shape, 1)
    first = (lane % HEAD_DIM) < (HEAD_DIM // 2)
    rot = jnp.where(first, pltpu.roll(xs, LANES - HEAD_DIM // 2, 1), pltpu.roll(xs, HEAD_DIM // 2, 1))
    return xs * cos + rot * sin_signed


def _rope_rows(xt, cos_t, sin_t):
    half = HEAD_DIM // 2
    rot = jnp.concatenate([xt[half:2 * half], xt[0:half], xt[3 * half:4 * half], xt[2 * half:3 * half]], axis=0)
    return xt * cos_t + rot * sin_t


def _pick_tile(n, cap, mult=8):
    t = min(n, cap)
    while n % t or t % mult:
        t -= 1
    return t


def _rms_bf16(x, g):
    ms = jnp.mean(x * x, axis=-1, keepdims=True)
    return (x * lax.rsqrt(ms + EPS) * g).astype(BF16)


def _inproj_common(h, w_ref, cos, sin, u_ref, q_ref, gn_ref, gm_ref):
    def proj(a, n):
        return _dot(h, w_ref[:, a:a + n])

    u_ref[...] = proj(O_GLU, D_CONV) * _sigmoid(proj(O_GLU + D_CONV, D_CONV))
    for s in range(N_HEADS):
        q_ref[:, s * LANES:(s + 1) * LANES] = _rope_slab(proj(O_Q + s * LANES, LANES), cos, sin).astype(BF16)
    gn_ref[...] = _sigmoid(proj(O_GN, LANES))
    gm_ref[...] = _sigmoid(proj(O_GM, 2 * D_MODEL)).astype(BF16)
    return proj


def _inproj_rows_kernel(x_ref, g_ref, w_ref, cos_ref, sin_ref,
                        u_ref, q_ref, ckv_ref, skv_ref, wkv_ref, gn_ref, gm_ref):
    h = _rms_bf16(x_ref[...], g_ref[...])
    cos = cos_ref[...]
    sin = sin_ref[...]
    proj = _inproj_common(h, w_ref, cos, sin, u_ref, q_ref, gn_ref, gm_ref)
    ckv_ref[...] = proj(O_CKV, KV_COLS)
    skv_ref[:, 0:LANES] = _rope_slab(proj(O_SKV, LANES), cos, sin)
    skv_ref[:, LANES:KV_COLS] = proj(O_SKV + LANES, LANES)
    wkv_ref[:, 0:LANES] = _rope_slab(proj(O_WKV, LANES), cos, sin)
    wkv_ref[:, LANES:KV_COLS] = proj(O_WKV + LANES, LANES)


GATE_ROWS = 32
R_Q = 0
R_CKV = R_Q + Q_EXP
R_SKV = R_CKV + KV_COLS
R_WKV = R_SKV + KV_COLS
R_GN = R_WKV + KV_COLS
ROWS_T = R_GN + GATE_ROWS


def _inproj_cols_kernel(x_ref, g_ref, w_ref, wt_ref, cos_ref, sin_ref, cost_ref, sint_ref,
                        u_ref, q_ref, ckv_ref, skv_ref, wkv_ref, skr_ref, wkr_ref, gn_ref, gm_ref):
    h = _rms_bf16(x_ref[...], g_ref[...])
    cos = cos_ref[...]
    sin = sin_ref[...]
    cos_t = cost_ref[...]
    sin_t = sint_ref[...]

    def proj(a, n):
        return _dot(h, w_ref[:, a:a + n])

    t_all = _dot_nt(wt_ref[...], h)

    def proj_t(a, n):
        return t_all[a:a + n]

    u_ref[...] = proj(O_GLU, D_CONV) * _sigmoid(proj(O_GLU + D_CONV, D_CONV))
    gm_ref[...] = _sigmoid(proj(O_GM, 2 * D_MODEL)).astype(BF16)
    skr_ref[...] = _rope_slab(proj(O_SKV, LANES), cos, sin).astype(BF16)
    wkr_ref[...] = _rope_slab(proj(O_WKV, LANES), cos, sin).astype(BF16)
    for s in range(N_HEADS):
        q_ref[0, s * LANES:(s + 1) * LANES, :] = _rope_rows(proj_t(R_Q + s * LANES, LANES), cos_t, sin_t).astype(BF16)
    ckv_ref[0] = proj_t(R_CKV, KV_COLS)
    skv_ref[0, 0:LANES] = _rope_rows(proj_t(R_SKV, LANES), cos_t, sin_t)
    skv_ref[0, LANES:KV_COLS] = proj_t(R_SKV + LANES, LANES)
    wkv_ref[0, 0:LANES] = _rope_rows(proj_t(R_WKV, LANES), cos_t, sin_t)
    wkv_ref[0, LANES:KV_COLS] = proj_t(R_WKV + LANES, LANES)
    gn_ref[0] = _sigmoid(proj_t(R_GN, GATE_ROWS))


def _inproj_rows(x2, g, w_packed, cos_t, sin_t, tm):
    m = x2.shape[0]
    row = lambda i: (i, 0)
    const = lambda i: (0, 0)
    out_shapes = (
        jax.ShapeDtypeStruct((m, D_CONV), F32),
        jax.ShapeDtypeStruct((m, Q_EXP), BF16),
        jax.ShapeDtypeStruct((m, KV_COLS), F32),
        jax.ShapeDtypeStruct((m, KV_COLS), F32),
        jax.ShapeDtypeStruct((m, KV_COLS), F32),
        jax.ShapeDtypeStruct((m, LANES), F32),
        jax.ShapeDtypeStruct((m, 2 * D_MODEL), BF16),
    )
    return pl.pallas_call(
        _inproj_rows_kernel,
        grid=(m // tm,),
        in_specs=[
            pl.BlockSpec((tm, D_MODEL), row),
            pl.BlockSpec((1, D_MODEL), const),
            pl.BlockSpec(w_packed.shape, const),
            pl.BlockSpec((tm, LANES), const),
            pl.BlockSpec((tm, LANES), const),
        ],
        out_specs=tuple(pl.BlockSpec((tm, s.shape[1]), row) for s in out_shapes),
        out_shape=out_shapes,
        compiler_params=_params(1),
    )(x2, g, w_packed, cos_t, sin_t)


def _inproj_cols(x2, g, w_packed, wkv_t, cos_t, sin_t, cos_tt, sin_tt, b, t_len, tm):
    m = x2.shape[0]
    n_tab = t_len // tm
    row = lambda i: (i, 0)
    const = lambda i: (0, 0)
    kv_map = lambda i: (i // n_tab, 0, i % n_tab)
    kv_shape = jax.ShapeDtypeStruct((b, KV_COLS, t_len), F32)
    out_shapes = (
        jax.ShapeDtypeStruct((m, D_CONV), F32),
        jax.ShapeDtypeStruct((b, Q_EXP, t_len), BF16),
        kv_shape, kv_shape, kv_shape,
        jax.ShapeDtypeStruct((m, LANES), BF16),
        jax.ShapeDtypeStruct((m, LANES), BF16),
        jax.ShapeDtypeStruct((b, GATE_ROWS, t_len), F32),
        jax.ShapeDtypeStruct((m, 2 * D_MODEL), BF16),
    )
    out_specs = (
        pl.BlockSpec((tm, D_CONV), row),
        pl.BlockSpec((1, Q_EXP, tm), kv_map),
        pl.BlockSpec((1, KV_COLS, tm), kv_map),
        pl.BlockSpec((1, KV_COLS, tm), kv_map),
        pl.BlockSpec((1, KV_COLS, tm), kv_map),
        pl.BlockSpec((tm, LANES), row),
        pl.BlockSpec((tm, LANES), row),
        pl.BlockSpec((1, GATE_ROWS, tm), kv_map),
        pl.BlockSpec((tm, 2 * D_MODEL), row),
    )
    return pl.pallas_call(
        _inproj_cols_kernel,
        grid=(m // tm,),
        in_specs=[
            pl.BlockSpec((tm, D_MODEL), row),
            pl.BlockSpec((1, D_MODEL), const),
            pl.BlockSpec(w_packed.shape, const),
            pl.BlockSpec(wkv_t.shape, const),
            pl.BlockSpec((tm, LANES), lambda i: (i % n_tab, 0)),
            pl.BlockSpec((tm, LANES), lambda i: (i % n_tab, 0)),
            pl.BlockSpec((LANES, tm), lambda i: (0, i % n_tab)),
            pl.BlockSpec((LANES, tm), lambda i: (0, i % n_tab)),
        ],
        out_specs=out_specs,
        out_shape=out_shapes,
        compiler_params=_params(1),
    )(x2, g, w_packed, wkv_t, cos_t, sin_t, cos_tt, sin_tt)


def _ln_swish(c, lg, lb):
    mu = jnp.mean(c, axis=-1, keepdims=True)
    d = c - mu
    var = jnp.mean(d * d, axis=-1, keepdims=True)
    y = d * lax.rsqrt(var + EPS) * lg + lb
    return y * _sigmoid(y)


CONV_PAD = 32
CONV_CHUNK = 64


def _conv_prompt_kernel(u_ref, past_ref, w_ref, b_ref, lg_ref, lb_ref, o_ref, uf_ref, *, t_len):
    off = CONV_PAD - (CONV_K - 1)
    uf_ref[0:SUBLANES, :] = jnp.zeros((SUBLANES, D_CONV), F32)
    uf_ref[off:CONV_PAD, :] = past_ref[0]
    uf_ref[CONV_PAD:CONV_PAD + t_len, :] = u_ref[0]
    bias = b_ref[...]
    lg = lg_ref[...]
    lb = lb_ref[...]
    ct = CONV_CHUNK

    def body(i, carry):
        base = pl.multiple_of(i * ct, ct)
        xw = uf_ref[pl.ds(base, ct + CONV_PAD), :]
        acc = jnp.zeros((ct, D_CONV), F32)
        n_win = ct + CONV_PAD
        for r in range(SUBLANES):
            yr = xw if r == 0 else pltpu.roll(xw, n_win - r, 0)
            for a in range((CONV_PAD // SUBLANES) + 1):
                k = SUBLANES * a + r - off
                if 0 <= k < CONV_K:
                    acc = acc + w_ref[k:k + 1, :] * yr[SUBLANES * a:SUBLANES * a + ct, :]
        o_ref[0, pl.ds(base, ct), :] = _ln_swish(acc + bias, lg, lb).astype(BF16)
        return carry

    lax.fori_loop(0, t_len // ct, body, 0)


def _conv_prompt(u3, past3, dw_w, dw_b, ln_g, ln_b):
    b, t_len, _ = u3.shape
    const2 = lambda i: (0, 0)
    return pl.pallas_call(
        functools.partial(_conv_prompt_kernel, t_len=t_len),
        grid=(b,),
        in_specs=[
            pl.BlockSpec((1, t_len, D_CONV), lambda i: (i, 0, 0)),
            pl.BlockSpec((1, CONV_K - 1, D_CONV), lambda i: (i, 0, 0)),
            pl.BlockSpec((CONV_K, D_CONV), const2),
            pl.BlockSpec((1, D_CONV), const2),
            pl.BlockSpec((1, D_CONV), const2),
            pl.BlockSpec((1, D_CONV), const2),
        ],
        out_specs=pl.BlockSpec((1, t_len, D_CONV), lambda i: (i, 0, 0)),
        out_shape=jax.ShapeDtypeStruct((b, t_len, D_CONV), BF16),
        scratch_shapes=[pltpu.VMEM((t_len + CONV_PAD, D_CONV), F32)],
        compiler_params=_params(1),
    )(u3, past3, dw_w, dw_b, ln_g, ln_b)


def _conv_sample_kernel(past_ref, u_ref, w_ref, b_ref, lg_ref, lb_ref, o_ref, *, tq):
    bias = b_ref[...]
    lg = lg_ref[...]
    lb = lb_ref[...]
    n_past = CONV_K - 1
    for t in range(tq):
        acc = jnp.zeros(o_ref.shape[1:], F32)
        for j in range(t, n_past):
            acc = acc + w_ref[j - t:j - t + 1, :] * past_ref[j]
        for i in range(t + 1):
            k = n_past - t + i
            acc = acc + w_ref[k:k + 1, :] * u_ref[i]
        o_ref[t] = _ln_swish(acc + bias, lg, lb).astype(BF16)


def _conv_sample(past_t, u_t, dw_w, dw_b, ln_g, ln_b):
    n_past, bd, _ = past_t.shape
    tq = u_t.shape[0]
    bt = _pick_tile(bd, 32)
    const2 = lambda i: (0, 0)
    return pl.pallas_call(
        functools.partial(_conv_sample_kernel, tq=tq),
        grid=(bd // bt,),
        in_specs=[
            pl.BlockSpec((n_past, bt, D_CONV), lambda i: (0, i, 0)),
            pl.BlockSpec((tq, bt, D_CONV), lambda i: (0, i, 0)),
            pl.BlockSpec((CONV_K, D_CONV), const2),
            pl.BlockSpec((1, D_CONV), const2),
            pl.BlockSpec((1, D_CONV), const2),
            pl.BlockSpec((1, D_CONV), const2),
        ],
        out_specs=pl.BlockSpec((tq, bt, D_CONV), lambda i: (0, i, 0)),
        out_shape=jax.ShapeDtypeStruct((tq, bd, D_CONV), BF16),
        compiler_params=_params(1),
    )(past_t, u_t, dw_w, dw_b, ln_g, ln_b)


def _chunkproj_kernel(x_ref, wk_ref, wv_ref, pk_ref, pv_ref, ab_ref, t_ref, *, n_pages, pages_on_lanes):
    n_rows = n_pages * CHUNKS_PER_PAGE
    for kv, (w_ref, p_ref) in enumerate(((wk_ref, pk_ref), (wv_ref, pv_ref))):
        for p in range(n_pages):
            page = x_ref[0, kv, :, p * PAGE_SIZE:(p + 1) * PAGE_SIZE] if pages_on_lanes else x_ref[p, kv]
            t_ref[p * PAGE_SIZE:(p + 1) * PAGE_SIZE, :] = page.T
        xs = jnp.concatenate([t_ref[pl.ds(l, n_rows, stride=CMP_STRIDE), :] for l in range(CMP_STRIDE)],
                             axis=1).astype(BF16)
        w = w_ref[...]
        part = _dot(xs, w)
        posb = _dot(p_ref[...], w)
        c0 = kv * 2 * LANES
        ab_ref[:, c0:c0 + LANES] = part[:, 0:LANES] + posb[0:1, 0:LANES]
        ab_ref[:, c0 + LANES:c0 + 2 * LANES] = part[:, LANES:2 * LANES] + posb[1:2, LANES:2 * LANES]


def _chunkproj(x4, wk2, wv2, pk2, pv2, page_off, n_total, pages_on_lanes):
    if pages_on_lanes:
        n_pages = x4.shape[3] // PAGE_SIZE
        n_steps = x4.shape[0]
        x_spec = pl.BlockSpec((1, 2, LANES, x4.shape[3]), lambda i: (i, 0, 0, 0))
    else:
        n_pages = _pick_tile(n_total, 64, 1)
        n_steps = n_total // n_pages
        off = page_off // n_pages
        x_spec = pl.BlockSpec((n_pages, 2, LANES, PAGE_SIZE), lambda i: (i + off, 0, 0, 0))
    const2 = lambda i: (0, 0)
    rows = n_pages * CHUNKS_PER_PAGE
    return pl.pallas_call(
        functools.partial(_chunkproj_kernel, n_pages=n_pages, pages_on_lanes=pages_on_lanes),
        grid=(n_steps,),
        in_specs=[
            x_spec,
            pl.BlockSpec(wk2.shape, const2),
            pl.BlockSpec(wv2.shape, const2),
            pl.BlockSpec(pk2.shape, const2),
            pl.BlockSpec(pv2.shape, const2),
        ],
        out_specs=pl.BlockSpec((rows, 4 * LANES), lambda i: (i, 0)),
        out_shape=jax.ShapeDtypeStruct((n_steps * rows, 4 * LANES), F32),
        scratch_shapes=[pltpu.VMEM((n_pages * PAGE_SIZE, LANES), F32)],
        compiler_params=_params(1),
    )(x4, wk2, wv2, pk2, pv2)


def _compressed_kv_f32(ab, cosc, sinc):
    n = ab.shape[0]
    kc = ab[:, 0:LANES] + pltpu.roll(ab[:, LANES:2 * LANES], n - 1, 0)
    vc = ab[:, 2 * LANES:3 * LANES] + pltpu.roll(ab[:, 3 * LANES:4 * LANES], n - 1, 0)
    return _rope_slab(kc, cosc, sinc), vc


def _compressed_kv(ab, cosc, sinc):
    kc, vc = _compressed_kv_f32(ab, cosc, sinc)
    return kc.astype(BF16), vc.astype(BF16)


def _softmax_parts(s, mask):
    sm = jnp.where(mask, s, NEG)
    m = jnp.max(sm, axis=-1, keepdims=True)
    p = jnp.where(mask, jnp.exp(sm - m), 0.0)
    l = jnp.maximum(jnp.sum(p, axis=-1, keepdims=True), 1e-30)
    return p, l


def _select_blocks(imp, qpos, n_sel, axis):
    jidx = lax.broadcasted_iota(jnp.int32, imp.shape, axis)
    cur = lax.shift_right_logical(qpos, 6)
    forced = (jidx == 0) | (jidx == cur) | (jidx == cur - 1)
    valid = (jidx * SEL_BLOCK <= qpos) & (jidx < n_sel)
    v = jnp.where(valid, jnp.where(forced, jnp.inf, imp), -jnp.inf)
    rank = jnp.zeros(imp.shape, jnp.int32)
    for k in range(n_sel):
        vk = v[k:k + 1, :] if axis == 0 else v[:, k:k + 1]
        ahead = (vk > v) | ((vk == v) & (jidx > k))
        rank = rank + ahead.astype(jnp.int32)
    return (rank < min(N_SEL, n_sel)) & valid


def _overlap(n_cmp, cmp_axis):
    i = lax.broadcasted_iota(jnp.int32, (LANES, LANES), cmp_axis)
    j = lax.broadcasted_iota(jnp.int32, (LANES, LANES), 1 - cmp_axis)
    hit = (i * CMP_STRIDE < (j + 1) * SEL_BLOCK) & (i * CMP_STRIDE + CMP_BLOCK > j * SEL_BLOCK) & (i < n_cmp)
    return hit.astype(F32)


Q_TILE = 256
K_TILE = 256


def _attn_prompt_kernel(q_ref, gn_ref, skr_ref, svt_ref, wkr_ref, wvt_ref, ab_ref, cosc_ref, sinc_ref, o_ref,
                        kc_ref, vct_ref, bias_ref, m_ref, acc_ref, out_ref, *, t_len):
    qb = pl.program_id(1)
    n_chunk = t_len // CMP_STRIDE
    n_cmp = (t_len - CMP_BLOCK) // CMP_STRIDE + 1
    n_sel = -(-t_len // SEL_BLOCK)
    sel_rows = -(-n_sel // SUBLANES) * SUBLANES
    qt, kt_sz = Q_TILE, K_TILE
    q_tiles = qt // LANES
    g_lanes = HPG * qt
    n_lane_tiles = N_HEADS * q_tiles

    @pl.when(qb == 0)
    def _():
        kc, vc = _compressed_kv_f32(ab_ref[...], cosc_ref[...], sinc_ref[...])
        kc_ref[...] = kc.astype(BF16)
        vct_ref[...] = vc.T.astype(BF16)

    q0 = qb * qt
    qpos = q0 + lax.broadcasted_iota(jnp.int32, (1, qt), 1)

    def lane_tile(c):
        return slice(c * LANES, (c + 1) * LANES)

    def q_part(c):
        return slice((c % q_tiles) * LANES, (c % q_tiles + 1) * LANES)

    def group_of(c):
        return c // (HPG * q_tiles)

    def group_lanes(g):
        return slice(g * g_lanes, (g + 1) * g_lanes)

    q_all = jnp.concatenate([q_ref[0, hh * LANES:(hh + 1) * LANES, :] for hh in range(N_HEADS)], axis=1)

    def gates(branch):
        r0 = branch * N_HEADS
        return jnp.concatenate([gn_ref[0, r0 + hh:r0 + hh + 1, :] for hh in range(N_HEADS)], axis=1)

    nrow = lax.broadcasted_iota(jnp.int32, (n_chunk, qt), 0)
    mask_c = (nrow * CMP_STRIDE + CMP_BLOCK - 1 <= qpos) & (nrow < n_cmp)
    acc_ref[...] = _dot(kc_ref[...], q_all)
    hs = [[jnp.zeros((n_chunk, LANES), F32) for _ in range(q_tiles)] for _ in range(N_KV)]
    p_parts = []
    for c in range(n_lane_tiles):
        mask = mask_c[:, q_part(c)]
        sm = jnp.where(mask, acc_ref[:, lane_tile(c)], NEG)
        p_c = jnp.where(mask, jnp.exp(sm - jnp.max(sm, axis=0, keepdims=True)), 0.0)
        p_c = p_c / jnp.maximum(jnp.sum(p_c, axis=0, keepdims=True), 1e-30)
        hs[group_of(c)][c % q_tiles] = hs[group_of(c)][c % q_tiles] + p_c
        p_parts.append(p_c.astype(BF16))
    out_ref[...] = gates(0) * _dot(vct_ref[...], jnp.concatenate(p_parts, axis=1))

    sel_t = []
    for g in range(N_KV):
        imp_t = jnp.dot(_overlap(n_cmp, 1), jnp.concatenate(hs[g], axis=1), preferred_element_type=F32,
                        precision=lax.Precision.HIGHEST)
        s_g = _select_blocks(imp_t[0:sel_rows], qpos, n_sel, 0).astype(F32)
        sel_t.append(jnp.concatenate([s_g, jnp.zeros((LANES - sel_rows, qt), F32)], axis=0).astype(BF16))

    def sel_bias(k_idx, kpos):
        krow = lax.broadcasted_iota(jnp.int32, (kt_sz, LANES), 0)
        jcol = lax.broadcasted_iota(jnp.int32, (kt_sz, LANES), 1)
        expand_t = (jcol == k_idx * (kt_sz // SEL_BLOCK) + lax.shift_right_logical(krow, 6)).astype(BF16)
        for g in range(N_KV):
            bias_ref[g] = jnp.where((_dot(expand_t, sel_t[g]) > 0.5) & (kpos <= qpos), 0.0, NEG)

    def win_bias(k_idx, kpos):
        b = jnp.where((kpos <= qpos) & (kpos > qpos - WINDOW), 0.0, NEG)
        for g in range(N_KV):
            bias_ref[g] = b

    def run_branch(kr_ref, vt_ref, n_tiles, tile_of, bias_fn, branch):
        m_ref[...] = jnp.full(m_ref.shape, NEG, F32)
        acc_ref[...] = jnp.zeros(acc_ref.shape, F32)

        def body(i, carry):
            k_idx = tile_of(i)
            k0 = pl.multiple_of(k_idx * kt_sz, kt_sz)
            kpos = k0 + lax.broadcasted_iota(jnp.int32, (kt_sz, qt), 0)
            bias_fn(k_idx, kpos)
            s = _dot(kr_ref[pl.ds(k0, kt_sz), :], q_all)
            vt = vt_ref[0, :, pl.ds(k0, kt_sz)]
            vrow = lax.broadcasted_iota(jnp.int32, (LANES, kt_sz), 0)
            p_parts, a_parts = [], []
            for c in range(n_lane_tiles):
                sb = s[:, lane_tile(c)] + bias_ref[group_of(c), :, q_part(c)]
                m_old = m_ref[0:1, lane_tile(c)]
                m_new = jnp.maximum(m_old, jnp.max(sb, axis=0, keepdims=True))
                p_parts.append(jnp.exp(sb - m_new).astype(BF16))
                a_parts.append(jnp.exp(m_old - m_new))
                m_ref[:, lane_tile(c)] = jnp.broadcast_to(m_new, (SUBLANES, LANES))
            for g in range(N_KV):
                v_own = (vrow >= g * HEAD_DIM) & (vrow < (g + 1) * HEAD_DIM)
                vaug = jnp.where(v_own, vt, 1.0).astype(BF16)
                tiles = range(g * HPG * q_tiles, (g + 1) * HPG * q_tiles)
                alpha = jnp.concatenate([a_parts[c] for c in tiles], axis=1)
                p = jnp.concatenate([p_parts[c] for c in tiles], axis=1)
                acc_ref[:, group_lanes(g)] = alpha * acc_ref[:, group_lanes(g)] + _dot(vaug, p)
            return carry

        lax.fori_loop(0, n_tiles, body, 0)
        gt = gates(branch)
        for g in range(N_KV):
            sum_row = (1 - g) * HEAD_DIM
            acc = acc_ref[:, group_lanes(g)]
            out_ref[:, group_lanes(g)] = (out_ref[:, group_lanes(g)]
                                          + gt[:, group_lanes(g)] * (acc / acc[sum_row:sum_row + 1, :]))

    run_branch(skr_ref, svt_ref, qb + 1, lambda i: i, sel_bias, 1)
    run_branch(wkr_ref, wvt_ref, jnp.minimum(qb, WINDOW // kt_sz) + 1, lambda i: qb - i, win_bias, 2)

    for hh in range(N_HEADS):
        o_ref[:, hh * LANES:(hh + 1) * LANES] = out_ref[:, hh * qt:(hh + 1) * qt].T.astype(BF16)


def _attn_prompt(q_t, gn_t, skr, skv_t, wkr, wkv_t, ab2, cosc, sinc, b, t_len):
    n_chunk = t_len // CMP_STRIDE
    nqb = t_len // Q_TILE
    const2 = lambda i, j: (0, 0)
    per_b2 = lambda i, j: (i, 0)
    v_rows = lambda i, j: (i, 1, 0)
    return pl.pallas_call(
        functools.partial(_attn_prompt_kernel, t_len=t_len),
        grid=(b, nqb),
        in_specs=[
            pl.BlockSpec((1, Q_EXP, Q_TILE), lambda i, j: (i, 0, j)),
            pl.BlockSpec((1, GATE_ROWS, Q_TILE), lambda i, j: (i, 0, j)),
            pl.BlockSpec((t_len, LANES), per_b2),
            pl.BlockSpec((1, LANES, t_len), v_rows),
            pl.BlockSpec((t_len, LANES), per_b2),
            pl.BlockSpec((1, LANES, t_len), v_rows),
            pl.BlockSpec((n_chunk, 4 * LANES), per_b2),
            pl.BlockSpec((n_chunk, LANES), const2),
            pl.BlockSpec((n_chunk, LANES), const2),
        ],
        out_specs=pl.BlockSpec((Q_TILE, Q_EXP), lambda i, j: (i * nqb + j, 0)),
        out_shape=jax.ShapeDtypeStruct((b * t_len, Q_EXP), BF16),
        scratch_shapes=[
            pltpu.VMEM((n_chunk, LANES), BF16),
            pltpu.VMEM((LANES, n_chunk), BF16),
            pltpu.VMEM((N_KV, K_TILE, Q_TILE), F32),
            pltpu.VMEM((SUBLANES, N_HEADS * Q_TILE), F32),
            pltpu.VMEM((LANES, N_HEADS * Q_TILE), F32),
            pltpu.VMEM((LANES, N_HEADS * Q_TILE), F32),
        ],
        compiler_params=_params(2),
    )(q_t, gn_t, skr, skv_t, wkr, wkv_t, ab2, cosc, sinc)


Q_ROWS = 8
SEQ_PER_STEP = 4


def _attn_sample_kernel(pt_ref, q_ref, gn_ref, sknew_ref, wknew_ref, win_ref, cosc_ref, sinc_ref, *rest,
                        past_len, tq, n_pages, n_seq):
    del pt_ref
    sel_pages = [rest[e * n_pages:(e + 1) * n_pages] for e in range(n_seq)]
    ab_pages = [rest[(n_seq + e) * n_pages:(n_seq + e + 1) * n_pages] for e in range(n_seq)]
    o_ref = rest[2 * n_seq * n_pages]
    seqs = range(n_seq)
    t_all = past_len + tq
    n_chunk = past_len // CMP_STRIDE
    n_cmp = (t_all - CMP_BLOCK) // CMP_STRIDE + 1
    n_sel = -(-t_all // SEL_BLOCK)
    rows = N_HEADS * Q_ROWS
    all_rows = n_seq * rows
    win_buf = win_ref.shape[3]

    def stack(parts):
        return jnp.concatenate(parts, axis=0)

    def per_seq(a, e):
        return a[e * rows:(e + 1) * rows]

    qbd = [stack([q_ref[e, :, hh * LANES:(hh + 1) * LANES] for hh in range(N_HEADS)]) for e in seqs]
    qpos = past_len + lax.broadcasted_iota(jnp.int32, (all_rows, 1), 0) % Q_ROWS

    cosc = cosc_ref[...]
    sinc = sinc_ref[...]
    kvc = [_compressed_kv(stack([r[0] for r in ab_pages[e]]), cosc, sinc) for e in seqs]
    ncol = lax.broadcasted_iota(jnp.int32, (all_rows, n_chunk), 1)
    mask_c = (ncol * CMP_STRIDE + CMP_BLOCK - 1 <= qpos) & (ncol < n_cmp)
    p_c, l_c = _softmax_parts(stack([_dot_nt(qbd[e], kvc[e][0]) for e in seqs]), mask_c)
    p_c = p_c / l_c
    p_c16 = p_c.astype(BF16)
    o_c = stack([_dot(per_seq(p_c16, e), kvc[e][1]) for e in seqs])

    hs = []
    for e in seqs:
        for g in range(N_KV):
            r0 = e * rows + g * HPG * Q_ROWS
            acc = p_c[r0:r0 + Q_ROWS]
            for h in range(1, HPG):
                acc = acc + p_c[r0 + h * Q_ROWS:r0 + (h + 1) * Q_ROWS]
            hs.append(acc)
    imp = jnp.dot(stack(hs), _overlap(n_cmp, 0), preferred_element_type=F32, precision=lax.Precision.HIGHEST)
    sel_g = _select_blocks(imp, qpos[0:n_seq * N_KV * Q_ROWS], n_sel, 1).astype(BF16)
    sel_rows = stack([sel_g[(e * N_KV + g) * Q_ROWS:(e * N_KV + g + 1) * Q_ROWS]
                      for e in seqs for g in range(N_KV) for _ in range(HPG)])

    def new_rows_tile(ref, e, c0):
        return stack([ref[e, :, c0:c0 + LANES], jnp.zeros((LANES - Q_ROWS, LANES), F32)]).astype(BF16)

    n_keys = (n_pages + 1) * PAGE_SIZE
    s_s = stack([jnp.concatenate([_dot(qbd[e], r[0, 0].astype(BF16)) for r in sel_pages[e]]
                                 + [_dot_nt(qbd[e], new_rows_tile(sknew_ref, e, 0))], axis=1) for e in seqs])
    jrow = lax.broadcasted_iota(jnp.int32, (LANES, n_keys), 0)
    kcol = lax.broadcasted_iota(jnp.int32, (LANES, n_keys), 1)
    expand = (jrow == lax.shift_right_logical(kcol, 6)).astype(BF16)
    kpos = lax.broadcasted_iota(jnp.int32, (all_rows, n_keys), 1)
    mask_s = (_dot(sel_rows, expand) > 0.5) & (kpos <= qpos)
    p_s, l_s = _softmax_parts(s_s, mask_s)
    p_s = p_s.astype(BF16)
    o_s = []
    for e in seqs:
        p_e = per_seq(p_s, e)
        acc = _dot(p_e[:, n_pages * PAGE_SIZE:], new_rows_tile(sknew_ref, e, LANES))
        for i in range(n_pages):
            acc = acc + _dot_nt(p_e[:, i * PAGE_SIZE:(i + 1) * PAGE_SIZE], sel_pages[e][i][0, 1].astype(BF16))
        o_s.append(acc)
    o_s = stack(o_s) / l_s

    s_w = stack([jnp.concatenate([_dot(qbd[e], win_ref[e, 0].astype(BF16)),
                                  _dot_nt(qbd[e], new_rows_tile(wknew_ref, e, 0))], axis=1) for e in seqs])
    wcol = lax.broadcasted_iota(jnp.int32, (all_rows, win_buf + LANES), 1)
    kpos_w = past_len - win_buf + wcol
    mask_w = (kpos_w <= qpos) & (kpos_w > qpos - WINDOW)
    p_w, l_w = _softmax_parts(s_w, mask_w)
    p_w = p_w.astype(BF16)
    o_w = stack([_dot_nt(per_seq(p_w, e)[:, 0:win_buf], win_ref[e, 1].astype(BF16))
                 + _dot(per_seq(p_w, e)[:, win_buf:], new_rows_tile(wknew_ref, e, LANES)) for e in seqs]) / l_w

    for e in seqs:
        gn = gn_ref[e]
        for hh in range(N_HEADS):
            r = slice(e * rows + hh * Q_ROWS, e * rows + (hh + 1) * Q_ROWS)
            out = (gn[:, hh:hh + 1] * o_c[r]
                   + gn[:, N_HEADS + hh:N_HEADS + hh + 1] * o_s[r]
                   + gn[:, 2 * N_HEADS + hh:2 * N_HEADS + hh + 1] * o_w[r])
            o_ref[e, :, hh * LANES:(hh + 1) * LANES] = out.astype(BF16)


def _attn_sample(pt_flat, q3, gn3, sknew3, wknew3, win4, cosc, sinc, sel_pool4, ab_pool3,
                 layer, n_pool, bd, past_len, tq):
    n_pages = past_len // PAGE_SIZE
    win_buf = win4.shape[3]
    n_chunk = past_len // CMP_STRIDE
    n_seq = _pick_tile(bd, SEQ_PER_STEP, 1)
    per_b = lambda i, pt: (i, 0, 0)
    const2 = lambda i, pt: (0, 0)
    sel_specs = [pl.BlockSpec((1, 2, LANES, PAGE_SIZE),
                              lambda i, pt, e=e, p=p: (layer * n_pool + pt[(i * n_seq + e) * n_pages + p], 0, 0, 0))
                 for e in range(n_seq) for p in range(n_pages)]
    ab_specs = [pl.BlockSpec((1, CHUNKS_PER_PAGE, 4 * LANES),
                             lambda i, pt, e=e, p=p: (pt[(i * n_seq + e) * n_pages + p], 0, 0))
                for e in range(n_seq) for p in range(n_pages)]
    grid_spec = pltpu.PrefetchScalarGridSpec(
        num_scalar_prefetch=1,
        grid=(bd // n_seq,),
        in_specs=[
            pl.BlockSpec((n_seq, Q_ROWS, Q_EXP), per_b),
            pl.BlockSpec((n_seq, Q_ROWS, LANES), per_b),
            pl.BlockSpec((n_seq, Q_ROWS, KV_COLS), per_b),
            pl.BlockSpec((n_seq, Q_ROWS, KV_COLS), per_b),
            pl.BlockSpec((n_seq, 2, LANES, win_buf), lambda i, pt: (layer * (bd // n_seq) + i, 0, 0, 0)),
            pl.BlockSpec((n_chunk, LANES), const2),
            pl.BlockSpec((n_chunk, LANES), const2),
        ] + sel_specs + ab_specs,
        out_specs=pl.BlockSpec((n_seq, Q_ROWS, Q_EXP), per_b),
    )
    n_refs = n_seq * n_pages
    return pl.pallas_call(
        functools.partial(_attn_sample_kernel, past_len=past_len, tq=tq, n_pages=n_pages, n_seq=n_seq),
        grid_spec=grid_spec,
        out_shape=jax.ShapeDtypeStruct((bd, Q_ROWS, Q_EXP), BF16),
        compiler_params=_params(1),
    )(pt_flat, q3, gn3, sknew3, wknew3, win4, cosc, sinc, *([sel_pool4] * n_refs), *([ab_pool3] * n_refs))


def _merge_kernel(c_ref, o_ref, gm_ref, x_ref, wc_ref, wn_ref, wo_ref, y_ref):
    conv_out = _dot(c_ref[...], wc_ref[...])
    nsa_out = _dot(o_ref[...], wn_ref[...])
    gm = gm_ref[...].astype(F32)
    merged = gm[:, 0:D_MODEL] * conv_out + gm[:, D_MODEL:2 * D_MODEL] * nsa_out
    y_ref[...] = x_ref[...] + _dot(merged.astype(BF16), wo_ref[...])


def _merge(c2, o2, gm2, x2, wc, wn, wo, tm):
    m = x2.shape[0]
    row = lambda i: (i, 0)
    const = lambda i: (0, 0)
    return pl.pallas_call(
        _merge_kernel,
        grid=(m // tm,),
        in_specs=[
            pl.BlockSpec((tm, D_CONV), row),
            pl.BlockSpec((tm, Q_EXP), row),
            pl.BlockSpec((tm, 2 * D_MODEL), row),
            pl.BlockSpec((tm, D_MODEL), row),
            pl.BlockSpec(wc.shape, const),
            pl.BlockSpec(wn.shape, const),
            pl.BlockSpec(wo.shape, const),
        ],
        out_specs=pl.BlockSpec((tm, D_MODEL), row),
        out_shape=jax.ShapeDtypeStruct((m, D_MODEL), F32),
        compiler_params=_params(1),
    )(c2, o2, gm2, x2, wc, wn, wo)


FF_TILE = 1024


def _mlp_kernel(x_ref, g_ref, wu_ref, wd_ref, gf_ref, y_ref, *, final):
    x = x_ref[...]
    h = _rms_bf16(x, g_ref[...])
    acc = x
    for f in range(D_FF // FF_TILE):
        up = jnp.maximum(_dot(h, wu_ref[:, f * FF_TILE:(f + 1) * FF_TILE]), 0.0)
        acc = acc + _dot((up * up).astype(BF16), wd_ref[f * FF_TILE:(f + 1) * FF_TILE, :])
    if final:
        ms2 = jnp.mean(acc * acc, axis=-1, keepdims=True)
        acc = acc * lax.rsqrt(ms2 + EPS) * gf_ref[...]
    y_ref[...] = acc


def _mlp(x2, g, wu, wd, gf, tm, final):
    m = x2.shape[0]
    row = lambda i: (i, 0)
    const = lambda i: (0, 0)
    return pl.pallas_call(
        functools.partial(_mlp_kernel, final=final),
        grid=(m // tm,),
        in_specs=[
            pl.BlockSpec((tm, D_MODEL), row),
            pl.BlockSpec((1, D_MODEL), const),
            pl.BlockSpec(wu.shape, const),
            pl.BlockSpec(wd.shape, const),
            pl.BlockSpec((1, D_MODEL), const),
        ],
        out_specs=pl.BlockSpec((tm, D_MODEL), row),
        out_shape=jax.ShapeDtypeStruct((m, D_MODEL), F32),
        compiler_params=_params(1),
    )(x2, g, wu, wd, gf)


def _rope_tables(pos):
    half = HEAD_DIM // 2
    inv = jnp.power(ROPE_THETA, -jnp.arange(half, dtype=F32) / half)
    ang = pos.astype(F32)[:, None] * inv[None, :]
    cos = jnp.cos(ang)
    sin = jnp.sin(ang)
    return jnp.concatenate([cos, cos, cos, cos], axis=1), jnp.concatenate([-sin, sin, -sin, sin], axis=1)


def _pack_w_in(w):
    o0 = 2 * D_CONV
    o1 = o0 + N_HEADS * HEAD_DIM
    o4 = o1 + 3 * KV_COLS
    o5 = o4 + N_GATES
    wq = w[:, o0:o1].reshape(D_MODEL, N_HEADS, HEAD_DIM) * (HEAD_DIM ** -0.5)
    zero = jnp.zeros_like(wq)
    in_g0 = (jnp.arange(N_HEADS) < HPG)[None, :, None]
    wq_exp = jnp.concatenate([jnp.where(in_g0, wq, zero), jnp.where(in_g0, zero, wq)], axis=2)
    wgn = jnp.pad(w[:, o4:o5], ((0, 0), (0, LANES - N_GATES)))
    wq_exp = wq_exp.reshape(D_MODEL, Q_EXP)
    packed = jnp.concatenate([w[:, :o0], wq_exp, w[:, o1:o4], wgn, w[:, o5:]], axis=1)
    w_t = jnp.concatenate([wq_exp, w[:, o1:o4], wgn[:, :GATE_ROWS]], axis=1).T
    return packed.astype(BF16), w_t.astype(BF16)


def _pack_w_nsa_out(w):
    wh = w.reshape(N_HEADS, HEAD_DIM, D_MODEL)
    zero = jnp.zeros_like(wh)
    in_g0 = (jnp.arange(N_HEADS) < HPG)[:, None, None]
    return jnp.concatenate([jnp.where(in_g0, wh, zero), jnp.where(in_g0, zero, wh)], axis=1).reshape(Q_EXP, D_MODEL).astype(BF16)


def _pack_w_cmp(w_kv, pos_kv):
    lo, hi = w_kv[:CMP_STRIDE], w_kv[CMP_STRIDE:]
    eye = jnp.eye(N_KV, dtype=w_kv.dtype)
    blk = lambda part: jnp.einsum('lde,gh->lgdhe', part, eye).reshape(CMP_STRIDE * N_KV * HEAD_DIM, N_KV * HEAD_DIM)
    w2 = jnp.concatenate([blk(lo), blk(hi)], axis=1).astype(BF16)
    tile = lambda p: jnp.broadcast_to(p[:, None, :], (CMP_STRIDE, N_KV, HEAD_DIM)).reshape(1, -1)
    p2 = jnp.concatenate([tile(pos_kv[:CMP_STRIDE]), tile(pos_kv[CMP_STRIDE:]),
                          jnp.zeros((SUBLANES - 2, CMP_STRIDE * N_KV * HEAD_DIM), pos_kv.dtype)], axis=0).astype(BF16)
    return w2, p2


def _feature_major(a):
    lead = a.shape[:-4]
    n = len(lead)
    a = jnp.transpose(a, tuple(range(n)) + (n + 1, n + 2, n + 3, n))
    return a.reshape(lead + (2, N_KV * HEAD_DIM, a.shape[-1]))


def _position_major(a_t, rows):
    b = a_t.shape[0]
    return jnp.transpose(a_t.reshape(b, 2, N_KV, HEAD_DIM, rows), (0, 4, 1, 2, 3))


def kernel(x_prompt, x_sample, cache_cmp_kv, cache_sel_kv, state_win_kv, state_conv, page_table, norm_mix_g, w_in, conv_dw_w, conv_dw_b, conv_ln_g, conv_ln_b, w_conv_out, cmp_pos, w_cmp, w_nsa_out, w_out, norm_mlp_g, w_up, w_down, norm_final_g):
    depth = w_in.shape[0]
    bp, tp, _ = x_prompt.shape
    bd, tq, _ = x_sample.shape
    n_pool = cache_cmp_kv.shape[1]
    n_pages = page_table.shape[1]
    past_len = n_pages * PAGE_SIZE
    win_buf = state_win_kv.shape[2]
    assert tp % Q_TILE == 0 and tp // CMP_STRIDE == LANES and tp >= WINDOW
    assert past_len // CMP_STRIDE == LANES and win_buf == WINDOW and tq <= Q_ROWS

    mp, ms = bp * tp, bd * tq
    tm_p = _pick_tile(tp, 512)
    tm_s = _pick_tile(ms, 256)
    assert tp % tm_p == 0 and tm_s % tq == 0

    cos_p, sin_p = _rope_tables(jnp.arange(tp))
    cos_pt, sin_pt = cos_p.T, sin_p.T
    cos_s, sin_s = _rope_tables(past_len + jnp.arange(tm_s) % tq)
    n_chunk = tp // CMP_STRIDE
    cos_c, sin_c = _rope_tables(jnp.arange(n_chunk) * CMP_STRIDE + CMP_BLOCK - 1)

    pt_flat = page_table.reshape(-1).astype(jnp.int32)
    cmp_pool4 = _feature_major(cache_cmp_kv).reshape(depth * n_pool, 2, LANES, PAGE_SIZE)
    sel_pool4 = _feature_major(cache_sel_kv).reshape(depth * n_pool, 2, LANES, PAGE_SIZE)
    win4 = _feature_major(state_win_kv).reshape(depth * bd, 2, LANES, win_buf)
    zeros_conv = jnp.zeros((bp, CONV_K - 1, D_CONV), F32)

    xp = x_prompt.reshape(mp, D_MODEL)
    xs = x_sample.reshape(ms, D_MODEL)
    outs = [[] for _ in range(8)]
    for l in range(depth):
        w_packed, w_t = _pack_w_in(w_in[l])
        wn = _pack_w_nsa_out(w_nsa_out[l])
        wc, wo = w_conv_out[l].astype(BF16), w_out[l].astype(BF16)
        wu, wd = w_up[l].astype(BF16), w_down[l].astype(BF16)
        wk2, pk2 = _pack_w_cmp(w_cmp[l, 0], cmp_pos[l, 0])
        wv2, pv2 = _pack_w_cmp(w_cmp[l, 1], cmp_pos[l, 1])
        g_mix, g_mlp = norm_mix_g[l][None], norm_mlp_g[l][None]
        dw_b, ln_g, ln_b = conv_dw_b[l][None], conv_ln_g[l][None], conv_ln_b[l][None]
        gf = norm_final_g[None]
        final = l == depth - 1

        u, q_t, ckv_t, skv_t, wkv_tm, skr, wkr, gn_t, gm = _inproj_cols(xp, g_mix, w_packed, w_t, cos_p, sin_p,
                                                                        cos_pt, sin_pt, bp, tp, tm_p)
        c_act = _conv_prompt(u.reshape(bp, tp, D_CONV), zeros_conv, conv_dw_w[l], dw_b, ln_g, ln_b)
        ab = _chunkproj(ckv_t.reshape(bp, 2, LANES, tp), wk2, wv2, pk2, pv2, 0, 0, True)
        o = _attn_prompt(q_t, gn_t, skr, skv_t, wkr, wkv_tm, ab, cos_c, sin_c, bp, tp)
        xp = _merge(c_act.reshape(mp, D_CONV), o, gm, xp, wc, wn, wo, tm_p)
        xp = _mlp(xp, g_mlp, wu, wd, gf, _pick_tile(mp, 512), final)
        keep = min(WINDOW, tp)
        outs[0].append(_position_major(ckv_t, tp))
        outs[2].append(_position_major(skv_t, tp))
        outs[4].append(_position_major(wkv_tm[:, :, tp - keep:], keep))
        outs[6].append(u.reshape(bp, tp, D_CONV)[:, tp - (CONV_K - 1):])

        u, q, ckv, skv, wkv, gn, gm = _inproj_rows(xs, g_mix, w_packed, cos_s, sin_s, tm_s)
        u3 = u.reshape(bd, tq, D_CONV)
        c_t = _conv_sample(jnp.swapaxes(state_conv[l], 0, 1), jnp.swapaxes(u3, 0, 1), conv_dw_w[l], dw_b, ln_g, ln_b)
        c_act = jnp.swapaxes(c_t, 0, 1).reshape(ms, D_CONV)
        ab_pool = _chunkproj(cmp_pool4, wk2, wv2, pk2, pv2, l * n_pool, n_pool, False)
        pad_q = lambda a: jnp.pad(a.reshape(bd, tq, -1), ((0, 0), (0, Q_ROWS - tq), (0, 0)))
        o = _attn_sample(pt_flat, pad_q(q), pad_q(gn), pad_q(skv), pad_q(wkv), win4, cos_c, sin_c, sel_pool4,
                         ab_pool.reshape(n_pool, CHUNKS_PER_PAGE, 4 * LANES), l, n_pool, bd, past_len, tq)
        xs = _merge(c_act, o[:, :tq].reshape(ms, Q_EXP), gm, xs, wc, wn, wo, tm_s)
        xs = _mlp(xs, g_mlp, wu, wd, gf, _pick_tile(ms, 512), final)
        wkv5 = wkv.reshape(bd, tq, 2, N_KV, HEAD_DIM)
        outs[1].append(ckv.reshape(bd, tq, 2, N_KV, HEAD_DIM))
        outs[3].append(skv.reshape(bd, tq, 2, N_KV, HEAD_DIM))
        outs[5].append(wkv5)
        outs[7].append(u3)

    y_prompt = xp.reshape(bp, tp, D_MODEL)
    y_sample = xs.reshape(bd, tq, D_MODEL)
    outs = [jnp.stack(o) for o in outs]
    outs[5] = jnp.concatenate([state_win_kv[:, :, tq:], outs[5]], axis=2)
    outs[7] = jnp.concatenate([state_conv[:, :, tq:], outs[7]], axis=2)
    return (y_prompt, y_sample) + tuple(outs)
```

```python
import functools

import jax
import jax.numpy as jnp
from jax import lax
from jax.experimental import pallas as pl
from jax.experimental.pallas import tpu as pltpu

D_MODEL = 1024
D_CONV = D_MODEL // 2
CONV_K = 31
N_HEADS = 8
HEAD_DIM = 64
N_KV = 2
HPG = N_HEADS // N_KV
KV_COLS = 2 * N_KV * HEAD_DIM
CMP_BLOCK = 32
CMP_STRIDE = 16
SEL_BLOCK = 64
N_SEL = 16
WINDOW = 512
D_FF = 4 * D_MODEL
ROPE_THETA = 10000.0
EPS = 1e-6
PAGE_SIZE = 128

LANES = 128
SUBLANES = 8
Q_EXP = N_HEADS * LANES
N_GATES = 3 * N_HEADS
CHUNKS_PER_PAGE = PAGE_SIZE // CMP_STRIDE
NEG = -1e30

O_GLU = 0
O_Q = O_GLU + 2 * D_CONV
O_CKV = O_Q + Q_EXP
O_SKV = O_CKV + KV_COLS
O_WKV = O_SKV + KV_COLS
O_GN = O_WKV + KV_COLS
O_GM = O_GN + LANES
IN_COLS_PACKED = O_GM + 2 * D_MODEL

VMEM_LIMIT = 56 * 1024 * 1024

F32 = jnp.float32
BF16 = jnp.bfloat16


def _params(n_axes, vmem=VMEM_LIMIT):
    return pltpu.CompilerParams(dimension_semantics=("arbitrary",) * n_axes, vmem_limit_bytes=vmem)


def _sigmoid(x):
    return 1.0 / (1.0 + jnp.exp(-x))


def _dot(a, b):
    return jnp.dot(a, b, preferred_element_type=F32)


def _dot_nt(a, b):
    return lax.dot_general(a, b, (((1,), (1,)), ((), ())), preferred_element_type=F32)


def _rope_slab(xs, cos, sin_signed):
    lane = lax.broadcasted_iota(jnp.int32, xs.shape, 1)
    first = (lane % HEAD_DIM) < (HEAD_DIM // 2)
    rot = jnp.where(first, pltpu.roll(xs, LANES - HEAD_DIM // 2, 1), pltpu.roll(xs, HEAD_DIM // 2, 1))
    return xs * cos + rot * sin_signed


def _rope_rows(xt, cos_t, sin_t):
    half = HEAD_DIM // 2
    rot = jnp.concatenate([xt[half:2 * half], xt[0:half], xt[3 * half:4 * half], xt[2 * half:3 * half]], axis=0)
    return xt * cos_t + rot * sin_t


def _pick_tile(n, cap, mult=8):
    t = min(n, cap)
    while n % t or t % mult:
        t -= 1
    return t


def _rms_bf16(x, g):
    ms = jnp.mean(x * x, axis=-1, keepdims=True)
    return (x * lax.rsqrt(ms + EPS) * g).astype(BF16)


def _inproj_common(h, w_ref, cos, sin, u_ref, q_ref, gn_ref, gm_ref):
    def proj(a, n):
        return _dot(h, w_ref[:, a:a + n])

    u_ref[...] = proj(O_GLU, D_CONV) * _sigmoid(proj(O_GLU + D_CONV, D_CONV))
    for s in range(N_HEADS):
        q_ref[:, s * LANES:(s + 1) * LANES] = _rope_slab(proj(O_Q + s * LANES, LANES), cos, sin).astype(BF16)
    gn_ref[...] = _sigmoid(proj(O_GN, LANES))
    gm_ref[...] = _sigmoid(proj(O_GM, 2 * D_MODEL)).astype(BF16)
    return proj


def _inproj_rows_kernel(x_ref, g_ref, w_ref, cos_ref, sin_ref,
                        u_ref, q_ref, ckv_ref, skv_ref, wkv_ref, gn_ref, gm_ref):
    h = _rms_bf16(x_ref[...], g_ref[...])
    cos = cos_ref[...]
    sin = sin_ref[...]
    proj = _inproj_common(h, w_ref, cos, sin, u_ref, q_ref, gn_ref, gm_ref)
    ckv_ref[...] = proj(O_CKV, KV_COLS)
    skv_ref[:, 0:LANES] = _rope_slab(proj(O_SKV, LANES), cos, sin)
    skv_ref[:, LANES:KV_COLS] = proj(O_SKV + LANES, LANES)
    wkv_ref[:, 0:LANES] = _rope_slab(proj(O_WKV, LANES), cos, sin)
    wkv_ref[:, LANES:KV_COLS] = proj(O_WKV + LANES, LANES)


GATE_ROWS = 32
R_Q = 0
R_CKV = R_Q + N_HEADS * HEAD_DIM
R_SKV = R_CKV + KV_COLS
R_WKV = R_SKV + KV_COLS
R_GN = R_WKV + KV_COLS
ROWS_T = R_GN + GATE_ROWS


def _inproj_cols_kernel(x_ref, g_ref, w_ref, wt_ref, cos_ref, sin_ref, cost_ref, sint_ref,
                        u_ref, q_ref, ckv_ref, skv_ref, wkv_ref, skr_ref, wkr_ref, gn_ref, gm_ref):
    h = _rms_bf16(x_ref[...], g_ref[...])
    cos = cos_ref[...]
    sin = sin_ref[...]
    cos_t = cost_ref[...]
    sin_t = sint_ref[...]

    def proj(a, n):
        return _dot(h, w_ref[:, a:a + n])

    t_all = _dot_nt(wt_ref[...], h)

    def proj_t(a, n):
        return t_all[a:a + n]

    u_ref[...] = proj(O_GLU, D_CONV) * _sigmoid(proj(O_GLU + D_CONV, D_CONV))
    gm_ref[...] = _sigmoid(proj(O_GM, 2 * D_MODEL)).astype(BF16)
    skr_ref[...] = _rope_slab(proj(O_SKV, LANES), cos, sin).astype(BF16)
    wkr_ref[...] = _rope_slab(proj(O_WKV, LANES), cos, sin).astype(BF16)
    for s in range(N_HEADS * HEAD_DIM // LANES):
        q_ref[0, s * LANES:(s + 1) * LANES, :] = _rope_rows(proj_t(R_Q + s * LANES, LANES), cos_t, sin_t).astype(BF16)
    ckv_ref[0] = proj_t(R_CKV, KV_COLS)
    skv_ref[0, 0:LANES] = _rope_rows(proj_t(R_SKV, LANES), cos_t, sin_t)
    skv_ref[0, LANES:KV_COLS] = proj_t(R_SKV + LANES, LANES)
    wkv_ref[0, 0:LANES] = _rope_rows(proj_t(R_WKV, LANES), cos_t, sin_t)
    wkv_ref[0, LANES:KV_COLS] = proj_t(R_WKV + LANES, LANES)
    gn_ref[0] = _sigmoid(proj_t(R_GN, GATE_ROWS))


def _inproj_rows(x2, g, w_packed, cos_t, sin_t, tm):
    m = x2.shape[0]
    row = lambda i: (i, 0)
    const = lambda i: (0, 0)
    out_shapes = (
        jax.ShapeDtypeStruct((m, D_CONV), F32),
        jax.ShapeDtypeStruct((m, Q_EXP), BF16),
        jax.ShapeDtypeStruct((m, KV_COLS), F32),
        jax.ShapeDtypeStruct((m, KV_COLS), F32),
        jax.ShapeDtypeStruct((m, KV_COLS), F32),
        jax.ShapeDtypeStruct((m, LANES), F32),
        jax.ShapeDtypeStruct((m, 2 * D_MODEL), BF16),
    )
    return pl.pallas_call(
        _inproj_rows_kernel,
        grid=(m // tm,),
        in_specs=[
            pl.BlockSpec((tm, D_MODEL), row),
            pl.BlockSpec((1, D_MODEL), const),
            pl.BlockSpec(w_packed.shape, const),
            pl.BlockSpec((tm, LANES), const),
            pl.BlockSpec((tm, LANES), const),
        ],
        out_specs=tuple(pl.BlockSpec((tm, s.shape[1]), row) for s in out_shapes),
        out_shape=out_shapes,
        compiler_params=_params(1),
    )(x2, g, w_packed, cos_t, sin_t)


def _inproj_cols(x2, g, w_packed, wkv_t, cos_t, sin_t, cos_tt, sin_tt, b, t_len, tm):
    m = x2.shape[0]
    n_tab = t_len // tm
    row = lambda i: (i, 0)
    const = lambda i: (0, 0)
    kv_map = lambda i: (i // n_tab, 0, i % n_tab)
    kv_shape = jax.ShapeDtypeStruct((b, KV_COLS, t_len), F32)
    out_shapes = (
        jax.ShapeDtypeStruct((m, D_CONV), F32),
        jax.ShapeDtypeStruct((b, N_HEADS * HEAD_DIM, t_len), BF16),
        kv_shape, kv_shape, kv_shape,
        jax.ShapeDtypeStruct((m, LANES), BF16),
        jax.ShapeDtypeStruct((m, LANES), BF16),
        jax.ShapeDtypeStruct((b, GATE_ROWS, t_len), F32),
        jax.ShapeDtypeStruct((m, 2 * D_MODEL), BF16),
    )
    out_specs = (
        pl.BlockSpec((tm, D_CONV), row),
        pl.BlockSpec((1, N_HEADS * HEAD_DIM, tm), kv_map),
        pl.BlockSpec((1, KV_COLS, tm), kv_map),
        pl.BlockSpec((1, KV_COLS, tm), kv_map),
        pl.BlockSpec((1, KV_COLS, tm), kv_map),
        pl.BlockSpec((tm, LANES), row),
        pl.BlockSpec((tm, LANES), row),
        pl.BlockSpec((1, GATE_ROWS, tm), kv_map),
        pl.BlockSpec((tm, 2 * D_MODEL), row),
    )
    return pl.pallas_call(
        _inproj_cols_kernel,
        grid=(m // tm,),
        in_specs=[
            pl.BlockSpec((tm, D_MODEL), row),
            pl.BlockSpec((1, D_MODEL), const),
            pl.BlockSpec(w_packed.shape, const),
            pl.BlockSpec(wkv_t.shape, const),
            pl.BlockSpec((tm, LANES), lambda i: (i % n_tab, 0)),
            pl.BlockSpec((tm, LANES), lambda i: (i % n_tab, 0)),
            pl.BlockSpec((LANES, tm), lambda i: (0, i % n_tab)),
            pl.BlockSpec((LANES, tm), lambda i: (0, i % n_tab)),
        ],
        out_specs=out_specs,
        out_shape=out_shapes,
        compiler_params=_params(1),
    )(x2, g, w_packed, wkv_t, cos_t, sin_t, cos_tt, sin_tt)


def _ln_swish(c, lg, lb):
    mu = jnp.mean(c, axis=-1, keepdims=True)
    d = c - mu
    var = jnp.mean(d * d, axis=-1, keepdims=True)
    y = d * lax.rsqrt(var + EPS) * lg + lb
    return y * _sigmoid(y)


CONV_PAD = 32
CONV_CHUNK = 64


def _conv_prompt_kernel(u_ref, past_ref, w_ref, b_ref, lg_ref, lb_ref, o_ref, uf_ref, *, t_len):
    off = CONV_PAD - (CONV_K - 1)
    uf_ref[0:SUBLANES, :] = jnp.zeros((SUBLANES, D_CONV), F32)
    uf_ref[off:CONV_PAD, :] = past_ref[0]
    uf_ref[CONV_PAD:CONV_PAD + t_len, :] = u_ref[0]
    bias = b_ref[...]
    lg = lg_ref[...]
    lb = lb_ref[...]
    ct = CONV_CHUNK

    def body(i, carry):
        base = pl.multiple_of(i * ct, ct)
        xw = uf_ref[pl.ds(base, ct + CONV_PAD), :]
        acc = jnp.zeros((ct, D_CONV), F32)
        n_win = ct + CONV_PAD
        for r in range(SUBLANES):
            yr = xw if r == 0 else pltpu.roll(xw, n_win - r, 0)
            for a in range((CONV_PAD // SUBLANES) + 1):
                k = SUBLANES * a + r - off
                if 0 <= k < CONV_K:
                    acc = acc + w_ref[k:k + 1, :] * yr[SUBLANES * a:SUBLANES * a + ct, :]
        o_ref[0, pl.ds(base, ct), :] = _ln_swish(acc + bias, lg, lb).astype(BF16)
        return carry

    lax.fori_loop(0, t_len // ct, body, 0)


def _conv_prompt(u3, past3, dw_w, dw_b, ln_g, ln_b):
    b, t_len, _ = u3.shape
    const2 = lambda i: (0, 0)
    return pl.pallas_call(
        functools.partial(_conv_prompt_kernel, t_len=t_len),
        grid=(b,),
        in_specs=[
            pl.BlockSpec((1, t_len, D_CONV), lambda i: (i, 0, 0)),
            pl.BlockSpec((1, CONV_K - 1, D_CONV), lambda i: (i, 0, 0)),
            pl.BlockSpec((CONV_K, D_CONV), const2),
            pl.BlockSpec((1, D_CONV), const2),
            pl.BlockSpec((1, D_CONV), const2),
            pl.BlockSpec((1, D_CONV), const2),
        ],
        out_specs=pl.BlockSpec((1, t_len, D_CONV), lambda i: (i, 0, 0)),
        out_shape=jax.ShapeDtypeStruct((b, t_len, D_CONV), BF16),
        scratch_shapes=[pltpu.VMEM((t_len + CONV_PAD, D_CONV), F32)],
        compiler_params=_params(1),
    )(u3, past3, dw_w, dw_b, ln_g, ln_b)


def _conv_sample_kernel(past_ref, u_ref, w_ref, b_ref, lg_ref, lb_ref, o_ref, *, tq):
    bias = b_ref[...]
    lg = lg_ref[...]
    lb = lb_ref[...]
    n_past = CONV_K - 1
    for t in range(tq):
        acc = jnp.zeros(o_ref.shape[1:], F32)
        for j in range(t, n_past):
            acc = acc + w_ref[j - t:j - t + 1, :] * past_ref[j]
        for i in range(t + 1):
            k = n_past - t + i
            acc = acc + w_ref[k:k + 1, :] * u_ref[i]
        o_ref[t] = _ln_swish(acc + bias, lg, lb).astype(BF16)


def _conv_sample(past_t, u_t, dw_w, dw_b, ln_g, ln_b):
    n_past, bd, _ = past_t.shape
    tq = u_t.shape[0]
    bt = _pick_tile(bd, 32)
    const2 = lambda i: (0, 0)
    return pl.pallas_call(
        functools.partial(_conv_sample_kernel, tq=tq),
        grid=(bd // bt,),
        in_specs=[
            pl.BlockSpec((n_past, bt, D_CONV), lambda i: (0, i, 0)),
            pl.BlockSpec((tq, bt, D_CONV), lambda i: (0, i, 0)),
            pl.BlockSpec((CONV_K, D_CONV), const2),
            pl.BlockSpec((1, D_CONV), const2),
            pl.BlockSpec((1, D_CONV), const2),
            pl.BlockSpec((1, D_CONV), const2),
        ],
        out_specs=pl.BlockSpec((tq, bt, D_CONV), lambda i: (0, i, 0)),
        out_shape=jax.ShapeDtypeStruct((tq, bd, D_CONV), BF16),
        compiler_params=_params(1),
    )(past_t, u_t, dw_w, dw_b, ln_g, ln_b)


def _chunkproj_kernel(x_ref, wk_ref, wv_ref, pk_ref, pv_ref, ab_ref, t_ref, *, n_pages, pages_on_lanes):
    n_rows = n_pages * CHUNKS_PER_PAGE
    for kv, (w_ref, p_ref) in enumerate(((wk_ref, pk_ref), (wv_ref, pv_ref))):
        for p in range(n_pages):
            page = x_ref[0, kv, :, p * PAGE_SIZE:(p + 1) * PAGE_SIZE] if pages_on_lanes else x_ref[p, kv]
            t_ref[p * PAGE_SIZE:(p + 1) * PAGE_SIZE, :] = page.T
        xs = jnp.concatenate([t_ref[pl.ds(l, n_rows, stride=CMP_STRIDE), :] for l in range(CMP_STRIDE)],
                             axis=1).astype(BF16)
        w = w_ref[...]
        part = _dot(xs, w)
        posb = _dot(p_ref[...], w)
        c0 = kv * 2 * LANES
        ab_ref[:, c0:c0 + LANES] = part[:, 0:LANES] + posb[0:1, 0:LANES]
        ab_ref[:, c0 + LANES:c0 + 2 * LANES] = part[:, LANES:2 * LANES] + posb[1:2, LANES:2 * LANES]


def _chunkproj(x4, wk2, wv2, pk2, pv2, page_off, n_total, pages_on_lanes):
    if pages_on_lanes:
        n_pages = x4.shape[3] // PAGE_SIZE
        n_steps = x4.shape[0]
        x_spec = pl.BlockSpec((1, 2, LANES, x4.shape[3]), lambda i: (i, 0, 0, 0))
    else:
        n_pages = _pick_tile(n_total, 64, 1)
        n_steps = n_total // n_pages
        off = page_off // n_pages
        x_spec = pl.BlockSpec((n_pages, 2, LANES, PAGE_SIZE), lambda i: (i + off, 0, 0, 0))
    const2 = lambda i: (0, 0)
    rows = n_pages * CHUNKS_PER_PAGE
    return pl.pallas_call(
        functools.partial(_chunkproj_kernel, n_pages=n_pages, pages_on_lanes=pages_on_lanes),
        grid=(n_steps,),
        in_specs=[
            x_spec,
            pl.BlockSpec(wk2.shape, const2),
            pl.BlockSpec(wv2.shape, const2),
            pl.BlockSpec(pk2.shape, const2),
            pl.BlockSpec(pv2.shape, const2),
        ],
        out_specs=pl.BlockSpec((rows, 4 * LANES), lambda i: (i, 0)),
        out_shape=jax.ShapeDtypeStruct((n_steps * rows, 4 * LANES), F32),
        scratch_shapes=[pltpu.VMEM((n_pages * PAGE_SIZE, LANES), F32)],
        compiler_params=_params(1),
    )(x4, wk2, wv2, pk2, pv2)


def _compressed_kv_f32(ab, cosc, sinc):
    n = ab.shape[0]
    kc = ab[:, 0:LANES] + pltpu.roll(ab[:, LANES:2 * LANES], n - 1, 0)
    vc = ab[:, 2 * LANES:3 * LANES] + pltpu.roll(ab[:, 3 * LANES:4 * LANES], n - 1, 0)
    return _rope_slab(kc, cosc, sinc), vc


def _compressed_kv(ab, cosc, sinc):
    kc, vc = _compressed_kv_f32(ab, cosc, sinc)
    return kc.astype(BF16), vc.astype(BF16)


def _softmax_parts(s, mask):
    sm = jnp.where(mask, s, NEG)
    m = jnp.max(sm, axis=-1, keepdims=True)
    p = jnp.where(mask, jnp.exp(sm - m), 0.0)
    l = jnp.maximum(jnp.sum(p, axis=-1, keepdims=True), 1e-30)
    return p, l


def _select_blocks(imp, qpos, n_sel, axis):
    jidx = lax.broadcasted_iota(jnp.int32, imp.shape, axis)
    cur = lax.shift_right_logical(qpos, 6)
    forced = (jidx == 0) | (jidx == cur) | (jidx == cur - 1)
    valid = (jidx * SEL_BLOCK <= qpos) & (jidx < n_sel)
    v = jnp.where(valid, jnp.where(forced, jnp.inf, imp), -jnp.inf)
    rank = jnp.zeros(imp.shape, jnp.int32)
    for k in range(n_sel):
        vk = v[k:k + 1, :] if axis == 0 else v[:, k:k + 1]
        ahead = (vk > v) | ((vk == v) & (jidx > k))
        rank = rank + ahead.astype(jnp.int32)
    return (rank < min(N_SEL, n_sel)) & valid


def _overlap(n_cmp, cmp_axis):
    i = lax.broadcasted_iota(jnp.int32, (LANES, LANES), cmp_axis)
    j = lax.broadcasted_iota(jnp.int32, (LANES, LANES), 1 - cmp_axis)
    hit = (i * CMP_STRIDE < (j + 1) * SEL_BLOCK) & (i * CMP_STRIDE + CMP_BLOCK > j * SEL_BLOCK) & (i < n_cmp)
    return hit.astype(F32)


Q_TILE = 256
K_TILE = 256


def _attn_prompt_kernel(q_ref, gn_ref, skr_ref, svt_ref, wkr_ref, wvt_ref, ab_ref, cosc_ref, sinc_ref, o_ref,
                        kc_ref, vct_ref, bias_ref, m_ref, acc_ref, out_ref, *, t_len):
    qb = pl.program_id(1)
    n_chunk = t_len // CMP_STRIDE
    n_cmp = (t_len - CMP_BLOCK) // CMP_STRIDE + 1
    n_sel = -(-t_len // SEL_BLOCK)
    sel_rows = -(-n_sel // SUBLANES) * SUBLANES
    qt, kt_sz = Q_TILE, K_TILE
    q_tiles = qt // LANES
    g_lanes = HPG * qt
    n_lane_tiles = N_HEADS * q_tiles

    @pl.when(qb == 0)
    def _():
        kc, vc = _compressed_kv_f32(ab_ref[...], cosc_ref[...], sinc_ref[...])
        kc_ref[...] = kc.astype(BF16)
        vct_ref[...] = vc.T.astype(BF16)

    q0 = qb * qt
    qpos = q0 + lax.broadcasted_iota(jnp.int32, (1, qt), 1)

    def lane_tile(c):
        return slice(c * LANES, (c + 1) * LANES)

    def q_part(c):
        return slice((c % q_tiles) * LANES, (c % q_tiles + 1) * LANES)

    def group_of(c):
        return c // (HPG * q_tiles)

    def group_lanes(g):
        return slice(g * g_lanes, (g + 1) * g_lanes)

    q_zero = jnp.zeros((HEAD_DIM, qt), BF16)

    def q_slab(hh):
        q_h = q_ref[0, hh * HEAD_DIM:(hh + 1) * HEAD_DIM, :]
        return jnp.concatenate([q_h, q_zero] if hh < HPG else [q_zero, q_h], axis=0)

    q_all = jnp.concatenate([q_slab(hh) for hh in range(N_HEADS)], axis=1)

    def gates(branch):
        r0 = branch * N_HEADS
        return jnp.concatenate([gn_ref[0, r0 + hh:r0 + hh + 1, :] for hh in range(N_HEADS)], axis=1)

    nrow = lax.broadcasted_iota(jnp.int32, (n_chunk, qt), 0)
    mask_c = (nrow * CMP_STRIDE + CMP_BLOCK - 1 <= qpos) & (nrow < n_cmp)
    acc_ref[...] = _dot(kc_ref[...], q_all)
    hs = [[jnp.zeros((n_chunk, LANES), F32) for _ in range(q_tiles)] for _ in range(N_KV)]
    p_parts = []
    for c in range(n_lane_tiles):
        mask = mask_c[:, q_part(c)]
        sm = jnp.where(mask, acc_ref[:, lane_tile(c)], NEG)
        p_c = jnp.where(mask, jnp.exp(sm - jnp.max(sm, axis=0, keepdims=True)), 0.0)
        p_c = p_c / jnp.maximum(jnp.sum(p_c, axis=0, keepdims=True), 1e-30)
        hs[group_of(c)][c % q_tiles] = hs[group_of(c)][c % q_tiles] + p_c
        p_parts.append(p_c.astype(BF16))
    out_ref[...] = gates(0) * _dot(vct_ref[...], jnp.concatenate(p_parts, axis=1))

    sel_t = []
    for g in range(N_KV):
        imp_t = jnp.dot(_overlap(n_cmp, 1), jnp.concatenate(hs[g], axis=1), preferred_element_type=F32,
                        precision=lax.Precision.HIGHEST)
        s_g = _select_blocks(imp_t[0:sel_rows], qpos, n_sel, 0).astype(F32)
        sel_t.append(jnp.concatenate([s_g, jnp.zeros((LANES - sel_rows, qt), F32)], axis=0).astype(BF16))

    def sel_bias(k_idx, kpos):
        krow = lax.broadcasted_iota(jnp.int32, (kt_sz, LANES), 0)
        jcol = lax.broadcasted_iota(jnp.int32, (kt_sz, LANES), 1)
        expand_t = (jcol == k_idx * (kt_sz // SEL_BLOCK) + lax.shift_right_logical(krow, 6)).astype(BF16)
        for g in range(N_KV):
            bias_ref[g] = jnp.where((_dot(expand_t, sel_t[g]) > 0.5) & (kpos <= qpos), 0.0, NEG)

    def win_bias(k_idx, kpos):
        b = jnp.where((kpos <= qpos) & (kpos > qpos - WINDOW), 0.0, NEG)
        for g in range(N_KV):
            bias_ref[g] = b

    def run_branch(kr_ref, vt_ref, n_tiles, tile_of, bias_fn, branch):
        m_ref[...] = jnp.full(m_ref.shape, NEG, F32)
        acc_ref[...] = jnp.zeros(acc_ref.shape, F32)

        def body(i, carry):
            k_idx = tile_of(i)
            k0 = pl.multiple_of(k_idx * kt_sz, kt_sz)
            kpos = k0 + lax.broadcasted_iota(jnp.int32, (kt_sz, qt), 0)
            bias_fn(k_idx, kpos)
            s = _dot(kr_ref[pl.ds(k0, kt_sz), :], q_all)
            vt = vt_ref[0, :, pl.ds(k0, kt_sz)]
            vrow = lax.broadcasted_iota(jnp.int32, (LANES, kt_sz), 0)
            p_parts, a_parts = [], []
            for c in range(n_lane_tiles):
                sb = s[:, lane_tile(c)] + bias_ref[group_of(c), :, q_part(c)]
                m_old = m_ref[0:1, lane_tile(c)]
                m_new = jnp.maximum(m_old, jnp.max(sb, axis=0, keepdims=True))
                p_parts.append(jnp.exp(sb - m_new).astype(BF16))
                a_parts.append(jnp.exp(m_old - m_new))
                m_ref[:, lane_tile(c)] = jnp.broadcast_to(m_new, (SUBLANES, LANES))
            for g in range(N_KV):
                v_own = (vrow >= g * HEAD_DIM) & (vrow < (g + 1) * HEAD_DIM)
                vaug = jnp.where(v_own, vt, 1.0).astype(BF16)
                tiles = range(g * HPG * q_tiles, (g + 1) * HPG * q_tiles)
                alpha = jnp.concatenate([a_parts[c] for c in tiles], axis=1)
                p = jnp.concatenate([p_parts[c] for c in tiles], axis=1)
                acc_ref[:, group_lanes(g)] = alpha * acc_ref[:, group_lanes(g)] + _dot(vaug, p)
            return carry

        lax.fori_loop(0, n_tiles, body, 0)
        gt = gates(branch)
        for g in range(N_KV):
            sum_row = (1 - g) * HEAD_DIM
            acc = acc_ref[:, group_lanes(g)]
            out_ref[:, group_lanes(g)] = (out_ref[:, group_lanes(g)]
                                          + gt[:, group_lanes(g)] * (acc / acc[sum_row:sum_row + 1, :]))

    run_branch(skr_ref, svt_ref, qb + 1, lambda i: i, sel_bias, 1)
    run_branch(wkr_ref, wvt_ref, jnp.minimum(qb, WINDOW // kt_sz) + 1, lambda i: qb - i, win_bias, 2)

    lane = lax.broadcasted_iota(jnp.int32, (qt, LANES), 1)
    for pair in range(N_HEADS // 2):
        a = out_ref[:, (2 * pair) * qt:(2 * pair + 1) * qt].T
        b = out_ref[:, (2 * pair + 1) * qt:(2 * pair + 2) * qt].T
        if 2 * pair < HPG:
            slab = jnp.where(lane < HEAD_DIM, a, pltpu.roll(b, HEAD_DIM, 1))
        else:
            slab = jnp.where(lane < HEAD_DIM, pltpu.roll(a, HEAD_DIM, 1), b)
        o_ref[:, pair * LANES:(pair + 1) * LANES] = slab.astype(BF16)


def _attn_prompt(q_t, gn_t, skr, skv_t, wkr, wkv_t, ab2, cosc, sinc, b, t_len):
    n_chunk = t_len // CMP_STRIDE
    nqb = t_len // Q_TILE
    const2 = lambda i, j: (0, 0)
    per_b2 = lambda i, j: (i, 0)
    v_rows = lambda i, j: (i, 1, 0)
    return pl.pallas_call(
        functools.partial(_attn_prompt_kernel, t_len=t_len),
        grid=(b, nqb),
        in_specs=[
            pl.BlockSpec((1, N_HEADS * HEAD_DIM, Q_TILE), lambda i, j: (i, 0, j)),
            pl.BlockSpec((1, GATE_ROWS, Q_TILE), lambda i, j: (i, 0, j)),
            pl.BlockSpec((t_len, LANES), per_b2),
            pl.BlockSpec((1, LANES, t_len), v_rows),
            pl.BlockSpec((t_len, LANES), per_b2),
            pl.BlockSpec((1, LANES, t_len), v_rows),
            pl.BlockSpec((n_chunk, 4 * LANES), per_b2),
            pl.BlockSpec((n_chunk, LANES), const2),
            pl.BlockSpec((n_chunk, LANES), const2),
        ],
        out_specs=pl.BlockSpec((Q_TILE, N_HEADS * HEAD_DIM), lambda i, j: (i * nqb + j, 0)),
        out_shape=jax.ShapeDtypeStruct((b * t_len, N_HEADS * HEAD_DIM), BF16),
        scratch_shapes=[
            pltpu.VMEM((n_chunk, LANES), BF16),
            pltpu.VMEM((LANES, n_chunk), BF16),
            pltpu.VMEM((N_KV, K_TILE, Q_TILE), F32),
            pltpu.VMEM((SUBLANES, N_HEADS * Q_TILE), F32),
            pltpu.VMEM((LANES, N_HEADS * Q_TILE), F32),
            pltpu.VMEM((LANES, N_HEADS * Q_TILE), F32),
        ],
        compiler_params=_params(2),
    )(q_t, gn_t, skr, skv_t, wkr, wkv_t, ab2, cosc, sinc)


Q_ROWS = 8
SEQ_PER_STEP = 4


def _attn_sample_kernel(pt_ref, q_ref, gn_ref, sknew_ref, wknew_ref, win_ref, cosc_ref, sinc_ref, *rest,
                        past_len, tq, n_pages, n_seq):
    del pt_ref
    sel_pages = [rest[e * n_pages:(e + 1) * n_pages] for e in range(n_seq)]
    ab_pages = [rest[(n_seq + e) * n_pages:(n_seq + e + 1) * n_pages] for e in range(n_seq)]
    o_ref = rest[2 * n_seq * n_pages]
    seqs = range(n_seq)
    t_all = past_len + tq
    n_chunk = past_len // CMP_STRIDE
    n_cmp = (t_all - CMP_BLOCK) // CMP_STRIDE + 1
    n_sel = -(-t_all // SEL_BLOCK)
    rows = N_HEADS * Q_ROWS
    all_rows = n_seq * rows
    win_buf = win_ref.shape[3]

    def stack(parts):
        return jnp.concatenate(parts, axis=0)

    def per_seq(a, e):
        return a[e * rows:(e + 1) * rows]

    qbd = [stack([q_ref[e, :, hh * LANES:(hh + 1) * LANES] for hh in range(N_HEADS)]) for e in seqs]
    qpos = past_len + lax.broadcasted_iota(jnp.int32, (all_rows, 1), 0) % Q_ROWS

    cosc = cosc_ref[...]
    sinc = sinc_ref[...]
    kvc = [_compressed_kv(stack([r[0] for r in ab_pages[e]]), cosc, sinc) for e in seqs]
    ncol = lax.broadcasted_iota(jnp.int32, (all_rows, n_chunk), 1)
    mask_c = (ncol * CMP_STRIDE + CMP_BLOCK - 1 <= qpos) & (ncol < n_cmp)
    p_c, l_c = _softmax_parts(stack([_dot_nt(qbd[e], kvc[e][0]) for e in seqs]), mask_c)
    p_c = p_c / l_c
    p_c16 = p_c.astype(BF16)
    o_c = stack([_dot(per_seq(p_c16, e), kvc[e][1]) for e in seqs])

    hs = []
    for e in seqs:
        for g in range(N_KV):
            r0 = e * rows + g * HPG * Q_ROWS
            acc = p_c[r0:r0 + Q_ROWS]
            for h in range(1, HPG):
                acc = acc + p_c[r0 + h * Q_ROWS:r0 + (h + 1) * Q_ROWS]
            hs.append(acc)
    imp = jnp.dot(stack(hs), _overlap(n_cmp, 0), preferred_element_type=F32, precision=lax.Precision.HIGHEST)
    sel_g = _select_blocks(imp, qpos[0:n_seq * N_KV * Q_ROWS], n_sel, 1).astype(BF16)
    sel_rows = stack([sel_g[(e * N_KV + g) * Q_ROWS:(e * N_KV + g + 1) * Q_ROWS]
                      for e in seqs for g in range(N_KV) for _ in range(HPG)])

    def new_rows_tile(ref, e, c0):
        return stack([ref[e, :, c0:c0 + LANES], jnp.zeros((LANES - Q_ROWS, LANES), F32)]).astype(BF16)

    n_keys = (n_pages + 1) * PAGE_SIZE
    s_s = stack([jnp.concatenate([_dot(qbd[e], r[0, 0].astype(BF16)) for r in sel_pages[e]]
                                 + [_dot_nt(qbd[e], new_rows_tile(sknew_ref, e, 0))], axis=1) for e in seqs])
    jrow = lax.broadcasted_iota(jnp.int32, (LANES, n_keys), 0)
    kcol = lax.broadcasted_iota(jnp.int32, (LANES, n_keys), 1)
    expand = (jrow == lax.shift_right_logical(kcol, 6)).astype(BF16)
    kpos = lax.broadcasted_iota(jnp.int32, (all_rows, n_keys), 1)
    mask_s = (_dot(sel_rows, expand) > 0.5) & (kpos <= qpos)
    p_s, l_s = _softmax_parts(s_s, mask_s)
    p_s = p_s.astype(BF16)
    o_s = []
    for e in seqs:
        p_e = per_seq(p_s, e)
        acc = _dot(p_e[:, n_pages * PAGE_SIZE:], new_rows_tile(sknew_ref, e, LANES))
        for i in range(n_pages):
            acc = acc + _dot_nt(p_e[:, i * PAGE_SIZE:(i + 1) * PAGE_SIZE], sel_pages[e][i][0, 1].astype(BF16))
        o_s.append(acc)
    o_s = stack(o_s) / l_s

    s_w = stack([jnp.concatenate([_dot(qbd[e], win_ref[e, 0].astype(BF16)),
                                  _dot_nt(qbd[e], new_rows_tile(wknew_ref, e, 0))], axis=1) for e in seqs])
    wcol = lax.broadcasted_iota(jnp.int32, (all_rows, win_buf + LANES), 1)
    kpos_w = past_len - win_buf + wcol
    mask_w = (kpos_w <= qpos) & (kpos_w > qpos - WINDOW)
    p_w, l_w = _softmax_parts(s_w, mask_w)
    p_w = p_w.astype(BF16)
    o_w = stack([_dot_nt(per_seq(p_w, e)[:, 0:win_buf], win_ref[e, 1].astype(BF16))
                 + _dot(per_seq(p_w, e)[:, win_buf:], new_rows_tile(wknew_ref, e, LANES)) for e in seqs]) / l_w

    for e in seqs:
        gn = gn_ref[e]
        for hh in range(N_HEADS):
            r = slice(e * rows + hh * Q_ROWS, e * rows + (hh + 1) * Q_ROWS)
            out = (gn[:, hh:hh + 1] * o_c[r]
                   + gn[:, N_HEADS + hh:N_HEADS + hh + 1] * o_s[r]
                   + gn[:, 2 * N_HEADS + hh:2 * N_HEADS + hh + 1] * o_w[r])
            o_ref[e, :, hh * LANES:(hh + 1) * LANES] = out.astype(BF16)


def _attn_sample(pt_flat, q3, gn3, sknew3, wknew3, win4, cosc, sinc, sel_pool4, ab_pool3,
                 layer, n_pool, bd, past_len, tq):
    n_pages = past_len // PAGE_SIZE
    win_buf = win4.shape[3]
    n_chunk = past_len // CMP_STRIDE
    n_seq = _pick_tile(bd, SEQ_PER_STEP, 1)
    per_b = lambda i, pt: (i, 0, 0)
    const2 = lambda i, pt: (0, 0)
    sel_specs = [pl.BlockSpec((1, 2, LANES, PAGE_SIZE),
                              lambda i, pt, e=e, p=p: (layer * n_pool + pt[(i * n_seq + e) * n_pages + p], 0, 0, 0))
                 for e in range(n_seq) for p in range(n_pages)]
    ab_specs = [pl.BlockSpec((1, CHUNKS_PER_PAGE, 4 * LANES),
                             lambda i, pt, e=e, p=p: (pt[(i * n_seq + e) * n_pages + p], 0, 0))
                for e in range(n_seq) for p in range(n_pages)]
    grid_spec = pltpu.PrefetchScalarGridSpec(
        num_scalar_prefetch=1,
        grid=(bd // n_seq,),
        in_specs=[
            pl.BlockSpec((n_seq, Q_ROWS, Q_EXP), per_b),
            pl.BlockSpec((n_seq, Q_ROWS, LANES), per_b),
            pl.BlockSpec((n_seq, Q_ROWS, KV_COLS), per_b),
            pl.BlockSpec((n_seq, Q_ROWS, KV_COLS), per_b),
            pl.BlockSpec((n_seq, 2, LANES, win_buf), lambda i, pt: (layer * (bd // n_seq) + i, 0, 0, 0)),
            pl.BlockSpec((n_chunk, LANES), const2),
            pl.BlockSpec((n_chunk, LANES), const2),
        ] + sel_specs + ab_specs,
        out_specs=pl.BlockSpec((n_seq, Q_ROWS, Q_EXP), per_b),
    )
    n_refs = n_seq * n_pages
    return pl.pallas_call(
        functools.partial(_attn_sample_kernel, past_len=past_len, tq=tq, n_pages=n_pages, n_seq=n_seq),
        grid_spec=grid_spec,
        out_shape=jax.ShapeDtypeStruct((bd, Q_ROWS, Q_EXP), BF16),
        compiler_params=_params(1),
    )(pt_flat, q3, gn3, sknew3, wknew3, win4, cosc, sinc, *([sel_pool4] * n_refs), *([ab_pool3] * n_refs))


def _post_kernel(c_ref, o_ref, gm_ref, x_ref, wc_ref, wn_ref, wo_ref, g_ref, wu_ref, wd_ref, gf_ref, y_ref, *, final):
    conv_out = _dot(c_ref[...], wc_ref[...])
    nsa_out = _dot(o_ref[...], wn_ref[...])
    gm = gm_ref[...].astype(F32)
    merged = gm[:, 0:D_MODEL] * conv_out + gm[:, D_MODEL:2 * D_MODEL] * nsa_out
    x = x_ref[...] + _dot(merged.astype(BF16), wo_ref[...])
    h = _rms_bf16(x, g_ref[...])
    acc = x
    for f in range(D_FF // FF_TILE):
        up = jnp.maximum(_dot(h, wu_ref[:, f * FF_TILE:(f + 1) * FF_TILE]), 0.0)
        acc = acc + _dot((up * up).astype(BF16), wd_ref[f * FF_TILE:(f + 1) * FF_TILE, :])
    if final:
        ms2 = jnp.mean(acc * acc, axis=-1, keepdims=True)
        acc = acc * lax.rsqrt(ms2 + EPS) * gf_ref[...]
    y_ref[...] = acc


FF_TILE = 1024


def _post(c2, o2, gm2, x2, wc, wn, wo, g, wu, wd, gf, tm, final):
    m = x2.shape[0]
    row = lambda i: (i, 0)
    const = lambda i: (0, 0)
    resident = lambda a: pl.BlockSpec(a.shape, const, pipeline_mode=pl.Buffered(1))
    return pl.pallas_call(
        functools.partial(_post_kernel, final=final),
        grid=(m // tm,),
        in_specs=[
            pl.BlockSpec((tm, c2.shape[1]), row),
            pl.BlockSpec((tm, o2.shape[1]), row),
            pl.BlockSpec((tm, 2 * D_MODEL), row),
            pl.BlockSpec((tm, D_MODEL), row),
            resident(wc), resident(wn), resident(wo),
            pl.BlockSpec((1, D_MODEL), const),
            resident(wu), resident(wd),
            pl.BlockSpec((1, D_MODEL), const),
        ],
        out_specs=pl.BlockSpec((tm, D_MODEL), row),
        out_shape=jax.ShapeDtypeStruct((m, D_MODEL), F32),
        compiler_params=_params(1),
    )(c2, o2, gm2, x2, wc, wn, wo, g, wu, wd, gf)


def _shift_append_kernel(old_ref, new_ref, o_ref, *, n_new):
    o_ref[...] = jnp.concatenate([old_ref[..., n_new:], new_ref[...]], axis=-1)


def _shift_append(old, new, rows_per_step):
    n, a, b, keep = old.shape
    n_new = new.shape[-1]
    r = _pick_tile(n, rows_per_step, 1)
    return pl.pallas_call(
        functools.partial(_shift_append_kernel, n_new=n_new),
        grid=(n // r,),
        in_specs=[pl.BlockSpec((r, a, b, keep), lambda i: (i, 0, 0, 0)),
                  pl.BlockSpec((r, a, b, n_new), lambda i: (i, 0, 0, 0))],
        out_specs=pl.BlockSpec((r, a, b, keep), lambda i: (i, 0, 0, 0)),
        out_shape=jax.ShapeDtypeStruct(old.shape, old.dtype),
        compiler_params=_params(1),
    )(old, new)


def _rope_tables(pos):
    half = HEAD_DIM // 2
    inv = jnp.power(ROPE_THETA, -jnp.arange(half, dtype=F32) / half)
    ang = pos.astype(F32)[:, None] * inv[None, :]
    cos = jnp.cos(ang)
    sin = jnp.sin(ang)
    return jnp.concatenate([cos, cos, cos, cos], axis=1), jnp.concatenate([-sin, sin, -sin, sin], axis=1)


def _pack_w_in(w):
    o0 = 2 * D_CONV
    o1 = o0 + N_HEADS * HEAD_DIM
    o4 = o1 + 3 * KV_COLS
    o5 = o4 + N_GATES
    wq = w[:, o0:o1].reshape(D_MODEL, N_HEADS, HEAD_DIM) * (HEAD_DIM ** -0.5)
    zero = jnp.zeros_like(wq)
    in_g0 = (jnp.arange(N_HEADS) < HPG)[None, :, None]
    wq_exp = jnp.concatenate([jnp.where(in_g0, wq, zero), jnp.where(in_g0, zero, wq)], axis=2)
    wgn = jnp.pad(w[:, o4:o5], ((0, 0), (0, LANES - N_GATES)))
    wq_exp = wq_exp.reshape(D_MODEL, Q_EXP)
    packed = jnp.concatenate([w[:, :o0], wq_exp, w[:, o1:o4], wgn, w[:, o5:]], axis=1)
    w_t = jnp.concatenate([wq.reshape(D_MODEL, N_HEADS * HEAD_DIM), w[:, o1:o4], wgn[:, :GATE_ROWS]], axis=1).T
    return packed.astype(BF16), w_t.astype(BF16)


def _pack_w_nsa_out(w):
    wh = w.reshape(N_HEADS, HEAD_DIM, D_MODEL)
    zero = jnp.zeros_like(wh)
    in_g0 = (jnp.arange(N_HEADS) < HPG)[:, None, None]
    return jnp.concatenate([jnp.where(in_g0, wh, zero), jnp.where(in_g0, zero, wh)], axis=1).reshape(Q_EXP, D_MODEL).astype(BF16)


def _pack_w_cmp(w_kv, pos_kv):
    lo, hi = w_kv[:CMP_STRIDE], w_kv[CMP_STRIDE:]
    eye = jnp.eye(N_KV, dtype=w_kv.dtype)
    blk = lambda part: jnp.einsum('lde,gh->lgdhe', part, eye).reshape(CMP_STRIDE * N_KV * HEAD_DIM, N_KV * HEAD_DIM)
    w2 = jnp.concatenate([blk(lo), blk(hi)], axis=1).astype(BF16)
    tile = lambda p: jnp.broadcast_to(p[:, None, :], (CMP_STRIDE, N_KV, HEAD_DIM)).reshape(1, -1)
    p2 = jnp.concatenate([tile(pos_kv[:CMP_STRIDE]), tile(pos_kv[CMP_STRIDE:]),
                          jnp.zeros((SUBLANES - 2, CMP_STRIDE * N_KV * HEAD_DIM), pos_kv.dtype)], axis=0).astype(BF16)
    return w2, p2


def _feature_major(a):
    lead = a.shape[:-4]
    n = len(lead)
    a = jnp.transpose(a, tuple(range(n)) + (n + 1, n + 2, n + 3, n))
    return a.reshape(lead + (2, N_KV * HEAD_DIM, a.shape[-1]))


def _position_major(a_t, rows):
    b = a_t.shape[0]
    return jnp.transpose(a_t.reshape(b, 2, N_KV, HEAD_DIM, rows), (0, 4, 1, 2, 3))


def kernel(x_prompt, x_sample, cache_cmp_kv, cache_sel_kv, state_win_kv, state_conv, page_table, norm_mix_g, w_in, conv_dw_w, conv_dw_b, conv_ln_g, conv_ln_b, w_conv_out, cmp_pos, w_cmp, w_nsa_out, w_out, norm_mlp_g, w_up, w_down, norm_final_g):
    depth = w_in.shape[0]
    bp, tp, _ = x_prompt.shape
    bd, tq, _ = x_sample.shape
    n_pool = cache_cmp_kv.shape[1]
    n_pages = page_table.shape[1]
    past_len = n_pages * PAGE_SIZE
    win_buf = state_win_kv.shape[2]
    assert tp % Q_TILE == 0 and tp // CMP_STRIDE == LANES and tp >= WINDOW
    assert past_len // CMP_STRIDE == LANES and win_buf == WINDOW and tq <= Q_ROWS

    mp, ms = bp * tp, bd * tq
    tm_p = _pick_tile(tp, 512)
    tm_s = _pick_tile(ms, 256)
    assert tp % tm_p == 0 and tm_s % tq == 0

    cos_p, sin_p = _rope_tables(jnp.arange(tp))
    cos_pt, sin_pt = cos_p.T, sin_p.T
    cos_s, sin_s = _rope_tables(past_len + jnp.arange(tm_s) % tq)
    n_chunk = tp // CMP_STRIDE
    cos_c, sin_c = _rope_tables(jnp.arange(n_chunk) * CMP_STRIDE + CMP_BLOCK - 1)

    pt_flat = page_table.reshape(-1).astype(jnp.int32)
    cmp_pool4 = _feature_major(cache_cmp_kv).reshape(depth * n_pool, 2, LANES, PAGE_SIZE)
    sel_pool4 = _feature_major(cache_sel_kv).reshape(depth * n_pool, 2, LANES, PAGE_SIZE)
    win4 = _feature_major(state_win_kv).reshape(depth * bd, 2, LANES, win_buf)
    zeros_conv = jnp.zeros((bp, CONV_K - 1, D_CONV), F32)

    xp = x_prompt.reshape(mp, D_MODEL)
    xs = x_sample.reshape(ms, D_MODEL)
    outs = [[] for _ in range(8)]
    for l in range(depth):
        w_packed, w_t = _pack_w_in(w_in[l])
        wn_exp = _pack_w_nsa_out(w_nsa_out[l])
        wn = w_nsa_out[l].astype(BF16)
        wc, wo = w_conv_out[l].astype(BF16), w_out[l].astype(BF16)
        wu, wd = w_up[l].astype(BF16), w_down[l].astype(BF16)
        wk2, pk2 = _pack_w_cmp(w_cmp[l, 0], cmp_pos[l, 0])
        wv2, pv2 = _pack_w_cmp(w_cmp[l, 1], cmp_pos[l, 1])
        g_mix, g_mlp = norm_mix_g[l][None], norm_mlp_g[l][None]
        dw_b, ln_g, ln_b = conv_dw_b[l][None], conv_ln_g[l][None], conv_ln_b[l][None]
        gf = norm_final_g[None]
        final = l == depth - 1

        u, q_t, ckv_t, skv_t, wkv_tm, skr, wkr, gn_t, gm = _inproj_cols(xp, g_mix, w_packed, w_t, cos_p, sin_p,
                                                                        cos_pt, sin_pt, bp, tp, tm_p)
        c_act = _conv_prompt(u.reshape(bp, tp, D_CONV), zeros_conv, conv_dw_w[l], dw_b, ln_g, ln_b)
        ab = _chunkproj(ckv_t.reshape(bp, 2, LANES, tp), wk2, wv2, pk2, pv2, 0, 0, True)
        o = _attn_prompt(q_t, gn_t, skr, skv_t, wkr, wkv_tm, ab, cos_c, sin_c, bp, tp)
        xp = _post(c_act.reshape(mp, D_CONV), o, gm, xp, wc, wn, wo, g_mlp, wu, wd, gf, tm_p, final)
        keep = min(WINDOW, tp)
        outs[0].append(_position_major(ckv_t, tp))
        outs[2].append(_position_major(skv_t, tp))
        outs[4].append(_position_major(wkv_tm[:, :, tp - keep:], keep))
        outs[6].append(u.reshape(bp, tp, D_CONV)[:, tp - (CONV_K - 1):])

        u, q, ckv, skv, wkv, gn, gm = _inproj_rows(xs, g_mix, w_packed, cos_s, sin_s, tm_s)
        u3 = u.reshape(bd, tq, D_CONV)
        c_t = _conv_sample(jnp.swapaxes(state_conv[l], 0, 1), jnp.swapaxes(u3, 0, 1), conv_dw_w[l], dw_b, ln_g, ln_b)
        c_act = jnp.swapaxes(c_t, 0, 1).reshape(ms, D_CONV)
        ab_pool = _chunkproj(cmp_pool4, wk2, wv2, pk2, pv2, l * n_pool, n_pool, False)
        pad_q = lambda a: jnp.pad(a.reshape(bd, tq, -1), ((0, 0), (0, Q_ROWS - tq), (0, 0)))
        o = _attn_sample(pt_flat, pad_q(q), pad_q(gn), pad_q(skv), pad_q(wkv), win4, cos_c, sin_c, sel_pool4,
                         ab_pool.reshape(n_pool, CHUNKS_PER_PAGE, 4 * LANES), l, n_pool, bd, past_len, tq)
        xs = _post(c_act, o[:, :tq].reshape(ms, Q_EXP), gm, xs, wc, wn_exp, wo, g_mlp, wu, wd, gf, tm_s, final)
        wkv5 = wkv.reshape(bd, tq, 2, N_KV, HEAD_DIM)
        outs[1].append(ckv.reshape(bd, tq, 2, N_KV, HEAD_DIM))
        outs[3].append(skv.reshape(bd, tq, 2, N_KV, HEAD_DIM))
        outs[5].append(wkv5)
        outs[7].append(u3)

    y_prompt = xp.reshape(bp, tp, D_MODEL)
    y_sample = xs.reshape(bd, tq, D_MODEL)
    outs = [jnp.stack(o) for o in outs]
    win_new = _shift_append(win4, _feature_major(outs[5]).reshape(depth * bd, 2, LANES, tq), 8)
    outs[5] = _position_major(win_new.reshape(depth * bd, KV_COLS, win_buf), win_buf).reshape(
        depth, bd, win_buf, 2, N_KV, HEAD_DIM)
    outs[7] = jnp.concatenate([state_conv[:, :, tq:], outs[7]], axis=2)
    return (y_prompt, y_sample) + tuple(outs)
```

```python
import functools

import jax
import jax.numpy as jnp
from jax import lax
from jax.experimental import pallas as pl
from jax.experimental.pallas import tpu as pltpu

D_MODEL = 1024
D_CONV = D_MODEL // 2
CONV_K = 31
N_HEADS = 8
HEAD_DIM = 64
N_KV = 2
HPG = N_HEADS // N_KV
KV_COLS = 2 * N_KV * HEAD_DIM
CMP_BLOCK = 32
CMP_STRIDE = 16
SEL_BLOCK = 64
N_SEL = 16
WINDOW = 512
D_FF = 4 * D_MODEL
ROPE_THETA = 10000.0
EPS = 1e-6
PAGE_SIZE = 128

LANES = 128
SUBLANES = 8
Q_EXP = N_HEADS * LANES
N_GATES = 3 * N_HEADS
CHUNKS_PER_PAGE = PAGE_SIZE // CMP_STRIDE
NEG = -1e30

O_GLU = 0
O_Q = O_GLU + 2 * D_CONV
O_CKV = O_Q + Q_EXP
O_SKV = O_CKV + KV_COLS
O_WKV = O_SKV + KV_COLS
O_GN = O_WKV + KV_COLS
O_GM = O_GN + LANES
IN_COLS_PACKED = O_GM + 2 * D_MODEL

VMEM_LIMIT = 56 * 1024 * 1024

F32 = jnp.float32
BF16 = jnp.bfloat16


def _params(n_axes, vmem=VMEM_LIMIT):
    return pltpu.CompilerParams(dimension_semantics=("arbitrary",) * n_axes, vmem_limit_bytes=vmem)


def _sigmoid(x):
    return 1.0 / (1.0 + jnp.exp(-x))


def _dot(a, b):
    return jnp.dot(a, b, preferred_element_type=F32)


def _dot_nt(a, b):
    return lax.dot_general(a, b, (((1,), (1,)), ((), ())), preferred_element_type=F32)


def _rope_slab(xs, cos, sin_signed):
    lane = lax.broadcasted_iota(jnp.int32, xs.shape, 1)
    first = (lane % HEAD_DIM) < (HEAD_DIM // 2)
    rot = jnp.where(first, pltpu.roll(xs, LANES - HEAD_DIM // 2, 1), pltpu.roll(xs, HEAD_DIM // 2, 1))
    return xs * cos + rot * sin_signed


def _rope_rows(xt, cos_t, sin_t):
    half = HEAD_DIM // 2
    rot = jnp.concatenate([xt[half:2 * half], xt[0:half], xt[3 * half:4 * half], xt[2 * half:3 * half]], axis=0)
    return xt * cos_t + rot * sin_t


def _pick_tile(n, cap, mult=8):
    t = min(n, cap)
    while n % t or t % mult:
        t -= 1
    return t


def _rms_bf16(x, g):
    ms = jnp.mean(x * x, axis=-1, keepdims=True)
    return (x * lax.rsqrt(ms + EPS) * g).astype(BF16)


def _inproj_common(h, w_ref, cos, sin, u_ref, q_ref, gn_ref, gm_ref):
    def proj(a, n):
        return _dot(h, w_ref[:, a:a + n])

    u_ref[...] = proj(O_GLU, D_CONV) * _sigmoid(proj(O_GLU + D_CONV, D_CONV))
    for s in range(N_HEADS):
        q_ref[:, s * LANES:(s + 1) * LANES] = _rope_slab(proj(O_Q + s * LANES, LANES), cos, sin).astype(BF16)
    gn_ref[...] = _sigmoid(proj(O_GN, LANES))
    gm_ref[...] = _sigmoid(proj(O_GM, 2 * D_MODEL)).astype(BF16)
    return proj


def _inproj_rows_kernel(x_ref, g_ref, w_ref, cos_ref, sin_ref,
                        u_ref, q_ref, ckv_ref, skv_ref, wkv_ref, gn_ref, gm_ref):
    h = _rms_bf16(x_ref[...], g_ref[...])
    cos = cos_ref[...]
    sin = sin_ref[...]
    proj = _inproj_common(h, w_ref, cos, sin, u_ref, q_ref, gn_ref, gm_ref)
    ckv_ref[...] = proj(O_CKV, KV_COLS)
    skv_ref[:, 0:LANES] = _rope_slab(proj(O_SKV, LANES), cos, sin)
    skv_ref[:, LANES:KV_COLS] = proj(O_SKV + LANES, LANES)
    wkv_ref[:, 0:LANES] = _rope_slab(proj(O_WKV, LANES), cos, sin)
    wkv_ref[:, LANES:KV_COLS] = proj(O_WKV + LANES, LANES)


GATE_ROWS = 32
R_Q = 0
R_CKV = R_Q + N_HEADS * HEAD_DIM
R_SKV = R_CKV + KV_COLS
R_WKV = R_SKV + KV_COLS
R_GN = R_WKV + KV_COLS
ROWS_T = R_GN + GATE_ROWS


def _inproj_cols_kernel(x_ref, g_ref, w_ref, wt_ref, cos_ref, sin_ref, cost_ref, sint_ref,
                        u_ref, q_ref, ckv_ref, skv_ref, wkv_ref, skr_ref, wkr_ref, gn_ref, gm_ref):
    h = _rms_bf16(x_ref[...], g_ref[...])
    cos = cos_ref[...]
    sin = sin_ref[...]
    cos_t = cost_ref[...]
    sin_t = sint_ref[...]

    def proj(a, n):
        return _dot(h, w_ref[:, a:a + n])

    t_all = _dot_nt(wt_ref[...], h)

    def proj_t(a, n):
        return t_all[a:a + n]

    u_ref[...] = proj(O_GLU, D_CONV) * _sigmoid(proj(O_GLU + D_CONV, D_CONV))
    gm_ref[...] = _sigmoid(proj(O_GM, 2 * D_MODEL)).astype(BF16)
    skr_ref[...] = _rope_slab(proj(O_SKV, LANES), cos, sin).astype(BF16)
    wkr_ref[...] = _rope_slab(proj(O_WKV, LANES), cos, sin).astype(BF16)
    for s in range(N_HEADS * HEAD_DIM // LANES):
        q_ref[0, s * LANES:(s + 1) * LANES, :] = _rope_rows(proj_t(R_Q + s * LANES, LANES), cos_t, sin_t).astype(BF16)
    ckv_ref[0] = proj_t(R_CKV, KV_COLS)
    skv_ref[0, 0:LANES] = _rope_rows(proj_t(R_SKV, LANES), cos_t, sin_t)
    skv_ref[0, LANES:KV_COLS] = proj_t(R_SKV + LANES, LANES)
    wkv_ref[0, 0:LANES] = _rope_rows(proj_t(R_WKV, LANES), cos_t, sin_t)
    wkv_ref[0, LANES:KV_COLS] = proj_t(R_WKV + LANES, LANES)
    gn_ref[0] = _sigmoid(proj_t(R_GN, GATE_ROWS))


def _inproj_rows(x2, g, w_packed, cos_t, sin_t, tm):
    m = x2.shape[0]
    row = lambda i: (i, 0)
    const = lambda i: (0, 0)
    out_shapes = (
        jax.ShapeDtypeStruct((m, D_CONV), F32),
        jax.ShapeDtypeStruct((m, Q_EXP), BF16),
        jax.ShapeDtypeStruct((m, KV_COLS), F32),
        jax.ShapeDtypeStruct((m, KV_COLS), F32),
        jax.ShapeDtypeStruct((m, KV_COLS), F32),
        jax.ShapeDtypeStruct((m, LANES), F32),
        jax.ShapeDtypeStruct((m, 2 * D_MODEL), BF16),
    )
    return pl.pallas_call(
        _inproj_rows_kernel,
        grid=(m // tm,),
        in_specs=[
            pl.BlockSpec((tm, D_MODEL), row),
            pl.BlockSpec((1, D_MODEL), const),
            pl.BlockSpec(w_packed.shape, const),
            pl.BlockSpec((tm, LANES), const),
            pl.BlockSpec((tm, LANES), const),
        ],
        out_specs=tuple(pl.BlockSpec((tm, s.shape[1]), row) for s in out_shapes),
        out_shape=out_shapes,
        compiler_params=_params(1),
    )(x2, g, w_packed, cos_t, sin_t)


def _inproj_cols(x2, g, w_packed, wkv_t, cos_t, sin_t, cos_tt, sin_tt, b, t_len, tm):
    m = x2.shape[0]
    n_tab = t_len // tm
    row = lambda i: (i, 0)
    const = lambda i: (0, 0)
    kv_map = lambda i: (i // n_tab, 0, i % n_tab)
    kv_shape = jax.ShapeDtypeStruct((b, KV_COLS, t_len), F32)
    out_shapes = (
        jax.ShapeDtypeStruct((m, D_CONV), F32),
        jax.ShapeDtypeStruct((b, N_HEADS * HEAD_DIM, t_len), BF16),
        kv_shape, kv_shape, kv_shape,
        jax.ShapeDtypeStruct((m, LANES), BF16),
        jax.ShapeDtypeStruct((m, LANES), BF16),
        jax.ShapeDtypeStruct((b, GATE_ROWS, t_len), F32),
        jax.ShapeDtypeStruct((m, 2 * D_MODEL), BF16),
    )
    out_specs = (
        pl.BlockSpec((tm, D_CONV), row),
        pl.BlockSpec((1, N_HEADS * HEAD_DIM, tm), kv_map),
        pl.BlockSpec((1, KV_COLS, tm), kv_map),
        pl.BlockSpec((1, KV_COLS, tm), kv_map),
        pl.BlockSpec((1, KV_COLS, tm), kv_map),
        pl.BlockSpec((tm, LANES), row),
        pl.BlockSpec((tm, LANES), row),
        pl.BlockSpec((1, GATE_ROWS, tm), kv_map),
        pl.BlockSpec((tm, 2 * D_MODEL), row),
    )
    return pl.pallas_call(
        _inproj_cols_kernel,
        grid=(m // tm,),
        in_specs=[
            pl.BlockSpec((tm, D_MODEL), row),
            pl.BlockSpec((1, D_MODEL), const),
            pl.BlockSpec(w_packed.shape, const),
            pl.BlockSpec(wkv_t.shape, const),
            pl.BlockSpec((tm, LANES), lambda i: (i % n_tab, 0)),
            pl.BlockSpec((tm, LANES), lambda i: (i % n_tab, 0)),
            pl.BlockSpec((LANES, tm), lambda i: (0, i % n_tab)),
            pl.BlockSpec((LANES, tm), lambda i: (0, i % n_tab)),
        ],
        out_specs=out_specs,
        out_shape=out_shapes,
        compiler_params=_params(1),
    )(x2, g, w_packed, wkv_t, cos_t, sin_t, cos_tt, sin_tt)


def _ln_swish(c, lg, lb):
    mu = jnp.mean(c, axis=-1, keepdims=True)
    d = c - mu
    var = jnp.mean(d * d, axis=-1, keepdims=True)
    y = d * lax.rsqrt(var + EPS) * lg + lb
    return y * _sigmoid(y)


CONV_PAD = 32
CONV_CHUNK = 64


def _conv_prompt_kernel(u_ref, past_ref, w_ref, b_ref, lg_ref, lb_ref, o_ref, uf_ref, *, t_len):
    off = CONV_PAD - (CONV_K - 1)
    uf_ref[0:SUBLANES, :] = jnp.zeros((SUBLANES, D_CONV), F32)
    uf_ref[off:CONV_PAD, :] = past_ref[0]
    uf_ref[CONV_PAD:CONV_PAD + t_len, :] = u_ref[0]
    bias = b_ref[...]
    lg = lg_ref[...]
    lb = lb_ref[...]
    ct = CONV_CHUNK

    def body(i, carry):
        base = pl.multiple_of(i * ct, ct)
        xw = uf_ref[pl.ds(base, ct + CONV_PAD), :]
        acc = jnp.zeros((ct, D_CONV), F32)
        n_win = ct + CONV_PAD
        for r in range(SUBLANES):
            yr = xw if r == 0 else pltpu.roll(xw, n_win - r, 0)
            for a in range((CONV_PAD // SUBLANES) + 1):
                k = SUBLANES * a + r - off
                if 0 <= k < CONV_K:
                    acc = acc + w_ref[k:k + 1, :] * yr[SUBLANES * a:SUBLANES * a + ct, :]
        o_ref[0, pl.ds(base, ct), :] = _ln_swish(acc + bias, lg, lb).astype(BF16)
        return carry

    lax.fori_loop(0, t_len // ct, body, 0)


def _conv_prompt(u3, past3, dw_w, dw_b, ln_g, ln_b):
    b, t_len, _ = u3.shape
    const2 = lambda i: (0, 0)
    return pl.pallas_call(
        functools.partial(_conv_prompt_kernel, t_len=t_len),
        grid=(b,),
        in_specs=[
            pl.BlockSpec((1, t_len, D_CONV), lambda i: (i, 0, 0)),
            pl.BlockSpec((1, CONV_K - 1, D_CONV), lambda i: (i, 0, 0)),
            pl.BlockSpec((CONV_K, D_CONV), const2),
            pl.BlockSpec((1, D_CONV), const2),
            pl.BlockSpec((1, D_CONV), const2),
            pl.BlockSpec((1, D_CONV), const2),
        ],
        out_specs=pl.BlockSpec((1, t_len, D_CONV), lambda i: (i, 0, 0)),
        out_shape=jax.ShapeDtypeStruct((b, t_len, D_CONV), BF16),
        scratch_shapes=[pltpu.VMEM((t_len + CONV_PAD, D_CONV), F32)],
        compiler_params=_params(1),
    )(u3, past3, dw_w, dw_b, ln_g, ln_b)


def _conv_sample_kernel(past_ref, u_ref, w_ref, b_ref, lg_ref, lb_ref, o_ref, *, tq):
    bias = b_ref[...]
    lg = lg_ref[...]
    lb = lb_ref[...]
    n_past = CONV_K - 1
    for t in range(tq):
        acc = jnp.zeros(o_ref.shape[1:], F32)
        for j in range(t, n_past):
            acc = acc + w_ref[j - t:j - t + 1, :] * past_ref[j]
        for i in range(t + 1):
            k = n_past - t + i
            acc = acc + w_ref[k:k + 1, :] * u_ref[i]
        o_ref[t] = _ln_swish(acc + bias, lg, lb).astype(BF16)


def _conv_sample(past_t, u_t, dw_w, dw_b, ln_g, ln_b):
    n_past, bd, _ = past_t.shape
    tq = u_t.shape[0]
    bt = _pick_tile(bd, 32)
    const2 = lambda i: (0, 0)
    return pl.pallas_call(
        functools.partial(_conv_sample_kernel, tq=tq),
        grid=(bd // bt,),
        in_specs=[
            pl.BlockSpec((n_past, bt, D_CONV), lambda i: (0, i, 0)),
            pl.BlockSpec((tq, bt, D_CONV), lambda i: (0, i, 0)),
            pl.BlockSpec((CONV_K, D_CONV), const2),
            pl.BlockSpec((1, D_CONV), const2),
            pl.BlockSpec((1, D_CONV), const2),
            pl.BlockSpec((1, D_CONV), const2),
        ],
        out_specs=pl.BlockSpec((tq, bt, D_CONV), lambda i: (0, i, 0)),
        out_shape=jax.ShapeDtypeStruct((tq, bd, D_CONV), BF16),
        compiler_params=_params(1),
    )(past_t, u_t, dw_w, dw_b, ln_g, ln_b)


def _chunkproj_kernel(x_ref, wk_ref, wv_ref, pk_ref, pv_ref, ab_ref, t_ref, *, n_pages, pages_on_lanes):
    n_rows = n_pages * CHUNKS_PER_PAGE
    for kv, (w_ref, p_ref) in enumerate(((wk_ref, pk_ref), (wv_ref, pv_ref))):
        for p in range(n_pages):
            page = x_ref[0, kv, :, p * PAGE_SIZE:(p + 1) * PAGE_SIZE] if pages_on_lanes else x_ref[p, kv]
            t_ref[p * PAGE_SIZE:(p + 1) * PAGE_SIZE, :] = page.T
        xs = jnp.concatenate([t_ref[pl.ds(l, n_rows, stride=CMP_STRIDE), :] for l in range(CMP_STRIDE)],
                             axis=1).astype(BF16)
        w = w_ref[...]
        part = _dot(xs, w)
        posb = _dot(p_ref[...], w)
        c0 = kv * 2 * LANES
        ab_ref[:, c0:c0 + LANES] = part[:, 0:LANES] + posb[0:1, 0:LANES]
        ab_ref[:, c0 + LANES:c0 + 2 * LANES] = part[:, LANES:2 * LANES] + posb[1:2, LANES:2 * LANES]


def _chunkproj(x4, wk2, wv2, pk2, pv2, page_off, n_total, pages_on_lanes):
    if pages_on_lanes:
        n_pages = x4.shape[3] // PAGE_SIZE
        n_steps = x4.shape[0]
        x_spec = pl.BlockSpec((1, 2, LANES, x4.shape[3]), lambda i: (i, 0, 0, 0))
    else:
        n_pages = _pick_tile(n_total, 64, 1)
        n_steps = n_total // n_pages
        off = page_off // n_pages
        x_spec = pl.BlockSpec((n_pages, 2, LANES, PAGE_SIZE), lambda i: (i + off, 0, 0, 0))
    const2 = lambda i: (0, 0)
    rows = n_pages * CHUNKS_PER_PAGE
    return pl.pallas_call(
        functools.partial(_chunkproj_kernel, n_pages=n_pages, pages_on_lanes=pages_on_lanes),
        grid=(n_steps,),
        in_specs=[
            x_spec,
            pl.BlockSpec(wk2.shape, const2),
            pl.BlockSpec(wv2.shape, const2),
            pl.BlockSpec(pk2.shape, const2),
            pl.BlockSpec(pv2.shape, const2),
        ],
        out_specs=pl.BlockSpec((rows, 4 * LANES), lambda i: (i, 0)),
        out_shape=jax.ShapeDtypeStruct((n_steps * rows, 4 * LANES), F32),
        scratch_shapes=[pltpu.VMEM((n_pages * PAGE_SIZE, LANES), F32)],
        compiler_params=_params(1),
    )(x4, wk2, wv2, pk2, pv2)


def _compressed_kv_f32(ab, cosc, sinc):
    n = ab.shape[0]
    kc = ab[:, 0:LANES] + pltpu.roll(ab[:, LANES:2 * LANES], n - 1, 0)
    vc = ab[:, 2 * LANES:3 * LANES] + pltpu.roll(ab[:, 3 * LANES:4 * LANES], n - 1, 0)
    return _rope_slab(kc, cosc, sinc), vc


def _compressed_kv(ab, cosc, sinc):
    kc, vc = _compressed_kv_f32(ab, cosc, sinc)
    return kc.astype(BF16), vc.astype(BF16)


def _softmax_parts(s, mask):
    sm = jnp.where(mask, s, NEG)
    m = jnp.max(sm, axis=-1, keepdims=True)
    p = jnp.where(mask, jnp.exp(sm - m), 0.0)
    l = jnp.maximum(jnp.sum(p, axis=-1, keepdims=True), 1e-30)
    return p, l


def _select_blocks(imp, qpos, n_sel, axis):
    jidx = lax.broadcasted_iota(jnp.int32, imp.shape, axis)
    cur = lax.shift_right_logical(qpos, 6)
    forced = (jidx == 0) | (jidx == cur) | (jidx == cur - 1)
    valid = (jidx * SEL_BLOCK <= qpos) & (jidx < n_sel)
    v = jnp.where(valid, jnp.where(forced, jnp.inf, imp), -jnp.inf)
    rank = jnp.zeros(imp.shape, jnp.int32)
    for k in range(n_sel):
        vk = v[k:k + 1, :] if axis == 0 else v[:, k:k + 1]
        ahead = (vk > v) | ((vk == v) & (jidx > k))
        rank = rank + ahead.astype(jnp.int32)
    return (rank < min(N_SEL, n_sel)) & valid


def _split_bf16(x):
    hi = x.astype(BF16)
    r1 = x - hi.astype(F32)
    mid = r1.astype(BF16)
    lo = (r1 - mid.astype(F32)).astype(BF16)
    return hi, mid, lo


def _overlap(n_cmp, cmp_axis):
    i = lax.broadcasted_iota(jnp.int32, (LANES, LANES), cmp_axis)
    j = lax.broadcasted_iota(jnp.int32, (LANES, LANES), 1 - cmp_axis)
    hit = (i * CMP_STRIDE < (j + 1) * SEL_BLOCK) & (i * CMP_STRIDE + CMP_BLOCK > j * SEL_BLOCK) & (i < n_cmp)
    return hit.astype(F32)


Q_TILE = 256
K_TILE = 256


def _attn_prompt_kernel(q_ref, gn_ref, skr_ref, svt_ref, wkr_ref, wvt_ref, ab_ref, cosc_ref, sinc_ref, o_ref,
                        kc_ref, vct_ref, bias_ref, m_ref, acc_ref, accb_ref, out_ref, *, t_len):
    qb = pl.program_id(1)
    n_chunk = t_len // CMP_STRIDE
    n_cmp = (t_len - CMP_BLOCK) // CMP_STRIDE + 1
    n_sel = -(-t_len // SEL_BLOCK)
    sel_rows = -(-n_sel // SUBLANES) * SUBLANES
    qt, kt_sz = Q_TILE, K_TILE
    q_tiles = qt // LANES
    g_lanes = HPG * qt
    n_lane_tiles = N_HEADS * q_tiles

    @pl.when(qb == 0)
    def _():
        kc, vc = _compressed_kv_f32(ab_ref[...], cosc_ref[...], sinc_ref[...])
        kc_ref[...] = kc.astype(BF16)
        vct_ref[...] = vc.T.astype(BF16)

    q0 = qb * qt
    qpos = q0 + lax.broadcasted_iota(jnp.int32, (1, qt), 1)

    def lane_tile(c):
        return slice(c * LANES, (c + 1) * LANES)

    def q_part(c):
        return slice((c % q_tiles) * LANES, (c % q_tiles + 1) * LANES)

    def group_of(c):
        return c // (HPG * q_tiles)

    def group_lanes(g):
        return slice(g * g_lanes, (g + 1) * g_lanes)

    q_zero = jnp.zeros((HEAD_DIM, qt), BF16)

    def q_slab(hh):
        q_h = q_ref[0, hh * HEAD_DIM:(hh + 1) * HEAD_DIM, :]
        return jnp.concatenate([q_h, q_zero] if hh < HPG else [q_zero, q_h], axis=0)

    q_all = jnp.concatenate([q_slab(hh) for hh in range(N_HEADS)], axis=1)

    def gates(branch):
        r0 = branch * N_HEADS
        return jnp.concatenate([gn_ref[0, r0 + hh:r0 + hh + 1, :] for hh in range(N_HEADS)], axis=1)

    nrow = lax.broadcasted_iota(jnp.int32, (n_chunk, qt), 0)
    mask_c = (nrow * CMP_STRIDE + CMP_BLOCK - 1 <= qpos) & (nrow < n_cmp)
    acc_ref[...] = _dot(kc_ref[...], q_all)
    hs = [[jnp.zeros((n_chunk, LANES), F32) for _ in range(q_tiles)] for _ in range(N_KV)]
    p_parts = []
    for c in range(n_lane_tiles):
        mask = mask_c[:, q_part(c)]
        sm = jnp.where(mask, acc_ref[:, lane_tile(c)], NEG)
        p_c = jnp.where(mask, jnp.exp(sm - jnp.max(sm, axis=0, keepdims=True)), 0.0)
        p_c = p_c / jnp.maximum(jnp.sum(p_c, axis=0, keepdims=True), 1e-30)
        hs[group_of(c)][c % q_tiles] = hs[group_of(c)][c % q_tiles] + p_c
        p_parts.append(p_c.astype(BF16))
    out_ref[...] = gates(0) * _dot(vct_ref[...], jnp.concatenate(p_parts, axis=1))

    sel_t = []
    for g in range(N_KV):
        ov_t = _overlap(n_cmp, 1).astype(BF16)
        imp_t = sum(_dot(ov_t, part) for part in _split_bf16(jnp.concatenate(hs[g], axis=1)))
        s_g = _select_blocks(imp_t[0:sel_rows], qpos, n_sel, 0).astype(F32)
        sel_t.append(jnp.concatenate([s_g, jnp.zeros((LANES - sel_rows, qt), F32)], axis=0).astype(BF16))

    m_ref[...] = jnp.full(m_ref.shape, NEG, F32)
    accb_ref[...] = jnp.zeros(accb_ref.shape, F32)

    def tile_step(br, kr_ref, vt_ref, k_idx, bias_fn):
        k0 = pl.multiple_of(k_idx * kt_sz, kt_sz)
        kpos = k0 + lax.broadcasted_iota(jnp.int32, (kt_sz, qt), 0)
        bias_fn(k_idx, kpos)
        s = _dot(kr_ref[pl.ds(k0, kt_sz), :], q_all)
        vt = vt_ref[0, :, pl.ds(k0, kt_sz)]
        vrow = lax.broadcasted_iota(jnp.int32, (LANES, kt_sz), 0)
        p_parts, a_parts = [], []
        for c in range(n_lane_tiles):
            sb = s[:, lane_tile(c)] + bias_ref[br, group_of(c), :, q_part(c)]
            m_old = m_ref[br, 0:1, lane_tile(c)]
            m_new = jnp.maximum(m_old, jnp.max(sb, axis=0, keepdims=True))
            p_parts.append(jnp.exp(sb - m_new).astype(BF16))
            a_parts.append(jnp.exp(m_old - m_new))
            m_ref[br, :, lane_tile(c)] = jnp.broadcast_to(m_new, (SUBLANES, LANES))
        for g in range(N_KV):
            v_own = (vrow >= g * HEAD_DIM) & (vrow < (g + 1) * HEAD_DIM)
            vaug = jnp.where(v_own, vt, 1.0).astype(BF16)
            tiles = range(g * HPG * q_tiles, (g + 1) * HPG * q_tiles)
            alpha = jnp.concatenate([a_parts[c] for c in tiles], axis=1)
            p = jnp.concatenate([p_parts[c] for c in tiles], axis=1)
            accb_ref[br, :, group_lanes(g)] = alpha * accb_ref[br, :, group_lanes(g)] + _dot(vaug, p)

    def sel_bias(k_idx, kpos):
        krow = lax.broadcasted_iota(jnp.int32, (kt_sz, LANES), 0)
        jcol = lax.broadcasted_iota(jnp.int32, (kt_sz, LANES), 1)
        expand_t = (jcol == k_idx * (kt_sz // SEL_BLOCK) + lax.shift_right_logical(krow, 6)).astype(BF16)
        for g in range(N_KV):
            bias_ref[0, g] = jnp.where((_dot(expand_t, sel_t[g]) > 0.5) & (kpos <= qpos), 0.0, NEG)

    def win_bias(k_idx, kpos):
        b = jnp.where((kpos <= qpos) & (kpos > qpos - WINDOW), 0.0, NEG)
        for g in range(N_KV):
            bias_ref[1, g] = b

    n_sel_tiles = qb + 1
    n_win_tiles = jnp.minimum(qb, WINDOW // kt_sz) + 1

    def both(i, carry):
        tile_step(0, skr_ref, svt_ref, i, sel_bias)
        tile_step(1, wkr_ref, wvt_ref, qb - i, win_bias)
        return carry

    def sel_only(i, carry):
        tile_step(0, skr_ref, svt_ref, i, sel_bias)
        return carry

    lax.fori_loop(0, n_win_tiles, both, 0)
    lax.fori_loop(n_win_tiles, n_sel_tiles, sel_only, 0)

    for br in range(2):
        gt = gates(br + 1)
        for g in range(N_KV):
            sum_row = (1 - g) * HEAD_DIM
            acc = accb_ref[br, :, group_lanes(g)]
            out_ref[:, group_lanes(g)] = (out_ref[:, group_lanes(g)]
                                          + gt[:, group_lanes(g)] * (acc / acc[sum_row:sum_row + 1, :]))

    lane = lax.broadcasted_iota(jnp.int32, (qt, LANES), 1)
    for pair in range(N_HEADS // 2):
        a = out_ref[:, (2 * pair) * qt:(2 * pair + 1) * qt].T
        b = out_ref[:, (2 * pair + 1) * qt:(2 * pair + 2) * qt].T
        if 2 * pair < HPG:
            slab = jnp.where(lane < HEAD_DIM, a, pltpu.roll(b, HEAD_DIM, 1))
        else:
            slab = jnp.where(lane < HEAD_DIM, pltpu.roll(a, HEAD_DIM, 1), b)
        o_ref[:, pair * LANES:(pair + 1) * LANES] = slab.astype(BF16)


def _attn_prompt(q_t, gn_t, skr, skv_t, wkr, wkv_t, ab2, cosc, sinc, b, t_len):
    n_chunk = t_len // CMP_STRIDE
    nqb = t_len // Q_TILE
    const2 = lambda i, j: (0, 0)
    per_b2 = lambda i, j: (i, 0)
    v_rows = lambda i, j: (i, 1, 0)
    return pl.pallas_call(
        functools.partial(_attn_prompt_kernel, t_len=t_len),
        grid=(b, nqb),
        in_specs=[
            pl.BlockSpec((1, N_HEADS * HEAD_DIM, Q_TILE), lambda i, j: (i, 0, j)),
            pl.BlockSpec((1, GATE_ROWS, Q_TILE), lambda i, j: (i, 0, j)),
            pl.BlockSpec((t_len, LANES), per_b2),
            pl.BlockSpec((1, LANES, t_len), v_rows),
            pl.BlockSpec((t_len, LANES), per_b2),
            pl.BlockSpec((1, LANES, t_len), v_rows),
            pl.BlockSpec((n_chunk, 4 * LANES), per_b2),
            pl.BlockSpec((n_chunk, LANES), const2),
            pl.BlockSpec((n_chunk, LANES), const2),
        ],
        out_specs=pl.BlockSpec((Q_TILE, N_HEADS * HEAD_DIM), lambda i, j: (i * nqb + j, 0)),
        out_shape=jax.ShapeDtypeStruct((b * t_len, N_HEADS * HEAD_DIM), BF16),
        scratch_shapes=[
            pltpu.VMEM((n_chunk, LANES), BF16),
            pltpu.VMEM((LANES, n_chunk), BF16),
            pltpu.VMEM((2, N_KV, K_TILE, Q_TILE), F32),
            pltpu.VMEM((2, SUBLANES, N_HEADS * Q_TILE), F32),
            pltpu.VMEM((LANES, N_HEADS * Q_TILE), F32),
            pltpu.VMEM((2, LANES, N_HEADS * Q_TILE), F32),
            pltpu.VMEM((LANES, N_HEADS * Q_TILE), F32),
        ],
        compiler_params=_params(2),
    )(q_t, gn_t, skr, skv_t, wkr, wkv_t, ab2, cosc, sinc)


Q_ROWS = 8
SEQ_PER_STEP = 4


def _attn_sample_kernel(pt_ref, q_ref, gn_ref, sknew_ref, wknew_ref, win_ref, cosc_ref, sinc_ref, *rest,
                        past_len, tq, n_pages, n_seq):
    del pt_ref
    sel_pages = [rest[e * n_pages:(e + 1) * n_pages] for e in range(n_seq)]
    ab_pages = [rest[(n_seq + e) * n_pages:(n_seq + e + 1) * n_pages] for e in range(n_seq)]
    o_ref = rest[2 * n_seq * n_pages]
    seqs = range(n_seq)
    t_all = past_len + tq
    n_chunk = past_len // CMP_STRIDE
    n_cmp = (t_all - CMP_BLOCK) // CMP_STRIDE + 1
    n_sel = -(-t_all // SEL_BLOCK)
    rows = N_HEADS * Q_ROWS
    all_rows = n_seq * rows
    win_buf = win_ref.shape[3]

    def stack(parts):
        return jnp.concatenate(parts, axis=0)

    def per_seq(a, e):
        return a[e * rows:(e + 1) * rows]

    qbd = [stack([q_ref[e, :, hh * LANES:(hh + 1) * LANES] for hh in range(N_HEADS)]) for e in seqs]
    qpos = past_len + lax.broadcasted_iota(jnp.int32, (all_rows, 1), 0) % Q_ROWS

    cosc = cosc_ref[...]
    sinc = sinc_ref[...]
    kvc = [_compressed_kv(stack([r[0] for r in ab_pages[e]]), cosc, sinc) for e in seqs]
    ncol = lax.broadcasted_iota(jnp.int32, (all_rows, n_chunk), 1)
    mask_c = (ncol * CMP_STRIDE + CMP_BLOCK - 1 <= qpos) & (ncol < n_cmp)
    p_c, l_c = _softmax_parts(stack([_dot_nt(qbd[e], kvc[e][0]) for e in seqs]), mask_c)
    p_c = p_c / l_c
    p_c16 = p_c.astype(BF16)
    o_c = stack([_dot(per_seq(p_c16, e), kvc[e][1]) for e in seqs])

    hs = []
    for e in seqs:
        for g in range(N_KV):
            r0 = e * rows + g * HPG * Q_ROWS
            acc = p_c[r0:r0 + Q_ROWS]
            for h in range(1, HPG):
                acc = acc + p_c[r0 + h * Q_ROWS:r0 + (h + 1) * Q_ROWS]
            hs.append(acc)
    ov = _overlap(n_cmp, 0).astype(BF16)
    imp = sum(_dot(part, ov) for part in _split_bf16(stack(hs)))
    sel_g = _select_blocks(imp, qpos[0:n_seq * N_KV * Q_ROWS], n_sel, 1).astype(BF16)
    sel_rows = stack([sel_g[(e * N_KV + g) * Q_ROWS:(e * N_KV + g + 1) * Q_ROWS]
                      for e in seqs for g in range(N_KV) for _ in range(HPG)])

    def new_rows_tile(ref, e, c0):
        return stack([ref[e, :, c0:c0 + LANES], jnp.zeros((LANES - Q_ROWS, LANES), F32)]).astype(BF16)

    n_keys = (n_pages + 1) * PAGE_SIZE
    s_s = stack([jnp.concatenate([_dot(qbd[e], r[0, 0].astype(BF16)) for r in sel_pages[e]]
                                 + [_dot_nt(qbd[e], new_rows_tile(sknew_ref, e, 0))], axis=1) for e in seqs])
    jrow = lax.broadcasted_iota(jnp.int32, (LANES, n_keys), 0)
    kcol = lax.broadcasted_iota(jnp.int32, (LANES, n_keys), 1)
    expand = (jrow == lax.shift_right_logical(kcol, 6)).astype(BF16)
    kpos = lax.broadcasted_iota(jnp.int32, (all_rows, n_keys), 1)
    mask_s = (_dot(sel_rows, expand) > 0.5) & (kpos <= qpos)
    p_s, l_s = _softmax_parts(s_s, mask_s)
    p_s = p_s.astype(BF16)
    o_s = []
    for e in seqs:
        p_e = per_seq(p_s, e)
        acc = _dot(p_e[:, n_pages * PAGE_SIZE:], new_rows_tile(sknew_ref, e, LANES))
        for i in range(n_pages):
            acc = acc + _dot_nt(p_e[:, i * PAGE_SIZE:(i + 1) * PAGE_SIZE], sel_pages[e][i][0, 1].astype(BF16))
        o_s.append(acc)
    o_s = stack(o_s) / l_s

    s_w = stack([jnp.concatenate([_dot(qbd[e], win_ref[e, 0].astype(BF16)),
                                  _dot_nt(qbd[e], new_rows_tile(wknew_ref, e, 0))], axis=1) for e in seqs])
    wcol = lax.broadcasted_iota(jnp.int32, (all_rows, win_buf + LANES), 1)
    kpos_w = past_len - win_buf + wcol
    mask_w = (kpos_w <= qpos) & (kpos_w > qpos - WINDOW)
    p_w, l_w = _softmax_parts(s_w, mask_w)
    p_w = p_w.astype(BF16)
    o_w = stack([_dot_nt(per_seq(p_w, e)[:, 0:win_buf], win_ref[e, 1].astype(BF16))
                 + _dot(per_seq(p_w, e)[:, win_buf:], new_rows_tile(wknew_ref, e, LANES)) for e in seqs]) / l_w

    for e in seqs:
        gn = gn_ref[e]
        for hh in range(N_HEADS):
            r = slice(e * rows + hh * Q_ROWS, e * rows + (hh + 1) * Q_ROWS)
            out = (gn[:, hh:hh + 1] * o_c[r]
                   + gn[:, N_HEADS + hh:N_HEADS + hh + 1] * o_s[r]
                   + gn[:, 2 * N_HEADS + hh:2 * N_HEADS + hh + 1] * o_w[r])
            o_ref[e, :, hh * LANES:(hh + 1) * LANES] = out.astype(BF16)


def _attn_sample(pt_flat, q3, gn3, sknew3, wknew3, win4, cosc, sinc, sel_pool4, ab_pool3,
                 layer, n_pool, bd, past_len, tq):
    n_pages = past_len // PAGE_SIZE
    win_buf = win4.shape[3]
    n_chunk = past_len // CMP_STRIDE
    n_seq = _pick_tile(bd, SEQ_PER_STEP, 1)
    per_b = lambda i, pt: (i, 0, 0)
    const2 = lambda i, pt: (0, 0)
    sel_specs = [pl.BlockSpec((1, 2, LANES, PAGE_SIZE),
                              lambda i, pt, e=e, p=p: (layer * n_pool + pt[(i * n_seq + e) * n_pages + p], 0, 0, 0))
                 for e in range(n_seq) for p in range(n_pages)]
    ab_specs = [pl.BlockSpec((1, CHUNKS_PER_PAGE, 4 * LANES),
                             lambda i, pt, e=e, p=p: (pt[(i * n_seq + e) * n_pages + p], 0, 0))
                for e in range(n_seq) for p in range(n_pages)]
    grid_spec = pltpu.PrefetchScalarGridSpec(
        num_scalar_prefetch=1,
        grid=(bd // n_seq,),
        in_specs=[
            pl.BlockSpec((n_seq, Q_ROWS, Q_EXP), per_b),
            pl.BlockSpec((n_seq, Q_ROWS, LANES), per_b),
            pl.BlockSpec((n_seq, Q_ROWS, KV_COLS), per_b),
            pl.BlockSpec((n_seq, Q_ROWS, KV_COLS), per_b),
            pl.BlockSpec((n_seq, 2, LANES, win_buf), lambda i, pt: (layer * (bd // n_seq) + i, 0, 0, 0)),
            pl.BlockSpec((n_chunk, LANES), const2),
            pl.BlockSpec((n_chunk, LANES), const2),
        ] + sel_specs + ab_specs,
        out_specs=pl.BlockSpec((n_seq, Q_ROWS, Q_EXP), per_b),
    )
    n_refs = n_seq * n_pages
    return pl.pallas_call(
        functools.partial(_attn_sample_kernel, past_len=past_len, tq=tq, n_pages=n_pages, n_seq=n_seq),
        grid_spec=grid_spec,
        out_shape=jax.ShapeDtypeStruct((bd, Q_ROWS, Q_EXP), BF16),
        compiler_params=_params(1),
    )(pt_flat, q3, gn3, sknew3, wknew3, win4, cosc, sinc, *([sel_pool4] * n_refs), *([ab_pool3] * n_refs))


def _post_kernel(c_ref, o_ref, gm_ref, x_ref, wc_ref, wn_ref, wo_ref, g_ref, wu_ref, wd_ref, gf_ref, y_ref, *, final):
    conv_out = _dot(c_ref[...], wc_ref[...])
    nsa_out = _dot(o_ref[...], wn_ref[...])
    gm = gm_ref[...].astype(F32)
    merged = gm[:, 0:D_MODEL] * conv_out + gm[:, D_MODEL:2 * D_MODEL] * nsa_out
    x = x_ref[...] + _dot(merged.astype(BF16), wo_ref[...])
    h = _rms_bf16(x, g_ref[...])
    acc = x
    for f in range(D_FF // FF_TILE):
        up = jnp.maximum(_dot(h, wu_ref[:, f * FF_TILE:(f + 1) * FF_TILE]), 0.0)
        acc = acc + _dot((up * up).astype(BF16), wd_ref[f * FF_TILE:(f + 1) * FF_TILE, :])
    if final:
        ms2 = jnp.mean(acc * acc, axis=-1, keepdims=True)
        acc = acc * lax.rsqrt(ms2 + EPS) * gf_ref[...]
    y_ref[...] = acc


FF_TILE = 1024


def _post(c2, o2, gm2, x2, wc, wn, wo, g, wu, wd, gf, tm, final):
    m = x2.shape[0]
    row = lambda i: (i, 0)
    const = lambda i: (0, 0)
    resident = lambda a: pl.BlockSpec(a.shape, const, pipeline_mode=pl.Buffered(1))
    return pl.pallas_call(
        functools.partial(_post_kernel, final=final),
        grid=(m // tm,),
        in_specs=[
            pl.BlockSpec((tm, c2.shape[1]), row),
            pl.BlockSpec((tm, o2.shape[1]), row),
            pl.BlockSpec((tm, 2 * D_MODEL), row),
            pl.BlockSpec((tm, D_MODEL), row),
            resident(wc), resident(wn), resident(wo),
            pl.BlockSpec((1, D_MODEL), const),
            resident(wu), resident(wd),
            pl.BlockSpec((1, D_MODEL), const),
        ],
        out_specs=pl.BlockSpec((tm, D_MODEL), row),
        out_shape=jax.ShapeDtypeStruct((m, D_MODEL), F32),
        compiler_params=_params(1),
    )(c2, o2, gm2, x2, wc, wn, wo, g, wu, wd, gf)


def _shift_append_kernel(old_ref, new_ref, o_ref, *, n_new):
    o_ref[...] = jnp.concatenate([old_ref[..., n_new:], new_ref[...]], axis=-1)


def _shift_append(old, new, rows_per_step):
    n, a, b, keep = old.shape
    n_new = new.shape[-1]
    r = _pick_tile(n, rows_per_step, 1)
    return pl.pallas_call(
        functools.partial(_shift_append_kernel, n_new=n_new),
        grid=(n // r,),
        in_specs=[pl.BlockSpec((r, a, b, keep), lambda i: (i, 0, 0, 0)),
                  pl.BlockSpec((r, a, b, n_new), lambda i: (i, 0, 0, 0))],
        out_specs=pl.BlockSpec((r, a, b, keep), lambda i: (i, 0, 0, 0)),
        out_shape=jax.ShapeDtypeStruct(old.shape, old.dtype),
        compiler_params=_params(1),
    )(old, new)


def _rope_tables(pos):
    half = HEAD_DIM // 2
    inv = jnp.power(ROPE_THETA, -jnp.arange(half, dtype=F32) / half)
    ang = pos.astype(F32)[:, None] * inv[None, :]
    cos = jnp.cos(ang)
    sin = jnp.sin(ang)
    return jnp.concatenate([cos, cos, cos, cos], axis=1), jnp.concatenate([-sin, sin, -sin, sin], axis=1)


def _pack_w_in(w):
    o0 = 2 * D_CONV
    o1 = o0 + N_HEADS * HEAD_DIM
    o4 = o1 + 3 * KV_COLS
    o5 = o4 + N_GATES
    wq = w[:, o0:o1].reshape(D_MODEL, N_HEADS, HEAD_DIM) * (HEAD_DIM ** -0.5)
    zero = jnp.zeros_like(wq)
    in_g0 = (jnp.arange(N_HEADS) < HPG)[None, :, None]
    wq_exp = jnp.concatenate([jnp.where(in_g0, wq, zero), jnp.where(in_g0, zero, wq)], axis=2)
    wgn = jnp.pad(w[:, o4:o5], ((0, 0), (0, LANES - N_GATES)))
    wq_exp = wq_exp.reshape(D_MODEL, Q_EXP)
    packed = jnp.concatenate([w[:, :o0], wq_exp, w[:, o1:o4], wgn, w[:, o5:]], axis=1)
    w_t = jnp.concatenate([wq.reshape(D_MODEL, N_HEADS * HEAD_DIM), w[:, o1:o4], wgn[:, :GATE_ROWS]], axis=1).T
    return packed.astype(BF16), w_t.astype(BF16)


def _pack_w_nsa_out(w):
    wh = w.reshape(N_HEADS, HEAD_DIM, D_MODEL)
    zero = jnp.zeros_like(wh)
    in_g0 = (jnp.arange(N_HEADS) < HPG)[:, None, None]
    return jnp.concatenate([jnp.where(in_g0, wh, zero), jnp.where(in_g0, zero, wh)], axis=1).reshape(Q_EXP, D_MODEL).astype(BF16)


def _pack_w_cmp(w_kv, pos_kv):
    lo, hi = w_kv[:CMP_STRIDE], w_kv[CMP_STRIDE:]
    eye = jnp.eye(N_KV, dtype=w_kv.dtype)
    blk = lambda part: jnp.einsum('lde,gh->lgdhe', part, eye).reshape(CMP_STRIDE * N_KV * HEAD_DIM, N_KV * HEAD_DIM)
    w2 = jnp.concatenate([blk(lo), blk(hi)], axis=1).astype(BF16)
    tile = lambda p: jnp.broadcast_to(p[:, None, :], (CMP_STRIDE, N_KV, HEAD_DIM)).reshape(1, -1)
    p2 = jnp.concatenate([tile(pos_kv[:CMP_STRIDE]), tile(pos_kv[CMP_STRIDE:]),
                          jnp.zeros((SUBLANES - 2, CMP_STRIDE * N_KV * HEAD_DIM), pos_kv.dtype)], axis=0).astype(BF16)
    return w2, p2


def _feature_major(a):
    lead = a.shape[:-4]
    n = len(lead)
    a = jnp.transpose(a, tuple(range(n)) + (n + 1, n + 2, n + 3, n))
    return a.reshape(lead + (2, N_KV * HEAD_DIM, a.shape[-1]))


def _position_major(a_t, rows):
    b = a_t.shape[0]
    return jnp.transpose(a_t.reshape(b, 2, N_KV, HEAD_DIM, rows), (0, 4, 1, 2, 3))


def kernel(x_prompt, x_sample, cache_cmp_kv, cache_sel_kv, state_win_kv, state_conv, page_table, norm_mix_g, w_in, conv_dw_w, conv_dw_b, conv_ln_g, conv_ln_b, w_conv_out, cmp_pos, w_cmp, w_nsa_out, w_out, norm_mlp_g, w_up, w_down, norm_final_g):
    depth = w_in.shape[0]
    bp, tp, _ = x_prompt.shape
    bd, tq, _ = x_sample.shape
    n_pool = cache_cmp_kv.shape[1]
    n_pages = page_table.shape[1]
    past_len = n_pages * PAGE_SIZE
    win_buf = state_win_kv.shape[2]
    assert tp % Q_TILE == 0 and tp // CMP_STRIDE == LANES and tp >= WINDOW
    assert past_len // CMP_STRIDE == LANES and win_buf == WINDOW and tq <= Q_ROWS

    mp, ms = bp * tp, bd * tq
    tm_p = _pick_tile(tp, 512)
    tm_s = _pick_tile(ms, 256)
    assert tp % tm_p == 0 and tm_s % tq == 0

    cos_p, sin_p = _rope_tables(jnp.arange(tp))
    cos_pt, sin_pt = cos_p.T, sin_p.T
    cos_s, sin_s = _rope_tables(past_len + jnp.arange(tm_s) % tq)
    n_chunk = tp // CMP_STRIDE
    cos_c, sin_c = _rope_tables(jnp.arange(n_chunk) * CMP_STRIDE + CMP_BLOCK - 1)

    pt_flat = page_table.reshape(-1).astype(jnp.int32)
    cmp_pool4 = _feature_major(cache_cmp_kv).reshape(depth * n_pool, 2, LANES, PAGE_SIZE)
    sel_pool4 = _feature_major(cache_sel_kv).reshape(depth * n_pool, 2, LANES, PAGE_SIZE)
    win4 = _feature_major(state_win_kv).reshape(depth * bd, 2, LANES, win_buf)
    zeros_conv = jnp.zeros((bp, CONV_K - 1, D_CONV), F32)

    xp = x_prompt.reshape(mp, D_MODEL)
    xs = x_sample.reshape(ms, D_MODEL)
    outs = [[] for _ in range(8)]
    for l in range(depth):
        w_packed, w_t = _pack_w_in(w_in[l])
        wn_exp = _pack_w_nsa_out(w_nsa_out[l])
        wn = w_nsa_out[l].astype(BF16)
        wc, wo = w_conv_out[l].astype(BF16), w_out[l].astype(BF16)
        wu, wd = w_up[l].astype(BF16), w_down[l].astype(BF16)
        wk2, pk2 = _pack_w_cmp(w_cmp[l, 0], cmp_pos[l, 0])
        wv2, pv2 = _pack_w_cmp(w_cmp[l, 1], cmp_pos[l, 1])
        g_mix, g_mlp = norm_mix_g[l][None], norm_mlp_g[l][None]
        dw_b, ln_g, ln_b = conv_dw_b[l][None], conv_ln_g[l][None], conv_ln_b[l][None]
        gf = norm_final_g[None]
        final = l == depth - 1

        u, q_t, ckv_t, skv_t, wkv_tm, skr, wkr, gn_t, gm = _inproj_cols(xp, g_mix, w_packed, w_t, cos_p, sin_p,
                                                                        cos_pt, sin_pt, bp, tp, tm_p)
        c_act = _conv_prompt(u.reshape(bp, tp, D_CONV), zeros_conv, conv_dw_w[l], dw_b, ln_g, ln_b)
        ab = _chunkproj(ckv_t.reshape(bp, 2, LANES, tp), wk2, wv2, pk2, pv2, 0, 0, True)
        o = _attn_prompt(q_t, gn_t, skr, skv_t, wkr, wkv_tm, ab, cos_c, sin_c, bp, tp)
        xp = _post(c_act.reshape(mp, D_CONV), o, gm, xp, wc, wn, wo, g_mlp, wu, wd, gf, tm_p, final)
        keep = min(WINDOW, tp)
        outs[0].append(_position_major(ckv_t, tp))
        outs[2].append(_position_major(skv_t, tp))
        outs[4].append(_position_major(wkv_tm[:, :, tp - keep:], keep))
        outs[6].append(u.reshape(bp, tp, D_CONV)[:, tp - (CONV_K - 1):])

        u, q, ckv, skv, wkv, gn, gm = _inproj_rows(xs, g_mix, w_packed, cos_s, sin_s, tm_s)
        u3 = u.reshape(bd, tq, D_CONV)
        c_t = _conv_sample(jnp.swapaxes(state_conv[l], 0, 1), jnp.swapaxes(u3, 0, 1), conv_dw_w[l], dw_b, ln_g, ln_b)
        c_act = jnp.swapaxes(c_t, 0, 1).reshape(ms, D_CONV)
        ab_pool = _chunkproj(cmp_pool4, wk2, wv2, pk2, pv2, l * n_pool, n_pool, False)
        pad_q = lambda a: jnp.pad(a.reshape(bd, tq, -1), ((0, 0), (0, Q_ROWS - tq), (0, 0)))
        o = _attn_sample(pt_flat, pad_q(q), pad_q(gn), pad_q(skv), pad_q(wkv), win4, cos_c, sin_c, sel_pool4,
                         ab_pool.reshape(n_pool, CHUNKS_PER_PAGE, 4 * LANES), l, n_pool, bd, past_len, tq)
        xs = _post(c_act, o[:, :tq].reshape(ms, Q_EXP), gm, xs, wc, wn_exp, wo, g_mlp, wu, wd, gf, tm_s, final)
        wkv5 = wkv.reshape(bd, tq, 2, N_KV, HEAD_DIM)
        outs[1].append(ckv.reshape(bd, tq, 2, N_KV, HEAD_DIM))
        outs[3].append(skv.reshape(bd, tq, 2, N_KV, HEAD_DIM))
        outs[5].append(wkv5)
        outs[7].append(u3)

    y_prompt = xp.reshape(bp, tp, D_MODEL)
    y_sample = xs.reshape(bd, tq, D_MODEL)
    outs = [jnp.stack(o) for o in outs]
    win_new = _shift_append(win4, _feature_major(outs[5]).reshape(depth * bd, 2, LANES, tq), 8)
    outs[5] = _position_major(win_new.reshape(depth * bd, KV_COLS, win_buf), win_buf).reshape(
        depth, bd, win_buf, 2, N_KV, HEAD_DIM)
    outs[7] = jnp.concatenate([state_conv[:, :, tq:], outs[7]], axis=2)
    return (y_prompt, y_sample) + tuple(outs)
```

```python
import functools

import jax
import jax.numpy as jnp
from jax import lax
from jax.experimental import pallas as pl
from jax.experimental.pallas import tpu as pltpu

D_MODEL = 1024
D_CONV = D_MODEL // 2
CONV_K = 31
N_HEADS = 8
HEAD_DIM = 64
N_KV = 2
HPG = N_HEADS // N_KV
KV_COLS = 2 * N_KV * HEAD_DIM
CMP_BLOCK = 32
CMP_STRIDE = 16
SEL_BLOCK = 64
N_SEL = 16
WINDOW = 512
D_FF = 4 * D_MODEL
ROPE_THETA = 10000.0
EPS = 1e-6
PAGE_SIZE = 128

LANES = 128
SUBLANES = 8
Q_EXP = N_HEADS * LANES
N_GATES = 3 * N_HEADS
CHUNKS_PER_PAGE = PAGE_SIZE // CMP_STRIDE
NEG = -1e30

O_GLU = 0
O_Q = O_GLU + 2 * D_CONV
O_CKV = O_Q + Q_EXP
O_SKV = O_CKV + KV_COLS
O_WKV = O_SKV + KV_COLS
O_GN = O_WKV + KV_COLS
O_GM = O_GN + LANES
IN_COLS_PACKED = O_GM + 2 * D_MODEL

VMEM_LIMIT = 56 * 1024 * 1024

F32 = jnp.float32
BF16 = jnp.bfloat16


def _params(n_axes, vmem=VMEM_LIMIT):
    return pltpu.CompilerParams(dimension_semantics=("arbitrary",) * n_axes, vmem_limit_bytes=vmem)


def _sigmoid(x):
    return 1.0 / (1.0 + jnp.exp(-x))


def _dot(a, b):
    return jnp.dot(a, b, preferred_element_type=F32)


def _dot_nt(a, b):
    return lax.dot_general(a, b, (((1,), (1,)), ((), ())), preferred_element_type=F32)


def _rope_slab(xs, cos, sin_signed):
    lane = lax.broadcasted_iota(jnp.int32, xs.shape, 1)
    first = (lane % HEAD_DIM) < (HEAD_DIM // 2)
    rot = jnp.where(first, pltpu.roll(xs, LANES - HEAD_DIM // 2, 1), pltpu.roll(xs, HEAD_DIM // 2, 1))
    return xs * cos + rot * sin_signed


def _rope_rows(xt, cos_t, sin_t):
    half = HEAD_DIM // 2
    rot = jnp.concatenate([xt[half:2 * half], xt[0:half], xt[3 * half:4 * half], xt[2 * half:3 * half]], axis=0)
    return xt * cos_t + rot * sin_t


def _pick_tile(n, cap, mult=8):
    t = min(n, cap)
    while n % t or t % mult:
        t -= 1
    return t


def _rms_bf16(x, g):
    ms = jnp.mean(x * x, axis=-1, keepdims=True)
    return (x * lax.rsqrt(ms + EPS) * g).astype(BF16)


def _inproj_common(h, w_ref, cos, sin, u_ref, q_ref, gn_ref, gm_ref):
    def proj(a, n):
        return _dot(h, w_ref[:, a:a + n])

    u_ref[...] = proj(O_GLU, D_CONV) * _sigmoid(proj(O_GLU + D_CONV, D_CONV))
    for s in range(N_HEADS):
        q_ref[:, s * LANES:(s + 1) * LANES] = _rope_slab(proj(O_Q + s * LANES, LANES), cos, sin).astype(BF16)
    gn_ref[...] = _sigmoid(proj(O_GN, LANES))
    gm_ref[...] = _sigmoid(proj(O_GM, 2 * D_MODEL)).astype(BF16)
    return proj


def _inproj_rows_kernel(x_ref, g_ref, w_ref, cos_ref, sin_ref,
                        u_ref, q_ref, ckv_ref, skv_ref, wkv_ref, gn_ref, gm_ref):
    h = _rms_bf16(x_ref[...], g_ref[...])
    cos = cos_ref[...]
    sin = sin_ref[...]
    proj = _inproj_common(h, w_ref, cos, sin, u_ref, q_ref, gn_ref, gm_ref)
    ckv_ref[...] = proj(O_CKV, KV_COLS)
    skv_ref[:, 0:LANES] = _rope_slab(proj(O_SKV, LANES), cos, sin)
    skv_ref[:, LANES:KV_COLS] = proj(O_SKV + LANES, LANES)
    wkv_ref[:, 0:LANES] = _rope_slab(proj(O_WKV, LANES), cos, sin)
    wkv_ref[:, LANES:KV_COLS] = proj(O_WKV + LANES, LANES)


GATE_ROWS = 32
R_Q = 0
R_CKV = R_Q + N_HEADS * HEAD_DIM
R_SKV = R_CKV + KV_COLS
R_WKV = R_SKV + KV_COLS
R_GN = R_WKV + KV_COLS
ROWS_T = R_GN + GATE_ROWS


def _inproj_cols_kernel(x_ref, g_ref, w_ref, wt_ref, cos_ref, sin_ref, cost_ref, sint_ref,
                        u_ref, q_ref, ckv_ref, skv_ref, wkv_ref, skr_ref, wkr_ref, gn_ref, gm_ref):
    h = _rms_bf16(x_ref[...], g_ref[...])
    cos = cos_ref[...]
    sin = sin_ref[...]
    cos_t = cost_ref[...]
    sin_t = sint_ref[...]

    def proj(a, n):
        return _dot(h, w_ref[:, a:a + n])

    t_all = _dot_nt(wt_ref[...], h)

    def proj_t(a, n):
        return t_all[a:a + n]

    u_ref[...] = proj(O_GLU, D_CONV) * _sigmoid(proj(O_GLU + D_CONV, D_CONV))
    gm_ref[...] = _sigmoid(proj(O_GM, 2 * D_MODEL)).astype(BF16)
    skr_ref[...] = _rope_slab(proj(O_SKV, LANES), cos, sin).astype(BF16)
    wkr_ref[...] = _rope_slab(proj(O_WKV, LANES), cos, sin).astype(BF16)
    for s in range(N_HEADS * HEAD_DIM // LANES):
        q_ref[0, s * LANES:(s + 1) * LANES, :] = _rope_rows(proj_t(R_Q + s * LANES, LANES), cos_t, sin_t).astype(BF16)
    ckv_ref[0] = proj_t(R_CKV, KV_COLS)
    skv_ref[0, 0:LANES] = _rope_rows(proj_t(R_SKV, LANES), cos_t, sin_t)
    skv_ref[0, LANES:KV_COLS] = proj_t(R_SKV + LANES, LANES)
    wkv_ref[0, 0:LANES] = _rope_rows(proj_t(R_WKV, LANES), cos_t, sin_t)
    wkv_ref[0, LANES:KV_COLS] = proj_t(R_WKV + LANES, LANES)
    gn_ref[0] = _sigmoid(proj_t(R_GN, GATE_ROWS))


def _inproj_rows(x2, g, w_packed, cos_t, sin_t, tm):
    m = x2.shape[0]
    row = lambda i: (i, 0)
    const = lambda i: (0, 0)
    out_shapes = (
        jax.ShapeDtypeStruct((m, D_CONV), F32),
        jax.ShapeDtypeStruct((m, Q_EXP), BF16),
        jax.ShapeDtypeStruct((m, KV_COLS), F32),
        jax.ShapeDtypeStruct((m, KV_COLS), F32),
        jax.ShapeDtypeStruct((m, KV_COLS), F32),
        jax.ShapeDtypeStruct((m, LANES), F32),
        jax.ShapeDtypeStruct((m, 2 * D_MODEL), BF16),
    )
    return pl.pallas_call(
        _inproj_rows_kernel,
        grid=(m // tm,),
        in_specs=[
            pl.BlockSpec((tm, D_MODEL), row),
            pl.BlockSpec((1, D_MODEL), const),
            pl.BlockSpec(w_packed.shape, const),
            pl.BlockSpec((tm, LANES), const),
            pl.BlockSpec((tm, LANES), const),
        ],
        out_specs=tuple(pl.BlockSpec((tm, s.shape[1]), row) for s in out_shapes),
        out_shape=out_shapes,
        compiler_params=_params(1),
    )(x2, g, w_packed, cos_t, sin_t)


def _inproj_cols(x2, g, w_packed, wkv_t, cos_t, sin_t, cos_tt, sin_tt, b, t_len, tm):
    m = x2.shape[0]
    n_tab = t_len // tm
    row = lambda i: (i, 0)
    const = lambda i: (0, 0)
    kv_map = lambda i: (i // n_tab, 0, i % n_tab)
    kv_shape = jax.ShapeDtypeStruct((b, KV_COLS, t_len), F32)
    out_shapes = (
        jax.ShapeDtypeStruct((m, D_CONV), F32),
        jax.ShapeDtypeStruct((b, N_HEADS * HEAD_DIM, t_len), BF16),
        kv_shape, kv_shape, kv_shape,
        jax.ShapeDtypeStruct((m, LANES), BF16),
        jax.ShapeDtypeStruct((m, LANES), BF16),
        jax.ShapeDtypeStruct((b, GATE_ROWS, t_len), F32),
        jax.ShapeDtypeStruct((m, 2 * D_MODEL), BF16),
    )
    out_specs = (
        pl.BlockSpec((tm, D_CONV), row),
        pl.BlockSpec((1, N_HEADS * HEAD_DIM, tm), kv_map),
        pl.BlockSpec((1, KV_COLS, tm), kv_map),
        pl.BlockSpec((1, KV_COLS, tm), kv_map),
        pl.BlockSpec((1, KV_COLS, tm), kv_map),
        pl.BlockSpec((tm, LANES), row),
        pl.BlockSpec((tm, LANES), row),
        pl.BlockSpec((1, GATE_ROWS, tm), kv_map),
        pl.BlockSpec((tm, 2 * D_MODEL), row),
    )
    return pl.pallas_call(
        _inproj_cols_kernel,
        grid=(m // tm,),
        in_specs=[
            pl.BlockSpec((tm, D_MODEL), row),
            pl.BlockSpec((1, D_MODEL), const),
            pl.BlockSpec(w_packed.shape, const),
            pl.BlockSpec(wkv_t.shape, const),
            pl.BlockSpec((tm, LANES), lambda i: (i % n_tab, 0)),
            pl.BlockSpec((tm, LANES), lambda i: (i % n_tab, 0)),
            pl.BlockSpec((LANES, tm), lambda i: (0, i % n_tab)),
            pl.BlockSpec((LANES, tm), lambda i: (0, i % n_tab)),
        ],
        out_specs=out_specs,
        out_shape=out_shapes,
        compiler_params=_params(1),
    )(x2, g, w_packed, wkv_t, cos_t, sin_t, cos_tt, sin_tt)


def _ln_swish(c, lg, lb):
    mu = jnp.mean(c, axis=-1, keepdims=True)
    d = c - mu
    var = jnp.mean(d * d, axis=-1, keepdims=True)
    y = d * lax.rsqrt(var + EPS) * lg + lb
    return y * _sigmoid(y)


CONV_PAD = 32
CONV_CHUNK = 64


def _conv_prompt_kernel(u_ref, past_ref, w_ref, b_ref, lg_ref, lb_ref, o_ref, uf_ref, *, t_len):
    off = CONV_PAD - (CONV_K - 1)
    uf_ref[0:SUBLANES, :] = jnp.zeros((SUBLANES, D_CONV), F32)
    uf_ref[off:CONV_PAD, :] = past_ref[0]
    uf_ref[CONV_PAD:CONV_PAD + t_len, :] = u_ref[0]
    bias = b_ref[...]
    lg = lg_ref[...]
    lb = lb_ref[...]
    ct = CONV_CHUNK

    def body(i, carry):
        base = pl.multiple_of(i * ct, ct)
        xw = uf_ref[pl.ds(base, ct + CONV_PAD), :]
        acc = jnp.zeros((ct, D_CONV), F32)
        n_win = ct + CONV_PAD
        for r in range(SUBLANES):
            yr = xw if r == 0 else pltpu.roll(xw, n_win - r, 0)
            for a in range((CONV_PAD // SUBLANES) + 1):
                k = SUBLANES * a + r - off
                if 0 <= k < CONV_K:
                    acc = acc + w_ref[k:k + 1, :] * yr[SUBLANES * a:SUBLANES * a + ct, :]
        o_ref[0, pl.ds(base, ct), :] = _ln_swish(acc + bias, lg, lb).astype(BF16)
        return carry

    lax.fori_loop(0, t_len // ct, body, 0)


def _conv_prompt(u3, past3, dw_w, dw_b, ln_g, ln_b):
    b, t_len, _ = u3.shape
    const2 = lambda i: (0, 0)
    return pl.pallas_call(
        functools.partial(_conv_prompt_kernel, t_len=t_len),
        grid=(b,),
        in_specs=[
            pl.BlockSpec((1, t_len, D_CONV), lambda i: (i, 0, 0)),
            pl.BlockSpec((1, CONV_K - 1, D_CONV), lambda i: (i, 0, 0)),
            pl.BlockSpec((CONV_K, D_CONV), const2),
            pl.BlockSpec((1, D_CONV), const2),
            pl.BlockSpec((1, D_CONV), const2),
            pl.BlockSpec((1, D_CONV), const2),
        ],
        out_specs=pl.BlockSpec((1, t_len, D_CONV), lambda i: (i, 0, 0)),
        out_shape=jax.ShapeDtypeStruct((b, t_len, D_CONV), BF16),
        scratch_shapes=[pltpu.VMEM((t_len + CONV_PAD, D_CONV), F32)],
        compiler_params=_params(1),
    )(u3, past3, dw_w, dw_b, ln_g, ln_b)


def _conv_sample_kernel(past_ref, u_ref, w_ref, b_ref, lg_ref, lb_ref, o_ref, *, tq):
    bias = b_ref[...]
    lg = lg_ref[...]
    lb = lb_ref[...]
    n_past = CONV_K - 1
    for t in range(tq):
        acc = jnp.zeros(o_ref.shape[1:], F32)
        for j in range(t, n_past):
            acc = acc + w_ref[j - t:j - t + 1, :] * past_ref[j]
        for i in range(t + 1):
            k = n_past - t + i
            acc = acc + w_ref[k:k + 1, :] * u_ref[i]
        o_ref[t] = _ln_swish(acc + bias, lg, lb).astype(BF16)


def _conv_sample(past_t, u_t, dw_w, dw_b, ln_g, ln_b):
    n_past, bd, _ = past_t.shape
    tq = u_t.shape[0]
    bt = _pick_tile(bd, 32)
    const2 = lambda i: (0, 0)
    return pl.pallas_call(
        functools.partial(_conv_sample_kernel, tq=tq),
        grid=(bd // bt,),
        in_specs=[
            pl.BlockSpec((n_past, bt, D_CONV), lambda i: (0, i, 0)),
            pl.BlockSpec((tq, bt, D_CONV), lambda i: (0, i, 0)),
            pl.BlockSpec((CONV_K, D_CONV), const2),
            pl.BlockSpec((1, D_CONV), const2),
            pl.BlockSpec((1, D_CONV), const2),
            pl.BlockSpec((1, D_CONV), const2),
        ],
        out_specs=pl.BlockSpec((tq, bt, D_CONV), lambda i: (0, i, 0)),
        out_shape=jax.ShapeDtypeStruct((tq, bd, D_CONV), BF16),
        compiler_params=_params(1),
    )(past_t, u_t, dw_w, dw_b, ln_g, ln_b)


def _chunkproj_kernel(x_ref, wk_ref, wv_ref, pk_ref, pv_ref, ab_ref, t_ref, *, n_pages, pages_on_lanes):
    n_rows = n_pages * CHUNKS_PER_PAGE
    for kv, (w_ref, p_ref) in enumerate(((wk_ref, pk_ref), (wv_ref, pv_ref))):
        for p in range(n_pages):
            page = x_ref[0, kv, :, p * PAGE_SIZE:(p + 1) * PAGE_SIZE] if pages_on_lanes else x_ref[p, kv]
            t_ref[p * PAGE_SIZE:(p + 1) * PAGE_SIZE, :] = page.T
        xs = jnp.concatenate([t_ref[pl.ds(l, n_rows, stride=CMP_STRIDE), :] for l in range(CMP_STRIDE)],
                             axis=1).astype(BF16)
        w = w_ref[...]
        part = _dot(xs, w)
        posb = _dot(p_ref[...], w)
        c0 = kv * 2 * LANES
        ab_ref[:, c0:c0 + LANES] = part[:, 0:LANES] + posb[0:1, 0:LANES]
        ab_ref[:, c0 + LANES:c0 + 2 * LANES] = part[:, LANES:2 * LANES] + posb[1:2, LANES:2 * LANES]


def _chunkproj(x4, wk2, wv2, pk2, pv2, page_off, n_total, pages_on_lanes):
    if pages_on_lanes:
        n_pages = x4.shape[3] // PAGE_SIZE
        n_steps = x4.shape[0]
        x_spec = pl.BlockSpec((1, 2, LANES, x4.shape[3]), lambda i: (i, 0, 0, 0))
    else:
        n_pages = _pick_tile(n_total, 64, 1)
        n_steps = n_total // n_pages
        off = page_off // n_pages
        x_spec = pl.BlockSpec((n_pages, 2, LANES, PAGE_SIZE), lambda i: (i + off, 0, 0, 0))
    const2 = lambda i: (0, 0)
    rows = n_pages * CHUNKS_PER_PAGE
    return pl.pallas_call(
        functools.partial(_chunkproj_kernel, n_pages=n_pages, pages_on_lanes=pages_on_lanes),
        grid=(n_steps,),
        in_specs=[
            x_spec,
            pl.BlockSpec(wk2.shape, const2),
            pl.BlockSpec(wv2.shape, const2),
            pl.BlockSpec(pk2.shape, const2),
            pl.BlockSpec(pv2.shape, const2),
        ],
        out_specs=pl.BlockSpec((rows, 4 * LANES), lambda i: (i, 0)),
        out_shape=jax.ShapeDtypeStruct((n_steps * rows, 4 * LANES), F32),
        scratch_shapes=[pltpu.VMEM((n_pages * PAGE_SIZE, LANES), F32)],
        compiler_params=_params(1),
    )(x4, wk2, wv2, pk2, pv2)


def _compressed_kv_f32(ab, cosc, sinc):
    n = ab.shape[0]
    kc = ab[:, 0:LANES] + pltpu.roll(ab[:, LANES:2 * LANES], n - 1, 0)
    vc = ab[:, 2 * LANES:3 * LANES] + pltpu.roll(ab[:, 3 * LANES:4 * LANES], n - 1, 0)
    return _rope_slab(kc, cosc, sinc), vc


def _compressed_kv(ab, cosc, sinc):
    kc, vc = _compressed_kv_f32(ab, cosc, sinc)
    return kc.astype(BF16), vc.astype(BF16)


def _softmax_parts(s, mask):
    sm = jnp.where(mask, s, NEG)
    m = jnp.max(sm, axis=-1, keepdims=True)
    p = jnp.where(mask, jnp.exp(sm - m), 0.0)
    l = jnp.maximum(jnp.sum(p, axis=-1, keepdims=True), 1e-30)
    return p, l


def _select_blocks(imp, qpos, n_sel, axis):
    jidx = lax.broadcasted_iota(jnp.int32, imp.shape, axis)
    cur = lax.shift_right_logical(qpos, 6)
    forced = (jidx == 0) | (jidx == cur) | (jidx == cur - 1)
    valid = (jidx * SEL_BLOCK <= qpos) & (jidx < n_sel)
    v = jnp.where(valid, jnp.where(forced, jnp.inf, imp), -jnp.inf)
    rank = jnp.zeros(imp.shape, jnp.int32)
    for k in range(n_sel):
        vk = v[k:k + 1, :] if axis == 0 else v[:, k:k + 1]
        ahead = (vk > v) | ((vk == v) & (jidx > k))
        rank = rank + ahead.astype(jnp.int32)
    return (rank < min(N_SEL, n_sel)) & valid


def _split_bf16(x):
    hi = x.astype(BF16)
    r1 = x - hi.astype(F32)
    mid = r1.astype(BF16)
    lo = (r1 - mid.astype(F32)).astype(BF16)
    return hi, mid, lo


def _overlap(n_cmp, cmp_axis):
    i = lax.broadcasted_iota(jnp.int32, (LANES, LANES), cmp_axis)
    j = lax.broadcasted_iota(jnp.int32, (LANES, LANES), 1 - cmp_axis)
    hit = (i * CMP_STRIDE < (j + 1) * SEL_BLOCK) & (i * CMP_STRIDE + CMP_BLOCK > j * SEL_BLOCK) & (i < n_cmp)
    return hit.astype(F32)


Q_TILE = 256
K_TILE = 256


def _attn_prompt_kernel(q_ref, gn_ref, skr_ref, svt_ref, wkr_ref, wvt_ref, ab_ref, cosc_ref, sinc_ref, o_ref,
                        kc_ref, vct_ref, bias_ref, m_ref, acc_ref, accb_ref, out_ref, *, t_len):
    qb = pl.program_id(1)
    n_chunk = t_len // CMP_STRIDE
    n_cmp = (t_len - CMP_BLOCK) // CMP_STRIDE + 1
    n_sel = -(-t_len // SEL_BLOCK)
    sel_rows = -(-n_sel // SUBLANES) * SUBLANES
    qt, kt_sz = Q_TILE, K_TILE
    q_tiles = qt // LANES
    g_lanes = HPG * qt
    n_lane_tiles = N_HEADS * q_tiles

    @pl.when(qb == 0)
    def _():
        kc, vc = _compressed_kv_f32(ab_ref[...], cosc_ref[...], sinc_ref[...])
        kc_ref[...] = kc.astype(BF16)
        vct_ref[...] = vc.T.astype(BF16)

    q0 = qb * qt
    qpos = q0 + lax.broadcasted_iota(jnp.int32, (1, qt), 1)

    def lane_tile(c):
        return slice(c * LANES, (c + 1) * LANES)

    def q_part(c):
        return slice((c % q_tiles) * LANES, (c % q_tiles + 1) * LANES)

    def group_of(c):
        return c // (HPG * q_tiles)

    def group_lanes(g):
        return slice(g * g_lanes, (g + 1) * g_lanes)

    q_zero = jnp.zeros((HEAD_DIM, qt), BF16)

    def q_slab(hh):
        q_h = q_ref[0, hh * HEAD_DIM:(hh + 1) * HEAD_DIM, :]
        return jnp.concatenate([q_h, q_zero] if hh < HPG else [q_zero, q_h], axis=0)

    q_all = jnp.concatenate([q_slab(hh) for hh in range(N_HEADS)], axis=1)

    def gates(branch):
        r0 = branch * N_HEADS
        return jnp.concatenate([gn_ref[0, r0 + hh:r0 + hh + 1, :] for hh in range(N_HEADS)], axis=1)

    nrow = lax.broadcasted_iota(jnp.int32, (n_chunk, qt), 0)
    mask_c = (nrow * CMP_STRIDE + CMP_BLOCK - 1 <= qpos) & (nrow < n_cmp)
    acc_ref[...] = _dot(kc_ref[...], q_all)
    hs = [[jnp.zeros((n_chunk, LANES), F32) for _ in range(q_tiles)] for _ in range(N_KV)]
    p_parts = []
    for c in range(n_lane_tiles):
        mask = mask_c[:, q_part(c)]
        sm = jnp.where(mask, acc_ref[:, lane_tile(c)], NEG)
        p_c = jnp.where(mask, jnp.exp(sm - jnp.max(sm, axis=0, keepdims=True)), 0.0)
        p_c = p_c / jnp.maximum(jnp.sum(p_c, axis=0, keepdims=True), 1e-30)
        hs[group_of(c)][c % q_tiles] = hs[group_of(c)][c % q_tiles] + p_c
        p_parts.append(p_c.astype(BF16))
    out_ref[...] = gates(0) * _dot(vct_ref[...], jnp.concatenate(p_parts, axis=1))

    sel_t = []
    for g in range(N_KV):
        ov_t = _overlap(n_cmp, 1).astype(BF16)
        imp_t = sum(_dot(ov_t, part) for part in _split_bf16(jnp.concatenate(hs[g], axis=1)))
        s_g = _select_blocks(imp_t[0:sel_rows], qpos, n_sel, 0).astype(F32)
        sel_t.append(jnp.concatenate([s_g, jnp.zeros((LANES - sel_rows, qt), F32)], axis=0).astype(BF16))

    m_ref[...] = jnp.full(m_ref.shape, NEG, F32)
    accb_ref[...] = jnp.zeros(accb_ref.shape, F32)

    def tile_step(br, kr_ref, vt_ref, k_idx, bias_fn):
        k0 = pl.multiple_of(k_idx * kt_sz, kt_sz)
        kpos = k0 + lax.broadcasted_iota(jnp.int32, (kt_sz, qt), 0)
        bias_fn(k_idx, kpos)
        s = _dot(kr_ref[pl.ds(k0, kt_sz), :], q_all)
        vt = vt_ref[0, :, pl.ds(k0, kt_sz)]
        vrow = lax.broadcasted_iota(jnp.int32, (LANES, kt_sz), 0)
        p_parts, a_parts = [], []
        for c in range(n_lane_tiles):
            sb = s[:, lane_tile(c)] + bias_ref[br, group_of(c), :, q_part(c)]
            m_old = m_ref[br, 0:1, lane_tile(c)]
            m_new = jnp.maximum(m_old, jnp.max(sb, axis=0, keepdims=True))
            p_parts.append(jnp.exp(sb - m_new).astype(BF16))
            a_parts.append(jnp.exp(m_old - m_new))
            m_ref[br, :, lane_tile(c)] = jnp.broadcast_to(m_new, (SUBLANES, LANES))
        for g in range(N_KV):
            v_own = (vrow >= g * HEAD_DIM) & (vrow < (g + 1) * HEAD_DIM)
            vaug = jnp.where(v_own, vt, 1.0).astype(BF16)
            tiles = range(g * HPG * q_tiles, (g + 1) * HPG * q_tiles)
            alpha = jnp.concatenate([a_parts[c] for c in tiles], axis=1)
            p = jnp.concatenate([p_parts[c] for c in tiles], axis=1)
            accb_ref[br, :, group_lanes(g)] = alpha * accb_ref[br, :, group_lanes(g)] + _dot(vaug, p)

    def sel_bias(k_idx, kpos):
        krow = lax.broadcasted_iota(jnp.int32, (kt_sz, LANES), 0)
        jcol = lax.broadcasted_iota(jnp.int32, (kt_sz, LANES), 1)
        expand_t = (jcol == k_idx * (kt_sz // SEL_BLOCK) + lax.shift_right_logical(krow, 6)).astype(BF16)
        for g in range(N_KV):
            bias_ref[0, g] = jnp.where((_dot(expand_t, sel_t[g]) > 0.5) & (kpos <= qpos), 0.0, NEG)

    def win_bias(k_idx, kpos):
        b = jnp.where((kpos <= qpos) & (kpos > qpos - WINDOW), 0.0, NEG)
        for g in range(N_KV):
            bias_ref[1, g] = b

    n_sel_tiles = qb + 1
    n_win_tiles = jnp.minimum(qb, WINDOW // kt_sz) + 1

    def both(i, carry):
        tile_step(0, skr_ref, svt_ref, i, sel_bias)
        tile_step(1, wkr_ref, wvt_ref, qb - i, win_bias)
        return carry

    def sel_only(i, carry):
        tile_step(0, skr_ref, svt_ref, i, sel_bias)
        return carry

    lax.fori_loop(0, n_win_tiles, both, 0)
    lax.fori_loop(n_win_tiles, n_sel_tiles, sel_only, 0)

    for br in range(2):
        gt = gates(br + 1)
        for g in range(N_KV):
            sum_row = (1 - g) * HEAD_DIM
            acc = accb_ref[br, :, group_lanes(g)]
            out_ref[:, group_lanes(g)] = (out_ref[:, group_lanes(g)]
                                          + gt[:, group_lanes(g)] * (acc / acc[sum_row:sum_row + 1, :]))

    lane = lax.broadcasted_iota(jnp.int32, (qt, LANES), 1)
    for pair in range(N_HEADS // 2):
        a = out_ref[:, (2 * pair) * qt:(2 * pair + 1) * qt].T
        b = out_ref[:, (2 * pair + 1) * qt:(2 * pair + 2) * qt].T
        if 2 * pair < HPG:
            slab = jnp.where(lane < HEAD_DIM, a, pltpu.roll(b, HEAD_DIM, 1))
        else:
            slab = jnp.where(lane < HEAD_DIM, pltpu.roll(a, HEAD_DIM, 1), b)
        o_ref[:, pair * LANES:(pair + 1) * LANES] = slab.astype(BF16)


def _attn_prompt(q_t, gn_t, skr, skv_t, wkr, wkv_t, ab2, cosc, sinc, b, t_len):
    n_chunk = t_len // CMP_STRIDE
    nqb = t_len // Q_TILE
    const2 = lambda i, j: (0, 0)
    per_b2 = lambda i, j: (i, 0)
    v_rows = lambda i, j: (i, 1, 0)
    return pl.pallas_call(
        functools.partial(_attn_prompt_kernel, t_len=t_len),
        grid=(b, nqb),
        in_specs=[
            pl.BlockSpec((1, N_HEADS * HEAD_DIM, Q_TILE), lambda i, j: (i, 0, j)),
            pl.BlockSpec((1, GATE_ROWS, Q_TILE), lambda i, j: (i, 0, j)),
            pl.BlockSpec((t_len, LANES), per_b2),
            pl.BlockSpec((1, LANES, t_len), v_rows),
            pl.BlockSpec((t_len, LANES), per_b2),
            pl.BlockSpec((1, LANES, t_len), v_rows),
            pl.BlockSpec((n_chunk, 4 * LANES), per_b2),
            pl.BlockSpec((n_chunk, LANES), const2),
            pl.BlockSpec((n_chunk, LANES), const2),
        ],
        out_specs=pl.BlockSpec((Q_TILE, N_HEADS * HEAD_DIM), lambda i, j: (i * nqb + j, 0)),
        out_shape=jax.ShapeDtypeStruct((b * t_len, N_HEADS * HEAD_DIM), BF16),
        scratch_shapes=[
            pltpu.VMEM((n_chunk, LANES), BF16),
            pltpu.VMEM((LANES, n_chunk), BF16),
            pltpu.VMEM((2, N_KV, K_TILE, Q_TILE), F32),
            pltpu.VMEM((2, SUBLANES, N_HEADS * Q_TILE), F32),
            pltpu.VMEM((LANES, N_HEADS * Q_TILE), F32),
            pltpu.VMEM((2, LANES, N_HEADS * Q_TILE), F32),
            pltpu.VMEM((LANES, N_HEADS * Q_TILE), F32),
        ],
        compiler_params=_params(2),
    )(q_t, gn_t, skr, skv_t, wkr, wkv_t, ab2, cosc, sinc)


Q_ROWS = 8
SEQ_PER_STEP = 4


def _attn_sample_kernel(pt_ref, q_ref, gn_ref, sknew_ref, wknew_ref, win_ref, cosc_ref, sinc_ref, sel_hbm, ab_hbm,
                        o_ref, sel_buf, ab_buf, sem, *, past_len, tq, n_pages, n_seq, page_base):
    step = pl.program_id(0)
    n_steps = pl.num_programs(0)
    slot = step % 2
    n_slot_pages = n_seq * n_pages

    def page_copies(for_step, to_slot):
        copies = []
        for j in range(n_slot_pages):
            page = pt_ref[for_step * n_slot_pages + j]
            copies.append(pltpu.make_async_copy(sel_hbm.at[page_base + page], sel_buf.at[to_slot, j],
                                                sem.at[to_slot, j]))
            copies.append(pltpu.make_async_copy(ab_hbm.at[page], ab_buf.at[to_slot, j],
                                                sem.at[to_slot, n_slot_pages + j]))
        return copies

    @pl.when(step == 0)
    def _():
        for c in page_copies(0, 0):
            c.start()

    @pl.when(step + 1 < n_steps)
    def _():
        for c in page_copies(step + 1, 1 - slot):
            c.start()

    for c in page_copies(step, slot):
        c.wait()

    def sel_page(e, p, kv):
        return sel_buf[slot, e * n_pages + p, kv]

    seqs = range(n_seq)
    t_all = past_len + tq
    n_chunk = past_len // CMP_STRIDE
    n_cmp = (t_all - CMP_BLOCK) // CMP_STRIDE + 1
    n_sel = -(-t_all // SEL_BLOCK)
    rows = N_HEADS * Q_ROWS
    all_rows = n_seq * rows
    win_buf = win_ref.shape[3]

    def stack(parts):
        return jnp.concatenate(parts, axis=0)

    def per_seq(a, e):
        return a[e * rows:(e + 1) * rows]

    qbd = [stack([q_ref[e, :, hh * LANES:(hh + 1) * LANES] for hh in range(N_HEADS)]) for e in seqs]
    qpos = past_len + lax.broadcasted_iota(jnp.int32, (all_rows, 1), 0) % Q_ROWS

    cosc = cosc_ref[...]
    sinc = sinc_ref[...]
    kvc = [_compressed_kv(stack([ab_buf[slot, e * n_pages + p] for p in range(n_pages)]), cosc, sinc) for e in seqs]
    ncol = lax.broadcasted_iota(jnp.int32, (all_rows, n_chunk), 1)
    mask_c = (ncol * CMP_STRIDE + CMP_BLOCK - 1 <= qpos) & (ncol < n_cmp)
    p_c, l_c = _softmax_parts(stack([_dot_nt(qbd[e], kvc[e][0]) for e in seqs]), mask_c)
    p_c = p_c / l_c
    p_c16 = p_c.astype(BF16)
    o_c = stack([_dot(per_seq(p_c16, e), kvc[e][1]) for e in seqs])

    hs = []
    for e in seqs:
        for g in range(N_KV):
            r0 = e * rows + g * HPG * Q_ROWS
            acc = p_c[r0:r0 + Q_ROWS]
            for h in range(1, HPG):
                acc = acc + p_c[r0 + h * Q_ROWS:r0 + (h + 1) * Q_ROWS]
            hs.append(acc)
    ov = _overlap(n_cmp, 0).astype(BF16)
    imp = sum(_dot(part, ov) for part in _split_bf16(stack(hs)))
    sel_g = _select_blocks(imp, qpos[0:n_seq * N_KV * Q_ROWS], n_sel, 1).astype(BF16)
    sel_rows = stack([sel_g[(e * N_KV + g) * Q_ROWS:(e * N_KV + g + 1) * Q_ROWS]
                      for e in seqs for g in range(N_KV) for _ in range(HPG)])

    def new_rows_tile(ref, e, c0):
        return stack([ref[e, :, c0:c0 + LANES], jnp.zeros((LANES - Q_ROWS, LANES), F32)]).astype(BF16)

    n_keys = (n_pages + 1) * PAGE_SIZE
    s_s = stack([jnp.concatenate([_dot(qbd[e], sel_page(e, p, 0).astype(BF16)) for p in range(n_pages)]
                                 + [_dot_nt(qbd[e], new_rows_tile(sknew_ref, e, 0))], axis=1) for e in seqs])
    jrow = lax.broadcasted_iota(jnp.int32, (LANES, n_keys), 0)
    kcol = lax.broadcasted_iota(jnp.int32, (LANES, n_keys), 1)
    expand = (jrow == lax.shift_right_logical(kcol, 6)).astype(BF16)
    kpos = lax.broadcasted_iota(jnp.int32, (all_rows, n_keys), 1)
    mask_s = (_dot(sel_rows, expand) > 0.5) & (kpos <= qpos)
    p_s, l_s = _softmax_parts(s_s, mask_s)
    p_s = p_s.astype(BF16)
    o_s = []
    for e in seqs:
        p_e = per_seq(p_s, e)
        acc = _dot(p_e[:, n_pages * PAGE_SIZE:], new_rows_tile(sknew_ref, e, LANES))
        for i in range(n_pages):
            acc = acc + _dot_nt(p_e[:, i * PAGE_SIZE:(i + 1) * PAGE_SIZE], sel_page(e, i, 1).astype(BF16))
        o_s.append(acc)
    o_s = stack(o_s) / l_s

    s_w = stack([jnp.concatenate([_dot(qbd[e], win_ref[e, 0].astype(BF16)),
                                  _dot_nt(qbd[e], new_rows_tile(wknew_ref, e, 0))], axis=1) for e in seqs])
    wcol = lax.broadcasted_iota(jnp.int32, (all_rows, win_buf + LANES), 1)
    kpos_w = past_len - win_buf + wcol
    mask_w = (kpos_w <= qpos) & (kpos_w > qpos - WINDOW)
    p_w, l_w = _softmax_parts(s_w, mask_w)
    p_w = p_w.astype(BF16)
    o_w = stack([_dot_nt(per_seq(p_w, e)[:, 0:win_buf], win_ref[e, 1].astype(BF16))
                 + _dot(per_seq(p_w, e)[:, win_buf:], new_rows_tile(wknew_ref, e, LANES)) for e in seqs]) / l_w

    for e in seqs:
        gn = gn_ref[e]
        for hh in range(N_HEADS):
            r = slice(e * rows + hh * Q_ROWS, e * rows + (hh + 1) * Q_ROWS)
            out = (gn[:, hh:hh + 1] * o_c[r]
                   + gn[:, N_HEADS + hh:N_HEADS + hh + 1] * o_s[r]
                   + gn[:, 2 * N_HEADS + hh:2 * N_HEADS + hh + 1] * o_w[r])
            o_ref[e, :, hh * LANES:(hh + 1) * LANES] = out.astype(BF16)


def _attn_sample(pt_flat, q3, gn3, sknew3, wknew3, win4, cosc, sinc, sel_pool4, ab_pool3,
                 layer, n_pool, bd, past_len, tq):
    n_pages = past_len // PAGE_SIZE
    win_buf = win4.shape[3]
    n_chunk = past_len // CMP_STRIDE
    n_seq = _pick_tile(bd, SEQ_PER_STEP, 1)
    per_b = lambda i, pt: (i, 0, 0)
    const2 = lambda i, pt: (0, 0)
    n_slot_pages = n_seq * n_pages
    grid_spec = pltpu.PrefetchScalarGridSpec(
        num_scalar_prefetch=1,
        grid=(bd // n_seq,),
        in_specs=[
            pl.BlockSpec((n_seq, Q_ROWS, Q_EXP), per_b),
            pl.BlockSpec((n_seq, Q_ROWS, LANES), per_b),
            pl.BlockSpec((n_seq, Q_ROWS, KV_COLS), per_b),
            pl.BlockSpec((n_seq, Q_ROWS, KV_COLS), per_b),
            pl.BlockSpec((n_seq, 2, LANES, win_buf), lambda i, pt: (layer * (bd // n_seq) + i, 0, 0, 0)),
            pl.BlockSpec((n_chunk, LANES), const2),
            pl.BlockSpec((n_chunk, LANES), const2),
            pl.BlockSpec(memory_space=pl.ANY),
            pl.BlockSpec(memory_space=pl.ANY),
        ],
        out_specs=pl.BlockSpec((n_seq, Q_ROWS, Q_EXP), per_b),
        scratch_shapes=[
            pltpu.VMEM((2, n_slot_pages, 2, LANES, PAGE_SIZE), F32),
            pltpu.VMEM((2, n_slot_pages, CHUNKS_PER_PAGE, 4 * LANES), F32),
            pltpu.SemaphoreType.DMA((2, 2 * n_slot_pages)),
        ],
    )
    return pl.pallas_call(
        functools.partial(_attn_sample_kernel, past_len=past_len, tq=tq, n_pages=n_pages, n_seq=n_seq,
                          page_base=layer * n_pool),
        grid_spec=grid_spec,
        out_shape=jax.ShapeDtypeStruct((bd, Q_ROWS, Q_EXP), BF16),
        compiler_params=_params(1),
    )(pt_flat, q3, gn3, sknew3, wknew3, win4, cosc, sinc, sel_pool4, ab_pool3)


def _post_kernel(c_ref, o_ref, gm_ref, x_ref, wc_ref, wn_ref, wo_ref, g_ref, wu_ref, wd_ref, gf_ref, y_ref, *, final):
    conv_out = _dot(c_ref[...], wc_ref[...])
    nsa_out = _dot(o_ref[...], wn_ref[...])
    gm = gm_ref[...].astype(F32)
    merged = gm[:, 0:D_MODEL] * conv_out + gm[:, D_MODEL:2 * D_MODEL] * nsa_out
    x = x_ref[...] + _dot(merged.astype(BF16), wo_ref[...])
    h = _rms_bf16(x, g_ref[...])
    acc = x
    for f in range(D_FF // FF_TILE):
        up = jnp.maximum(_dot(h, wu_ref[:, f * FF_TILE:(f + 1) * FF_TILE]), 0.0)
        acc = acc + _dot((up * up).astype(BF16), wd_ref[f * FF_TILE:(f + 1) * FF_TILE, :])
    if final:
        ms2 = jnp.mean(acc * acc, axis=-1, keepdims=True)
        acc = acc * lax.rsqrt(ms2 + EPS) * gf_ref[...]
    y_ref[...] = acc


FF_TILE = 1024


def _post(c2, o2, gm2, x2, wc, wn, wo, g, wu, wd, gf, tm, final):
    m = x2.shape[0]
    row = lambda i: (i, 0)
    const = lambda i: (0, 0)
    resident = lambda a: pl.BlockSpec(a.shape, const, pipeline_mode=pl.Buffered(1))
    return pl.pallas_call(
        functools.partial(_post_kernel, final=final),
        grid=(m // tm,),
        in_specs=[
            pl.BlockSpec((tm, c2.shape[1]), row),
            pl.BlockSpec((tm, o2.shape[1]), row),
            pl.BlockSpec((tm, 2 * D_MODEL), row),
            pl.BlockSpec((tm, D_MODEL), row),
            resident(wc), resident(wn), resident(wo),
            pl.BlockSpec((1, D_MODEL), const),
            resident(wu), resident(wd),
            pl.BlockSpec((1, D_MODEL), const),
        ],
        out_specs=pl.BlockSpec((tm, D_MODEL), row),
        out_shape=jax.ShapeDtypeStruct((m, D_MODEL), F32),
        compiler_params=_params(1),
    )(c2, o2, gm2, x2, wc, wn, wo, g, wu, wd, gf)


def _shift_append_kernel(old_ref, new_ref, o_ref, *, n_new):
    o_ref[...] = jnp.concatenate([old_ref[..., n_new:], new_ref[...]], axis=-1)


def _shift_append(old, new, rows_per_step):
    n, a, b, keep = old.shape
    n_new = new.shape[-1]
    r = _pick_tile(n, rows_per_step, 1)
    return pl.pallas_call(
        functools.partial(_shift_append_kernel, n_new=n_new),
        grid=(n // r,),
        in_specs=[pl.BlockSpec((r, a, b, keep), lambda i: (i, 0, 0, 0)),
                  pl.BlockSpec((r, a, b, n_new), lambda i: (i, 0, 0, 0))],
        out_specs=pl.BlockSpec((r, a, b, keep), lambda i: (i, 0, 0, 0)),
        out_shape=jax.ShapeDtypeStruct(old.shape, old.dtype),
        compiler_params=_params(1),
    )(old, new)


def _rope_tables(pos):
    half = HEAD_DIM // 2
    inv = jnp.power(ROPE_THETA, -jnp.arange(half, dtype=F32) / half)
    ang = pos.astype(F32)[:, None] * inv[None, :]
    cos = jnp.cos(ang)
    sin = jnp.sin(ang)
    return jnp.concatenate([cos, cos, cos, cos], axis=1), jnp.concatenate([-sin, sin, -sin, sin], axis=1)


def _pack_w_in(w):
    o0 = 2 * D_CONV
    o1 = o0 + N_HEADS * HEAD_DIM
    o4 = o1 + 3 * KV_COLS
    o5 = o4 + N_GATES
    wq = w[:, o0:o1].reshape(D_MODEL, N_HEADS, HEAD_DIM) * (HEAD_DIM ** -0.5)
    zero = jnp.zeros_like(wq)
    in_g0 = (jnp.arange(N_HEADS) < HPG)[None, :, None]
    wq_exp = jnp.concatenate([jnp.where(in_g0, wq, zero), jnp.where(in_g0, zero, wq)], axis=2)
    wgn = jnp.pad(w[:, o4:o5], ((0, 0), (0, LANES - N_GATES)))
    wq_exp = wq_exp.reshape(D_MODEL, Q_EXP)
    packed = jnp.concatenate([w[:, :o0], wq_exp, w[:, o1:o4], wgn, w[:, o5:]], axis=1)
    w_t = jnp.concatenate([wq.reshape(D_MODEL, N_HEADS * HEAD_DIM), w[:, o1:o4], wgn[:, :GATE_ROWS]], axis=1).T
    return packed.astype(BF16), w_t.astype(BF16)


def _pack_w_nsa_out(w):
    wh = w.reshape(N_HEADS, HEAD_DIM, D_MODEL)
    zero = jnp.zeros_like(wh)
    in_g0 = (jnp.arange(N_HEADS) < HPG)[:, None, None]
    return jnp.concatenate([jnp.where(in_g0, wh, zero), jnp.where(in_g0, zero, wh)], axis=1).reshape(Q_EXP, D_MODEL).astype(BF16)


def _pack_w_cmp(w_kv, pos_kv):
    lo, hi = w_kv[:CMP_STRIDE], w_kv[CMP_STRIDE:]
    eye = jnp.eye(N_KV, dtype=w_kv.dtype)
    blk = lambda part: jnp.einsum('lde,gh->lgdhe', part, eye).reshape(CMP_STRIDE * N_KV * HEAD_DIM, N_KV * HEAD_DIM)
    w2 = jnp.concatenate([blk(lo), blk(hi)], axis=1).astype(BF16)
    tile = lambda p: jnp.broadcast_to(p[:, None, :], (CMP_STRIDE, N_KV, HEAD_DIM)).reshape(1, -1)
    p2 = jnp.concatenate([tile(pos_kv[:CMP_STRIDE]), tile(pos_kv[CMP_STRIDE:]),
                          jnp.zeros((SUBLANES - 2, CMP_STRIDE * N_KV * HEAD_DIM), pos_kv.dtype)], axis=0).astype(BF16)
    return w2, p2


def _feature_major(a):
    lead = a.shape[:-4]
    n = len(lead)
    a = jnp.transpose(a, tuple(range(n)) + (n + 1, n + 2, n + 3, n))
    return a.reshape(lead + (2, N_KV * HEAD_DIM, a.shape[-1]))


def _position_major(a_t, rows):
    b = a_t.shape[0]
    return jnp.transpose(a_t.reshape(b, 2, N_KV, HEAD_DIM, rows), (0, 4, 1, 2, 3))


def kernel(x_prompt, x_sample, cache_cmp_kv, cache_sel_kv, state_win_kv, state_conv, page_table, norm_mix_g, w_in, conv_dw_w, conv_dw_b, conv_ln_g, conv_ln_b, w_conv_out, cmp_pos, w_cmp, w_nsa_out, w_out, norm_mlp_g, w_up, w_down, norm_final_g):
    depth = w_in.shape[0]
    bp, tp, _ = x_prompt.shape
    bd, tq, _ = x_sample.shape
    n_pool = cache_cmp_kv.shape[1]
    n_pages = page_table.shape[1]
    past_len = n_pages * PAGE_SIZE
    win_buf = state_win_kv.shape[2]
    assert tp % Q_TILE == 0 and tp // CMP_STRIDE == LANES and tp >= WINDOW
    assert past_len // CMP_STRIDE == LANES and win_buf == WINDOW and tq <= Q_ROWS

    mp, ms = bp * tp, bd * tq
    tm_p = _pick_tile(tp, 512)
    tm_s = _pick_tile(ms, 256)
    assert tp % tm_p == 0 and tm_s % tq == 0

    cos_p, sin_p = _rope_tables(jnp.arange(tp))
    cos_pt, sin_pt = cos_p.T, sin_p.T
    cos_s, sin_s = _rope_tables(past_len + jnp.arange(tm_s) % tq)
    n_chunk = tp // CMP_STRIDE
    cos_c, sin_c = _rope_tables(jnp.arange(n_chunk) * CMP_STRIDE + CMP_BLOCK - 1)

    pt_flat = page_table.reshape(-1).astype(jnp.int32)
    cmp_pool4 = _feature_major(cache_cmp_kv).reshape(depth * n_pool, 2, LANES, PAGE_SIZE)
    sel_pool4 = _feature_major(cache_sel_kv).reshape(depth * n_pool, 2, LANES, PAGE_SIZE)
    win4 = _feature_major(state_win_kv).reshape(depth * bd, 2, LANES, win_buf)
    zeros_conv = jnp.zeros((bp, CONV_K - 1, D_CONV), F32)

    xp = x_prompt.reshape(mp, D_MODEL)
    xs = x_sample.reshape(ms, D_MODEL)
    outs = [[] for _ in range(8)]
    for l in range(depth):
        w_packed, w_t = _pack_w_in(w_in[l])
        wn_exp = _pack_w_nsa_out(w_nsa_out[l])
        wn = w_nsa_out[l].astype(BF16)
        wc, wo = w_conv_out[l].astype(BF16), w_out[l].astype(BF16)
        wu, wd = w_up[l].astype(BF16), w_down[l].astype(BF16)
        wk2, pk2 = _pack_w_cmp(w_cmp[l, 0], cmp_pos[l, 0])
        wv2, pv2 = _pack_w_cmp(w_cmp[l, 1], cmp_pos[l, 1])
        g_mix, g_mlp = norm_mix_g[l][None], norm_mlp_g[l][None]
        dw_b, ln_g, ln_b = conv_dw_b[l][None], conv_ln_g[l][None], conv_ln_b[l][None]
        gf = norm_final_g[None]
        final = l == depth - 1

        u, q_t, ckv_t, skv_t, wkv_tm, skr, wkr, gn_t, gm = _inproj_cols(xp, g_mix, w_packed, w_t, cos_p, sin_p,
                                                                        cos_pt, sin_pt, bp, tp, tm_p)
        c_act = _conv_prompt(u.reshape(bp, tp, D_CONV), zeros_conv, conv_dw_w[l], dw_b, ln_g, ln_b)
        ab = _chunkproj(ckv_t.reshape(bp, 2, LANES, tp), wk2, wv2, pk2, pv2, 0, 0, True)
        o = _attn_prompt(q_t, gn_t, skr, skv_t, wkr, wkv_tm, ab, cos_c, sin_c, bp, tp)
        xp = _post(c_act.reshape(mp, D_CONV), o, gm, xp, wc, wn, wo, g_mlp, wu, wd, gf, tm_p, final)
        keep = min(WINDOW, tp)
        outs[0].append(_position_major(ckv_t, tp))
        outs[2].append(_position_major(skv_t, tp))
        outs[4].append(_position_major(wkv_tm[:, :, tp - keep:], keep))
        outs[6].append(u.reshape(bp, tp, D_CONV)[:, tp - (CONV_K - 1):])

        u, q, ckv, skv, wkv, gn, gm = _inproj_rows(xs, g_mix, w_packed, cos_s, sin_s, tm_s)
        u3 = u.reshape(bd, tq, D_CONV)
        c_t = _conv_sample(jnp.swapaxes(state_conv[l], 0, 1), jnp.swapaxes(u3, 0, 1), conv_dw_w[l], dw_b, ln_g, ln_b)
        c_act = jnp.swapaxes(c_t, 0, 1).reshape(ms, D_CONV)
        ab_pool = _chunkproj(cmp_pool4, wk2, wv2, pk2, pv2, l * n_pool, n_pool, False)
        pad_q = lambda a: jnp.pad(a.reshape(bd, tq, -1), ((0, 0), (0, Q_ROWS - tq), (0, 0)))
        o = _attn_sample(pt_flat, pad_q(q), pad_q(gn), pad_q(skv), pad_q(wkv), win4, cos_c, sin_c, sel_pool4,
                         ab_pool.reshape(n_pool, CHUNKS_PER_PAGE, 4 * LANES), l, n_pool, bd, past_len, tq)
        xs = _post(c_act, o[:, :tq].reshape(ms, Q_EXP), gm, xs, wc, wn_exp, wo, g_mlp, wu, wd, gf, tm_s, final)
        wkv5 = wkv.reshape(bd, tq, 2, N_KV, HEAD_DIM)
        outs[1].append(ckv.reshape(bd, tq, 2, N_KV, HEAD_DIM))
        outs[3].append(skv.reshape(bd, tq, 2, N_KV, HEAD_DIM))
        outs[5].append(wkv5)
        outs[7].append(u3)

    y_prompt = xp.reshape(bp, tp, D_MODEL)
    y_sample = xs.reshape(bd, tq, D_MODEL)
    outs = [jnp.stack(o) for o in outs]
    win_new = _shift_append(win4, _feature_major(outs[5]).reshape(depth * bd, 2, LANES, tq), 8)
    outs[5] = _position_major(win_new.reshape(depth * bd, KV_COLS, win_buf), win_buf).reshape(
        depth, bd, win_buf, 2, N_KV, HEAD_DIM)
    outs[7] = jnp.concatenate([state_conv[:, :, tq:], outs[7]], axis=2)
    return (y_prompt, y_sample) + tuple(outs)
```

```python
import functools

import jax
import jax.numpy as jnp
from jax import lax
from jax.experimental import pallas as pl
from jax.experimental.pallas import tpu as pltpu

D_MODEL = 1024
D_CONV = D_MODEL // 2
CONV_K = 31
N_HEADS = 8
HEAD_DIM = 64
N_KV = 2
HPG = N_HEADS // N_KV
KV_COLS = 2 * N_KV * HEAD_DIM
CMP_BLOCK = 32
CMP_STRIDE = 16
SEL_BLOCK = 64
N_SEL = 16
WINDOW = 512
D_FF = 4 * D_MODEL
ROPE_THETA = 10000.0
EPS = 1e-6
PAGE_SIZE = 128

LANES = 128
SUBLANES = 8
Q_EXP = N_HEADS * LANES
N_GATES = 3 * N_HEADS
CHUNKS_PER_PAGE = PAGE_SIZE // CMP_STRIDE
NEG = -1e30

O_GLU = 0
O_Q = O_GLU + 2 * D_CONV
O_CKV = O_Q + Q_EXP
O_SKV = O_CKV + KV_COLS
O_WKV = O_SKV + KV_COLS
O_GN = O_WKV + KV_COLS
O_GM = O_GN + LANES
IN_COLS_PACKED = O_GM + 2 * D_MODEL

VMEM_LIMIT = 56 * 1024 * 1024

F32 = jnp.float32
BF16 = jnp.bfloat16


def _params(n_axes, vmem=VMEM_LIMIT):
    return pltpu.CompilerParams(dimension_semantics=("arbitrary",) * n_axes, vmem_limit_bytes=vmem)


def _sigmoid(x):
    return 1.0 / (1.0 + jnp.exp(-x))


def _dot(a, b):
    return jnp.dot(a, b, preferred_element_type=F32)


def _dot_nt(a, b):
    return lax.dot_general(a, b, (((1,), (1,)), ((), ())), preferred_element_type=F32)


def _rope_slab(xs, cos, sin_signed):
    lane = lax.broadcasted_iota(jnp.int32, xs.shape, 1)
    first = (lane % HEAD_DIM) < (HEAD_DIM // 2)
    rot = jnp.where(first, pltpu.roll(xs, LANES - HEAD_DIM // 2, 1), pltpu.roll(xs, HEAD_DIM // 2, 1))
    return xs * cos + rot * sin_signed


def _rope_rows(xt, cos_t, sin_t):
    half = HEAD_DIM // 2
    rot = jnp.concatenate([xt[half:2 * half], xt[0:half], xt[3 * half:4 * half], xt[2 * half:3 * half]], axis=0)
    return xt * cos_t + rot * sin_t


def _pick_tile(n, cap, mult=8):
    t = min(n, cap)
    while n % t or t % mult:
        t -= 1
    return t


def _rms_bf16(x, g):
    ms = jnp.mean(x * x, axis=-1, keepdims=True)
    return (x * lax.rsqrt(ms + EPS) * g).astype(BF16)


def _inproj_common(h, w_ref, cos, sin, u_ref, q_ref, gn_ref, gm_ref):
    def proj(a, n):
        return _dot(h, w_ref[:, a:a + n])

    u_ref[...] = proj(O_GLU, D_CONV) * _sigmoid(proj(O_GLU + D_CONV, D_CONV))
    for s in range(N_HEADS):
        q_ref[:, s * LANES:(s + 1) * LANES] = _rope_slab(proj(O_Q + s * LANES, LANES), cos, sin).astype(BF16)
    gn_ref[...] = _sigmoid(proj(O_GN, LANES))
    gm_ref[...] = _sigmoid(proj(O_GM, 2 * D_MODEL)).astype(BF16)
    return proj


def _inproj_rows_kernel(x_ref, g_ref, w_ref, cos_ref, sin_ref,
                        u_ref, q_ref, ckv_ref, skv_ref, wkv_ref, gn_ref, gm_ref):
    h = _rms_bf16(x_ref[...], g_ref[...])
    cos = cos_ref[...]
    sin = sin_ref[...]
    proj = _inproj_common(h, w_ref, cos, sin, u_ref, q_ref, gn_ref, gm_ref)
    ckv_ref[...] = proj(O_CKV, KV_COLS)
    skv_ref[:, 0:LANES] = _rope_slab(proj(O_SKV, LANES), cos, sin)
    skv_ref[:, LANES:KV_COLS] = proj(O_SKV + LANES, LANES)
    wkv_ref[:, 0:LANES] = _rope_slab(proj(O_WKV, LANES), cos, sin)
    wkv_ref[:, LANES:KV_COLS] = proj(O_WKV + LANES, LANES)


GATE_ROWS = 32
R_Q = 0
R_CKV = R_Q + N_HEADS * HEAD_DIM
R_SKV = R_CKV + KV_COLS
R_WKV = R_SKV + KV_COLS
R_GN = R_WKV + KV_COLS
ROWS_T = R_GN + GATE_ROWS


def _inproj_cols_kernel(x_ref, g_ref, w_ref, wt_ref, cos_ref, sin_ref, cost_ref, sint_ref,
                        u_ref, q_ref, ckv_ref, skv_ref, wkv_ref, skr_ref, wkr_ref, gn_ref, gm_ref):
    h = _rms_bf16(x_ref[...], g_ref[...])
    cos = cos_ref[...]
    sin = sin_ref[...]
    cos_t = cost_ref[...]
    sin_t = sint_ref[...]

    def proj(a, n):
        return _dot(h, w_ref[:, a:a + n])

    t_all = _dot_nt(wt_ref[...], h)

    def proj_t(a, n):
        return t_all[a:a + n]

    u_ref[...] = proj(O_GLU, D_CONV) * _sigmoid(proj(O_GLU + D_CONV, D_CONV))
    gm_ref[...] = _sigmoid(proj(O_GM, 2 * D_MODEL)).astype(BF16)
    skr_ref[...] = _rope_slab(proj(O_SKV, LANES), cos, sin).astype(BF16)
    wkr_ref[...] = _rope_slab(proj(O_WKV, LANES), cos, sin).astype(BF16)
    for s in range(N_HEADS * HEAD_DIM // LANES):
        q_ref[0, s * LANES:(s + 1) * LANES, :] = _rope_rows(proj_t(R_Q + s * LANES, LANES), cos_t, sin_t).astype(BF16)
    ckv_ref[0] = proj_t(R_CKV, KV_COLS)
    skv_ref[0, 0:LANES] = _rope_rows(proj_t(R_SKV, LANES), cos_t, sin_t)
    skv_ref[0, LANES:KV_COLS] = proj_t(R_SKV + LANES, LANES)
    wkv_ref[0, 0:LANES] = _rope_rows(proj_t(R_WKV, LANES), cos_t, sin_t)
    wkv_ref[0, LANES:KV_COLS] = proj_t(R_WKV + LANES, LANES)
    gn_ref[0] = _sigmoid(proj_t(R_GN, GATE_ROWS))


def _inproj_rows(x2, g, w_packed, cos_t, sin_t, tm):
    m = x2.shape[0]
    row = lambda i: (i, 0)
    const = lambda i: (0, 0)
    out_shapes = (
        jax.ShapeDtypeStruct((m, D_CONV), F32),
        jax.ShapeDtypeStruct((m, Q_EXP), BF16),
        jax.ShapeDtypeStruct((m, KV_COLS), F32),
        jax.ShapeDtypeStruct((m, KV_COLS), F32),
        jax.ShapeDtypeStruct((m, KV_COLS), F32),
        jax.ShapeDtypeStruct((m, LANES), F32),
        jax.ShapeDtypeStruct((m, 2 * D_MODEL), BF16),
    )
    return pl.pallas_call(
        _inproj_rows_kernel,
        grid=(m // tm,),
        in_specs=[
            pl.BlockSpec((tm, D_MODEL), row),
            pl.BlockSpec((1, D_MODEL), const),
            pl.BlockSpec(w_packed.shape, const),
            pl.BlockSpec((tm, LANES), const),
            pl.BlockSpec((tm, LANES), const),
        ],
        out_specs=tuple(pl.BlockSpec((tm, s.shape[1]), row) for s in out_shapes),
        out_shape=out_shapes,
        compiler_params=_params(1),
    )(x2, g, w_packed, cos_t, sin_t)


def _inproj_cols(x2, g, w_packed, wkv_t, cos_t, sin_t, cos_tt, sin_tt, b, t_len, tm):
    m = x2.shape[0]
    n_tab = t_len // tm
    row = lambda i: (i, 0)
    const = lambda i: (0, 0)
    kv_map = lambda i: (i // n_tab, 0, i % n_tab)
    kv_shape = jax.ShapeDtypeStruct((b, KV_COLS, t_len), F32)
    out_shapes = (
        jax.ShapeDtypeStruct((m, D_CONV), F32),
        jax.ShapeDtypeStruct((b, N_HEADS * HEAD_DIM, t_len), BF16),
        kv_shape, kv_shape, kv_shape,
        jax.ShapeDtypeStruct((m, LANES), BF16),
        jax.ShapeDtypeStruct((m, LANES), BF16),
        jax.ShapeDtypeStruct((b, GATE_ROWS, t_len), F32),
        jax.ShapeDtypeStruct((m, 2 * D_MODEL), BF16),
    )
    out_specs = (
        pl.BlockSpec((tm, D_CONV), row),
        pl.BlockSpec((1, N_HEADS * HEAD_DIM, tm), kv_map),
        pl.BlockSpec((1, KV_COLS, tm), kv_map),
        pl.BlockSpec((1, KV_COLS, tm), kv_map),
        pl.BlockSpec((1, KV_COLS, tm), kv_map),
        pl.BlockSpec((tm, LANES), row),
        pl.BlockSpec((tm, LANES), row),
        pl.BlockSpec((1, GATE_ROWS, tm), kv_map),
        pl.BlockSpec((tm, 2 * D_MODEL), row),
    )
    return pl.pallas_call(
        _inproj_cols_kernel,
        grid=(m // tm,),
        in_specs=[
            pl.BlockSpec((tm, D_MODEL), row),
            pl.BlockSpec((1, D_MODEL), const),
            pl.BlockSpec(w_packed.shape, const),
            pl.BlockSpec(wkv_t.shape, const),
            pl.BlockSpec((tm, LANES), lambda i: (i % n_tab, 0)),
            pl.BlockSpec((tm, LANES), lambda i: (i % n_tab, 0)),
            pl.BlockSpec((LANES, tm), lambda i: (0, i % n_tab)),
            pl.BlockSpec((LANES, tm), lambda i: (0, i % n_tab)),
        ],
        out_specs=out_specs,
        out_shape=out_shapes,
        compiler_params=_params(1),
    )(x2, g, w_packed, wkv_t, cos_t, sin_t, cos_tt, sin_tt)


def _ln_swish(c, lg, lb):
    mu = jnp.mean(c, axis=-1, keepdims=True)
    d = c - mu
    var = jnp.mean(d * d, axis=-1, keepdims=True)
    y = d * lax.rsqrt(var + EPS) * lg + lb
    return y * _sigmoid(y)


CONV_PAD = 32
CONV_CHUNK = 64


def _conv_prompt_kernel(u_ref, past_ref, w_ref, b_ref, lg_ref, lb_ref, o_ref, uf_ref, *, t_len):
    off = CONV_PAD - (CONV_K - 1)
    uf_ref[0:SUBLANES, :] = jnp.zeros((SUBLANES, D_CONV), F32)
    uf_ref[off:CONV_PAD, :] = past_ref[0]
    uf_ref[CONV_PAD:CONV_PAD + t_len, :] = u_ref[0]
    bias = b_ref[...]
    lg = lg_ref[...]
    lb = lb_ref[...]
    ct = CONV_CHUNK

    def body(i, carry):
        base = pl.multiple_of(i * ct, ct)
        xw = uf_ref[pl.ds(base, ct + CONV_PAD), :]
        acc = jnp.zeros((ct, D_CONV), F32)
        n_win = ct + CONV_PAD
        for r in range(SUBLANES):
            yr = xw if r == 0 else pltpu.roll(xw, n_win - r, 0)
            for a in range((CONV_PAD // SUBLANES) + 1):
                k = SUBLANES * a + r - off
                if 0 <= k < CONV_K:
                    acc = acc + w_ref[k:k + 1, :] * yr[SUBLANES * a:SUBLANES * a + ct, :]
        o_ref[0, pl.ds(base, ct), :] = _ln_swish(acc + bias, lg, lb).astype(BF16)
        return carry

    lax.fori_loop(0, t_len // ct, body, 0)


def _conv_prompt(u3, past3, dw_w, dw_b, ln_g, ln_b):
    b, t_len, _ = u3.shape
    const2 = lambda i: (0, 0)
    return pl.pallas_call(
        functools.partial(_conv_prompt_kernel, t_len=t_len),
        grid=(b,),
        in_specs=[
            pl.BlockSpec((1, t_len, D_CONV), lambda i: (i, 0, 0)),
            pl.BlockSpec((1, CONV_K - 1, D_CONV), lambda i: (i, 0, 0)),
            pl.BlockSpec((CONV_K, D_CONV), const2),
            pl.BlockSpec((1, D_CONV), const2),
            pl.BlockSpec((1, D_CONV), const2),
            pl.BlockSpec((1, D_CONV), const2),
        ],
        out_specs=pl.BlockSpec((1, t_len, D_CONV), lambda i: (i, 0, 0)),
        out_shape=jax.ShapeDtypeStruct((b, t_len, D_CONV), BF16),
        scratch_shapes=[pltpu.VMEM((t_len + CONV_PAD, D_CONV), F32)],
        compiler_params=_params(1),
    )(u3, past3, dw_w, dw_b, ln_g, ln_b)


def _conv_sample_kernel(past_ref, u_ref, w_ref, b_ref, lg_ref, lb_ref, o_ref, *, tq):
    bias = b_ref[...]
    lg = lg_ref[...]
    lb = lb_ref[...]
    n_past = CONV_K - 1
    for t in range(tq):
        acc = jnp.zeros(o_ref.shape[1:], F32)
        for j in range(t, n_past):
            acc = acc + w_ref[j - t:j - t + 1, :] * past_ref[j]
        for i in range(t + 1):
            k = n_past - t + i
            acc = acc + w_ref[k:k + 1, :] * u_ref[i]
        o_ref[t] = _ln_swish(acc + bias, lg, lb).astype(BF16)


def _conv_sample(past_t, layer, u_t, dw_w, dw_b, ln_g, ln_b):
    n_past, bd = CONV_K - 1, past_t.shape[1]
    tq = u_t.shape[0]
    bt = _pick_tile(bd, 32)
    const2 = lambda i: (0, 0)
    return pl.pallas_call(
        functools.partial(_conv_sample_kernel, tq=tq),
        grid=(bd // bt,),
        in_specs=[
            pl.BlockSpec((n_past, bt, D_CONV), lambda i: (layer, i, 0)),
            pl.BlockSpec((tq, bt, D_CONV), lambda i: (0, i, 0)),
            pl.BlockSpec((CONV_K, D_CONV), const2),
            pl.BlockSpec((1, D_CONV), const2),
            pl.BlockSpec((1, D_CONV), const2),
            pl.BlockSpec((1, D_CONV), const2),
        ],
        out_specs=pl.BlockSpec((tq, bt, D_CONV), lambda i: (0, i, 0)),
        out_shape=jax.ShapeDtypeStruct((tq, bd, D_CONV), BF16),
        compiler_params=_params(1),
    )(past_t, u_t, dw_w, dw_b, ln_g, ln_b)


def _chunkproj_pages(get_page, n_pages, wk_ref, wv_ref, pk_ref, pv_ref, ab_ref, t_ref):
    n_rows = n_pages * CHUNKS_PER_PAGE
    for kv, (w_ref, p_ref) in enumerate(((wk_ref, pk_ref), (wv_ref, pv_ref))):
        for p in range(n_pages):
            t_ref[p * PAGE_SIZE:(p + 1) * PAGE_SIZE, :] = get_page(p, kv).T
        xs = jnp.concatenate([t_ref[pl.ds(l, n_rows, stride=CMP_STRIDE), :] for l in range(CMP_STRIDE)],
                             axis=1).astype(BF16)
        w = w_ref[...]
        part = _dot(xs, w)
        posb = _dot(p_ref[...], w)
        c0 = kv * 2 * LANES
        ab_ref[:, c0:c0 + LANES] = part[:, 0:LANES] + posb[0:1, 0:LANES]
        ab_ref[:, c0 + LANES:c0 + 2 * LANES] = part[:, LANES:2 * LANES] + posb[1:2, LANES:2 * LANES]


def _chunkproj_rows_kernel(x_ref, wk_ref, wv_ref, pk_ref, pv_ref, ab_ref, t_ref, *, n_pages):
    get_page = lambda p, kv: x_ref[0, kv, :, p * PAGE_SIZE:(p + 1) * PAGE_SIZE]
    _chunkproj_pages(get_page, n_pages, wk_ref, wv_ref, pk_ref, pv_ref, ab_ref, t_ref)


def _chunkproj_gather_kernel(pt_ref, pool_hbm, wk_ref, wv_ref, pk_ref, pv_ref, ab_ref, x_buf, t_ref, sem,
                             *, n_pages, page_base):
    step = pl.program_id(0)
    n_steps = pl.num_programs(0)
    slot = step % 2

    def page_copies(for_step, to_slot):
        return [pltpu.make_async_copy(pool_hbm.at[page_base + pt_ref[for_step * n_pages + j]], x_buf.at[to_slot, j],
                                      sem.at[to_slot, j]) for j in range(n_pages)]

    @pl.when(step == 0)
    def _():
        for c in page_copies(0, 0):
            c.start()

    @pl.when(step + 1 < n_steps)
    def _():
        for c in page_copies(step + 1, 1 - slot):
            c.start()

    for c in page_copies(step, slot):
        c.wait()
    _chunkproj_pages(lambda p, kv: x_buf[slot, p, kv], n_pages, wk_ref, wv_ref, pk_ref, pv_ref, ab_ref, t_ref)


def _chunkproj_rows(x4, wk2, wv2, pk2, pv2):
    n_pages = x4.shape[3] // PAGE_SIZE
    const2 = lambda i: (0, 0)
    rows = n_pages * CHUNKS_PER_PAGE
    return pl.pallas_call(
        functools.partial(_chunkproj_rows_kernel, n_pages=n_pages),
        grid=(x4.shape[0],),
        in_specs=[
            pl.BlockSpec((1, 2, LANES, x4.shape[3]), lambda i: (i, 0, 0, 0)),
            pl.BlockSpec(wk2.shape, const2),
            pl.BlockSpec(wv2.shape, const2),
            pl.BlockSpec(pk2.shape, const2),
            pl.BlockSpec(pv2.shape, const2),
        ],
        out_specs=pl.BlockSpec((rows, 4 * LANES), lambda i: (i, 0)),
        out_shape=jax.ShapeDtypeStruct((x4.shape[0] * rows, 4 * LANES), F32),
        scratch_shapes=[pltpu.VMEM((n_pages * PAGE_SIZE, LANES), F32)],
        compiler_params=_params(1),
    )(x4, wk2, wv2, pk2, pv2)


GATHER_PAGES = 64


def _chunkproj_gather(pt_flat, pool4, wk2, wv2, pk2, pv2, page_base):
    n_total = pt_flat.shape[0]
    n_pages = _pick_tile(n_total, GATHER_PAGES, 1)
    const2 = lambda i, pt: (0, 0)
    rows = n_pages * CHUNKS_PER_PAGE
    grid_spec = pltpu.PrefetchScalarGridSpec(
        num_scalar_prefetch=1,
        grid=(n_total // n_pages,),
        in_specs=[
            pl.BlockSpec(memory_space=pl.ANY),
            pl.BlockSpec(wk2.shape, const2),
            pl.BlockSpec(wv2.shape, const2),
            pl.BlockSpec(pk2.shape, const2),
            pl.BlockSpec(pv2.shape, const2),
        ],
        out_specs=pl.BlockSpec((rows, 4 * LANES), lambda i, pt: (i, 0)),
        scratch_shapes=[
            pltpu.VMEM((2, n_pages, 2, LANES, PAGE_SIZE), F32),
            pltpu.VMEM((n_pages * PAGE_SIZE, LANES), F32),
            pltpu.SemaphoreType.DMA((2, n_pages)),
        ],
    )
    return pl.pallas_call(
        functools.partial(_chunkproj_gather_kernel, n_pages=n_pages, page_base=page_base),
        grid_spec=grid_spec,
        out_shape=jax.ShapeDtypeStruct((n_total * CHUNKS_PER_PAGE, 4 * LANES), F32),
        compiler_params=_params(1),
    )(pt_flat, pool4, wk2, wv2, pk2, pv2)


def _compressed_kv_f32(ab, cosc, sinc):
    n = ab.shape[0]
    kc = ab[:, 0:LANES] + pltpu.roll(ab[:, LANES:2 * LANES], n - 1, 0)
    vc = ab[:, 2 * LANES:3 * LANES] + pltpu.roll(ab[:, 3 * LANES:4 * LANES], n - 1, 0)
    return _rope_slab(kc, cosc, sinc), vc


def _compressed_kv(ab, cosc, sinc):
    kc, vc = _compressed_kv_f32(ab, cosc, sinc)
    return kc.astype(BF16), vc.astype(BF16)


def _softmax_parts(s, mask):
    sm = jnp.where(mask, s, NEG)
    m = jnp.max(sm, axis=-1, keepdims=True)
    p = jnp.where(mask, jnp.exp(sm - m), 0.0)
    l = jnp.maximum(jnp.sum(p, axis=-1, keepdims=True), 1e-30)
    return p, l


def _select_blocks(imp, qpos, n_sel, axis):
    jidx = lax.broadcasted_iota(jnp.int32, imp.shape, axis)
    cur = lax.shift_right_logical(qpos, 6)
    forced = (jidx == 0) | (jidx == cur) | (jidx == cur - 1)
    valid = (jidx * SEL_BLOCK <= qpos) & (jidx < n_sel)
    v = jnp.where(valid, jnp.where(forced, jnp.inf, imp), -jnp.inf)
    rank = jnp.zeros(imp.shape, jnp.int32)
    for k in range(n_sel):
        vk = v[k:k + 1, :] if axis == 0 else v[:, k:k + 1]
        ahead = (vk > v) | ((vk == v) & (jidx > k))
        rank = rank + ahead.astype(jnp.int32)
    return (rank < min(N_SEL, n_sel)) & valid


def _split_bf16(x):
    hi = x.astype(BF16)
    r1 = x - hi.astype(F32)
    mid = r1.astype(BF16)
    lo = (r1 - mid.astype(F32)).astype(BF16)
    return hi, mid, lo


def _overlap(n_cmp, cmp_axis):
    i = lax.broadcasted_iota(jnp.int32, (LANES, LANES), cmp_axis)
    j = lax.broadcasted_iota(jnp.int32, (LANES, LANES), 1 - cmp_axis)
    hit = (i * CMP_STRIDE < (j + 1) * SEL_BLOCK) & (i * CMP_STRIDE + CMP_BLOCK > j * SEL_BLOCK) & (i < n_cmp)
    return hit.astype(F32)


Q_TILE = 256
K_TILE = 256


def _attn_prompt_kernel(q_ref, gn_ref, skr_ref, svt_ref, wkr_ref, wvt_ref, ab_ref, cosc_ref, sinc_ref, o_ref,
                        kc_ref, vct_ref, bias_ref, m_ref, acc_ref, accb_ref, out_ref, *, t_len):
    qb = pl.program_id(1)
    n_chunk = t_len // CMP_STRIDE
    n_cmp = (t_len - CMP_BLOCK) // CMP_STRIDE + 1
    n_sel = -(-t_len // SEL_BLOCK)
    sel_rows = -(-n_sel // SUBLANES) * SUBLANES
    qt, kt_sz = Q_TILE, K_TILE
    q_tiles = qt // LANES
    g_lanes = HPG * qt
    n_lane_tiles = N_HEADS * q_tiles

    @pl.when(qb == 0)
    def _():
        kc, vc = _compressed_kv_f32(ab_ref[...], cosc_ref[...], sinc_ref[...])
        kc_ref[...] = kc.astype(BF16)
        vct_ref[...] = vc.T.astype(BF16)

    q0 = qb * qt
    qpos = q0 + lax.broadcasted_iota(jnp.int32, (1, qt), 1)

    def lane_tile(c):
        return slice(c * LANES, (c + 1) * LANES)

    def q_part(c):
        return slice((c % q_tiles) * LANES, (c % q_tiles + 1) * LANES)

    def group_of(c):
        return c // (HPG * q_tiles)

    def group_lanes(g):
        return slice(g * g_lanes, (g + 1) * g_lanes)

    q_zero = jnp.zeros((HEAD_DIM, qt), BF16)

    def q_slab(hh):
        q_h = q_ref[0, hh * HEAD_DIM:(hh + 1) * HEAD_DIM, :]
        return jnp.concatenate([q_h, q_zero] if hh < HPG else [q_zero, q_h], axis=0)

    q_all = jnp.concatenate([q_slab(hh) for hh in range(N_HEADS)], axis=1)

    def gates(branch):
        r0 = branch * N_HEADS
        return jnp.concatenate([gn_ref[0, r0 + hh:r0 + hh + 1, :] for hh in range(N_HEADS)], axis=1)

    nrow = lax.broadcasted_iota(jnp.int32, (n_chunk, qt), 0)
    mask_c = (nrow * CMP_STRIDE + CMP_BLOCK - 1 <= qpos) & (nrow < n_cmp)
    acc_ref[...] = _dot(kc_ref[...], q_all)
    hs = [[jnp.zeros((n_chunk, LANES), F32) for _ in range(q_tiles)] for _ in range(N_KV)]
    p_parts = []
    for c in range(n_lane_tiles):
        mask = mask_c[:, q_part(c)]
        sm = jnp.where(mask, acc_ref[:, lane_tile(c)], NEG)
        p_c = jnp.where(mask, jnp.exp(sm - jnp.max(sm, axis=0, keepdims=True)), 0.0)
        p_c = p_c / jnp.maximum(jnp.sum(p_c, axis=0, keepdims=True), 1e-30)
        hs[group_of(c)][c % q_tiles] = hs[group_of(c)][c % q_tiles] + p_c
        p_parts.append(p_c.astype(BF16))
    out_ref[...] = gates(0) * _dot(vct_ref[...], jnp.concatenate(p_parts, axis=1))

    sel_t = []
    for g in range(N_KV):
        ov_t = _overlap(n_cmp, 1).astype(BF16)
        imp_t = sum(_dot(ov_t, part) for part in _split_bf16(jnp.concatenate(hs[g], axis=1)))
        s_g = _select_blocks(imp_t[0:sel_rows], qpos, n_sel, 0).astype(F32)
        sel_t.append(jnp.concatenate([s_g, jnp.zeros((LANES - sel_rows, qt), F32)], axis=0).astype(BF16))

    m_ref[...] = jnp.full(m_ref.shape, NEG, F32)
    accb_ref[...] = jnp.zeros(accb_ref.shape, F32)

    def tile_step(br, kr_ref, vt_ref, k_idx, bias_fn):
        k0 = pl.multiple_of(k_idx * kt_sz, kt_sz)
        kpos = k0 + lax.broadcasted_iota(jnp.int32, (kt_sz, qt), 0)
        bias_fn(k_idx, kpos)
        s = _dot(kr_ref[pl.ds(k0, kt_sz), :], q_all)
        vt = vt_ref[0, :, pl.ds(k0, kt_sz)]
        vrow = lax.broadcasted_iota(jnp.int32, (LANES, kt_sz), 0)
        p_parts, a_parts = [], []
        for c in range(n_lane_tiles):
            sb = s[:, lane_tile(c)] + bias_ref[br, group_of(c), :, q_part(c)]
            m_old = m_ref[br, 0:1, lane_tile(c)]
            m_new = jnp.maximum(m_old, jnp.max(sb, axis=0, keepdims=True))
            p_parts.append(jnp.exp(sb - m_new).astype(BF16))
            a_parts.append(jnp.exp(m_old - m_new))
            m_ref[br, :, lane_tile(c)] = jnp.broadcast_to(m_new, (SUBLANES, LANES))
        for g in range(N_KV):
            v_own = (vrow >= g * HEAD_DIM) & (vrow < (g + 1) * HEAD_DIM)
            vaug = jnp.where(v_own, vt, 1.0).astype(BF16)
            tiles = range(g * HPG * q_tiles, (g + 1) * HPG * q_tiles)
            alpha = jnp.concatenate([a_parts[c] for c in tiles], axis=1)
            p = jnp.concatenate([p_parts[c] for c in tiles], axis=1)
            accb_ref[br, :, group_lanes(g)] = alpha * accb_ref[br, :, group_lanes(g)] + _dot(vaug, p)

    def sel_bias(k_idx, kpos):
        krow = lax.broadcasted_iota(jnp.int32, (kt_sz, LANES), 0)
        jcol = lax.broadcasted_iota(jnp.int32, (kt_sz, LANES), 1)
        expand_t = (jcol == k_idx * (kt_sz // SEL_BLOCK) + lax.shift_right_logical(krow, 6)).astype(BF16)
        for g in range(N_KV):
            bias_ref[0, g] = jnp.where((_dot(expand_t, sel_t[g]) > 0.5) & (kpos <= qpos), 0.0, NEG)

    def win_bias(k_idx, kpos):
        b = jnp.where((kpos <= qpos) & (kpos > qpos - WINDOW), 0.0, NEG)
        for g in range(N_KV):
            bias_ref[1, g] = b

    n_sel_tiles = qb + 1
    n_win_tiles = jnp.minimum(qb, WINDOW // kt_sz) + 1

    def both(i, carry):
        tile_step(0, skr_ref, svt_ref, i, sel_bias)
        tile_step(1, wkr_ref, wvt_ref, qb - i, win_bias)
        return carry

    def sel_only(i, carry):
        tile_step(0, skr_ref, svt_ref, i, sel_bias)
        return carry

    lax.fori_loop(0, n_win_tiles, both, 0)
    lax.fori_loop(n_win_tiles, n_sel_tiles, sel_only, 0)

    for br in range(2):
        gt = gates(br + 1)
        for g in range(N_KV):
            sum_row = (1 - g) * HEAD_DIM
            acc = accb_ref[br, :, group_lanes(g)]
            out_ref[:, group_lanes(g)] = (out_ref[:, group_lanes(g)]
                                          + gt[:, group_lanes(g)] * (acc / acc[sum_row:sum_row + 1, :]))

    lane = lax.broadcasted_iota(jnp.int32, (qt, LANES), 1)
    for pair in range(N_HEADS // 2):
        a = out_ref[:, (2 * pair) * qt:(2 * pair + 1) * qt].T
        b = out_ref[:, (2 * pair + 1) * qt:(2 * pair + 2) * qt].T
        if 2 * pair < HPG:
            slab = jnp.where(lane < HEAD_DIM, a, pltpu.roll(b, HEAD_DIM, 1))
        else:
            slab = jnp.where(lane < HEAD_DIM, pltpu.roll(a, HEAD_DIM, 1), b)
        o_ref[:, pair * LANES:(pair + 1) * LANES] = slab.astype(BF16)


def _attn_prompt(q_t, gn_t, skr, skv_t, wkr, wkv_t, ab2, cosc, sinc, b, t_len):
    n_chunk = t_len // CMP_STRIDE
    nqb = t_len // Q_TILE
    const2 = lambda i, j: (0, 0)
    per_b2 = lambda i, j: (i, 0)
    v_rows = lambda i, j: (i, 1, 0)
    return pl.pallas_call(
        functools.partial(_attn_prompt_kernel, t_len=t_len),
        grid=(b, nqb),
        in_specs=[
            pl.BlockSpec((1, N_HEADS * HEAD_DIM, Q_TILE), lambda i, j: (i, 0, j)),
            pl.BlockSpec((1, GATE_ROWS, Q_TILE), lambda i, j: (i, 0, j)),
            pl.BlockSpec((t_len, LANES), per_b2),
            pl.BlockSpec((1, LANES, t_len), v_rows),
            pl.BlockSpec((t_len, LANES), per_b2),
            pl.BlockSpec((1, LANES, t_len), v_rows),
            pl.BlockSpec((n_chunk, 4 * LANES), per_b2),
            pl.BlockSpec((n_chunk, LANES), const2),
            pl.BlockSpec((n_chunk, LANES), const2),
        ],
        out_specs=pl.BlockSpec((Q_TILE, N_HEADS * HEAD_DIM), lambda i, j: (i * nqb + j, 0)),
        out_shape=jax.ShapeDtypeStruct((b * t_len, N_HEADS * HEAD_DIM), BF16),
        scratch_shapes=[
            pltpu.VMEM((n_chunk, LANES), BF16),
            pltpu.VMEM((LANES, n_chunk), BF16),
            pltpu.VMEM((2, N_KV, K_TILE, Q_TILE), F32),
            pltpu.VMEM((2, SUBLANES, N_HEADS * Q_TILE), F32),
            pltpu.VMEM((LANES, N_HEADS * Q_TILE), F32),
            pltpu.VMEM((2, LANES, N_HEADS * Q_TILE), F32),
            pltpu.VMEM((LANES, N_HEADS * Q_TILE), F32),
        ],
        compiler_params=_params(2),
    )(q_t, gn_t, skr, skv_t, wkr, wkv_t, ab2, cosc, sinc)


Q_ROWS = 8
SEQ_PER_STEP = 4


def _attn_sample_kernel(pt_ref, q_ref, gn_ref, sknew_ref, wknew_ref, win_ref, cosc_ref, sinc_ref, ab_ref, sel_hbm,
                        o_ref, sel_buf, sem, *, past_len, tq, n_pages, n_seq, page_base):
    step = pl.program_id(0)
    n_steps = pl.num_programs(0)
    slot = step % 2
    n_slot_pages = n_seq * n_pages

    def page_copies(for_step, to_slot):
        return [pltpu.make_async_copy(sel_hbm.at[page_base + pt_ref[for_step * n_slot_pages + j]],
                                      sel_buf.at[to_slot, j], sem.at[to_slot, j]) for j in range(n_slot_pages)]

    @pl.when(step == 0)
    def _():
        for c in page_copies(0, 0):
            c.start()

    @pl.when(step + 1 < n_steps)
    def _():
        for c in page_copies(step + 1, 1 - slot):
            c.start()

    for c in page_copies(step, slot):
        c.wait()

    def sel_page(e, p, kv):
        return sel_buf[slot, e * n_pages + p, kv]

    seqs = range(n_seq)
    t_all = past_len + tq
    n_chunk = past_len // CMP_STRIDE
    n_cmp = (t_all - CMP_BLOCK) // CMP_STRIDE + 1
    n_sel = -(-t_all // SEL_BLOCK)
    rows = N_HEADS * Q_ROWS
    all_rows = n_seq * rows
    win_buf = win_ref.shape[3]

    def stack(parts):
        return jnp.concatenate(parts, axis=0)

    def per_seq(a, e):
        return a[e * rows:(e + 1) * rows]

    qbd = [stack([q_ref[e, :, hh * LANES:(hh + 1) * LANES] for hh in range(N_HEADS)]) for e in seqs]
    qpos = past_len + lax.broadcasted_iota(jnp.int32, (all_rows, 1), 0) % Q_ROWS

    cosc = cosc_ref[...]
    sinc = sinc_ref[...]
    kvc = [_compressed_kv(ab_ref[e], cosc, sinc) for e in seqs]
    ncol = lax.broadcasted_iota(jnp.int32, (all_rows, n_chunk), 1)
    mask_c = (ncol * CMP_STRIDE + CMP_BLOCK - 1 <= qpos) & (ncol < n_cmp)
    p_c, l_c = _softmax_parts(stack([_dot_nt(qbd[e], kvc[e][0]) for e in seqs]), mask_c)
    p_c = p_c / l_c
    p_c16 = p_c.astype(BF16)
    o_c = stack([_dot(per_seq(p_c16, e), kvc[e][1]) for e in seqs])

    hs = []
    for e in seqs:
        for g in range(N_KV):
            r0 = e * rows + g * HPG * Q_ROWS
            acc = p_c[r0:r0 + Q_ROWS]
            for h in range(1, HPG):
                acc = acc + p_c[r0 + h * Q_ROWS:r0 + (h + 1) * Q_ROWS]
            hs.append(acc)
    ov = _overlap(n_cmp, 0).astype(BF16)
    imp = sum(_dot(part, ov) for part in _split_bf16(stack(hs)))
    sel_g = _select_blocks(imp, qpos[0:n_seq * N_KV * Q_ROWS], n_sel, 1).astype(BF16)
    sel_rows = stack([sel_g[(e * N_KV + g) * Q_ROWS:(e * N_KV + g + 1) * Q_ROWS]
                      for e in seqs for g in range(N_KV) for _ in range(HPG)])

    def new_rows_tile(ref, e, c0):
        return stack([ref[e, :, c0:c0 + LANES], jnp.zeros((LANES - Q_ROWS, LANES), F32)]).astype(BF16)

    n_keys = (n_pages + 1) * PAGE_SIZE
    s_s = stack([jnp.concatenate([_dot(qbd[e], sel_page(e, p, 0).astype(BF16)) for p in range(n_pages)]
                                 + [_dot_nt(qbd[e], new_rows_tile(sknew_ref, e, 0))], axis=1) for e in seqs])
    jrow = lax.broadcasted_iota(jnp.int32, (LANES, n_keys), 0)
    kcol = lax.broadcasted_iota(jnp.int32, (LANES, n_keys), 1)
    expand = (jrow == lax.shift_right_logical(kcol, 6)).astype(BF16)
    kpos = lax.broadcasted_iota(jnp.int32, (all_rows, n_keys), 1)
    mask_s = (_dot(sel_rows, expand) > 0.5) & (kpos <= qpos)
    p_s, l_s = _softmax_parts(s_s, mask_s)
    p_s = p_s.astype(BF16)
    o_s = []
    for e in seqs:
        p_e = per_seq(p_s, e)
        acc = _dot(p_e[:, n_pages * PAGE_SIZE:], new_rows_tile(sknew_ref, e, LANES))
        for i in range(n_pages):
            acc = acc + _dot_nt(p_e[:, i * PAGE_SIZE:(i + 1) * PAGE_SIZE], sel_page(e, i, 1).astype(BF16))
        o_s.append(acc)
    o_s = stack(o_s) / l_s

    s_w = stack([jnp.concatenate([_dot(qbd[e], win_ref[e, 0].astype(BF16)),
                                  _dot_nt(qbd[e], new_rows_tile(wknew_ref, e, 0))], axis=1) for e in seqs])
    wcol = lax.broadcasted_iota(jnp.int32, (all_rows, win_buf + LANES), 1)
    kpos_w = past_len - win_buf + wcol
    mask_w = (kpos_w <= qpos) & (kpos_w > qpos - WINDOW)
    p_w, l_w = _softmax_parts(s_w, mask_w)
    p_w = p_w.astype(BF16)
    o_w = stack([_dot_nt(per_seq(p_w, e)[:, 0:win_buf], win_ref[e, 1].astype(BF16))
                 + _dot(per_seq(p_w, e)[:, win_buf:], new_rows_tile(wknew_ref, e, LANES)) for e in seqs]) / l_w

    for e in seqs:
        gn = gn_ref[e]
        for hh in range(N_HEADS):
            r = slice(e * rows + hh * Q_ROWS, e * rows + (hh + 1) * Q_ROWS)
            out = (gn[:, hh:hh + 1] * o_c[r]
                   + gn[:, N_HEADS + hh:N_HEADS + hh + 1] * o_s[r]
                   + gn[:, 2 * N_HEADS + hh:2 * N_HEADS + hh + 1] * o_w[r])
            o_ref[e, :, hh * LANES:(hh + 1) * LANES] = out.astype(BF16)


def _attn_sample(pt_flat, q3, gn3, sknew3, wknew3, win4, cosc, sinc, sel_pool4, ab_seq3,
                 layer, n_pool, bd, past_len, tq):
    n_pages = past_len // PAGE_SIZE
    win_buf = win4.shape[3]
    n_chunk = past_len // CMP_STRIDE
    n_seq = _pick_tile(bd, SEQ_PER_STEP, 1)
    per_b = lambda i, pt: (i, 0, 0)
    const2 = lambda i, pt: (0, 0)
    n_slot_pages = n_seq * n_pages
    grid_spec = pltpu.PrefetchScalarGridSpec(
        num_scalar_prefetch=1,
        grid=(bd // n_seq,),
        in_specs=[
            pl.BlockSpec((n_seq, Q_ROWS, Q_EXP), per_b),
            pl.BlockSpec((n_seq, Q_ROWS, LANES), per_b),
            pl.BlockSpec((n_seq, Q_ROWS, KV_COLS), per_b),
            pl.BlockSpec((n_seq, Q_ROWS, KV_COLS), per_b),
            pl.BlockSpec((n_seq, 2, LANES, win_buf), lambda i, pt: (layer * (bd // n_seq) + i, 0, 0, 0)),
            pl.BlockSpec((n_chunk, LANES), const2),
            pl.BlockSpec((n_chunk, LANES), const2),
            pl.BlockSpec((n_seq, n_chunk, 4 * LANES), per_b),
            pl.BlockSpec(memory_space=pl.ANY),
        ],
        out_specs=pl.BlockSpec((n_seq, Q_ROWS, Q_EXP), per_b),
        scratch_shapes=[
            pltpu.VMEM((2, n_slot_pages, 2, LANES, PAGE_SIZE), F32),
            pltpu.SemaphoreType.DMA((2, n_slot_pages)),
        ],
    )
    return pl.pallas_call(
        functools.partial(_attn_sample_kernel, past_len=past_len, tq=tq, n_pages=n_pages, n_seq=n_seq,
                          page_base=layer * n_pool),
        grid_spec=grid_spec,
        out_shape=jax.ShapeDtypeStruct((bd, Q_ROWS, Q_EXP), BF16),
        compiler_params=_params(1),
    )(pt_flat, q3, gn3, sknew3, wknew3, win4, cosc, sinc, ab_seq3, sel_pool4)


def _post_kernel(c_ref, o_ref, gm_ref, x_ref, wc_ref, wn_ref, wo_ref, g_ref, wu_ref, wd_ref, gf_ref, y_ref, *, final):
    conv_out = _dot(c_ref[...], wc_ref[...])
    nsa_out = _dot(o_ref[...], wn_ref[...])
    gm = gm_ref[...].astype(F32)
    merged = gm[:, 0:D_MODEL] * conv_out + gm[:, D_MODEL:2 * D_MODEL] * nsa_out
    x = x_ref[...] + _dot(merged.astype(BF16), wo_ref[...])
    h = _rms_bf16(x, g_ref[...])
    acc = x
    for f in range(D_FF // FF_TILE):
        up = jnp.maximum(_dot(h, wu_ref[:, f * FF_TILE:(f + 1) * FF_TILE]), 0.0)
        acc = acc + _dot((up * up).astype(BF16), wd_ref[f * FF_TILE:(f + 1) * FF_TILE, :])
    if final:
        ms2 = jnp.mean(acc * acc, axis=-1, keepdims=True)
        acc = acc * lax.rsqrt(ms2 + EPS) * gf_ref[...]
    y_ref[...] = acc


FF_TILE = 1024


def _post(c2, o2, gm2, x2, wc, wn, wo, g, wu, wd, gf, tm, final):
    m = x2.shape[0]
    row = lambda i: (i, 0)
    const = lambda i: (0, 0)
    resident = lambda a: pl.BlockSpec(a.shape, const, pipeline_mode=pl.Buffered(1))
    return pl.pallas_call(
        functools.partial(_post_kernel, final=final),
        grid=(m // tm,),
        in_specs=[
            pl.BlockSpec((tm, c2.shape[1]), row),
            pl.BlockSpec((tm, o2.shape[1]), row),
            pl.BlockSpec((tm, 2 * D_MODEL), row),
            pl.BlockSpec((tm, D_MODEL), row),
            resident(wc), resident(wn), resident(wo),
            pl.BlockSpec((1, D_MODEL), const),
            resident(wu), resident(wd),
            pl.BlockSpec((1, D_MODEL), const),
        ],
        out_specs=pl.BlockSpec((tm, D_MODEL), row),
        out_shape=jax.ShapeDtypeStruct((m, D_MODEL), F32),
        compiler_params=_params(1),
    )(c2, o2, gm2, x2, wc, wn, wo, g, wu, wd, gf)


def _shift_append_kernel(old_ref, new_ref, o_ref, *, n_new):
    o_ref[...] = jnp.concatenate([old_ref[..., n_new:], new_ref[...]], axis=-1)


def _shift_append(old, new, rows_per_step):
    n, a, b, keep = old.shape
    n_new = new.shape[-1]
    r = _pick_tile(n, rows_per_step, 1)
    return pl.pallas_call(
        functools.partial(_shift_append_kernel, n_new=n_new),
        grid=(n // r,),
        in_specs=[pl.BlockSpec((r, a, b, keep), lambda i: (i, 0, 0, 0)),
                  pl.BlockSpec((r, a, b, n_new), lambda i: (i, 0, 0, 0))],
        out_specs=pl.BlockSpec((r, a, b, keep), lambda i: (i, 0, 0, 0)),
        out_shape=jax.ShapeDtypeStruct(old.shape, old.dtype),
        compiler_params=_params(1),
    )(old, new)


def _rope_tables(pos):
    half = HEAD_DIM // 2
    inv = jnp.power(ROPE_THETA, -jnp.arange(half, dtype=F32) / half)
    ang = pos.astype(F32)[:, None] * inv[None, :]
    cos = jnp.cos(ang)
    sin = jnp.sin(ang)
    return jnp.concatenate([cos, cos, cos, cos], axis=1), jnp.concatenate([-sin, sin, -sin, sin], axis=1)


def _pack_w_in(w):
    o0 = 2 * D_CONV
    o1 = o0 + N_HEADS * HEAD_DIM
    o4 = o1 + 3 * KV_COLS
    o5 = o4 + N_GATES
    wq = w[:, o0:o1].reshape(D_MODEL, N_HEADS, HEAD_DIM) * (HEAD_DIM ** -0.5)
    zero = jnp.zeros_like(wq)
    in_g0 = (jnp.arange(N_HEADS) < HPG)[None, :, None]
    wq_exp = jnp.concatenate([jnp.where(in_g0, wq, zero), jnp.where(in_g0, zero, wq)], axis=2)
    wgn = jnp.pad(w[:, o4:o5], ((0, 0), (0, LANES - N_GATES)))
    wq_exp = wq_exp.reshape(D_MODEL, Q_EXP)
    packed = jnp.concatenate([w[:, :o0], wq_exp, w[:, o1:o4], wgn, w[:, o5:]], axis=1)
    w_t = jnp.concatenate([wq.reshape(D_MODEL, N_HEADS * HEAD_DIM), w[:, o1:o4], wgn[:, :GATE_ROWS]], axis=1).T
    return packed.astype(BF16), w_t.astype(BF16)


def _pack_w_nsa_out(w):
    wh = w.reshape(N_HEADS, HEAD_DIM, D_MODEL)
    zero = jnp.zeros_like(wh)
    in_g0 = (jnp.arange(N_HEADS) < HPG)[:, None, None]
    return jnp.concatenate([jnp.where(in_g0, wh, zero), jnp.where(in_g0, zero, wh)], axis=1).reshape(Q_EXP, D_MODEL).astype(BF16)


def _pack_w_cmp(w_kv, pos_kv):
    lo, hi = w_kv[:CMP_STRIDE], w_kv[CMP_STRIDE:]
    eye = jnp.eye(N_KV, dtype=w_kv.dtype)
    blk = lambda part: jnp.einsum('lde,gh->lgdhe', part, eye).reshape(CMP_STRIDE * N_KV * HEAD_DIM, N_KV * HEAD_DIM)
    w2 = jnp.concatenate([blk(lo), blk(hi)], axis=1).astype(BF16)
    tile = lambda p: jnp.broadcast_to(p[:, None, :], (CMP_STRIDE, N_KV, HEAD_DIM)).reshape(1, -1)
    p2 = jnp.concatenate([tile(pos_kv[:CMP_STRIDE]), tile(pos_kv[CMP_STRIDE:]),
                          jnp.zeros((SUBLANES - 2, CMP_STRIDE * N_KV * HEAD_DIM), pos_kv.dtype)], axis=0).astype(BF16)
    return w2, p2


def _feature_major(a):
    lead = a.shape[:-4]
    n = len(lead)
    a = jnp.transpose(a, tuple(range(n)) + (n + 1, n + 2, n + 3, n))
    return a.reshape(lead + (2, N_KV * HEAD_DIM, a.shape[-1]))


def _position_major(a_t, rows):
    b = a_t.shape[0]
    return jnp.transpose(a_t.reshape(b, 2, N_KV, HEAD_DIM, rows), (0, 4, 1, 2, 3))


def kernel(x_prompt, x_sample, cache_cmp_kv, cache_sel_kv, state_win_kv, state_conv, page_table, norm_mix_g, w_in, conv_dw_w, conv_dw_b, conv_ln_g, conv_ln_b, w_conv_out, cmp_pos, w_cmp, w_nsa_out, w_out, norm_mlp_g, w_up, w_down, norm_final_g):
    depth = w_in.shape[0]
    bp, tp, _ = x_prompt.shape
    bd, tq, _ = x_sample.shape
    n_pool = cache_cmp_kv.shape[1]
    n_pages = page_table.shape[1]
    past_len = n_pages * PAGE_SIZE
    win_buf = state_win_kv.shape[2]
    assert tp % Q_TILE == 0 and tp // CMP_STRIDE == LANES and tp >= WINDOW
    assert past_len // CMP_STRIDE == LANES and win_buf == WINDOW and tq <= Q_ROWS

    mp, ms = bp * tp, bd * tq
    tm_p = _pick_tile(tp, 512)
    tm_s = _pick_tile(ms, 256)
    assert tp % tm_p == 0 and tm_s % tq == 0

    cos_p, sin_p = _rope_tables(jnp.arange(tp))
    cos_pt, sin_pt = cos_p.T, sin_p.T
    cos_s, sin_s = _rope_tables(past_len + jnp.arange(tm_s) % tq)
    n_chunk = tp // CMP_STRIDE
    cos_c, sin_c = _rope_tables(jnp.arange(n_chunk) * CMP_STRIDE + CMP_BLOCK - 1)

    pt_flat = page_table.reshape(-1).astype(jnp.int32)
    cmp_pool4 = _feature_major(cache_cmp_kv).reshape(depth * n_pool, 2, LANES, PAGE_SIZE)
    sel_pool4 = _feature_major(cache_sel_kv).reshape(depth * n_pool, 2, LANES, PAGE_SIZE)
    win4 = _feature_major(state_win_kv).reshape(depth * bd, 2, LANES, win_buf)
    zeros_conv = jnp.zeros((bp, CONV_K - 1, D_CONV), F32)
    conv_past = jnp.swapaxes(state_conv, 1, 2).reshape(depth * (CONV_K - 1), bd, D_CONV)

    xp = x_prompt.reshape(mp, D_MODEL)
    xs = x_sample.reshape(ms, D_MODEL)
    outs = [[] for _ in range(8)]
    for l in range(depth):
        w_packed, w_t = _pack_w_in(w_in[l])
        wn_exp = _pack_w_nsa_out(w_nsa_out[l])
        wn = w_nsa_out[l].astype(BF16)
        wc, wo = w_conv_out[l].astype(BF16), w_out[l].astype(BF16)
        wu, wd = w_up[l].astype(BF16), w_down[l].astype(BF16)
        wk2, pk2 = _pack_w_cmp(w_cmp[l, 0], cmp_pos[l, 0])
        wv2, pv2 = _pack_w_cmp(w_cmp[l, 1], cmp_pos[l, 1])
        g_mix, g_mlp = norm_mix_g[l][None], norm_mlp_g[l][None]
        dw_b, ln_g, ln_b = conv_dw_b[l][None], conv_ln_g[l][None], conv_ln_b[l][None]
        gf = norm_final_g[None]
        final = l == depth - 1

        u, q_t, ckv_t, skv_t, wkv_tm, skr, wkr, gn_t, gm = _inproj_cols(xp, g_mix, w_packed, w_t, cos_p, sin_p,
                                                                        cos_pt, sin_pt, bp, tp, tm_p)
        c_act = _conv_prompt(u.reshape(bp, tp, D_CONV), zeros_conv, conv_dw_w[l], dw_b, ln_g, ln_b)
        ab = _chunkproj_rows(ckv_t.reshape(bp, 2, LANES, tp), wk2, wv2, pk2, pv2)
        o = _attn_prompt(q_t, gn_t, skr, skv_t, wkr, wkv_tm, ab, cos_c, sin_c, bp, tp)
        xp = _post(c_act.reshape(mp, D_CONV), o, gm, xp, wc, wn, wo, g_mlp, wu, wd, gf, tm_p, final)
        keep = min(WINDOW, tp)
        outs[0].append(_position_major(ckv_t, tp))
        outs[2].append(_position_major(skv_t, tp))
        outs[4].append(_position_major(wkv_tm[:, :, tp - keep:], keep))
        outs[6].append(u.reshape(bp, tp, D_CONV)[:, tp - (CONV_K - 1):])

        u, q, ckv, skv, wkv, gn, gm = _inproj_rows(xs, g_mix, w_packed, cos_s, sin_s, tm_s)
        u3 = u.reshape(bd, tq, D_CONV)
        c_t = _conv_sample(conv_past, l, jnp.swapaxes(u3, 0, 1), conv_dw_w[l], dw_b, ln_g, ln_b)
        c_act = jnp.swapaxes(c_t, 0, 1).reshape(ms, D_CONV)
        ab_seq = _chunkproj_gather(pt_flat, cmp_pool4, wk2, wv2, pk2, pv2, l * n_pool)
        pad_q = lambda a: jnp.pad(a.reshape(bd, tq, -1), ((0, 0), (0, Q_ROWS - tq), (0, 0)))
        o = _attn_sample(pt_flat, pad_q(q), pad_q(gn), pad_q(skv), pad_q(wkv), win4, cos_c, sin_c, sel_pool4,
                         ab_seq.reshape(bd, n_pages * CHUNKS_PER_PAGE, 4 * LANES), l, n_pool, bd, past_len, tq)
        xs = _post(c_act, o[:, :tq].reshape(ms, Q_EXP), gm, xs, wc, wn_exp, wo, g_mlp, wu, wd, gf, tm_s, final)
        wkv5 = wkv.reshape(bd, tq, 2, N_KV, HEAD_DIM)
        outs[1].append(ckv.reshape(bd, tq, 2, N_KV, HEAD_DIM))
        outs[3].append(skv.reshape(bd, tq, 2, N_KV, HEAD_DIM))
        outs[5].append(wkv5)
        outs[7].append(u3)

    y_prompt = xp.reshape(bp, tp, D_MODEL)
    y_sample = xs.reshape(bd, tq, D_MODEL)
    outs = [jnp.stack(o) for o in outs]
    win_new = _shift_append(win4, _feature_major(outs[5]).reshape(depth * bd, 2, LANES, tq), 8)
    outs[5] = _position_major(win_new.reshape(depth * bd, KV_COLS, win_buf), win_buf).reshape(
        depth, bd, win_buf, 2, N_KV, HEAD_DIM)
    outs[7] = jnp.concatenate([state_conv[:, :, tq:], outs[7]], axis=2)
    return (y_prompt, y_sample) + tuple(outs)
```

```python
import functools

import jax
import jax.numpy as jnp
from jax import lax
from jax.experimental import pallas as pl
from jax.experimental.pallas import tpu as pltpu

D_MODEL = 1024
D_CONV = D_MODEL // 2
CONV_K = 31
N_HEADS = 8
HEAD_DIM = 64
N_KV = 2
HPG = N_HEADS // N_KV
KV_COLS = 2 * N_KV * HEAD_DIM
CMP_BLOCK = 32
CMP_STRIDE = 16
SEL_BLOCK = 64
N_SEL = 16
WINDOW = 512
D_FF = 4 * D_MODEL
ROPE_THETA = 10000.0
EPS = 1e-6
PAGE_SIZE = 128

LANES = 128
SUBLANES = 8
Q_EXP = N_HEADS * LANES
N_GATES = 3 * N_HEADS
CHUNKS_PER_PAGE = PAGE_SIZE // CMP_STRIDE
NEG = -1e30

O_GLU = 0
O_Q = O_GLU + 2 * D_CONV
O_CKV = O_Q + Q_EXP
O_SKV = O_CKV + KV_COLS
O_WKV = O_SKV + KV_COLS
O_GN = O_WKV + KV_COLS
O_GM = O_GN + LANES
IN_COLS_PACKED = O_GM + 2 * D_MODEL

VMEM_LIMIT = 56 * 1024 * 1024

F32 = jnp.float32
BF16 = jnp.bfloat16


def _params(n_axes, vmem=VMEM_LIMIT):
    return pltpu.CompilerParams(dimension_semantics=("arbitrary",) * n_axes, vmem_limit_bytes=vmem)


def _sigmoid(x):
    return 1.0 / (1.0 + jnp.exp(-x))


def _dot(a, b):
    return jnp.dot(a, b, preferred_element_type=F32)


def _dot_nt(a, b):
    return lax.dot_general(a, b, (((1,), (1,)), ((), ())), preferred_element_type=F32)


def _rope_slab(xs, cos, sin_signed):
    lane = lax.broadcasted_iota(jnp.int32, xs.shape, 1)
    first = (lane % HEAD_DIM) < (HEAD_DIM // 2)
    rot = jnp.where(first, pltpu.roll(xs, LANES - HEAD_DIM // 2, 1), pltpu.roll(xs, HEAD_DIM // 2, 1))
    return xs * cos + rot * sin_signed


def _rope_rows(xt, cos_t, sin_t):
    half = HEAD_DIM // 2
    rot = jnp.concatenate([xt[half:2 * half], xt[0:half], xt[3 * half:4 * half], xt[2 * half:3 * half]], axis=0)
    return xt * cos_t + rot * sin_t


def _pick_tile(n, cap, mult=8):
    t = min(n, cap)
    while n % t or t % mult:
        t -= 1
    return t


def _rms_bf16(x, g):
    ms = jnp.mean(x * x, axis=-1, keepdims=True)
    return (x * lax.rsqrt(ms + EPS) * g).astype(BF16)


def _inproj_common(h, w_ref, cos, sin, u_ref, q_ref, gn_ref, gm_ref):
    def proj(a, n):
        return _dot(h, w_ref[:, a:a + n])

    u_ref[...] = proj(O_GLU, D_CONV) * _sigmoid(proj(O_GLU + D_CONV, D_CONV))
    for s in range(N_HEADS):
        q_ref[:, s * LANES:(s + 1) * LANES] = _rope_slab(proj(O_Q + s * LANES, LANES), cos, sin).astype(BF16)
    gn_ref[...] = _sigmoid(proj(O_GN, LANES))
    gm_ref[...] = _sigmoid(proj(O_GM, 2 * D_MODEL)).astype(BF16)
    return proj


def _inproj_rows_kernel(x_ref, g_ref, w_ref, cos_ref, sin_ref,
                        u_ref, q_ref, ckv_ref, skv_ref, wkv_ref, gn_ref, gm_ref):
    h = _rms_bf16(x_ref[...], g_ref[...])
    cos = cos_ref[...]
    sin = sin_ref[...]
    proj = _inproj_common(h, w_ref, cos, sin, u_ref, q_ref, gn_ref, gm_ref)
    ckv_ref[...] = proj(O_CKV, KV_COLS)
    skv_ref[:, 0:LANES] = _rope_slab(proj(O_SKV, LANES), cos, sin)
    skv_ref[:, LANES:KV_COLS] = proj(O_SKV + LANES, LANES)
    wkv_ref[:, 0:LANES] = _rope_slab(proj(O_WKV, LANES), cos, sin)
    wkv_ref[:, LANES:KV_COLS] = proj(O_WKV + LANES, LANES)


GATE_ROWS = 32
R_Q = 0
R_CKV = R_Q + N_HEADS * HEAD_DIM
R_SKV = R_CKV + KV_COLS
R_WKV = R_SKV + KV_COLS
R_GN = R_WKV + KV_COLS
ROWS_T = R_GN + GATE_ROWS


def _inproj_cols_kernel(x_ref, g_ref, w_ref, wt_ref, cos_ref, sin_ref, cost_ref, sint_ref,
                        u_ref, q_ref, ckv_ref, skv_ref, wkv_ref, skr_ref, wkr_ref, gn_ref, gm_ref):
    h = _rms_bf16(x_ref[...], g_ref[...])
    cos = cos_ref[...]
    sin = sin_ref[...]
    cos_t = cost_ref[...]
    sin_t = sint_ref[...]

    def proj(a, n):
        return _dot(h, w_ref[:, a:a + n])

    t_all = _dot_nt(wt_ref[...], h)

    def proj_t(a, n):
        return t_all[a:a + n]

    u_ref[...] = proj(O_GLU, D_CONV) * _sigmoid(proj(O_GLU + D_CONV, D_CONV))
    gm_ref[...] = _sigmoid(proj(O_GM, 2 * D_MODEL)).astype(BF16)
    skr_ref[...] = _rope_slab(proj(O_SKV, LANES), cos, sin).astype(BF16)
    wkr_ref[...] = _rope_slab(proj(O_WKV, LANES), cos, sin).astype(BF16)
    for s in range(N_HEADS * HEAD_DIM // LANES):
        q_ref[0, s * LANES:(s + 1) * LANES, :] = _rope_rows(proj_t(R_Q + s * LANES, LANES), cos_t, sin_t).astype(BF16)
    ckv_ref[0] = proj_t(R_CKV, KV_COLS)
    skv_ref[0, 0:LANES] = _rope_rows(proj_t(R_SKV, LANES), cos_t, sin_t)
    skv_ref[0, LANES:KV_COLS] = proj_t(R_SKV + LANES, LANES)
    wkv_ref[0, 0:LANES] = _rope_rows(proj_t(R_WKV, LANES), cos_t, sin_t)
    wkv_ref[0, LANES:KV_COLS] = proj_t(R_WKV + LANES, LANES)
    gn_ref[0] = _sigmoid(proj_t(R_GN, GATE_ROWS))


def _inproj_rows(x2, g, w_packed, cos_t, sin_t, tm):
    m = x2.shape[0]
    row = lambda i: (i, 0)
    const = lambda i: (0, 0)
    out_shapes = (
        jax.ShapeDtypeStruct((m, D_CONV), F32),
        jax.ShapeDtypeStruct((m, Q_EXP), BF16),
        jax.ShapeDtypeStruct((m, KV_COLS), F32),
        jax.ShapeDtypeStruct((m, KV_COLS), F32),
        jax.ShapeDtypeStruct((m, KV_COLS), F32),
        jax.ShapeDtypeStruct((m, LANES), F32),
        jax.ShapeDtypeStruct((m, 2 * D_MODEL), BF16),
    )
    return pl.pallas_call(
        _inproj_rows_kernel,
        grid=(m // tm,),
        in_specs=[
            pl.BlockSpec((tm, D_MODEL), row),
            pl.BlockSpec((1, D_MODEL), const),
            pl.BlockSpec(w_packed.shape, const),
            pl.BlockSpec((tm, LANES), const),
            pl.BlockSpec((tm, LANES), const),
        ],
        out_specs=tuple(pl.BlockSpec((tm, s.shape[1]), row) for s in out_shapes),
        out_shape=out_shapes,
        compiler_params=_params(1),
    )(x2, g, w_packed, cos_t, sin_t)


def _inproj_cols(x2, g, w_packed, wkv_t, cos_t, sin_t, cos_tt, sin_tt, b, t_len, tm):
    m = x2.shape[0]
    n_tab = t_len // tm
    row = lambda i: (i, 0)
    const = lambda i: (0, 0)
    kv_map = lambda i: (i // n_tab, 0, i % n_tab)
    kv_shape = jax.ShapeDtypeStruct((b, KV_COLS, t_len), F32)
    out_shapes = (
        jax.ShapeDtypeStruct((m, D_CONV), F32),
        jax.ShapeDtypeStruct((b, N_HEADS * HEAD_DIM, t_len), BF16),
        kv_shape, kv_shape, kv_shape,
        jax.ShapeDtypeStruct((m, LANES), BF16),
        jax.ShapeDtypeStruct((m, LANES), BF16),
        jax.ShapeDtypeStruct((b, GATE_ROWS, t_len), F32),
        jax.ShapeDtypeStruct((m, 2 * D_MODEL), BF16),
    )
    out_specs = (
        pl.BlockSpec((tm, D_CONV), row),
        pl.BlockSpec((1, N_HEADS * HEAD_DIM, tm), kv_map),
        pl.BlockSpec((1, KV_COLS, tm), kv_map),
        pl.BlockSpec((1, KV_COLS, tm), kv_map),
        pl.BlockSpec((1, KV_COLS, tm), kv_map),
        pl.BlockSpec((tm, LANES), row),
        pl.BlockSpec((tm, LANES), row),
        pl.BlockSpec((1, GATE_ROWS, tm), kv_map),
        pl.BlockSpec((tm, 2 * D_MODEL), row),
    )
    return pl.pallas_call(
        _inproj_cols_kernel,
        grid=(m // tm,),
        in_specs=[
            pl.BlockSpec((tm, D_MODEL), row),
            pl.BlockSpec((1, D_MODEL), const),
            pl.BlockSpec(w_packed.shape, const),
            pl.BlockSpec(wkv_t.shape, const),
            pl.BlockSpec((tm, LANES), lambda i: (i % n_tab, 0)),
            pl.BlockSpec((tm, LANES), lambda i: (i % n_tab, 0)),
            pl.BlockSpec((LANES, tm), lambda i: (0, i % n_tab)),
            pl.BlockSpec((LANES, tm), lambda i: (0, i % n_tab)),
        ],
        out_specs=out_specs,
        out_shape=out_shapes,
        compiler_params=_params(1),
    )(x2, g, w_packed, wkv_t, cos_t, sin_t, cos_tt, sin_tt)


def _ln_swish(c, lg, lb):
    mu = jnp.mean(c, axis=-1, keepdims=True)
    d = c - mu
    var = jnp.mean(d * d, axis=-1, keepdims=True)
    y = d * lax.rsqrt(var + EPS) * lg + lb
    return y * _sigmoid(y)


CONV_PAD = 32
CONV_CHUNK = 64


def _conv_prompt_kernel(u_ref, past_ref, w_ref, b_ref, lg_ref, lb_ref, o_ref, uf_ref, *, t_len):
    off = CONV_PAD - (CONV_K - 1)
    uf_ref[0:SUBLANES, :] = jnp.zeros((SUBLANES, D_CONV), F32)
    uf_ref[off:CONV_PAD, :] = past_ref[0]
    uf_ref[CONV_PAD:CONV_PAD + t_len, :] = u_ref[0]
    bias = b_ref[...]
    lg = lg_ref[...]
    lb = lb_ref[...]
    ct = CONV_CHUNK

    def body(i, carry):
        base = pl.multiple_of(i * ct, ct)
        xw = uf_ref[pl.ds(base, ct + CONV_PAD), :]
        acc = jnp.zeros((ct, D_CONV), F32)
        n_win = ct + CONV_PAD
        for r in range(SUBLANES):
            yr = xw if r == 0 else pltpu.roll(xw, n_win - r, 0)
            for a in range((CONV_PAD // SUBLANES) + 1):
                k = SUBLANES * a + r - off
                if 0 <= k < CONV_K:
                    acc = acc + w_ref[k:k + 1, :] * yr[SUBLANES * a:SUBLANES * a + ct, :]
        o_ref[0, pl.ds(base, ct), :] = _ln_swish(acc + bias, lg, lb).astype(BF16)
        return carry

    lax.fori_loop(0, t_len // ct, body, 0)


def _conv_prompt(u3, past3, dw_w, dw_b, ln_g, ln_b):
    b, t_len, _ = u3.shape
    const2 = lambda i: (0, 0)
    return pl.pallas_call(
        functools.partial(_conv_prompt_kernel, t_len=t_len),
        grid=(b,),
        in_specs=[
            pl.BlockSpec((1, t_len, D_CONV), lambda i: (i, 0, 0)),
            pl.BlockSpec((1, CONV_K - 1, D_CONV), lambda i: (i, 0, 0)),
            pl.BlockSpec((CONV_K, D_CONV), const2),
            pl.BlockSpec((1, D_CONV), const2),
            pl.BlockSpec((1, D_CONV), const2),
            pl.BlockSpec((1, D_CONV), const2),
        ],
        out_specs=pl.BlockSpec((1, t_len, D_CONV), lambda i: (i, 0, 0)),
        out_shape=jax.ShapeDtypeStruct((b, t_len, D_CONV), BF16),
        scratch_shapes=[pltpu.VMEM((t_len + CONV_PAD, D_CONV), F32)],
        compiler_params=_params(1),
    )(u3, past3, dw_w, dw_b, ln_g, ln_b)


def _conv_sample_kernel(past_ref, u_ref, w_ref, b_ref, lg_ref, lb_ref, o_ref, *, tq):
    bias = b_ref[...]
    lg = lg_ref[...]
    lb = lb_ref[...]
    n_past = CONV_K - 1
    for t in range(tq):
        acc = jnp.zeros(o_ref.shape[1:], F32)
        for j in range(t, n_past):
            acc = acc + w_ref[j - t:j - t + 1, :] * past_ref[j]
        for i in range(t + 1):
            k = n_past - t + i
            acc = acc + w_ref[k:k + 1, :] * u_ref[i]
        o_ref[t] = _ln_swish(acc + bias, lg, lb).astype(BF16)


def _conv_sample(past_t, layer, u_t, dw_w, dw_b, ln_g, ln_b):
    n_past, bd = CONV_K - 1, past_t.shape[1]
    tq = u_t.shape[0]
    bt = _pick_tile(bd, 32)
    const2 = lambda i: (0, 0)
    return pl.pallas_call(
        functools.partial(_conv_sample_kernel, tq=tq),
        grid=(bd // bt,),
        in_specs=[
            pl.BlockSpec((n_past, bt, D_CONV), lambda i: (layer, i, 0)),
            pl.BlockSpec((tq, bt, D_CONV), lambda i: (0, i, 0)),
            pl.BlockSpec((CONV_K, D_CONV), const2),
            pl.BlockSpec((1, D_CONV), const2),
            pl.BlockSpec((1, D_CONV), const2),
            pl.BlockSpec((1, D_CONV), const2),
        ],
        out_specs=pl.BlockSpec((tq, bt, D_CONV), lambda i: (0, i, 0)),
        out_shape=jax.ShapeDtypeStruct((tq, bd, D_CONV), BF16),
        compiler_params=_params(1),
    )(past_t, u_t, dw_w, dw_b, ln_g, ln_b)


def _chunkproj_pages(get_page, n_pages, wk_ref, wv_ref, pk_ref, pv_ref, ab_ref, t_ref):
    n_rows = n_pages * CHUNKS_PER_PAGE
    for kv, (w_ref, p_ref) in enumerate(((wk_ref, pk_ref), (wv_ref, pv_ref))):
        for p in range(n_pages):
            t_ref[p * PAGE_SIZE:(p + 1) * PAGE_SIZE, :] = get_page(p, kv).T
        xs = jnp.concatenate([t_ref[pl.ds(l, n_rows, stride=CMP_STRIDE), :] for l in range(CMP_STRIDE)],
                             axis=1).astype(BF16)
        w = w_ref[...]
        part = _dot(xs, w)
        posb = _dot(p_ref[...], w)
        c0 = kv * 2 * LANES
        ab_ref[:, c0:c0 + LANES] = part[:, 0:LANES] + posb[0:1, 0:LANES]
        ab_ref[:, c0 + LANES:c0 + 2 * LANES] = part[:, LANES:2 * LANES] + posb[1:2, LANES:2 * LANES]


def _chunkproj_rows_kernel(x_ref, wk_ref, wv_ref, pk_ref, pv_ref, ab_ref, t_ref, *, n_pages):
    get_page = lambda p, kv: x_ref[0, kv, :, p * PAGE_SIZE:(p + 1) * PAGE_SIZE]
    _chunkproj_pages(get_page, n_pages, wk_ref, wv_ref, pk_ref, pv_ref, ab_ref, t_ref)


def _chunkproj_gather_kernel(pt_ref, pool_hbm, wk_ref, wv_ref, pk_ref, pv_ref, ab_ref, x_buf, t_ref, sem,
                             *, n_pages, page_base):
    step = pl.program_id(0)
    n_steps = pl.num_programs(0)
    slot = step % 2

    def page_copies(for_step, to_slot):
        return [pltpu.make_async_copy(pool_hbm.at[page_base + pt_ref[for_step * n_pages + j]], x_buf.at[to_slot, j],
                                      sem.at[to_slot, j]) for j in range(n_pages)]

    @pl.when(step == 0)
    def _():
        for c in page_copies(0, 0):
            c.start()

    @pl.when(step + 1 < n_steps)
    def _():
        for c in page_copies(step + 1, 1 - slot):
            c.start()

    for c in page_copies(step, slot):
        c.wait()
    _chunkproj_pages(lambda p, kv: x_buf[slot, p, kv], n_pages, wk_ref, wv_ref, pk_ref, pv_ref, ab_ref, t_ref)


def _chunkproj_rows(x4, wk2, wv2, pk2, pv2):
    n_pages = x4.shape[3] // PAGE_SIZE
    const2 = lambda i: (0, 0)
    rows = n_pages * CHUNKS_PER_PAGE
    return pl.pallas_call(
        functools.partial(_chunkproj_rows_kernel, n_pages=n_pages),
        grid=(x4.shape[0],),
        in_specs=[
            pl.BlockSpec((1, 2, LANES, x4.shape[3]), lambda i: (i, 0, 0, 0)),
            pl.BlockSpec(wk2.shape, const2),
            pl.BlockSpec(wv2.shape, const2),
            pl.BlockSpec(pk2.shape, const2),
            pl.BlockSpec(pv2.shape, const2),
        ],
        out_specs=pl.BlockSpec((rows, 4 * LANES), lambda i: (i, 0)),
        out_shape=jax.ShapeDtypeStruct((x4.shape[0] * rows, 4 * LANES), F32),
        scratch_shapes=[pltpu.VMEM((n_pages * PAGE_SIZE, LANES), F32)],
        compiler_params=_params(1),
    )(x4, wk2, wv2, pk2, pv2)


GATHER_PAGES = 64


def _chunkproj_gather(pt_flat, pool4, wk2, wv2, pk2, pv2, page_base):
    n_total = pt_flat.shape[0]
    n_pages = _pick_tile(n_total, GATHER_PAGES, 1)
    const2 = lambda i, pt: (0, 0)
    rows = n_pages * CHUNKS_PER_PAGE
    grid_spec = pltpu.PrefetchScalarGridSpec(
        num_scalar_prefetch=1,
        grid=(n_total // n_pages,),
        in_specs=[
            pl.BlockSpec(memory_space=pl.ANY),
            pl.BlockSpec(wk2.shape, const2),
            pl.BlockSpec(wv2.shape, const2),
            pl.BlockSpec(pk2.shape, const2),
            pl.BlockSpec(pv2.shape, const2),
        ],
        out_specs=pl.BlockSpec((rows, 4 * LANES), lambda i, pt: (i, 0)),
        scratch_shapes=[
            pltpu.VMEM((2, n_pages, 2, LANES, PAGE_SIZE), F32),
            pltpu.VMEM((n_pages * PAGE_SIZE, LANES), F32),
            pltpu.SemaphoreType.DMA((2, n_pages)),
        ],
    )
    return pl.pallas_call(
        functools.partial(_chunkproj_gather_kernel, n_pages=n_pages, page_base=page_base),
        grid_spec=grid_spec,
        out_shape=jax.ShapeDtypeStruct((n_total * CHUNKS_PER_PAGE, 4 * LANES), F32),
        compiler_params=_params(1),
    )(pt_flat, pool4, wk2, wv2, pk2, pv2)


def _compressed_kv_f32(ab, cosc, sinc):
    n = ab.shape[0]
    kc = ab[:, 0:LANES] + pltpu.roll(ab[:, LANES:2 * LANES], n - 1, 0)
    vc = ab[:, 2 * LANES:3 * LANES] + pltpu.roll(ab[:, 3 * LANES:4 * LANES], n - 1, 0)
    return _rope_slab(kc, cosc, sinc), vc


def _compressed_kv(ab, cosc, sinc):
    kc, vc = _compressed_kv_f32(ab, cosc, sinc)
    return kc.astype(BF16), vc.astype(BF16)


def _softmax_parts(s, mask):
    sm = jnp.where(mask, s, NEG)
    m = jnp.max(sm, axis=-1, keepdims=True)
    p = jnp.where(mask, jnp.exp(sm - m), 0.0)
    l = jnp.maximum(jnp.sum(p, axis=-1, keepdims=True), 1e-30)
    return p, l


def _select_blocks(imp, qpos, n_sel, axis):
    jidx = lax.broadcasted_iota(jnp.int32, imp.shape, axis)
    cur = lax.shift_right_logical(qpos, 6)
    forced = (jidx == 0) | (jidx == cur) | (jidx == cur - 1)
    valid = (jidx * SEL_BLOCK <= qpos) & (jidx < n_sel)
    v = jnp.where(valid, jnp.where(forced, jnp.inf, imp), -jnp.inf)
    rank = jnp.zeros(imp.shape, jnp.int32)
    for k in range(n_sel):
        vk = v[k:k + 1, :] if axis == 0 else v[:, k:k + 1]
        ahead = (vk > v) | ((vk == v) & (jidx > k))
        rank = rank + ahead.astype(jnp.int32)
    return (rank < min(N_SEL, n_sel)) & valid


def _split_bf16(x):
    hi = x.astype(BF16)
    r1 = x - hi.astype(F32)
    mid = r1.astype(BF16)
    lo = (r1 - mid.astype(F32)).astype(BF16)
    return hi, mid, lo


def _overlap(n_cmp, cmp_axis):
    i = lax.broadcasted_iota(jnp.int32, (LANES, LANES), cmp_axis)
    j = lax.broadcasted_iota(jnp.int32, (LANES, LANES), 1 - cmp_axis)
    hit = (i * CMP_STRIDE < (j + 1) * SEL_BLOCK) & (i * CMP_STRIDE + CMP_BLOCK > j * SEL_BLOCK) & (i < n_cmp)
    return hit.astype(F32)


Q_TILE = 256
K_TILE = 256


def _attn_prompt_kernel(q_ref, gn_ref, skr_ref, svt_ref, wkr_ref, wvt_ref, ab_ref, cosc_ref, sinc_ref, o_ref,
                        kc_ref, vct_ref, bias_ref, m_ref, acc_ref, accb_ref, out_ref, *, t_len):
    qb = pl.program_id(1)
    n_chunk = t_len // CMP_STRIDE
    n_cmp = (t_len - CMP_BLOCK) // CMP_STRIDE + 1
    n_sel = -(-t_len // SEL_BLOCK)
    sel_rows = -(-n_sel // SUBLANES) * SUBLANES
    qt, kt_sz = Q_TILE, K_TILE
    q_tiles = qt // LANES
    g_lanes = HPG * qt
    n_lane_tiles = N_HEADS * q_tiles

    @pl.when(qb == 0)
    def _():
        kc, vc = _compressed_kv_f32(ab_ref[...], cosc_ref[...], sinc_ref[...])
        kc_ref[...] = kc.astype(BF16)
        vct_ref[...] = vc.T.astype(BF16)

    q0 = qb * qt
    qpos = q0 + lax.broadcasted_iota(jnp.int32, (1, qt), 1)

    def lane_tile(c):
        return slice(c * LANES, (c + 1) * LANES)

    def q_part(c):
        return slice((c % q_tiles) * LANES, (c % q_tiles + 1) * LANES)

    def group_of(c):
        return c // (HPG * q_tiles)

    def group_lanes(g):
        return slice(g * g_lanes, (g + 1) * g_lanes)

    q_zero = jnp.zeros((HEAD_DIM, qt), BF16)

    def q_slab(hh):
        q_h = q_ref[0, hh * HEAD_DIM:(hh + 1) * HEAD_DIM, :]
        return jnp.concatenate([q_h, q_zero] if hh < HPG else [q_zero, q_h], axis=0)

    q_all = jnp.concatenate([q_slab(hh) for hh in range(N_HEADS)], axis=1)

    def gates(branch):
        r0 = branch * N_HEADS
        return jnp.concatenate([gn_ref[0, r0 + hh:r0 + hh + 1, :] for hh in range(N_HEADS)], axis=1)

    nrow = lax.broadcasted_iota(jnp.int32, (n_chunk, qt), 0)
    mask_c = (nrow * CMP_STRIDE + CMP_BLOCK - 1 <= qpos) & (nrow < n_cmp)
    acc_ref[...] = _dot(kc_ref[...], q_all)
    hs = [[jnp.zeros((n_chunk, LANES), F32) for _ in range(q_tiles)] for _ in range(N_KV)]
    p_parts = []
    for c in range(n_lane_tiles):
        mask = mask_c[:, q_part(c)]
        sm = jnp.where(mask, acc_ref[:, lane_tile(c)], NEG)
        p_c = jnp.where(mask, jnp.exp(sm - jnp.max(sm, axis=0, keepdims=True)), 0.0)
        p_c = p_c / jnp.maximum(jnp.sum(p_c, axis=0, keepdims=True), 1e-30)
        hs[group_of(c)][c % q_tiles] = hs[group_of(c)][c % q_tiles] + p_c
        p_parts.append(p_c.astype(BF16))
    out_ref[...] = gates(0) * _dot(vct_ref[...], jnp.concatenate(p_parts, axis=1))

    sel_t = []
    for g in range(N_KV):
        ov_t = _overlap(n_cmp, 1).astype(BF16)
        imp_t = sum(_dot(ov_t, part) for part in _split_bf16(jnp.concatenate(hs[g], axis=1)))
        s_g = _select_blocks(imp_t[0:sel_rows], qpos, n_sel, 0).astype(F32)
        sel_t.append(jnp.concatenate([s_g, jnp.zeros((LANES - sel_rows, qt), F32)], axis=0).astype(BF16))

    m_ref[...] = jnp.full(m_ref.shape, NEG, F32)
    accb_ref[...] = jnp.zeros(accb_ref.shape, F32)

    def tile_step(br, kr_ref, vt_ref, k_idx, bias_fn):
        k0 = pl.multiple_of(k_idx * kt_sz, kt_sz)
        kpos = k0 + lax.broadcasted_iota(jnp.int32, (kt_sz, qt), 0)
        bias_fn(k_idx, kpos)
        s = _dot(kr_ref[pl.ds(k0, kt_sz), :], q_all)
        vt = vt_ref[0, :, pl.ds(k0, kt_sz)]
        vrow = lax.broadcasted_iota(jnp.int32, (LANES, kt_sz), 0)
        p_parts, a_parts = [], []
        for c in range(n_lane_tiles):
            sb = s[:, lane_tile(c)] + bias_ref[br, group_of(c), :, q_part(c)]
            m_old = m_ref[br, 0:1, lane_tile(c)]
            m_new = jnp.maximum(m_old, jnp.max(sb, axis=0, keepdims=True))
            p_parts.append(jnp.exp(sb - m_new).astype(BF16))
            a_parts.append(jnp.exp(m_old - m_new))
            m_ref[br, :, lane_tile(c)] = jnp.broadcast_to(m_new, (SUBLANES, LANES))
        for g in range(N_KV):
            v_own = (vrow >= g * HEAD_DIM) & (vrow < (g + 1) * HEAD_DIM)
            vaug = jnp.where(v_own, vt, 1.0).astype(BF16)
            tiles = range(g * HPG * q_tiles, (g + 1) * HPG * q_tiles)
            alpha = jnp.concatenate([a_parts[c] for c in tiles], axis=1)
            p = jnp.concatenate([p_parts[c] for c in tiles], axis=1)
            accb_ref[br, :, group_lanes(g)] = alpha * accb_ref[br, :, group_lanes(g)] + _dot(vaug, p)

    def sel_bias(k_idx, kpos):
        krow = lax.broadcasted_iota(jnp.int32, (kt_sz, LANES), 0)
        jcol = lax.broadcasted_iota(jnp.int32, (kt_sz, LANES), 1)
        expand_t = (jcol == k_idx * (kt_sz // SEL_BLOCK) + lax.shift_right_logical(krow, 6)).astype(BF16)
        for g in range(N_KV):
            bias_ref[0, g] = jnp.where((_dot(expand_t, sel_t[g]) > 0.5) & (kpos <= qpos), 0.0, NEG)

    def win_bias(k_idx, kpos):
        b = jnp.where((kpos <= qpos) & (kpos > qpos - WINDOW), 0.0, NEG)
        for g in range(N_KV):
            bias_ref[1, g] = b

    n_sel_tiles = qb + 1
    n_win_tiles = jnp.minimum(qb, WINDOW // kt_sz) + 1

    def both(i, carry):
        tile_step(0, skr_ref, svt_ref, i, sel_bias)
        tile_step(1, wkr_ref, wvt_ref, qb - i, win_bias)
        return carry

    def sel_only(i, carry):
        tile_step(0, skr_ref, svt_ref, i, sel_bias)
        return carry

    lax.fori_loop(0, n_win_tiles, both, 0)
    lax.fori_loop(n_win_tiles, n_sel_tiles, sel_only, 0)

    for br in range(2):
        gt = gates(br + 1)
        for g in range(N_KV):
            sum_row = (1 - g) * HEAD_DIM
            acc = accb_ref[br, :, group_lanes(g)]
            out_ref[:, group_lanes(g)] = (out_ref[:, group_lanes(g)]
                                          + gt[:, group_lanes(g)] * (acc / acc[sum_row:sum_row + 1, :]))

    lane = lax.broadcasted_iota(jnp.int32, (qt, LANES), 1)
    for pair in range(N_HEADS // 2):
        a = out_ref[:, (2 * pair) * qt:(2 * pair + 1) * qt].T
        b = out_ref[:, (2 * pair + 1) * qt:(2 * pair + 2) * qt].T
        if 2 * pair < HPG:
            slab = jnp.where(lane < HEAD_DIM, a, pltpu.roll(b, HEAD_DIM, 1))
        else:
            slab = jnp.where(lane < HEAD_DIM, pltpu.roll(a, HEAD_DIM, 1), b)
        o_ref[:, pair * LANES:(pair + 1) * LANES] = slab.astype(BF16)


def _attn_prompt(q_t, gn_t, skr, skv_t, wkr, wkv_t, ab2, cosc, sinc, b, t_len):
    n_chunk = t_len // CMP_STRIDE
    nqb = t_len // Q_TILE
    const2 = lambda i, j: (0, 0)
    per_b2 = lambda i, j: (i, 0)
    v_rows = lambda i, j: (i, 1, 0)
    return pl.pallas_call(
        functools.partial(_attn_prompt_kernel, t_len=t_len),
        grid=(b, nqb),
        in_specs=[
            pl.BlockSpec((1, N_HEADS * HEAD_DIM, Q_TILE), lambda i, j: (i, 0, j)),
            pl.BlockSpec((1, GATE_ROWS, Q_TILE), lambda i, j: (i, 0, j)),
            pl.BlockSpec((t_len, LANES), per_b2),
            pl.BlockSpec((1, LANES, t_len), v_rows),
            pl.BlockSpec((t_len, LANES), per_b2),
            pl.BlockSpec((1, LANES, t_len), v_rows),
            pl.BlockSpec((n_chunk, 4 * LANES), per_b2),
            pl.BlockSpec((n_chunk, LANES), const2),
            pl.BlockSpec((n_chunk, LANES), const2),
        ],
        out_specs=pl.BlockSpec((Q_TILE, N_HEADS * HEAD_DIM), lambda i, j: (i * nqb + j, 0)),
        out_shape=jax.ShapeDtypeStruct((b * t_len, N_HEADS * HEAD_DIM), BF16),
        scratch_shapes=[
            pltpu.VMEM((n_chunk, LANES), BF16),
            pltpu.VMEM((LANES, n_chunk), BF16),
            pltpu.VMEM((2, N_KV, K_TILE, Q_TILE), F32),
            pltpu.VMEM((2, SUBLANES, N_HEADS * Q_TILE), F32),
            pltpu.VMEM((LANES, N_HEADS * Q_TILE), F32),
            pltpu.VMEM((2, LANES, N_HEADS * Q_TILE), F32),
            pltpu.VMEM((LANES, N_HEADS * Q_TILE), F32),
        ],
        compiler_params=_params(2),
    )(q_t, gn_t, skr, skv_t, wkr, wkv_t, ab2, cosc, sinc)


Q_ROWS = 8
SEQ_PER_STEP = 4


def _attn_sample_kernel(pt_ref, q_ref, gn_ref, sknew_ref, wknew_ref, win_ref, cosc_ref, sinc_ref, ab_ref, sel_hbm,
                        o_ref, sel_buf, sem, *, past_len, tq, n_pages, n_seq, page_base):
    step = pl.program_id(0)
    n_steps = pl.num_programs(0)
    slot = step % 2
    n_slot_pages = n_seq * n_pages

    def page_copies(for_step, to_slot):
        return [pltpu.make_async_copy(sel_hbm.at[page_base + pt_ref[for_step * n_slot_pages + j]],
                                      sel_buf.at[to_slot, j], sem.at[to_slot, j]) for j in range(n_slot_pages)]

    @pl.when(step == 0)
    def _():
        for c in page_copies(0, 0):
            c.start()

    @pl.when(step + 1 < n_steps)
    def _():
        for c in page_copies(step + 1, 1 - slot):
            c.start()

    for c in page_copies(step, slot):
        c.wait()

    def sel_page(e, p, kv):
        return sel_buf[slot, e * n_pages + p, kv]

    seqs = range(n_seq)
    t_all = past_len + tq
    n_chunk = past_len // CMP_STRIDE
    n_cmp = (t_all - CMP_BLOCK) // CMP_STRIDE + 1
    n_sel = -(-t_all // SEL_BLOCK)
    rows = N_HEADS * Q_ROWS
    all_rows = n_seq * rows
    win_buf = win_ref.shape[3]

    def stack(parts):
        return jnp.concatenate(parts, axis=0)

    def per_seq(a, e):
        return a[e * rows:(e + 1) * rows]

    qbd = [stack([q_ref[e, :, hh * LANES:(hh + 1) * LANES] for hh in range(N_HEADS)]) for e in seqs]
    qpos = past_len + lax.broadcasted_iota(jnp.int32, (all_rows, 1), 0) % Q_ROWS

    cosc = cosc_ref[...]
    sinc = sinc_ref[...]
    kvc = [_compressed_kv(ab_ref[e], cosc, sinc) for e in seqs]
    ncol = lax.broadcasted_iota(jnp.int32, (all_rows, n_chunk), 1)
    mask_c = (ncol * CMP_STRIDE + CMP_BLOCK - 1 <= qpos) & (ncol < n_cmp)
    p_c, l_c = _softmax_parts(stack([_dot_nt(qbd[e], kvc[e][0]) for e in seqs]), mask_c)
    p_c = p_c / l_c
    p_c16 = p_c.astype(BF16)
    o_c = stack([_dot(per_seq(p_c16, e), kvc[e][1]) for e in seqs])

    hs = []
    for e in seqs:
        for g in range(N_KV):
            r0 = e * rows + g * HPG * Q_ROWS
            acc = p_c[r0:r0 + Q_ROWS]
            for h in range(1, HPG):
                acc = acc + p_c[r0 + h * Q_ROWS:r0 + (h + 1) * Q_ROWS]
            hs.append(acc)
    ov = _overlap(n_cmp, 0).astype(BF16)
    imp = sum(_dot(part, ov) for part in _split_bf16(stack(hs)))
    sel_g = _select_blocks(imp, qpos[0:n_seq * N_KV * Q_ROWS], n_sel, 1).astype(BF16)
    sel_rows = stack([sel_g[(e * N_KV + g) * Q_ROWS:(e * N_KV + g + 1) * Q_ROWS]
                      for e in seqs for g in range(N_KV) for _ in range(HPG)])

    def new_rows_tile(ref, e, c0):
        return stack([ref[e, :, c0:c0 + LANES], jnp.zeros((LANES - Q_ROWS, LANES), F32)]).astype(BF16)

    n_keys = (n_pages + 1) * PAGE_SIZE
    s_s = stack([jnp.concatenate([_dot(qbd[e], sel_page(e, p, 0).astype(BF16)) for p in range(n_pages)]
                                 + [_dot_nt(qbd[e], new_rows_tile(sknew_ref, e, 0))], axis=1) for e in seqs])
    jrow = lax.broadcasted_iota(jnp.int32, (LANES, n_keys), 0)
    kcol = lax.broadcasted_iota(jnp.int32, (LANES, n_keys), 1)
    expand = (jrow == lax.shift_right_logical(kcol, 6)).astype(BF16)
    kpos = lax.broadcasted_iota(jnp.int32, (all_rows, n_keys), 1)
    mask_s = (_dot(sel_rows, expand) > 0.5) & (kpos <= qpos)
    p_s, l_s = _softmax_parts(s_s, mask_s)
    p_s = p_s.astype(BF16)
    o_s = []
    for e in seqs:
        p_e = per_seq(p_s, e)
        acc = _dot(p_e[:, n_pages * PAGE_SIZE:], new_rows_tile(sknew_ref, e, LANES))
        for i in range(n_pages):
            acc = acc + _dot_nt(p_e[:, i * PAGE_SIZE:(i + 1) * PAGE_SIZE], sel_page(e, i, 1).astype(BF16))
        o_s.append(acc)
    o_s = stack(o_s) / l_s

    s_w = stack([jnp.concatenate([_dot(qbd[e], win_ref[e, 0].astype(BF16)),
                                  _dot_nt(qbd[e], new_rows_tile(wknew_ref, e, 0))], axis=1) for e in seqs])
    wcol = lax.broadcasted_iota(jnp.int32, (all_rows, win_buf + LANES), 1)
    kpos_w = past_len - win_buf + wcol
    mask_w = (kpos_w <= qpos) & (kpos_w > qpos - WINDOW)
    p_w, l_w = _softmax_parts(s_w, mask_w)
    p_w = p_w.astype(BF16)
    o_w = stack([_dot_nt(per_seq(p_w, e)[:, 0:win_buf], win_ref[e, 1].astype(BF16))
                 + _dot(per_seq(p_w, e)[:, win_buf:], new_rows_tile(wknew_ref, e, LANES)) for e in seqs]) / l_w

    for e in seqs:
        gn = gn_ref[e]
        for hh in range(N_HEADS):
            r = slice(e * rows + hh * Q_ROWS, e * rows + (hh + 1) * Q_ROWS)
            out = (gn[:, hh:hh + 1] * o_c[r]
                   + gn[:, N_HEADS + hh:N_HEADS + hh + 1] * o_s[r]
                   + gn[:, 2 * N_HEADS + hh:2 * N_HEADS + hh + 1] * o_w[r])
            o_ref[e, :, hh * LANES:(hh + 1) * LANES] = out.astype(BF16)


def _attn_sample(pt_flat, q3, gn3, sknew3, wknew3, win4, cosc, sinc, sel_pool4, ab_seq3,
                 layer, n_pool, bd, past_len, tq):
    n_pages = past_len // PAGE_SIZE
    win_buf = win4.shape[3]
    n_chunk = past_len // CMP_STRIDE
    n_seq = _pick_tile(bd, SEQ_PER_STEP, 1)
    per_b = lambda i, pt: (i, 0, 0)
    const2 = lambda i, pt: (0, 0)
    n_slot_pages = n_seq * n_pages
    grid_spec = pltpu.PrefetchScalarGridSpec(
        num_scalar_prefetch=1,
        grid=(bd // n_seq,),
        in_specs=[
            pl.BlockSpec((n_seq, Q_ROWS, Q_EXP), per_b),
            pl.BlockSpec((n_seq, Q_ROWS, LANES), per_b),
            pl.BlockSpec((n_seq, Q_ROWS, KV_COLS), per_b),
            pl.BlockSpec((n_seq, Q_ROWS, KV_COLS), per_b),
            pl.BlockSpec((n_seq, 2, LANES, win_buf), lambda i, pt: (layer * (bd // n_seq) + i, 0, 0, 0)),
            pl.BlockSpec((n_chunk, LANES), const2),
            pl.BlockSpec((n_chunk, LANES), const2),
            pl.BlockSpec((n_seq, n_chunk, 4 * LANES), per_b),
            pl.BlockSpec(memory_space=pl.ANY),
        ],
        out_specs=pl.BlockSpec((n_seq, Q_ROWS, Q_EXP), per_b),
        scratch_shapes=[
            pltpu.VMEM((2, n_slot_pages, 2, LANES, PAGE_SIZE), F32),
            pltpu.SemaphoreType.DMA((2, n_slot_pages)),
        ],
    )
    return pl.pallas_call(
        functools.partial(_attn_sample_kernel, past_len=past_len, tq=tq, n_pages=n_pages, n_seq=n_seq,
                          page_base=layer * n_pool),
        grid_spec=grid_spec,
        out_shape=jax.ShapeDtypeStruct((bd, Q_ROWS, Q_EXP), BF16),
        compiler_params=_params(1),
    )(pt_flat, q3, gn3, sknew3, wknew3, win4, cosc, sinc, ab_seq3, sel_pool4)


def _post_kernel(c_ref, o_ref, gm_ref, x_ref, wc_ref, wn_ref, wo_ref, g_ref, wu_ref, wd_ref, gf_ref, y_ref, *, final):
    conv_out = _dot(c_ref[...], wc_ref[...])
    nsa_out = _dot(o_ref[...], wn_ref[...])
    gm = gm_ref[...].astype(F32)
    merged = gm[:, 0:D_MODEL] * conv_out + gm[:, D_MODEL:2 * D_MODEL] * nsa_out
    x = x_ref[...] + _dot(merged.astype(BF16), wo_ref[...])
    h = _rms_bf16(x, g_ref[...])
    acc = x
    for f in range(D_FF // FF_TILE):
        up = jnp.maximum(_dot(h, wu_ref[0, :, f * FF_TILE:(f + 1) * FF_TILE]), 0.0)
        acc = acc + _dot((up * up).astype(BF16), wd_ref[0, f * FF_TILE:(f + 1) * FF_TILE, :])
    if final:
        ms2 = jnp.mean(acc * acc, axis=-1, keepdims=True)
        acc = acc * lax.rsqrt(ms2 + EPS) * gf_ref[...]
    y_ref[...] = acc


FF_TILE = 1024


def _post(c2, o2, gm2, x2, wc, wn, wo, g, wu, wd, gf, layer, tm, final):
    m = x2.shape[0]
    row = lambda i: (i, 0)
    const = lambda i: (0, 0)
    resident = lambda a: pl.BlockSpec(a.shape, const, pipeline_mode=pl.Buffered(1))
    per_layer = lambda a: pl.BlockSpec((1,) + a.shape[1:], lambda i: (layer, 0, 0), pipeline_mode=pl.Buffered(1))
    return pl.pallas_call(
        functools.partial(_post_kernel, final=final),
        grid=(m // tm,),
        in_specs=[
            pl.BlockSpec((tm, c2.shape[1]), row),
            pl.BlockSpec((tm, o2.shape[1]), row),
            pl.BlockSpec((tm, 2 * D_MODEL), row),
            pl.BlockSpec((tm, D_MODEL), row),
            resident(wc), resident(wn), resident(wo),
            pl.BlockSpec((1, D_MODEL), const),
            per_layer(wu), per_layer(wd),
            pl.BlockSpec((1, D_MODEL), const),
        ],
        out_specs=pl.BlockSpec((tm, D_MODEL), row),
        out_shape=jax.ShapeDtypeStruct((m, D_MODEL), F32),
        compiler_params=_params(1),
    )(c2, o2, gm2, x2, wc, wn, wo, g, wu, wd, gf)


def _shift_append_kernel(old_ref, new_ref, o_ref, *, n_new):
    o_ref[...] = jnp.concatenate([old_ref[..., n_new:], new_ref[...]], axis=-1)


def _shift_append(old, new, rows_per_step):
    n, a, b, keep = old.shape
    n_new = new.shape[-1]
    r = _pick_tile(n, rows_per_step, 1)
    return pl.pallas_call(
        functools.partial(_shift_append_kernel, n_new=n_new),
        grid=(n // r,),
        in_specs=[pl.BlockSpec((r, a, b, keep), lambda i: (i, 0, 0, 0)),
                  pl.BlockSpec((r, a, b, n_new), lambda i: (i, 0, 0, 0))],
        out_specs=pl.BlockSpec((r, a, b, keep), lambda i: (i, 0, 0, 0)),
        out_shape=jax.ShapeDtypeStruct(old.shape, old.dtype),
        compiler_params=_params(1),
    )(old, new)


def _rope_tables(pos):
    half = HEAD_DIM // 2
    inv = jnp.power(ROPE_THETA, -jnp.arange(half, dtype=F32) / half)
    ang = pos.astype(F32)[:, None] * inv[None, :]
    cos = jnp.cos(ang)
    sin = jnp.sin(ang)
    return jnp.concatenate([cos, cos, cos, cos], axis=1), jnp.concatenate([-sin, sin, -sin, sin], axis=1)


def _pack_w_in(w):
    o0 = 2 * D_CONV
    o1 = o0 + N_HEADS * HEAD_DIM
    o4 = o1 + 3 * KV_COLS
    o5 = o4 + N_GATES
    wq = w[:, o0:o1].reshape(D_MODEL, N_HEADS, HEAD_DIM) * (HEAD_DIM ** -0.5)
    zero = jnp.zeros_like(wq)
    in_g0 = (jnp.arange(N_HEADS) < HPG)[None, :, None]
    wq_exp = jnp.concatenate([jnp.where(in_g0, wq, zero), jnp.where(in_g0, zero, wq)], axis=2)
    wgn = jnp.pad(w[:, o4:o5], ((0, 0), (0, LANES - N_GATES)))
    wq_exp = wq_exp.reshape(D_MODEL, Q_EXP)
    packed = jnp.concatenate([w[:, :o0], wq_exp, w[:, o1:o4], wgn, w[:, o5:]], axis=1)
    w_t = jnp.concatenate([wq.reshape(D_MODEL, N_HEADS * HEAD_DIM), w[:, o1:o4], wgn[:, :GATE_ROWS]], axis=1).T
    return packed.astype(BF16), w_t.astype(BF16)


def _pack_w_nsa_out(w):
    wh = w.reshape(N_HEADS, HEAD_DIM, D_MODEL)
    zero = jnp.zeros_like(wh)
    in_g0 = (jnp.arange(N_HEADS) < HPG)[:, None, None]
    return jnp.concatenate([jnp.where(in_g0, wh, zero), jnp.where(in_g0, zero, wh)], axis=1).reshape(Q_EXP, D_MODEL).astype(BF16)


def _pack_w_cmp(w_kv, pos_kv):
    lo, hi = w_kv[:CMP_STRIDE], w_kv[CMP_STRIDE:]
    eye = jnp.eye(N_KV, dtype=w_kv.dtype)
    blk = lambda part: jnp.einsum('lde,gh->lgdhe', part, eye).reshape(CMP_STRIDE * N_KV * HEAD_DIM, N_KV * HEAD_DIM)
    w2 = jnp.concatenate([blk(lo), blk(hi)], axis=1).astype(BF16)
    tile = lambda p: jnp.broadcast_to(p[:, None, :], (CMP_STRIDE, N_KV, HEAD_DIM)).reshape(1, -1)
    p2 = jnp.concatenate([tile(pos_kv[:CMP_STRIDE]), tile(pos_kv[CMP_STRIDE:]),
                          jnp.zeros((SUBLANES - 2, CMP_STRIDE * N_KV * HEAD_DIM), pos_kv.dtype)], axis=0).astype(BF16)
    return w2, p2


def _feature_major(a):
    lead = a.shape[:-4]
    n = len(lead)
    a = jnp.transpose(a, tuple(range(n)) + (n + 1, n + 2, n + 3, n))
    return a.reshape(lead + (2, N_KV * HEAD_DIM, a.shape[-1]))


def _position_major(a_t, rows):
    b = a_t.shape[0]
    return jnp.transpose(a_t.reshape(b, 2, N_KV, HEAD_DIM, rows), (0, 4, 1, 2, 3))


def kernel(x_prompt, x_sample, cache_cmp_kv, cache_sel_kv, state_win_kv, state_conv, page_table, norm_mix_g, w_in, conv_dw_w, conv_dw_b, conv_ln_g, conv_ln_b, w_conv_out, cmp_pos, w_cmp, w_nsa_out, w_out, norm_mlp_g, w_up, w_down, norm_final_g):
    depth = w_in.shape[0]
    bp, tp, _ = x_prompt.shape
    bd, tq, _ = x_sample.shape
    n_pool = cache_cmp_kv.shape[1]
    n_pages = page_table.shape[1]
    past_len = n_pages * PAGE_SIZE
    win_buf = state_win_kv.shape[2]
    assert tp % Q_TILE == 0 and tp // CMP_STRIDE == LANES and tp >= WINDOW
    assert past_len // CMP_STRIDE == LANES and win_buf == WINDOW and tq <= Q_ROWS

    mp, ms = bp * tp, bd * tq
    tm_p = _pick_tile(tp, 512)
    tm_s = _pick_tile(ms, 512)
    assert tp % tm_p == 0 and tm_s % tq == 0

    cos_p, sin_p = _rope_tables(jnp.arange(tp))
    cos_pt, sin_pt = cos_p.T, sin_p.T
    cos_s, sin_s = _rope_tables(past_len + jnp.arange(tm_s) % tq)
    n_chunk = tp // CMP_STRIDE
    cos_c, sin_c = _rope_tables(jnp.arange(n_chunk) * CMP_STRIDE + CMP_BLOCK - 1)

    pt_flat = page_table.reshape(-1).astype(jnp.int32)
    cmp_pool4 = _feature_major(cache_cmp_kv).reshape(depth * n_pool, 2, LANES, PAGE_SIZE)
    sel_pool4 = _feature_major(cache_sel_kv).reshape(depth * n_pool, 2, LANES, PAGE_SIZE)
    win4 = _feature_major(state_win_kv).reshape(depth * bd, 2, LANES, win_buf)
    wu, wd = w_up.astype(BF16), w_down.astype(BF16)
    zeros_conv = jnp.zeros((bp, CONV_K - 1, D_CONV), F32)
    conv_past = jnp.swapaxes(state_conv, 1, 2).reshape(depth * (CONV_K - 1), bd, D_CONV)

    xp = x_prompt.reshape(mp, D_MODEL)
    xs = x_sample.reshape(ms, D_MODEL)
    outs = [[] for _ in range(8)]
    for l in range(depth):
        w_packed, w_t = _pack_w_in(w_in[l])
        wn_exp = _pack_w_nsa_out(w_nsa_out[l])
        wn = w_nsa_out[l].astype(BF16)
        wc, wo = w_conv_out[l].astype(BF16), w_out[l].astype(BF16)
        wk2, pk2 = _pack_w_cmp(w_cmp[l, 0], cmp_pos[l, 0])
        wv2, pv2 = _pack_w_cmp(w_cmp[l, 1], cmp_pos[l, 1])
        g_mix, g_mlp = norm_mix_g[l][None], norm_mlp_g[l][None]
        dw_b, ln_g, ln_b = conv_dw_b[l][None], conv_ln_g[l][None], conv_ln_b[l][None]
        gf = norm_final_g[None]
        final = l == depth - 1

        u, q_t, ckv_t, skv_t, wkv_tm, skr, wkr, gn_t, gm = _inproj_cols(xp, g_mix, w_packed, w_t, cos_p, sin_p,
                                                                        cos_pt, sin_pt, bp, tp, tm_p)
        c_act = _conv_prompt(u.reshape(bp, tp, D_CONV), zeros_conv, conv_dw_w[l], dw_b, ln_g, ln_b)
        ab = _chunkproj_rows(ckv_t.reshape(bp, 2, LANES, tp), wk2, wv2, pk2, pv2)
        o = _attn_prompt(q_t, gn_t, skr, skv_t, wkr, wkv_tm, ab, cos_c, sin_c, bp, tp)
        xp = _post(c_act.reshape(mp, D_CONV), o, gm, xp, wc, wn, wo, g_mlp, wu, wd, gf, l, tm_p, final)
        keep = min(WINDOW, tp)
        outs[0].append(_position_major(ckv_t, tp))
        outs[2].append(_position_major(skv_t, tp))
        outs[4].append(_position_major(wkv_tm[:, :, tp - keep:], keep))
        outs[6].append(u.reshape(bp, tp, D_CONV)[:, tp - (CONV_K - 1):])

        u, q, ckv, skv, wkv, gn, gm = _inproj_rows(xs, g_mix, w_packed, cos_s, sin_s, tm_s)
        u3 = u.reshape(bd, tq, D_CONV)
        c_t = _conv_sample(conv_past, l, jnp.swapaxes(u3, 0, 1), conv_dw_w[l], dw_b, ln_g, ln_b)
        c_act = jnp.swapaxes(c_t, 0, 1).reshape(ms, D_CONV)
        ab_seq = _chunkproj_gather(pt_flat, cmp_pool4, wk2, wv2, pk2, pv2, l * n_pool)
        pad_q = lambda a: jnp.pad(a.reshape(bd, tq, -1), ((0, 0), (0, Q_ROWS - tq), (0, 0)))
        o = _attn_sample(pt_flat, pad_q(q), pad_q(gn), pad_q(skv), pad_q(wkv), win4, cos_c, sin_c, sel_pool4,
                         ab_seq.reshape(bd, n_pages * CHUNKS_PER_PAGE, 4 * LANES), l, n_pool, bd, past_len, tq)
        xs = _post(c_act, o[:, :tq].reshape(ms, Q_EXP), gm, xs, wc, wn_exp, wo, g_mlp, wu, wd, gf, l, tm_s, final)
        wkv5 = wkv.reshape(bd, tq, 2, N_KV, HEAD_DIM)
        outs[1].append(ckv.reshape(bd, tq, 2, N_KV, HEAD_DIM))
        outs[3].append(skv.reshape(bd, tq, 2, N_KV, HEAD_DIM))
        outs[5].append(wkv5)
        outs[7].append(u3)

    y_prompt = xp.reshape(bp, tp, D_MODEL)
    y_sample = xs.reshape(bd, tq, D_MODEL)
    outs = [jnp.stack(o) for o in outs]
    win_new = _shift_append(win4, _feature_major(outs[5]).reshape(depth * bd, 2, LANES, tq), 8)
    outs[5] = _position_major(win_new.reshape(depth * bd, KV_COLS, win_buf), win_buf).reshape(
        depth, bd, win_buf, 2, N_KV, HEAD_DIM)
    outs[7] = jnp.concatenate([state_conv[:, :, tq:], outs[7]], axis=2)
    return (y_prompt, y_sample) + tuple(outs)
```

```python
import functools

import jax
import jax.numpy as jnp
from jax import lax
from jax.experimental import pallas as pl
from jax.experimental.pallas import tpu as pltpu

D_MODEL = 1024
D_CONV = D_MODEL // 2
CONV_K = 31
N_HEADS = 8
HEAD_DIM = 64
N_KV = 2
HPG = N_HEADS // N_KV
KV_COLS = 2 * N_KV * HEAD_DIM
CMP_BLOCK = 32
CMP_STRIDE = 16
SEL_BLOCK = 64
N_SEL = 16
WINDOW = 512
D_FF = 4 * D_MODEL
ROPE_THETA = 10000.0
EPS = 1e-6
PAGE_SIZE = 128

LANES = 128
SUBLANES = 8
Q_EXP = N_HEADS * LANES
N_GATES = 3 * N_HEADS
CHUNKS_PER_PAGE = PAGE_SIZE // CMP_STRIDE
NEG = -1e30

O_GLU = 0
O_Q = O_GLU + 2 * D_CONV
O_CKV = O_Q + Q_EXP
O_SKV = O_CKV + KV_COLS
O_WKV = O_SKV + KV_COLS
O_GN = O_WKV + KV_COLS
O_GM = O_GN + LANES
IN_COLS_PACKED = O_GM + 2 * D_MODEL

VMEM_LIMIT = 56 * 1024 * 1024

F32 = jnp.float32
BF16 = jnp.bfloat16


def _params(n_axes, vmem=VMEM_LIMIT):
    return pltpu.CompilerParams(dimension_semantics=("arbitrary",) * n_axes, vmem_limit_bytes=vmem)


def _sigmoid(x):
    return 1.0 / (1.0 + jnp.exp(-x))


def _dot(a, b):
    return jnp.dot(a, b, preferred_element_type=F32)


def _dot_nt(a, b):
    return lax.dot_general(a, b, (((1,), (1,)), ((), ())), preferred_element_type=F32)


def _rope_slab(xs, cos, sin_signed):
    lane = lax.broadcasted_iota(jnp.int32, xs.shape, 1)
    first = (lane % HEAD_DIM) < (HEAD_DIM // 2)
    rot = jnp.where(first, pltpu.roll(xs, LANES - HEAD_DIM // 2, 1), pltpu.roll(xs, HEAD_DIM // 2, 1))
    return xs * cos + rot * sin_signed


def _rope_rows(xt, cos_t, sin_t):
    half = HEAD_DIM // 2
    rot = jnp.concatenate([xt[half:2 * half], xt[0:half], xt[3 * half:4 * half], xt[2 * half:3 * half]], axis=0)
    return xt * cos_t + rot * sin_t


def _pick_tile(n, cap, mult=8):
    t = min(n, cap)
    while n % t or t % mult:
        t -= 1
    return t


def _ln_swish(c, lg, lb):
    mu = jnp.mean(c, axis=-1, keepdims=True)
    d = c - mu
    var = jnp.mean(d * d, axis=-1, keepdims=True)
    y = d * lax.rsqrt(var + EPS) * lg + lb
    return y * _sigmoid(y)


CONV_PAD = 32
CONV_CHUNK = 64


def _rms_bf16(x, g):
    ms = jnp.mean(x * x, axis=-1, keepdims=True)
    return (x * lax.rsqrt(ms + EPS) * g).astype(BF16)


def _inproj_common(h, w_ref, cos, sin, u_ref, q_ref, gn_ref, gm_ref):
    def proj(a, n):
        return _dot(h, w_ref[:, a:a + n])

    u_ref[...] = proj(O_GLU, D_CONV) * _sigmoid(proj(O_GLU + D_CONV, D_CONV))
    for s in range(N_HEADS):
        q_ref[:, s * LANES:(s + 1) * LANES] = _rope_slab(proj(O_Q + s * LANES, LANES), cos, sin).astype(BF16)
    gn_ref[...] = _sigmoid(proj(O_GN, LANES))
    gm_ref[...] = _sigmoid(proj(O_GM, 2 * D_MODEL)).astype(BF16)
    return proj


def _inproj_rows_kernel(x_ref, g_ref, w_ref, cos_ref, sin_ref,
                        u_ref, q_ref, ckv_ref, skv_ref, wkv_ref, gn_ref, gm_ref):
    h = _rms_bf16(x_ref[...], g_ref[...])
    cos = cos_ref[...]
    sin = sin_ref[...]
    proj = _inproj_common(h, w_ref, cos, sin, u_ref, q_ref, gn_ref, gm_ref)
    ckv_ref[...] = proj(O_CKV, KV_COLS)
    skv_ref[:, 0:LANES] = _rope_slab(proj(O_SKV, LANES), cos, sin)
    skv_ref[:, LANES:KV_COLS] = proj(O_SKV + LANES, LANES)
    wkv_ref[:, 0:LANES] = _rope_slab(proj(O_WKV, LANES), cos, sin)
    wkv_ref[:, LANES:KV_COLS] = proj(O_WKV + LANES, LANES)


GATE_ROWS = 32
R_Q = 0
R_CKV = R_Q + N_HEADS * HEAD_DIM
R_SKV = R_CKV + KV_COLS
R_WKV = R_SKV + KV_COLS
R_GN = R_WKV + KV_COLS
ROWS_T = R_GN + GATE_ROWS


def _conv_tile(uf_ref, w_ref, bias, lg, lb, c_ref, n_rows):
    off = CONV_PAD - (CONV_K - 1)
    ct = CONV_CHUNK
    n_win = ct + CONV_PAD
    for c in range(n_rows // ct):
        xw = uf_ref[c * ct:c * ct + n_win, :]
        acc = jnp.zeros((ct, D_CONV), F32)
        for r in range(SUBLANES):
            yr = xw if r == 0 else pltpu.roll(xw, n_win - r, 0)
            for a in range((CONV_PAD // SUBLANES) + 1):
                k = SUBLANES * a + r - off
                if 0 <= k < CONV_K:
                    acc = acc + w_ref[k:k + 1, :] * yr[SUBLANES * a:SUBLANES * a + ct, :]
        c_ref[c * ct:(c + 1) * ct, :] = _ln_swish(acc + bias, lg, lb).astype(BF16)


def _inproj_cols_kernel(x_ref, g_ref, w_ref, wt_ref, cos_ref, sin_ref, cost_ref, sint_ref, dw_ref, db_ref, lg_ref, lb_ref,
                        u_ref, q_ref, ckv_ref, skv_ref, wkv_ref, skr_ref, wkr_ref, gn_ref, gm_ref, c_ref,
                        uf_ref, carry_ref, *, tiles_per_seq):
    h = _rms_bf16(x_ref[...], g_ref[...])
    cos = cos_ref[...]
    sin = sin_ref[...]
    cos_t = cost_ref[...]
    sin_t = sint_ref[...]

    def proj(a, n):
        return _dot(h, w_ref[:, a:a + n])

    t_all = _dot_nt(wt_ref[...], h)

    def proj_t(a, n):
        return t_all[a:a + n]

    tm = x_ref.shape[0]
    u = proj(O_GLU, D_CONV) * _sigmoid(proj(O_GLU + D_CONV, D_CONV))
    u_ref[...] = u
    first = pl.program_id(0) % tiles_per_seq == 0

    @pl.when(first)
    def _():
        uf_ref[0:CONV_PAD, :] = jnp.zeros((CONV_PAD, D_CONV), F32)

    @pl.when(jnp.logical_not(first))
    def _():
        uf_ref[0:CONV_PAD, :] = carry_ref[...]

    uf_ref[CONV_PAD:CONV_PAD + tm, :] = u
    carry_ref[...] = u[tm - CONV_PAD:tm, :]
    _conv_tile(uf_ref, dw_ref, db_ref[...], lg_ref[...], lb_ref[...], c_ref, tm)
    gm_ref[...] = _sigmoid(proj(O_GM, 2 * D_MODEL)).astype(BF16)
    skr_ref[...] = _rope_slab(proj(O_SKV, LANES), cos, sin).astype(BF16)
    wkr_ref[...] = _rope_slab(proj(O_WKV, LANES), cos, sin).astype(BF16)
    for s in range(N_HEADS * HEAD_DIM // LANES):
        q_ref[0, s * LANES:(s + 1) * LANES, :] = _rope_rows(proj_t(R_Q + s * LANES, LANES), cos_t, sin_t).astype(BF16)
    ckv_ref[0] = proj_t(R_CKV, KV_COLS)
    skv_ref[0, 0:LANES] = _rope_rows(proj_t(R_SKV, LANES), cos_t, sin_t)
    skv_ref[0, LANES:KV_COLS] = proj_t(R_SKV + LANES, LANES)
    wkv_ref[0, 0:LANES] = _rope_rows(proj_t(R_WKV, LANES), cos_t, sin_t)
    wkv_ref[0, LANES:KV_COLS] = proj_t(R_WKV + LANES, LANES)
    gn_ref[0] = _sigmoid(proj_t(R_GN, GATE_ROWS))


def _inproj_rows(x2, g, w_packed, cos_t, sin_t, tm):
    m = x2.shape[0]
    row = lambda i: (i, 0)
    const = lambda i: (0, 0)
    out_shapes = (
        jax.ShapeDtypeStruct((m, D_CONV), F32),
        jax.ShapeDtypeStruct((m, Q_EXP), BF16),
        jax.ShapeDtypeStruct((m, KV_COLS), F32),
        jax.ShapeDtypeStruct((m, KV_COLS), F32),
        jax.ShapeDtypeStruct((m, KV_COLS), F32),
        jax.ShapeDtypeStruct((m, LANES), F32),
        jax.ShapeDtypeStruct((m, 2 * D_MODEL), BF16),
    )
    return pl.pallas_call(
        _inproj_rows_kernel,
        grid=(m // tm,),
        in_specs=[
            pl.BlockSpec((tm, D_MODEL), row),
            pl.BlockSpec((1, D_MODEL), const),
            pl.BlockSpec(w_packed.shape, const),
            pl.BlockSpec((tm, LANES), const),
            pl.BlockSpec((tm, LANES), const),
        ],
        out_specs=tuple(pl.BlockSpec((tm, s.shape[1]), row) for s in out_shapes),
        out_shape=out_shapes,
        compiler_params=_params(1),
    )(x2, g, w_packed, cos_t, sin_t)


def _inproj_cols(x2, g, w_packed, wkv_t, cos_t, sin_t, cos_tt, sin_tt, dw_w, dw_b, ln_g, ln_b, b, t_len, tm):
    m = x2.shape[0]
    n_tab = t_len // tm
    row = lambda i: (i, 0)
    const = lambda i: (0, 0)
    kv_map = lambda i: (i // n_tab, 0, i % n_tab)
    kv_shape = jax.ShapeDtypeStruct((b, KV_COLS, t_len), F32)
    out_shapes = (
        jax.ShapeDtypeStruct((m, D_CONV), F32),
        jax.ShapeDtypeStruct((b, N_HEADS * HEAD_DIM, t_len), BF16),
        kv_shape, kv_shape, kv_shape,
        jax.ShapeDtypeStruct((m, LANES), BF16),
        jax.ShapeDtypeStruct((m, LANES), BF16),
        jax.ShapeDtypeStruct((b, GATE_ROWS, t_len), F32),
        jax.ShapeDtypeStruct((m, 2 * D_MODEL), BF16),
        jax.ShapeDtypeStruct((m, D_CONV), BF16),
    )
    out_specs = (
        pl.BlockSpec((tm, D_CONV), row),
        pl.BlockSpec((1, N_HEADS * HEAD_DIM, tm), kv_map),
        pl.BlockSpec((1, KV_COLS, tm), kv_map),
        pl.BlockSpec((1, KV_COLS, tm), kv_map),
        pl.BlockSpec((1, KV_COLS, tm), kv_map),
        pl.BlockSpec((tm, LANES), row),
        pl.BlockSpec((tm, LANES), row),
        pl.BlockSpec((1, GATE_ROWS, tm), kv_map),
        pl.BlockSpec((tm, 2 * D_MODEL), row),
        pl.BlockSpec((tm, D_CONV), row),
    )
    return pl.pallas_call(
        functools.partial(_inproj_cols_kernel, tiles_per_seq=n_tab),
        grid=(m // tm,),
        in_specs=[
            pl.BlockSpec((tm, D_MODEL), row),
            pl.BlockSpec((1, D_MODEL), const),
            pl.BlockSpec(w_packed.shape, const),
            pl.BlockSpec(wkv_t.shape, const),
            pl.BlockSpec((tm, LANES), lambda i: (i % n_tab, 0)),
            pl.BlockSpec((tm, LANES), lambda i: (i % n_tab, 0)),
            pl.BlockSpec((LANES, tm), lambda i: (0, i % n_tab)),
            pl.BlockSpec((LANES, tm), lambda i: (0, i % n_tab)),
            pl.BlockSpec((CONV_K, D_CONV), const),
            pl.BlockSpec((1, D_CONV), const),
            pl.BlockSpec((1, D_CONV), const),
            pl.BlockSpec((1, D_CONV), const),
        ],
        out_specs=out_specs,
        out_shape=out_shapes,
        scratch_shapes=[pltpu.VMEM((tm + CONV_PAD, D_CONV), F32), pltpu.VMEM((CONV_PAD, D_CONV), F32)],
        compiler_params=_params(1),
    )(x2, g, w_packed, wkv_t, cos_t, sin_t, cos_tt, sin_tt, dw_w, dw_b, ln_g, ln_b)


def _conv_sample_kernel(past_ref, u_ref, w_ref, b_ref, lg_ref, lb_ref, o_ref, *, tq):
    bias = b_ref[...]
    lg = lg_ref[...]
    lb = lb_ref[...]
    n_past = CONV_K - 1
    for t in range(tq):
        acc = jnp.zeros(o_ref.shape[1:], F32)
        for j in range(t, n_past):
            acc = acc + w_ref[j - t:j - t + 1, :] * past_ref[j]
        for i in range(t + 1):
            k = n_past - t + i
            acc = acc + w_ref[k:k + 1, :] * u_ref[i]
        o_ref[t] = _ln_swish(acc + bias, lg, lb).astype(BF16)


def _conv_sample(past_t, layer, u_t, dw_w, dw_b, ln_g, ln_b):
    n_past, bd = CONV_K - 1, past_t.shape[1]
    tq = u_t.shape[0]
    bt = _pick_tile(bd, 32)
    const2 = lambda i: (0, 0)
    return pl.pallas_call(
        functools.partial(_conv_sample_kernel, tq=tq),
        grid=(bd // bt,),
        in_specs=[
            pl.BlockSpec((n_past, bt, D_CONV), lambda i: (layer, i, 0)),
            pl.BlockSpec((tq, bt, D_CONV), lambda i: (0, i, 0)),
            pl.BlockSpec((CONV_K, D_CONV), const2),
            pl.BlockSpec((1, D_CONV), const2),
            pl.BlockSpec((1, D_CONV), const2),
            pl.BlockSpec((1, D_CONV), const2),
        ],
        out_specs=pl.BlockSpec((tq, bt, D_CONV), lambda i: (0, i, 0)),
        out_shape=jax.ShapeDtypeStruct((tq, bd, D_CONV), BF16),
        compiler_params=_params(1),
    )(past_t, u_t, dw_w, dw_b, ln_g, ln_b)


def _chunkproj_pages(get_page, n_pages, wk_ref, wv_ref, pk_ref, pv_ref, ab_ref, t_ref):
    n_rows = n_pages * CHUNKS_PER_PAGE
    for kv, (w_ref, p_ref) in enumerate(((wk_ref, pk_ref), (wv_ref, pv_ref))):
        for p in range(n_pages):
            t_ref[p * PAGE_SIZE:(p + 1) * PAGE_SIZE, :] = get_page(p, kv).T
        xs = jnp.concatenate([t_ref[pl.ds(l, n_rows, stride=CMP_STRIDE), :] for l in range(CMP_STRIDE)],
                             axis=1).astype(BF16)
        w = w_ref[...]
        part = _dot(xs, w)
        posb = _dot(p_ref[...], w)
        c0 = kv * 2 * LANES
        ab_ref[:, c0:c0 + LANES] = part[:, 0:LANES] + posb[0:1, 0:LANES]
        ab_ref[:, c0 + LANES:c0 + 2 * LANES] = part[:, LANES:2 * LANES] + posb[1:2, LANES:2 * LANES]


def _chunkproj_rows_kernel(x_ref, wk_ref, wv_ref, pk_ref, pv_ref, ab_ref, t_ref, *, n_pages):
    get_page = lambda p, kv: x_ref[0, kv, :, p * PAGE_SIZE:(p + 1) * PAGE_SIZE]
    _chunkproj_pages(get_page, n_pages, wk_ref, wv_ref, pk_ref, pv_ref, ab_ref, t_ref)


def _chunkproj_gather_kernel(pt_ref, pool_hbm, wk_ref, wv_ref, pk_ref, pv_ref, ab_ref, x_buf, t_ref, sem,
                             *, n_pages, page_base):
    step = pl.program_id(0)
    n_steps = pl.num_programs(0)
    slot = step % 2

    def page_copies(for_step, to_slot):
        return [pltpu.make_async_copy(pool_hbm.at[page_base + pt_ref[for_step * n_pages + j]], x_buf.at[to_slot, j],
                                      sem.at[to_slot, j]) for j in range(n_pages)]

    @pl.when(step == 0)
    def _():
        for c in page_copies(0, 0):
            c.start()

    @pl.when(step + 1 < n_steps)
    def _():
        for c in page_copies(step + 1, 1 - slot):
            c.start()

    for c in page_copies(step, slot):
        c.wait()
    _chunkproj_pages(lambda p, kv: x_buf[slot, p, kv], n_pages, wk_ref, wv_ref, pk_ref, pv_ref, ab_ref, t_ref)


def _chunkproj_rows(x4, wk2, wv2, pk2, pv2):
    n_pages = x4.shape[3] // PAGE_SIZE
    const2 = lambda i: (0, 0)
    rows = n_pages * CHUNKS_PER_PAGE
    return pl.pallas_call(
        functools.partial(_chunkproj_rows_kernel, n_pages=n_pages),
        grid=(x4.shape[0],),
        in_specs=[
            pl.BlockSpec((1, 2, LANES, x4.shape[3]), lambda i: (i, 0, 0, 0)),
            pl.BlockSpec(wk2.shape, const2),
            pl.BlockSpec(wv2.shape, const2),
            pl.BlockSpec(pk2.shape, const2),
            pl.BlockSpec(pv2.shape, const2),
        ],
        out_specs=pl.BlockSpec((rows, 4 * LANES), lambda i: (i, 0)),
        out_shape=jax.ShapeDtypeStruct((x4.shape[0] * rows, 4 * LANES), F32),
        scratch_shapes=[pltpu.VMEM((n_pages * PAGE_SIZE, LANES), F32)],
        compiler_params=_params(1),
    )(x4, wk2, wv2, pk2, pv2)


GATHER_PAGES = 64


def _chunkproj_gather(pt_flat, pool4, wk2, wv2, pk2, pv2, page_base):
    n_total = pt_flat.shape[0]
    n_pages = _pick_tile(n_total, GATHER_PAGES, 1)
    const2 = lambda i, pt: (0, 0)
    rows = n_pages * CHUNKS_PER_PAGE
    grid_spec = pltpu.PrefetchScalarGridSpec(
        num_scalar_prefetch=1,
        grid=(n_total // n_pages,),
        in_specs=[
            pl.BlockSpec(memory_space=pl.ANY),
            pl.BlockSpec(wk2.shape, const2),
            pl.BlockSpec(wv2.shape, const2),
            pl.BlockSpec(pk2.shape, const2),
            pl.BlockSpec(pv2.shape, const2),
        ],
        out_specs=pl.BlockSpec((rows, 4 * LANES), lambda i, pt: (i, 0)),
        scratch_shapes=[
            pltpu.VMEM((2, n_pages, 2, LANES, PAGE_SIZE), F32),
            pltpu.VMEM((n_pages * PAGE_SIZE, LANES), F32),
            pltpu.SemaphoreType.DMA((2, n_pages)),
        ],
    )
    return pl.pallas_call(
        functools.partial(_chunkproj_gather_kernel, n_pages=n_pages, page_base=page_base),
        grid_spec=grid_spec,
        out_shape=jax.ShapeDtypeStruct((n_total * CHUNKS_PER_PAGE, 4 * LANES), F32),
        compiler_params=_params(1),
    )(pt_flat, pool4, wk2, wv2, pk2, pv2)


def _compressed_kv_f32(ab, cosc, sinc):
    n = ab.shape[0]
    kc = ab[:, 0:LANES] + pltpu.roll(ab[:, LANES:2 * LANES], n - 1, 0)
    vc = ab[:, 2 * LANES:3 * LANES] + pltpu.roll(ab[:, 3 * LANES:4 * LANES], n - 1, 0)
    return _rope_slab(kc, cosc, sinc), vc


def _compressed_kv(ab, cosc, sinc):
    kc, vc = _compressed_kv_f32(ab, cosc, sinc)
    return kc.astype(BF16), vc.astype(BF16)


def _softmax_parts(s, mask):
    sm = jnp.where(mask, s, NEG)
    m = jnp.max(sm, axis=-1, keepdims=True)
    p = jnp.where(mask, jnp.exp(sm - m), 0.0)
    l = jnp.maximum(jnp.sum(p, axis=-1, keepdims=True), 1e-30)
    return p, l


def _select_blocks(imp, qpos, n_sel, axis):
    jidx = lax.broadcasted_iota(jnp.int32, imp.shape, axis)
    cur = lax.shift_right_logical(qpos, 6)
    forced = (jidx == 0) | (jidx == cur) | (jidx == cur - 1)
    valid = (jidx * SEL_BLOCK <= qpos) & (jidx < n_sel)
    v = jnp.where(valid, jnp.where(forced, jnp.inf, imp), -jnp.inf)
    rank = jnp.zeros(imp.shape, jnp.int32)
    for k in range(n_sel):
        vk = v[k:k + 1, :] if axis == 0 else v[:, k:k + 1]
        ahead = (vk > v) | ((vk == v) & (jidx > k))
        rank = rank + ahead.astype(jnp.int32)
    return (rank < min(N_SEL, n_sel)) & valid


def _split_bf16(x):
    hi = x.astype(BF16)
    r1 = x - hi.astype(F32)
    mid = r1.astype(BF16)
    lo = (r1 - mid.astype(F32)).astype(BF16)
    return hi, mid, lo


def _overlap(n_cmp, cmp_axis):
    i = lax.broadcasted_iota(jnp.int32, (LANES, LANES), cmp_axis)
    j = lax.broadcasted_iota(jnp.int32, (LANES, LANES), 1 - cmp_axis)
    hit = (i * CMP_STRIDE < (j + 1) * SEL_BLOCK) & (i * CMP_STRIDE + CMP_BLOCK > j * SEL_BLOCK) & (i < n_cmp)
    return hit.astype(F32)


Q_TILE = 256
K_TILE = 256


def _attn_prompt_kernel(q_ref, gn_ref, skr_ref, svt_ref, wkr_ref, wvt_ref, ab_ref, cosc_ref, sinc_ref, o_ref,
                        kc_ref, vct_ref, bias_ref, m_ref, acc_ref, accb_ref, out_ref, *, t_len):
    qb = pl.program_id(1)
    n_chunk = t_len // CMP_STRIDE
    n_cmp = (t_len - CMP_BLOCK) // CMP_STRIDE + 1
    n_sel = -(-t_len // SEL_BLOCK)
    sel_rows = -(-n_sel // SUBLANES) * SUBLANES
    qt, kt_sz = Q_TILE, K_TILE
    q_tiles = qt // LANES
    g_lanes = HPG * qt
    n_lane_tiles = N_HEADS * q_tiles

    @pl.when(qb == 0)
    def _():
        kc, vc = _compressed_kv_f32(ab_ref[...], cosc_ref[...], sinc_ref[...])
        kc_ref[...] = kc.astype(BF16)
        vct_ref[...] = vc.T.astype(BF16)

    q0 = qb * qt
    qpos = q0 + lax.broadcasted_iota(jnp.int32, (1, qt), 1)

    def lane_tile(c):
        return slice(c * LANES, (c + 1) * LANES)

    def q_part(c):
        return slice((c % q_tiles) * LANES, (c % q_tiles + 1) * LANES)

    def group_of(c):
        return c // (HPG * q_tiles)

    def group_lanes(g):
        return slice(g * g_lanes, (g + 1) * g_lanes)

    q_zero = jnp.zeros((HEAD_DIM, qt), BF16)

    def q_slab(hh):
        q_h = q_ref[0, hh * HEAD_DIM:(hh + 1) * HEAD_DIM, :]
        return jnp.concatenate([q_h, q_zero] if hh < HPG else [q_zero, q_h], axis=0)

    q_all = jnp.concatenate([q_slab(hh) for hh in range(N_HEADS)], axis=1)

    def gates(branch):
        r0 = branch * N_HEADS
        return jnp.concatenate([gn_ref[0, r0 + hh:r0 + hh + 1, :] for hh in range(N_HEADS)], axis=1)

    nrow = lax.broadcasted_iota(jnp.int32, (n_chunk, qt), 0)
    mask_c = (nrow * CMP_STRIDE + CMP_BLOCK - 1 <= qpos) & (nrow < n_cmp)
    acc_ref[...] = _dot(kc_ref[...], q_all)
    hs = [[jnp.zeros((n_chunk, LANES), F32) for _ in range(q_tiles)] for _ in range(N_KV)]
    p_parts = []
    for c in range(n_lane_tiles):
        mask = mask_c[:, q_part(c)]
        sm = jnp.where(mask, acc_ref[:, lane_tile(c)], NEG)
        p_c = jnp.where(mask, jnp.exp(sm - jnp.max(sm, axis=0, keepdims=True)), 0.0)
        p_c = p_c / jnp.maximum(jnp.sum(p_c, axis=0, keepdims=True), 1e-30)
        hs[group_of(c)][c % q_tiles] = hs[group_of(c)][c % q_tiles] + p_c
        p_parts.append(p_c.astype(BF16))
    out_ref[...] = gates(0) * _dot(vct_ref[...], jnp.concatenate(p_parts, axis=1))

    sel_t = []
    for g in range(N_KV):
        ov_t = _overlap(n_cmp, 1).astype(BF16)
        imp_t = sum(_dot(ov_t, part) for part in _split_bf16(jnp.concatenate(hs[g], axis=1)))
        s_g = _select_blocks(imp_t[0:sel_rows], qpos, n_sel, 0).astype(F32)
        sel_t.append(jnp.concatenate([s_g, jnp.zeros((LANES - sel_rows, qt), F32)], axis=0).astype(BF16))

    m_ref[...] = jnp.full(m_ref.shape, NEG, F32)
    accb_ref[...] = jnp.zeros(accb_ref.shape, F32)

    def tile_step(br, kr_ref, vt_ref, k_idx, bias_fn):
        k0 = pl.multiple_of(k_idx * kt_sz, kt_sz)
        kpos = k0 + lax.broadcasted_iota(jnp.int32, (kt_sz, qt), 0)
        bias_fn(k_idx, kpos)
        s = _dot(kr_ref[pl.ds(k0, kt_sz), :], q_all)
        vt = vt_ref[0, :, pl.ds(k0, kt_sz)]
        vrow = lax.broadcasted_iota(jnp.int32, (LANES, kt_sz), 0)
        p_parts, a_parts = [], []
        for c in range(n_lane_tiles):
            sb = s[:, lane_tile(c)] + bias_ref[br, group_of(c), :, q_part(c)]
            m_old = m_ref[br, 0:1, lane_tile(c)]
            m_new = jnp.maximum(m_old, jnp.max(sb, axis=0, keepdims=True))
            p_parts.append(jnp.exp(sb - m_new).astype(BF16))
            a_parts.append(jnp.exp(m_old - m_new))
            m_ref[br, :, lane_tile(c)] = jnp.broadcast_to(m_new, (SUBLANES, LANES))
        for g in range(N_KV):
            v_own = (vrow >= g * HEAD_DIM) & (vrow < (g + 1) * HEAD_DIM)
            vaug = jnp.where(v_own, vt, 1.0).astype(BF16)
            tiles = range(g * HPG * q_tiles, (g + 1) * HPG * q_tiles)
            alpha = jnp.concatenate([a_parts[c] for c in tiles], axis=1)
            p = jnp.concatenate([p_parts[c] for c in tiles], axis=1)
            accb_ref[br, :, group_lanes(g)] = alpha * accb_ref[br, :, group_lanes(g)] + _dot(vaug, p)

    def sel_bias(k_idx, kpos):
        krow = lax.broadcasted_iota(jnp.int32, (kt_sz, LANES), 0)
        jcol = lax.broadcasted_iota(jnp.int32, (kt_sz, LANES), 1)
        expand_t = (jcol == k_idx * (kt_sz // SEL_BLOCK) + lax.shift_right_logical(krow, 6)).astype(BF16)
        for g in range(N_KV):
            bias_ref[0, g] = jnp.where((_dot(expand_t, sel_t[g]) > 0.5) & (kpos <= qpos), 0.0, NEG)

    def win_bias(k_idx, kpos):
        b = jnp.where((kpos <= qpos) & (kpos > qpos - WINDOW), 0.0, NEG)
        for g in range(N_KV):
            bias_ref[1, g] = b

    n_sel_tiles = qb + 1
    n_win_tiles = jnp.minimum(qb, WINDOW // kt_sz) + 1

    def both(i, carry):
        tile_step(0, skr_ref, svt_ref, i, sel_bias)
        tile_step(1, wkr_ref, wvt_ref, qb - i, win_bias)
        return carry

    def sel_only(i, carry):
        tile_step(0, skr_ref, svt_ref, i, sel_bias)
        return carry

    lax.fori_loop(0, n_win_tiles, both, 0)
    lax.fori_loop(n_win_tiles, n_sel_tiles, sel_only, 0)

    for br in range(2):
        gt = gates(br + 1)
        for g in range(N_KV):
            sum_row = (1 - g) * HEAD_DIM
            acc = accb_ref[br, :, group_lanes(g)]
            out_ref[:, group_lanes(g)] = (out_ref[:, group_lanes(g)]
                                          + gt[:, group_lanes(g)] * (acc / acc[sum_row:sum_row + 1, :]))

    lane = lax.broadcasted_iota(jnp.int32, (qt, LANES), 1)
    for pair in range(N_HEADS // 2):
        a = out_ref[:, (2 * pair) * qt:(2 * pair + 1) * qt].T
        b = out_ref[:, (2 * pair + 1) * qt:(2 * pair + 2) * qt].T
        if 2 * pair < HPG:
            slab = jnp.where(lane < HEAD_DIM, a, pltpu.roll(b, HEAD_DIM, 1))
        else:
            slab = jnp.where(lane < HEAD_DIM, pltpu.roll(a, HEAD_DIM, 1), b)
        o_ref[:, pair * LANES:(pair + 1) * LANES] = slab.astype(BF16)


def _attn_prompt(q_t, gn_t, skr, skv_t, wkr, wkv_t, ab2, cosc, sinc, b, t_len):
    n_chunk = t_len // CMP_STRIDE
    nqb = t_len // Q_TILE
    const2 = lambda i, j: (0, 0)
    per_b2 = lambda i, j: (i, 0)
    v_rows = lambda i, j: (i, 1, 0)
    return pl.pallas_call(
        functools.partial(_attn_prompt_kernel, t_len=t_len),
        grid=(b, nqb),
        in_specs=[
            pl.BlockSpec((1, N_HEADS * HEAD_DIM, Q_TILE), lambda i, j: (i, 0, j)),
            pl.BlockSpec((1, GATE_ROWS, Q_TILE), lambda i, j: (i, 0, j)),
            pl.BlockSpec((t_len, LANES), per_b2),
            pl.BlockSpec((1, LANES, t_len), v_rows),
            pl.BlockSpec((t_len, LANES), per_b2),
            pl.BlockSpec((1, LANES, t_len), v_rows),
            pl.BlockSpec((n_chunk, 4 * LANES), per_b2),
            pl.BlockSpec((n_chunk, LANES), const2),
            pl.BlockSpec((n_chunk, LANES), const2),
        ],
        out_specs=pl.BlockSpec((Q_TILE, N_HEADS * HEAD_DIM), lambda i, j: (i * nqb + j, 0)),
        out_shape=jax.ShapeDtypeStruct((b * t_len, N_HEADS * HEAD_DIM), BF16),
        scratch_shapes=[
            pltpu.VMEM((n_chunk, LANES), BF16),
            pltpu.VMEM((LANES, n_chunk), BF16),
            pltpu.VMEM((2, N_KV, K_TILE, Q_TILE), F32),
            pltpu.VMEM((2, SUBLANES, N_HEADS * Q_TILE), F32),
            pltpu.VMEM((LANES, N_HEADS * Q_TILE), F32),
            pltpu.VMEM((2, LANES, N_HEADS * Q_TILE), F32),
            pltpu.VMEM((LANES, N_HEADS * Q_TILE), F32),
        ],
        compiler_params=_params(2),
    )(q_t, gn_t, skr, skv_t, wkr, wkv_t, ab2, cosc, sinc)


Q_ROWS = 8
SEQ_PER_STEP = 4


def _attn_sample_kernel(pt_ref, q_ref, gn_ref, sknew_ref, wknew_ref, win_ref, cosc_ref, sinc_ref, ab_ref, sel_hbm,
                        o_ref, sel_buf, sem, *, past_len, tq, n_pages, n_seq, page_base):
    step = pl.program_id(0)
    n_steps = pl.num_programs(0)
    slot = step % 2
    n_slot_pages = n_seq * n_pages

    def page_copies(for_step, to_slot):
        return [pltpu.make_async_copy(sel_hbm.at[page_base + pt_ref[for_step * n_slot_pages + j]],
                                      sel_buf.at[to_slot, j], sem.at[to_slot, j]) for j in range(n_slot_pages)]

    @pl.when(step == 0)
    def _():
        for c in page_copies(0, 0):
            c.start()

    @pl.when(step + 1 < n_steps)
    def _():
        for c in page_copies(step + 1, 1 - slot):
            c.start()

    for c in page_copies(step, slot):
        c.wait()

    def sel_page(e, p, kv):
        return sel_buf[slot, e * n_pages + p, kv]

    seqs = range(n_seq)
    t_all = past_len + tq
    n_chunk = past_len // CMP_STRIDE
    n_cmp = (t_all - CMP_BLOCK) // CMP_STRIDE + 1
    n_sel = -(-t_all // SEL_BLOCK)
    rows = N_HEADS * Q_ROWS
    all_rows = n_seq * rows
    win_buf = win_ref.shape[3]

    def stack(parts):
        return jnp.concatenate(parts, axis=0)

    def per_seq(a, e):
        return a[e * rows:(e + 1) * rows]

    qbd = [stack([q_ref[e, :, hh * LANES:(hh + 1) * LANES] for hh in range(N_HEADS)]) for e in seqs]
    qpos = past_len + lax.broadcasted_iota(jnp.int32, (all_rows, 1), 0) % Q_ROWS

    cosc = cosc_ref[...]
    sinc = sinc_ref[...]
    kvc = [_compressed_kv(ab_ref[e], cosc, sinc) for e in seqs]
    ncol = lax.broadcasted_iota(jnp.int32, (all_rows, n_chunk), 1)
    mask_c = (ncol * CMP_STRIDE + CMP_BLOCK - 1 <= qpos) & (ncol < n_cmp)
    p_c, l_c = _softmax_parts(stack([_dot_nt(qbd[e], kvc[e][0]) for e in seqs]), mask_c)
    p_c = p_c / l_c
    p_c16 = p_c.astype(BF16)
    o_c = stack([_dot(per_seq(p_c16, e), kvc[e][1]) for e in seqs])

    hs = []
    for e in seqs:
        for g in range(N_KV):
            r0 = e * rows + g * HPG * Q_ROWS
            acc = p_c[r0:r0 + Q_ROWS]
            for h in range(1, HPG):
                acc = acc + p_c[r0 + h * Q_ROWS:r0 + (h + 1) * Q_ROWS]
            hs.append(acc)
    ov = _overlap(n_cmp, 0).astype(BF16)
    imp = sum(_dot(part, ov) for part in _split_bf16(stack(hs)))
    sel_g = _select_blocks(imp, qpos[0:n_seq * N_KV * Q_ROWS], n_sel, 1).astype(BF16)
    sel_rows = stack([sel_g[(e * N_KV + g) * Q_ROWS:(e * N_KV + g + 1) * Q_ROWS]
                      for e in seqs for g in range(N_KV) for _ in range(HPG)])

    def new_rows_tile(ref, e, c0):
        return stack([ref[e, :, c0:c0 + LANES], jnp.zeros((LANES - Q_ROWS, LANES), F32)]).astype(BF16)

    n_keys = (n_pages + 1) * PAGE_SIZE
    s_s = stack([jnp.concatenate([_dot(qbd[e], sel_page(e, p, 0).astype(BF16)) for p in range(n_pages)]
                                 + [_dot_nt(qbd[e], new_rows_tile(sknew_ref, e, 0))], axis=1) for e in seqs])
    jrow = lax.broadcasted_iota(jnp.int32, (LANES, n_keys), 0)
    kcol = lax.broadcasted_iota(jnp.int32, (LANES, n_keys), 1)
    expand = (jrow == lax.shift_right_logical(kcol, 6)).astype(BF16)
    kpos = lax.broadcasted_iota(jnp.int32, (all_rows, n_keys), 1)
    mask_s = (_dot(sel_rows, expand) > 0.5) & (kpos <= qpos)
    p_s, l_s = _softmax_parts(s_s, mask_s)
    p_s = p_s.astype(BF16)
    o_s = []
    for e in seqs:
        p_e = per_seq(p_s, e)
        acc = _dot(p_e[:, n_pages * PAGE_SIZE:], new_rows_tile(sknew_ref, e, LANES))
        for i in range(n_pages):
            acc = acc + _dot_nt(p_e[:, i * PAGE_SIZE:(i + 1) * PAGE_SIZE], sel_page(e, i, 1).astype(BF16))
        o_s.append(acc)
    o_s = stack(o_s) / l_s

    s_w = stack([jnp.concatenate([_dot(qbd[e], win_ref[e, 0].astype(BF16)),
                                  _dot_nt(qbd[e], new_rows_tile(wknew_ref, e, 0))], axis=1) for e in seqs])
    wcol = lax.broadcasted_iota(jnp.int32, (all_rows, win_buf + LANES), 1)
    kpos_w = past_len - win_buf + wcol
    mask_w = (kpos_w <= qpos) & (kpos_w > qpos - WINDOW)
    p_w, l_w = _softmax_parts(s_w, mask_w)
    p_w = p_w.astype(BF16)
    o_w = stack([_dot_nt(per_seq(p_w, e)[:, 0:win_buf], win_ref[e, 1].astype(BF16))
                 + _dot(per_seq(p_w, e)[:, win_buf:], new_rows_tile(wknew_ref, e, LANES)) for e in seqs]) / l_w

    for e in seqs:
        gn = gn_ref[e]
        for hh in range(N_HEADS):
            r = slice(e * rows + hh * Q_ROWS, e * rows + (hh + 1) * Q_ROWS)
            out = (gn[:, hh:hh + 1] * o_c[r]
                   + gn[:, N_HEADS + hh:N_HEADS + hh + 1] * o_s[r]
                   + gn[:, 2 * N_HEADS + hh:2 * N_HEADS + hh + 1] * o_w[r])
            o_ref[e, :, hh * LANES:(hh + 1) * LANES] = out.astype(BF16)


def _attn_sample(pt_flat, q3, gn3, sknew3, wknew3, win4, cosc, sinc, sel_pool4, ab_seq3,
                 layer, n_pool, bd, past_len, tq):
    n_pages = past_len // PAGE_SIZE
    win_buf = win4.shape[3]
    n_chunk = past_len // CMP_STRIDE
    n_seq = _pick_tile(bd, SEQ_PER_STEP, 1)
    per_b = lambda i, pt: (i, 0, 0)
    const2 = lambda i, pt: (0, 0)
    n_slot_pages = n_seq * n_pages
    grid_spec = pltpu.PrefetchScalarGridSpec(
        num_scalar_prefetch=1,
        grid=(bd // n_seq,),
        in_specs=[
            pl.BlockSpec((n_seq, Q_ROWS, Q_EXP), per_b),
            pl.BlockSpec((n_seq, Q_ROWS, LANES), per_b),
            pl.BlockSpec((n_seq, Q_ROWS, KV_COLS), per_b),
            pl.BlockSpec((n_seq, Q_ROWS, KV_COLS), per_b),
            pl.BlockSpec((n_seq, 2, LANES, win_buf), lambda i, pt: (layer * (bd // n_seq) + i, 0, 0, 0)),
            pl.BlockSpec((n_chunk, LANES), const2),
            pl.BlockSpec((n_chunk, LANES), const2),
            pl.BlockSpec((n_seq, n_chunk, 4 * LANES), per_b),
            pl.BlockSpec(memory_space=pl.ANY),
        ],
        out_specs=pl.BlockSpec((n_seq, Q_ROWS, Q_EXP), per_b),
        scratch_shapes=[
            pltpu.VMEM((2, n_slot_pages, 2, LANES, PAGE_SIZE), F32),
            pltpu.SemaphoreType.DMA((2, n_slot_pages)),
        ],
    )
    return pl.pallas_call(
        functools.partial(_attn_sample_kernel, past_len=past_len, tq=tq, n_pages=n_pages, n_seq=n_seq,
                          page_base=layer * n_pool),
        grid_spec=grid_spec,
        out_shape=jax.ShapeDtypeStruct((bd, Q_ROWS, Q_EXP), BF16),
        compiler_params=_params(1),
    )(pt_flat, q3, gn3, sknew3, wknew3, win4, cosc, sinc, ab_seq3, sel_pool4)


def _post_kernel(c_ref, o_ref, gm_ref, x_ref, wc_ref, wn_ref, wo_ref, g_ref, wu_ref, wd_ref, gf_ref, y_ref, *, final):
    conv_out = _dot(c_ref[...], wc_ref[...])
    nsa_out = _dot(o_ref[...], wn_ref[...])
    gm = gm_ref[...].astype(F32)
    merged = gm[:, 0:D_MODEL] * conv_out + gm[:, D_MODEL:2 * D_MODEL] * nsa_out
    x = x_ref[...] + _dot(merged.astype(BF16), wo_ref[...])
    h = _rms_bf16(x, g_ref[...])
    acc = x
    for f in range(D_FF // FF_TILE):
        up = jnp.maximum(_dot(h, wu_ref[:, f * FF_TILE:(f + 1) * FF_TILE]), 0.0)
        acc = acc + _dot((up * up).astype(BF16), wd_ref[f * FF_TILE:(f + 1) * FF_TILE, :])
    if final:
        ms2 = jnp.mean(acc * acc, axis=-1, keepdims=True)
        acc = acc * lax.rsqrt(ms2 + EPS) * gf_ref[...]
    y_ref[...] = acc


FF_TILE = 1024


def _post(c2, o2, gm2, x2, wc, wn, wo, g, wu, wd, gf, tm, final):
    m = x2.shape[0]
    row = lambda i: (i, 0)
    const = lambda i: (0, 0)
    resident = lambda a: pl.BlockSpec(a.shape, const, pipeline_mode=pl.Buffered(1))
    return pl.pallas_call(
        functools.partial(_post_kernel, final=final),
        grid=(m // tm,),
        in_specs=[
            pl.BlockSpec((tm, c2.shape[1]), row),
            pl.BlockSpec((tm, o2.shape[1]), row),
            pl.BlockSpec((tm, 2 * D_MODEL), row),
            pl.BlockSpec((tm, D_MODEL), row),
            resident(wc), resident(wn), resident(wo),
            pl.BlockSpec((1, D_MODEL), const),
            resident(wu), resident(wd),
            pl.BlockSpec((1, D_MODEL), const),
        ],
        out_specs=pl.BlockSpec((tm, D_MODEL), row),
        out_shape=jax.ShapeDtypeStruct((m, D_MODEL), F32),
        compiler_params=_params(1),
    )(c2, o2, gm2, x2, wc, wn, wo, g, wu, wd, gf)


def _shift_append_kernel(old_ref, new_ref, o_ref, *, n_new):
    o_ref[...] = jnp.concatenate([old_ref[..., n_new:], new_ref[...]], axis=-1)


def _shift_append(old, new, rows_per_step):
    n, a, b, keep = old.shape
    n_new = new.shape[-1]
    r = _pick_tile(n, rows_per_step, 1)
    return pl.pallas_call(
        functools.partial(_shift_append_kernel, n_new=n_new),
        grid=(n // r,),
        in_specs=[pl.BlockSpec((r, a, b, keep), lambda i: (i, 0, 0, 0)),
                  pl.BlockSpec((r, a, b, n_new), lambda i: (i, 0, 0, 0))],
        out_specs=pl.BlockSpec((r, a, b, keep), lambda i: (i, 0, 0, 0)),
        out_shape=jax.ShapeDtypeStruct(old.shape, old.dtype),
        compiler_params=_params(1),
    )(old, new)


def _rope_tables(pos):
    half = HEAD_DIM // 2
    inv = jnp.power(ROPE_THETA, -jnp.arange(half, dtype=F32) / half)
    ang = pos.astype(F32)[:, None] * inv[None, :]
    cos = jnp.cos(ang)
    sin = jnp.sin(ang)
    return jnp.concatenate([cos, cos, cos, cos], axis=1), jnp.concatenate([-sin, sin, -sin, sin], axis=1)


def _pack_w_in(w):
    o0 = 2 * D_CONV
    o1 = o0 + N_HEADS * HEAD_DIM
    o4 = o1 + 3 * KV_COLS
    o5 = o4 + N_GATES
    wq = w[:, o0:o1].reshape(D_MODEL, N_HEADS, HEAD_DIM) * (HEAD_DIM ** -0.5)
    zero = jnp.zeros_like(wq)
    in_g0 = (jnp.arange(N_HEADS) < HPG)[None, :, None]
    wq_exp = jnp.concatenate([jnp.where(in_g0, wq, zero), jnp.where(in_g0, zero, wq)], axis=2)
    wgn = jnp.pad(w[:, o4:o5], ((0, 0), (0, LANES - N_GATES)))
    wq_exp = wq_exp.reshape(D_MODEL, Q_EXP)
    packed = jnp.concatenate([w[:, :o0], wq_exp, w[:, o1:o4], wgn, w[:, o5:]], axis=1)
    w_t = jnp.concatenate([wq.reshape(D_MODEL, N_HEADS * HEAD_DIM), w[:, o1:o4], wgn[:, :GATE_ROWS]], axis=1).T
    return packed.astype(BF16), w_t.astype(BF16)


def _pack_w_nsa_out(w):
    wh = w.reshape(N_HEADS, HEAD_DIM, D_MODEL)
    zero = jnp.zeros_like(wh)
    in_g0 = (jnp.arange(N_HEADS) < HPG)[:, None, None]
    return jnp.concatenate([jnp.where(in_g0, wh, zero), jnp.where(in_g0, zero, wh)], axis=1).reshape(Q_EXP, D_MODEL).astype(BF16)


def _pack_w_cmp(w_kv, pos_kv):
    lo, hi = w_kv[:CMP_STRIDE], w_kv[CMP_STRIDE:]
    eye = jnp.eye(N_KV, dtype=w_kv.dtype)
    blk = lambda part: jnp.einsum('lde,gh->lgdhe', part, eye).reshape(CMP_STRIDE * N_KV * HEAD_DIM, N_KV * HEAD_DIM)
    w2 = jnp.concatenate([blk(lo), blk(hi)], axis=1).astype(BF16)
    tile = lambda p: jnp.broadcast_to(p[:, None, :], (CMP_STRIDE, N_KV, HEAD_DIM)).reshape(1, -1)
    p2 = jnp.concatenate([tile(pos_kv[:CMP_STRIDE]), tile(pos_kv[CMP_STRIDE:]),
                          jnp.zeros((SUBLANES - 2, CMP_STRIDE * N_KV * HEAD_DIM), pos_kv.dtype)], axis=0).astype(BF16)
    return w2, p2


def _feature_major(a):
    lead = a.shape[:-4]
    n = len(lead)
    a = jnp.transpose(a, tuple(range(n)) + (n + 1, n + 2, n + 3, n))
    return a.reshape(lead + (2, N_KV * HEAD_DIM, a.shape[-1]))


def _position_major(a_t, rows):
    b = a_t.shape[0]
    return jnp.transpose(a_t.reshape(b, 2, N_KV, HEAD_DIM, rows), (0, 4, 1, 2, 3))


def kernel(x_prompt, x_sample, cache_cmp_kv, cache_sel_kv, state_win_kv, state_conv, page_table, norm_mix_g, w_in, conv_dw_w, conv_dw_b, conv_ln_g, conv_ln_b, w_conv_out, cmp_pos, w_cmp, w_nsa_out, w_out, norm_mlp_g, w_up, w_down, norm_final_g):
    depth = w_in.shape[0]
    bp, tp, _ = x_prompt.shape
    bd, tq, _ = x_sample.shape
    n_pool = cache_cmp_kv.shape[1]
    n_pages = page_table.shape[1]
    past_len = n_pages * PAGE_SIZE
    win_buf = state_win_kv.shape[2]
    assert tp % Q_TILE == 0 and tp // CMP_STRIDE == LANES and tp >= WINDOW
    assert past_len // CMP_STRIDE == LANES and win_buf == WINDOW and tq <= Q_ROWS

    mp, ms = bp * tp, bd * tq
    tm_p = _pick_tile(tp, 512)
    tm_s = _pick_tile(ms, 256)
    assert tp % tm_p == 0 and tm_s % tq == 0

    cos_p, sin_p = _rope_tables(jnp.arange(tp))
    cos_pt, sin_pt = cos_p.T, sin_p.T
    cos_s, sin_s = _rope_tables(past_len + jnp.arange(tm_s) % tq)
    n_chunk = tp // CMP_STRIDE
    cos_c, sin_c = _rope_tables(jnp.arange(n_chunk) * CMP_STRIDE + CMP_BLOCK - 1)

    pt_flat = page_table.reshape(-1).astype(jnp.int32)
    cmp_pool4 = _feature_major(cache_cmp_kv).reshape(depth * n_pool, 2, LANES, PAGE_SIZE)
    sel_pool4 = _feature_major(cache_sel_kv).reshape(depth * n_pool, 2, LANES, PAGE_SIZE)
    win4 = _feature_major(state_win_kv).reshape(depth * bd, 2, LANES, win_buf)
    conv_past = jnp.swapaxes(state_conv, 1, 2).reshape(depth * (CONV_K - 1), bd, D_CONV)

    xp = x_prompt.reshape(mp, D_MODEL)
    xs = x_sample.reshape(ms, D_MODEL)
    outs = [[] for _ in range(8)]
    for l in range(depth):
        w_packed, w_t = _pack_w_in(w_in[l])
        wn_exp = _pack_w_nsa_out(w_nsa_out[l])
        wn = w_nsa_out[l].astype(BF16)
        wc, wo = w_conv_out[l].astype(BF16), w_out[l].astype(BF16)
        wu, wd = w_up[l].astype(BF16), w_down[l].astype(BF16)
        wk2, pk2 = _pack_w_cmp(w_cmp[l, 0], cmp_pos[l, 0])
        wv2, pv2 = _pack_w_cmp(w_cmp[l, 1], cmp_pos[l, 1])
        g_mix, g_mlp = norm_mix_g[l][None], norm_mlp_g[l][None]
        dw_b, ln_g, ln_b = conv_dw_b[l][None], conv_ln_g[l][None], conv_ln_b[l][None]
        gf = norm_final_g[None]
        final = l == depth - 1

        u, q_t, ckv_t, skv_t, wkv_tm, skr, wkr, gn_t, gm, c_act = _inproj_cols(
            xp, g_mix, w_packed, w_t, cos_p, sin_p, cos_pt, sin_pt, conv_dw_w[l], dw_b, ln_g, ln_b, bp, tp, tm_p)
        ab = _chunkproj_rows(ckv_t.reshape(bp, 2, LANES, tp), wk2, wv2, pk2, pv2)
        o = _attn_prompt(q_t, gn_t, skr, skv_t, wkr, wkv_tm, ab, cos_c, sin_c, bp, tp)
        xp = _post(c_act, o, gm, xp, wc, wn, wo, g_mlp, wu, wd, gf, tm_p, final)
        keep = min(WINDOW, tp)
        outs[0].append(_position_major(ckv_t, tp))
        outs[2].append(_position_major(skv_t, tp))
        outs[4].append(_position_major(wkv_tm[:, :, tp - keep:], keep))
        outs[6].append(u.reshape(bp, tp, D_CONV)[:, tp - (CONV_K - 1):])

        u, q, ckv, skv, wkv, gn, gm = _inproj_rows(xs, g_mix, w_packed, cos_s, sin_s, tm_s)
        u3 = u.reshape(bd, tq, D_CONV)
        c_t = _conv_sample(conv_past, l, jnp.swapaxes(u3, 0, 1), conv_dw_w[l], dw_b, ln_g, ln_b)
        c_act = jnp.swapaxes(c_t, 0, 1).reshape(ms, D_CONV)
        ab_seq = _chunkproj_gather(pt_flat, cmp_pool4, wk2, wv2, pk2, pv2, l * n_pool)
        pad_q = lambda a: jnp.pad(a.reshape(bd, tq, -1), ((0, 0), (0, Q_ROWS - tq), (0, 0)))
        o = _attn_sample(pt_flat, pad_q(q), pad_q(gn), pad_q(skv), pad_q(wkv), win4, cos_c, sin_c, sel_pool4,
                         ab_seq.reshape(bd, n_pages * CHUNKS_PER_PAGE, 4 * LANES), l, n_pool, bd, past_len, tq)
        xs = _post(c_act, o[:, :tq].reshape(ms, Q_EXP), gm, xs, wc, wn_exp, wo, g_mlp, wu, wd, gf, tm_s, final)
        wkv5 = wkv.reshape(bd, tq, 2, N_KV, HEAD_DIM)
        outs[1].append(ckv.reshape(bd, tq, 2, N_KV, HEAD_DIM))
        outs[3].append(skv.reshape(bd, tq, 2, N_KV, HEAD_DIM))
        outs[5].append(wkv5)
        outs[7].append(u3)

    y_prompt = xp.reshape(bp, tp, D_MODEL)
    y_sample = xs.reshape(bd, tq, D_MODEL)
    outs = [jnp.stack(o) for o in outs]
    win_new = _shift_append(win4, _feature_major(outs[5]).reshape(depth * bd, 2, LANES, tq), 8)
    outs[5] = _position_major(win_new.reshape(depth * bd, KV_COLS, win_buf), win_buf).reshape(
        depth, bd, win_buf, 2, N_KV, HEAD_DIM)
    outs[7] = jnp.concatenate([state_conv[:, :, tq:], outs[7]], axis=2)
    return (y_prompt, y_sample) + tuple(outs)
```

```python
import functools

import jax
import jax.numpy as jnp
from jax import lax
from jax.experimental import pallas as pl
from jax.experimental.pallas import tpu as pltpu

D_MODEL = 1024
D_CONV = D_MODEL // 2
CONV_K = 31
N_HEADS = 8
HEAD_DIM = 64
N_KV = 2
HPG = N_HEADS // N_KV
KV_COLS = 2 * N_KV * HEAD_DIM
CMP_BLOCK = 32
CMP_STRIDE = 16
SEL_BLOCK = 64
N_SEL = 16
WINDOW = 512
D_FF = 4 * D_MODEL
ROPE_THETA = 10000.0
EPS = 1e-6
PAGE_SIZE = 128

LANES = 128
SUBLANES = 8
Q_EXP = N_HEADS * LANES
N_GATES = 3 * N_HEADS
CHUNKS_PER_PAGE = PAGE_SIZE // CMP_STRIDE
NEG = -1e30

O_GLU = 0
O_Q = O_GLU + 2 * D_CONV
O_CKV = O_Q + Q_EXP
O_SKV = O_CKV + KV_COLS
O_WKV = O_SKV + KV_COLS
O_GN = O_WKV + KV_COLS
O_GM = O_GN + LANES
IN_COLS_PACKED = O_GM + 2 * D_MODEL

VMEM_LIMIT = 56 * 1024 * 1024

F32 = jnp.float32
BF16 = jnp.bfloat16


def _params(n_axes, vmem=VMEM_LIMIT):
    return pltpu.CompilerParams(dimension_semantics=("arbitrary",) * n_axes, vmem_limit_bytes=vmem)


def _sigmoid(x):
    return 1.0 / (1.0 + jnp.exp(-x))


def _dot(a, b):
    return jnp.dot(a, b, preferred_element_type=F32)


def _dot_nt(a, b):
    return lax.dot_general(a, b, (((1,), (1,)), ((), ())), preferred_element_type=F32)


def _rope_slab(xs, cos, sin_signed):
    lane = lax.broadcasted_iota(jnp.int32, xs.shape, 1)
    first = (lane % HEAD_DIM) < (HEAD_DIM // 2)
    rot = jnp.where(first, pltpu.roll(xs, LANES - HEAD_DIM // 2, 1), pltpu.roll(xs, HEAD_DIM // 2, 1))
    return xs * cos + rot * sin_signed


def _rope_rows(xt, cos_t, sin_t):
    half = HEAD_DIM // 2
    rot = jnp.concatenate([xt[half:2 * half], xt[0:half], xt[3 * half:4 * half], xt[2 * half:3 * half]], axis=0)
    return xt * cos_t + rot * sin_t


def _pick_tile(n, cap, mult=8):
    t = min(n, cap)
    while n % t or t % mult:
        t -= 1
    return t


def _ln_swish(c, lg, lb):
    mu = jnp.mean(c, axis=-1, keepdims=True)
    d = c - mu
    var = jnp.mean(d * d, axis=-1, keepdims=True)
    y = d * lax.rsqrt(var + EPS) * lg + lb
    return y * _sigmoid(y)


CONV_PAD = 32
CONV_CHUNK = 64


def _rms_bf16(x, g):
    ms = jnp.mean(x * x, axis=-1, keepdims=True)
    return (x * lax.rsqrt(ms + EPS) * g).astype(BF16)


def _inproj_common(h, w_ref, cos, sin, u_ref, q_ref, gn_ref, gm_ref):
    def proj(a, n):
        return _dot(h, w_ref[:, a:a + n])

    u_ref[...] = proj(O_GLU, D_CONV) * _sigmoid(proj(O_GLU + D_CONV, D_CONV))
    for s in range(N_HEADS):
        q_ref[:, s * LANES:(s + 1) * LANES] = _rope_slab(proj(O_Q + s * LANES, LANES), cos, sin).astype(BF16)
    gn_ref[...] = _sigmoid(proj(O_GN, LANES))
    gm_ref[...] = _sigmoid(proj(O_GM, 2 * D_MODEL)).astype(BF16)
    return proj


def _inproj_rows_kernel(x_ref, g_ref, w_ref, cos_ref, sin_ref,
                        u_ref, q_ref, ckv_ref, skv_ref, wkv_ref, gn_ref, gm_ref):
    h = _rms_bf16(x_ref[...], g_ref[...])
    cos = cos_ref[...]
    sin = sin_ref[...]
    proj = _inproj_common(h, w_ref, cos, sin, u_ref, q_ref, gn_ref, gm_ref)
    ckv_ref[...] = proj(O_CKV, KV_COLS)
    skv_ref[:, 0:LANES] = _rope_slab(proj(O_SKV, LANES), cos, sin)
    skv_ref[:, LANES:KV_COLS] = proj(O_SKV + LANES, LANES)
    wkv_ref[:, 0:LANES] = _rope_slab(proj(O_WKV, LANES), cos, sin)
    wkv_ref[:, LANES:KV_COLS] = proj(O_WKV + LANES, LANES)


GATE_ROWS = 32
R_Q = 0
R_CKV = R_Q + N_HEADS * HEAD_DIM
R_SKV = R_CKV + KV_COLS
R_WKV = R_SKV + KV_COLS
R_GN = R_WKV + KV_COLS
ROWS_T = R_GN + GATE_ROWS


def _conv_tile(uf_ref, w_ref, bias, lg, lb, c_ref, n_rows):
    off = CONV_PAD - (CONV_K - 1)
    ct = CONV_CHUNK
    n_win = ct + CONV_PAD
    for c in range(n_rows // ct):
        xw = uf_ref[c * ct:c * ct + n_win, :]
        acc = jnp.zeros((ct, D_CONV), F32)
        for r in range(SUBLANES):
            yr = xw if r == 0 else pltpu.roll(xw, n_win - r, 0)
            for a in range((CONV_PAD // SUBLANES) + 1):
                k = SUBLANES * a + r - off
                if 0 <= k < CONV_K:
                    acc = acc + w_ref[k:k + 1, :] * yr[SUBLANES * a:SUBLANES * a + ct, :]
        c_ref[c * ct:(c + 1) * ct, :] = _ln_swish(acc + bias, lg, lb).astype(BF16)


def _inproj_cols_kernel(x_ref, g_ref, w_ref, wt_ref, cos_ref, sin_ref, cost_ref, sint_ref, dw_ref, db_ref, lg_ref, lb_ref,
                        u_ref, q_ref, ckv_ref, skv_ref, wkv_ref, skr_ref, wkr_ref, gn_ref, gm_ref, c_ref,
                        uf_ref, carry_ref, *, tiles_per_seq):
    h = _rms_bf16(x_ref[...], g_ref[...])
    cos = cos_ref[...]
    sin = sin_ref[...]
    cos_t = cost_ref[...]
    sin_t = sint_ref[...]

    def proj(a, n):
        return _dot(h, w_ref[:, a:a + n])

    t_all = _dot_nt(wt_ref[...], h)

    def proj_t(a, n):
        return t_all[a:a + n]

    tm = x_ref.shape[0]
    u = proj(O_GLU, D_CONV) * _sigmoid(proj(O_GLU + D_CONV, D_CONV))
    u_ref[...] = u
    first = pl.program_id(0) % tiles_per_seq == 0

    @pl.when(first)
    def _():
        uf_ref[0:CONV_PAD, :] = jnp.zeros((CONV_PAD, D_CONV), F32)

    @pl.when(jnp.logical_not(first))
    def _():
        uf_ref[0:CONV_PAD, :] = carry_ref[...]

    uf_ref[CONV_PAD:CONV_PAD + tm, :] = u
    carry_ref[...] = u[tm - CONV_PAD:tm, :]
    _conv_tile(uf_ref, dw_ref, db_ref[...], lg_ref[...], lb_ref[...], c_ref, tm)
    gm_ref[...] = _sigmoid(proj(O_GM, 2 * D_MODEL)).astype(BF16)
    skr_ref[...] = _rope_slab(proj(O_SKV, LANES), cos, sin).astype(BF16)
    wkr_ref[...] = _rope_slab(proj(O_WKV, LANES), cos, sin).astype(BF16)
    for s in range(N_HEADS * HEAD_DIM // LANES):
        q_ref[0, s * LANES:(s + 1) * LANES, :] = _rope_rows(proj_t(R_Q + s * LANES, LANES), cos_t, sin_t).astype(BF16)
    ckv_ref[0] = proj_t(R_CKV, KV_COLS)
    skv_ref[0, 0:LANES] = _rope_rows(proj_t(R_SKV, LANES), cos_t, sin_t)
    skv_ref[0, LANES:KV_COLS] = proj_t(R_SKV + LANES, LANES)
    wkv_ref[0, 0:LANES] = _rope_rows(proj_t(R_WKV, LANES), cos_t, sin_t)
    wkv_ref[0, LANES:KV_COLS] = proj_t(R_WKV + LANES, LANES)
    gn_ref[0] = _sigmoid(proj_t(R_GN, GATE_ROWS))


def _inproj_rows(x2, g, w_packed, cos_t, sin_t, tm):
    m = x2.shape[0]
    row = lambda i: (i, 0)
    const = lambda i: (0, 0)
    out_shapes = (
        jax.ShapeDtypeStruct((m, D_CONV), F32),
        jax.ShapeDtypeStruct((m, Q_EXP), BF16),
        jax.ShapeDtypeStruct((m, KV_COLS), F32),
        jax.ShapeDtypeStruct((m, KV_COLS), F32),
        jax.ShapeDtypeStruct((m, KV_COLS), F32),
        jax.ShapeDtypeStruct((m, LANES), F32),
        jax.ShapeDtypeStruct((m, 2 * D_MODEL), BF16),
    )
    return pl.pallas_call(
        _inproj_rows_kernel,
        grid=(m // tm,),
        in_specs=[
            pl.BlockSpec((tm, D_MODEL), row),
            pl.BlockSpec((1, D_MODEL), const),
            pl.BlockSpec(w_packed.shape, const),
            pl.BlockSpec((tm, LANES), const),
            pl.BlockSpec((tm, LANES), const),
        ],
        out_specs=tuple(pl.BlockSpec((tm, s.shape[1]), row) for s in out_shapes),
        out_shape=out_shapes,
        compiler_params=_params(1),
    )(x2, g, w_packed, cos_t, sin_t)


def _inproj_cols(x2, g, w_packed, wkv_t, cos_t, sin_t, cos_tt, sin_tt, dw_w, dw_b, ln_g, ln_b, b, t_len, tm):
    m = x2.shape[0]
    n_tab = t_len // tm
    row = lambda i: (i, 0)
    const = lambda i: (0, 0)
    kv_map = lambda i: (i // n_tab, 0, i % n_tab)
    kv_shape = jax.ShapeDtypeStruct((b, KV_COLS, t_len), F32)
    out_shapes = (
        jax.ShapeDtypeStruct((m, D_CONV), F32),
        jax.ShapeDtypeStruct((b, N_HEADS * HEAD_DIM, t_len), BF16),
        kv_shape, kv_shape, kv_shape,
        jax.ShapeDtypeStruct((m, LANES), BF16),
        jax.ShapeDtypeStruct((m, LANES), BF16),
        jax.ShapeDtypeStruct((b, GATE_ROWS, t_len), F32),
        jax.ShapeDtypeStruct((m, 2 * D_MODEL), BF16),
        jax.ShapeDtypeStruct((m, D_CONV), BF16),
    )
    out_specs = (
        pl.BlockSpec((tm, D_CONV), row),
        pl.BlockSpec((1, N_HEADS * HEAD_DIM, tm), kv_map),
        pl.BlockSpec((1, KV_COLS, tm), kv_map),
        pl.BlockSpec((1, KV_COLS, tm), kv_map),
        pl.BlockSpec((1, KV_COLS, tm), kv_map),
        pl.BlockSpec((tm, LANES), row),
        pl.BlockSpec((tm, LANES), row),
        pl.BlockSpec((1, GATE_ROWS, tm), kv_map),
        pl.BlockSpec((tm, 2 * D_MODEL), row),
        pl.BlockSpec((tm, D_CONV), row),
    )
    return pl.pallas_call(
        functools.partial(_inproj_cols_kernel, tiles_per_seq=n_tab),
        grid=(m // tm,),
        in_specs=[
            pl.BlockSpec((tm, D_MODEL), row),
            pl.BlockSpec((1, D_MODEL), const),
            pl.BlockSpec(w_packed.shape, const),
            pl.BlockSpec(wkv_t.shape, const),
            pl.BlockSpec((tm, LANES), lambda i: (i % n_tab, 0)),
            pl.BlockSpec((tm, LANES), lambda i: (i % n_tab, 0)),
            pl.BlockSpec((LANES, tm), lambda i: (0, i % n_tab)),
            pl.BlockSpec((LANES, tm), lambda i: (0, i % n_tab)),
            pl.BlockSpec((CONV_K, D_CONV), const),
            pl.BlockSpec((1, D_CONV), const),
            pl.BlockSpec((1, D_CONV), const),
            pl.BlockSpec((1, D_CONV), const),
        ],
        out_specs=out_specs,
        out_shape=out_shapes,
        scratch_shapes=[pltpu.VMEM((tm + CONV_PAD, D_CONV), F32), pltpu.VMEM((CONV_PAD, D_CONV), F32)],
        compiler_params=_params(1),
    )(x2, g, w_packed, wkv_t, cos_t, sin_t, cos_tt, sin_tt, dw_w, dw_b, ln_g, ln_b)


def _conv_sample_kernel(past_ref, u_ref, w_ref, b_ref, lg_ref, lb_ref, o_ref, *, tq):
    bias = b_ref[...]
    lg = lg_ref[...]
    lb = lb_ref[...]
    n_past = CONV_K - 1
    for t in range(tq):
        acc = jnp.zeros(o_ref.shape[1:], F32)
        for j in range(t, n_past):
            acc = acc + w_ref[j - t:j - t + 1, :] * past_ref[j]
        for i in range(t + 1):
            k = n_past - t + i
            acc = acc + w_ref[k:k + 1, :] * u_ref[i]
        o_ref[t] = _ln_swish(acc + bias, lg, lb).astype(BF16)


def _conv_sample(past_t, layer, u_t, dw_w, dw_b, ln_g, ln_b):
    n_past, bd = CONV_K - 1, past_t.shape[1]
    tq = u_t.shape[0]
    bt = _pick_tile(bd, 32)
    const2 = lambda i: (0, 0)
    return pl.pallas_call(
        functools.partial(_conv_sample_kernel, tq=tq),
        grid=(bd // bt,),
        in_specs=[
            pl.BlockSpec((n_past, bt, D_CONV), lambda i: (layer, i, 0)),
            pl.BlockSpec((tq, bt, D_CONV), lambda i: (0, i, 0)),
            pl.BlockSpec((CONV_K, D_CONV), const2),
            pl.BlockSpec((1, D_CONV), const2),
            pl.BlockSpec((1, D_CONV), const2),
            pl.BlockSpec((1, D_CONV), const2),
        ],
        out_specs=pl.BlockSpec((tq, bt, D_CONV), lambda i: (0, i, 0)),
        out_shape=jax.ShapeDtypeStruct((tq, bd, D_CONV), BF16),
        compiler_params=_params(1),
    )(past_t, u_t, dw_w, dw_b, ln_g, ln_b)


def _chunkproj_pages(get_page, n_pages, wk_ref, wv_ref, pk_ref, pv_ref, ab_ref, t_ref):
    n_rows = n_pages * CHUNKS_PER_PAGE
    for kv, (w_ref, p_ref) in enumerate(((wk_ref, pk_ref), (wv_ref, pv_ref))):
        for p in range(n_pages):
            t_ref[p * PAGE_SIZE:(p + 1) * PAGE_SIZE, :] = get_page(p, kv).T
        xs = jnp.concatenate([t_ref[pl.ds(l, n_rows, stride=CMP_STRIDE), :] for l in range(CMP_STRIDE)],
                             axis=1).astype(BF16)
        w = w_ref[...]
        part = _dot(xs, w)
        posb = _dot(p_ref[...], w)
        c0 = kv * 2 * LANES
        ab_ref[:, c0:c0 + LANES] = part[:, 0:LANES] + posb[0:1, 0:LANES]
        ab_ref[:, c0 + LANES:c0 + 2 * LANES] = part[:, LANES:2 * LANES] + posb[1:2, LANES:2 * LANES]


def _chunkproj_rows_kernel(x_ref, wk_ref, wv_ref, pk_ref, pv_ref, ab_ref, t_ref, *, n_pages):
    get_page = lambda p, kv: x_ref[0, kv, :, p * PAGE_SIZE:(p + 1) * PAGE_SIZE]
    _chunkproj_pages(get_page, n_pages, wk_ref, wv_ref, pk_ref, pv_ref, ab_ref, t_ref)


def _chunkproj_gather_kernel(pt_ref, pool_hbm, wk_ref, wv_ref, pk_ref, pv_ref, ab_ref, x_buf, t_ref, sem,
                             *, n_pages, page_base):
    step = pl.program_id(0)
    n_steps = pl.num_programs(0)
    slot = step % 2

    def page_copies(for_step, to_slot):
        return [pltpu.make_async_copy(pool_hbm.at[page_base + pt_ref[for_step * n_pages + j]], x_buf.at[to_slot, j],
                                      sem.at[to_slot, j]) for j in range(n_pages)]

    @pl.when(step == 0)
    def _():
        for j, c in enumerate(page_copies(0, 0)):
            c.start(priority=j % 2)

    @pl.when(step + 1 < n_steps)
    def _():
        for j, c in enumerate(page_copies(step + 1, 1 - slot)):
            c.start(priority=j % 2)

    for c in page_copies(step, slot):
        c.wait()
    _chunkproj_pages(lambda p, kv: x_buf[slot, p, kv], n_pages, wk_ref, wv_ref, pk_ref, pv_ref, ab_ref, t_ref)


def _chunkproj_rows(x4, wk2, wv2, pk2, pv2):
    n_pages = x4.shape[3] // PAGE_SIZE
    const2 = lambda i: (0, 0)
    rows = n_pages * CHUNKS_PER_PAGE
    return pl.pallas_call(
        functools.partial(_chunkproj_rows_kernel, n_pages=n_pages),
        grid=(x4.shape[0],),
        in_specs=[
            pl.BlockSpec((1, 2, LANES, x4.shape[3]), lambda i: (i, 0, 0, 0)),
            pl.BlockSpec(wk2.shape, const2),
            pl.BlockSpec(wv2.shape, const2),
            pl.BlockSpec(pk2.shape, const2),
            pl.BlockSpec(pv2.shape, const2),
        ],
        out_specs=pl.BlockSpec((rows, 4 * LANES), lambda i: (i, 0)),
        out_shape=jax.ShapeDtypeStruct((x4.shape[0] * rows, 4 * LANES), F32),
        scratch_shapes=[pltpu.VMEM((n_pages * PAGE_SIZE, LANES), F32)],
        compiler_params=_params(1),
    )(x4, wk2, wv2, pk2, pv2)


GATHER_PAGES = 64


def _chunkproj_gather(pt_flat, pool4, wk2, wv2, pk2, pv2, page_base):
    n_total = pt_flat.shape[0]
    n_pages = _pick_tile(n_total, GATHER_PAGES, 1)
    const2 = lambda i, pt: (0, 0)
    rows = n_pages * CHUNKS_PER_PAGE
    grid_spec = pltpu.PrefetchScalarGridSpec(
        num_scalar_prefetch=1,
        grid=(n_total // n_pages,),
        in_specs=[
            pl.BlockSpec(memory_space=pl.ANY),
            pl.BlockSpec(wk2.shape, const2),
            pl.BlockSpec(wv2.shape, const2),
            pl.BlockSpec(pk2.shape, const2),
            pl.BlockSpec(pv2.shape, const2),
        ],
        out_specs=pl.BlockSpec((rows, 4 * LANES), lambda i, pt: (i, 0)),
        scratch_shapes=[
            pltpu.VMEM((2, n_pages, 2, LANES, PAGE_SIZE), F32),
            pltpu.VMEM((n_pages * PAGE_SIZE, LANES), F32),
            pltpu.SemaphoreType.DMA((2, n_pages)),
        ],
    )
    return pl.pallas_call(
        functools.partial(_chunkproj_gather_kernel, n_pages=n_pages, page_base=page_base),
        grid_spec=grid_spec,
        out_shape=jax.ShapeDtypeStruct((n_total * CHUNKS_PER_PAGE, 4 * LANES), F32),
        compiler_params=_params(1),
    )(pt_flat, pool4, wk2, wv2, pk2, pv2)


def _compressed_kv_f32(ab, cosc, sinc):
    n = ab.shape[0]
    kc = ab[:, 0:LANES] + pltpu.roll(ab[:, LANES:2 * LANES], n - 1, 0)
    vc = ab[:, 2 * LANES:3 * LANES] + pltpu.roll(ab[:, 3 * LANES:4 * LANES], n - 1, 0)
    return _rope_slab(kc, cosc, sinc), vc


def _compressed_kv(ab, cosc, sinc):
    kc, vc = _compressed_kv_f32(ab, cosc, sinc)
    return kc.astype(BF16), vc.astype(BF16)


def _softmax_parts(s, mask):
    sm = jnp.where(mask, s, NEG)
    m = jnp.max(sm, axis=-1, keepdims=True)
    p = jnp.where(mask, jnp.exp(sm - m), 0.0)
    l = jnp.maximum(jnp.sum(p, axis=-1, keepdims=True), 1e-30)
    return p, l


def _select_blocks(imp, qpos, n_sel, axis):
    jidx = lax.broadcasted_iota(jnp.int32, imp.shape, axis)
    cur = lax.shift_right_logical(qpos, 6)
    forced = (jidx == 0) | (jidx == cur) | (jidx == cur - 1)
    valid = (jidx * SEL_BLOCK <= qpos) & (jidx < n_sel)
    v = jnp.where(valid, jnp.where(forced, jnp.inf, imp), -jnp.inf)
    rank = jnp.zeros(imp.shape, jnp.int32)
    for k in range(n_sel):
        vk = v[k:k + 1, :] if axis == 0 else v[:, k:k + 1]
        ahead = (vk > v) | ((vk == v) & (jidx > k))
        rank = rank + ahead.astype(jnp.int32)
    return (rank < min(N_SEL, n_sel)) & valid


def _split_bf16(x):
    hi = x.astype(BF16)
    r1 = x - hi.astype(F32)
    mid = r1.astype(BF16)
    lo = (r1 - mid.astype(F32)).astype(BF16)
    return hi, mid, lo


def _overlap(n_cmp, cmp_axis):
    i = lax.broadcasted_iota(jnp.int32, (LANES, LANES), cmp_axis)
    j = lax.broadcasted_iota(jnp.int32, (LANES, LANES), 1 - cmp_axis)
    hit = (i * CMP_STRIDE < (j + 1) * SEL_BLOCK) & (i * CMP_STRIDE + CMP_BLOCK > j * SEL_BLOCK) & (i < n_cmp)
    return hit.astype(F32)


Q_TILE = 256
K_TILE = 256


def _attn_prompt_kernel(q_ref, gn_ref, skr_ref, svt_ref, wkr_ref, wvt_ref, ab_ref, cosc_ref, sinc_ref, o_ref,
                        kc_ref, vct_ref, bias_ref, m_ref, acc_ref, accb_ref, out_ref, *, t_len):
    qb = pl.program_id(1)
    n_chunk = t_len // CMP_STRIDE
    n_cmp = (t_len - CMP_BLOCK) // CMP_STRIDE + 1
    n_sel = -(-t_len // SEL_BLOCK)
    sel_rows = -(-n_sel // SUBLANES) * SUBLANES
    qt, kt_sz = Q_TILE, K_TILE
    q_tiles = qt // LANES
    g_lanes = HPG * qt
    n_lane_tiles = N_HEADS * q_tiles

    @pl.when(qb == 0)
    def _():
        kc, vc = _compressed_kv_f32(ab_ref[...], cosc_ref[...], sinc_ref[...])
        kc_ref[...] = kc.astype(BF16)
        vct_ref[...] = vc.T.astype(BF16)

    q0 = qb * qt
    qpos = q0 + lax.broadcasted_iota(jnp.int32, (1, qt), 1)

    def lane_tile(c):
        return slice(c * LANES, (c + 1) * LANES)

    def q_part(c):
        return slice((c % q_tiles) * LANES, (c % q_tiles + 1) * LANES)

    def group_of(c):
        return c // (HPG * q_tiles)

    def group_lanes(g):
        return slice(g * g_lanes, (g + 1) * g_lanes)

    q_zero = jnp.zeros((HEAD_DIM, qt), BF16)

    def q_slab(hh):
        q_h = q_ref[0, hh * HEAD_DIM:(hh + 1) * HEAD_DIM, :]
        return jnp.concatenate([q_h, q_zero] if hh < HPG else [q_zero, q_h], axis=0)

    q_all = jnp.concatenate([q_slab(hh) for hh in range(N_HEADS)], axis=1)

    def gates(branch):
        r0 = branch * N_HEADS
        return jnp.concatenate([gn_ref[0, r0 + hh:r0 + hh + 1, :] for hh in range(N_HEADS)], axis=1)

    nrow = lax.broadcasted_iota(jnp.int32, (n_chunk, qt), 0)
    mask_c = (nrow * CMP_STRIDE + CMP_BLOCK - 1 <= qpos) & (nrow < n_cmp)
    acc_ref[...] = _dot(kc_ref[...], q_all)
    hs = [[jnp.zeros((n_chunk, LANES), F32) for _ in range(q_tiles)] for _ in range(N_KV)]
    p_parts = []
    for c in range(n_lane_tiles):
        mask = mask_c[:, q_part(c)]
        sm = jnp.where(mask, acc_ref[:, lane_tile(c)], NEG)
        p_c = jnp.where(mask, jnp.exp(sm - jnp.max(sm, axis=0, keepdims=True)), 0.0)
        p_c = p_c / jnp.maximum(jnp.sum(p_c, axis=0, keepdims=True), 1e-30)
        hs[group_of(c)][c % q_tiles] = hs[group_of(c)][c % q_tiles] + p_c
        p_parts.append(p_c.astype(BF16))
    out_ref[...] = gates(0) * _dot(vct_ref[...], jnp.concatenate(p_parts, axis=1))

    sel_t = []
    for g in range(N_KV):
        ov_t = _overlap(n_cmp, 1).astype(BF16)
        imp_t = sum(_dot(ov_t, part) for part in _split_bf16(jnp.concatenate(hs[g], axis=1)))
        s_g = _select_blocks(imp_t[0:sel_rows], qpos, n_sel, 0).astype(F32)
        sel_t.append(jnp.concatenate([s_g, jnp.zeros((LANES - sel_rows, qt), F32)], axis=0).astype(BF16))

    m_ref[...] = jnp.full(m_ref.shape, NEG, F32)
    accb_ref[...] = jnp.zeros(accb_ref.shape, F32)

    def tile_step(br, kr_ref, vt_ref, k_idx, bias_fn):
        k0 = pl.multiple_of(k_idx * kt_sz, kt_sz)
        kpos = k0 + lax.broadcasted_iota(jnp.int32, (kt_sz, qt), 0)
        bias_fn(k_idx, kpos)
        s = _dot(kr_ref[pl.ds(k0, kt_sz), :], q_all)
        vt = vt_ref[0, :, pl.ds(k0, kt_sz)]
        vrow = lax.broadcasted_iota(jnp.int32, (LANES, kt_sz), 0)
        p_parts, a_parts = [], []
        for c in range(n_lane_tiles):
            sb = s[:, lane_tile(c)] + bias_ref[br, group_of(c), :, q_part(c)]
            m_old = m_ref[br, 0:1, lane_tile(c)]
            m_new = jnp.maximum(m_old, jnp.max(sb, axis=0, keepdims=True))
            p_parts.append(jnp.exp(sb - m_new).astype(BF16))
            a_parts.append(jnp.exp(m_old - m_new))
            m_ref[br, :, lane_tile(c)] = jnp.broadcast_to(m_new, (SUBLANES, LANES))
        for g in range(N_KV):
            v_own = (vrow >= g * HEAD_DIM) & (vrow < (g + 1) * HEAD_DIM)
            vaug = jnp.where(v_own, vt, 1.0).astype(BF16)
            tiles = range(g * HPG * q_tiles, (g + 1) * HPG * q_tiles)
            alpha = jnp.concatenate([a_parts[c] for c in tiles], axis=1)
            p = jnp.concatenate([p_parts[c] for c in tiles], axis=1)
            accb_ref[br, :, group_lanes(g)] = alpha * accb_ref[br, :, group_lanes(g)] + _dot(vaug, p)

    def sel_bias(k_idx, kpos):
        krow = lax.broadcasted_iota(jnp.int32, (kt_sz, LANES), 0)
        jcol = lax.broadcasted_iota(jnp.int32, (kt_sz, LANES), 1)
        expand_t = (jcol == k_idx * (kt_sz // SEL_BLOCK) + lax.shift_right_logical(krow, 6)).astype(BF16)
        for g in range(N_KV):
            bias_ref[0, g] = jnp.where((_dot(expand_t, sel_t[g]) > 0.5) & (kpos <= qpos), 0.0, NEG)

    def win_bias(k_idx, kpos):
        b = jnp.where((kpos <= qpos) & (kpos > qpos - WINDOW), 0.0, NEG)
        for g in range(N_KV):
            bias_ref[1, g] = b

    n_sel_tiles = qb + 1
    n_win_tiles = jnp.minimum(qb, WINDOW // kt_sz) + 1

    def both(i, carry):
        tile_step(0, skr_ref, svt_ref, i, sel_bias)
        tile_step(1, wkr_ref, wvt_ref, qb - i, win_bias)
        return carry

    def sel_only(i, carry):
        tile_step(0, skr_ref, svt_ref, i, sel_bias)
        return carry

    lax.fori_loop(0, n_win_tiles, both, 0)
    lax.fori_loop(n_win_tiles, n_sel_tiles, sel_only, 0)

    for br in range(2):
        gt = gates(br + 1)
        for g in range(N_KV):
            sum_row = (1 - g) * HEAD_DIM
            acc = accb_ref[br, :, group_lanes(g)]
            out_ref[:, group_lanes(g)] = (out_ref[:, group_lanes(g)]
                                          + gt[:, group_lanes(g)] * (acc / acc[sum_row:sum_row + 1, :]))

    lane = lax.broadcasted_iota(jnp.int32, (qt, LANES), 1)
    for pair in range(N_HEADS // 2):
        a = out_ref[:, (2 * pair) * qt:(2 * pair + 1) * qt].T
        b = out_ref[:, (2 * pair + 1) * qt:(2 * pair + 2) * qt].T
        if 2 * pair < HPG:
            slab = jnp.where(lane < HEAD_DIM, a, pltpu.roll(b, HEAD_DIM, 1))
        else:
            slab = jnp.where(lane < HEAD_DIM, pltpu.roll(a, HEAD_DIM, 1), b)
        o_ref[:, pair * LANES:(pair + 1) * LANES] = slab.astype(BF16)


def _attn_prompt(q_t, gn_t, skr, skv_t, wkr, wkv_t, ab2, cosc, sinc, b, t_len):
    n_chunk = t_len // CMP_STRIDE
    nqb = t_len // Q_TILE
    const2 = lambda i, j: (0, 0)
    per_b2 = lambda i, j: (i, 0)
    v_rows = lambda i, j: (i, 1, 0)
    return pl.pallas_call(
        functools.partial(_attn_prompt_kernel, t_len=t_len),
        grid=(b, nqb),
        in_specs=[
            pl.BlockSpec((1, N_HEADS * HEAD_DIM, Q_TILE), lambda i, j: (i, 0, j)),
            pl.BlockSpec((1, GATE_ROWS, Q_TILE), lambda i, j: (i, 0, j)),
            pl.BlockSpec((t_len, LANES), per_b2),
            pl.BlockSpec((1, LANES, t_len), v_rows),
            pl.BlockSpec((t_len, LANES), per_b2),
            pl.BlockSpec((1, LANES, t_len), v_rows),
            pl.BlockSpec((n_chunk, 4 * LANES), per_b2),
            pl.BlockSpec((n_chunk, LANES), const2),
            pl.BlockSpec((n_chunk, LANES), const2),
        ],
        out_specs=pl.BlockSpec((Q_TILE, N_HEADS * HEAD_DIM), lambda i, j: (i * nqb + j, 0)),
        out_shape=jax.ShapeDtypeStruct((b * t_len, N_HEADS * HEAD_DIM), BF16),
        scratch_shapes=[
            pltpu.VMEM((n_chunk, LANES), BF16),
            pltpu.VMEM((LANES, n_chunk), BF16),
            pltpu.VMEM((2, N_KV, K_TILE, Q_TILE), F32),
            pltpu.VMEM((2, SUBLANES, N_HEADS * Q_TILE), F32),
            pltpu.VMEM((LANES, N_HEADS * Q_TILE), F32),
            pltpu.VMEM((2, LANES, N_HEADS * Q_TILE), F32),
            pltpu.VMEM((LANES, N_HEADS * Q_TILE), F32),
        ],
        compiler_params=_params(2),
    )(q_t, gn_t, skr, skv_t, wkr, wkv_t, ab2, cosc, sinc)


Q_ROWS = 8
SEQ_PER_STEP = 4


def _attn_sample_kernel(pt_ref, q_ref, gn_ref, sknew_ref, wknew_ref, win_ref, cosc_ref, sinc_ref, ab_ref, sel_hbm,
                        o_ref, sel_buf, sem, *, past_len, tq, n_pages, n_seq, page_base):
    step = pl.program_id(0)
    n_steps = pl.num_programs(0)
    slot = step % 2
    n_slot_pages = n_seq * n_pages

    def page_copies(for_step, to_slot):
        return [pltpu.make_async_copy(sel_hbm.at[page_base + pt_ref[for_step * n_slot_pages + j]],
                                      sel_buf.at[to_slot, j], sem.at[to_slot, j]) for j in range(n_slot_pages)]

    @pl.when(step == 0)
    def _():
        for j, c in enumerate(page_copies(0, 0)):
            c.start(priority=j % 2)

    @pl.when(step + 1 < n_steps)
    def _():
        for j, c in enumerate(page_copies(step + 1, 1 - slot)):
            c.start(priority=j % 2)

    for c in page_copies(step, slot):
        c.wait()

    def sel_page(e, p, kv):
        return sel_buf[slot, e * n_pages + p, kv]

    seqs = range(n_seq)
    t_all = past_len + tq
    n_chunk = past_len // CMP_STRIDE
    n_cmp = (t_all - CMP_BLOCK) // CMP_STRIDE + 1
    n_sel = -(-t_all // SEL_BLOCK)
    rows = N_HEADS * Q_ROWS
    all_rows = n_seq * rows
    win_buf = win_ref.shape[3]

    def stack(parts):
        return jnp.concatenate(parts, axis=0)

    def per_seq(a, e):
        return a[e * rows:(e + 1) * rows]

    qbd = [stack([q_ref[e, :, hh * LANES:(hh + 1) * LANES] for hh in range(N_HEADS)]) for e in seqs]
    qpos = past_len + lax.broadcasted_iota(jnp.int32, (all_rows, 1), 0) % Q_ROWS

    cosc = cosc_ref[...]
    sinc = sinc_ref[...]
    kvc = [_compressed_kv(ab_ref[e], cosc, sinc) for e in seqs]
    ncol = lax.broadcasted_iota(jnp.int32, (all_rows, n_chunk), 1)
    mask_c = (ncol * CMP_STRIDE + CMP_BLOCK - 1 <= qpos) & (ncol < n_cmp)
    p_c, l_c = _softmax_parts(stack([_dot_nt(qbd[e], kvc[e][0]) for e in seqs]), mask_c)
    p_c = p_c / l_c
    p_c16 = p_c.astype(BF16)
    o_c = stack([_dot(per_seq(p_c16, e), kvc[e][1]) for e in seqs])

    hs = []
    for e in seqs:
        for g in range(N_KV):
            r0 = e * rows + g * HPG * Q_ROWS
            acc = p_c[r0:r0 + Q_ROWS]
            for h in range(1, HPG):
                acc = acc + p_c[r0 + h * Q_ROWS:r0 + (h + 1) * Q_ROWS]
            hs.append(acc)
    ov = _overlap(n_cmp, 0).astype(BF16)
    imp = sum(_dot(part, ov) for part in _split_bf16(stack(hs)))
    sel_g = _select_blocks(imp, qpos[0:n_seq * N_KV * Q_ROWS], n_sel, 1).astype(BF16)
    sel_rows = stack([sel_g[(e * N_KV + g) * Q_ROWS:(e * N_KV + g + 1) * Q_ROWS]
                      for e in seqs for g in range(N_KV) for _ in range(HPG)])

    def new_rows_tile(ref, e, c0):
        return stack([ref[e, :, c0:c0 + LANES], jnp.zeros((LANES - Q_ROWS, LANES), F32)]).astype(BF16)

    n_keys = (n_pages + 1) * PAGE_SIZE
    s_s = stack([jnp.concatenate([_dot(qbd[e], sel_page(e, p, 0).astype(BF16)) for p in range(n_pages)]
                                 + [_dot_nt(qbd[e], new_rows_tile(sknew_ref, e, 0))], axis=1) for e in seqs])
    jrow = lax.broadcasted_iota(jnp.int32, (LANES, n_keys), 0)
    kcol = lax.broadcasted_iota(jnp.int32, (LANES, n_keys), 1)
    expand = (jrow == lax.shift_right_logical(kcol, 6)).astype(BF16)
    kpos = lax.broadcasted_iota(jnp.int32, (all_rows, n_keys), 1)
    mask_s = (_dot(sel_rows, expand) > 0.5) & (kpos <= qpos)
    p_s, l_s = _softmax_parts(s_s, mask_s)
    p_s = p_s.astype(BF16)
    o_s = []
    for e in seqs:
        p_e = per_seq(p_s, e)
        acc = _dot(p_e[:, n_pages * PAGE_SIZE:], new_rows_tile(sknew_ref, e, LANES))
        for i in range(n_pages):
            acc = acc + _dot_nt(p_e[:, i * PAGE_SIZE:(i + 1) * PAGE_SIZE], sel_page(e, i, 1).astype(BF16))
        o_s.append(acc)
    o_s = stack(o_s) / l_s

    s_w = stack([jnp.concatenate([_dot(qbd[e], win_ref[e, 0].astype(BF16)),
                                  _dot_nt(qbd[e], new_rows_tile(wknew_ref, e, 0))], axis=1) for e in seqs])
    wcol = lax.broadcasted_iota(jnp.int32, (all_rows, win_buf + LANES), 1)
    kpos_w = past_len - win_buf + wcol
    mask_w = (kpos_w <= qpos) & (kpos_w > qpos - WINDOW)
    p_w, l_w = _softmax_parts(s_w, mask_w)
    p_w = p_w.astype(BF16)
    o_w = stack([_dot_nt(per_seq(p_w, e)[:, 0:win_buf], win_ref[e, 1].astype(BF16))
                 + _dot(per_seq(p_w, e)[:, win_buf:], new_rows_tile(wknew_ref, e, LANES)) for e in seqs]) / l_w

    for e in seqs:
        gn = gn_ref[e]
        for hh in range(N_HEADS):
            r = slice(e * rows + hh * Q_ROWS, e * rows + (hh + 1) * Q_ROWS)
            out = (gn[:, hh:hh + 1] * o_c[r]
                   + gn[:, N_HEADS + hh:N_HEADS + hh + 1] * o_s[r]
                   + gn[:, 2 * N_HEADS + hh:2 * N_HEADS + hh + 1] * o_w[r])
            o_ref[e, :, hh * LANES:(hh + 1) * LANES] = out.astype(BF16)


def _attn_sample(pt_flat, q3, gn3, sknew3, wknew3, win4, cosc, sinc, sel_pool4, ab_seq3,
                 layer, n_pool, bd, past_len, tq):
    n_pages = past_len // PAGE_SIZE
    win_buf = win4.shape[3]
    n_chunk = past_len // CMP_STRIDE
    n_seq = _pick_tile(bd, SEQ_PER_STEP, 1)
    per_b = lambda i, pt: (i, 0, 0)
    const2 = lambda i, pt: (0, 0)
    n_slot_pages = n_seq * n_pages
    grid_spec = pltpu.PrefetchScalarGridSpec(
        num_scalar_prefetch=1,
        grid=(bd // n_seq,),
        in_specs=[
            pl.BlockSpec((n_seq, Q_ROWS, Q_EXP), per_b),
            pl.BlockSpec((n_seq, Q_ROWS, LANES), per_b),
            pl.BlockSpec((n_seq, Q_ROWS, KV_COLS), per_b),
            pl.BlockSpec((n_seq, Q_ROWS, KV_COLS), per_b),
            pl.BlockSpec((n_seq, 2, LANES, win_buf), lambda i, pt: (layer * (bd // n_seq) + i, 0, 0, 0)),
            pl.BlockSpec((n_chunk, LANES), const2),
            pl.BlockSpec((n_chunk, LANES), const2),
            pl.BlockSpec((n_seq, n_chunk, 4 * LANES), per_b),
            pl.BlockSpec(memory_space=pl.ANY),
        ],
        out_specs=pl.BlockSpec((n_seq, Q_ROWS, Q_EXP), per_b),
        scratch_shapes=[
            pltpu.VMEM((2, n_slot_pages, 2, LANES, PAGE_SIZE), F32),
            pltpu.SemaphoreType.DMA((2, n_slot_pages)),
        ],
    )
    return pl.pallas_call(
        functools.partial(_attn_sample_kernel, past_len=past_len, tq=tq, n_pages=n_pages, n_seq=n_seq,
                          page_base=layer * n_pool),
        grid_spec=grid_spec,
        out_shape=jax.ShapeDtypeStruct((bd, Q_ROWS, Q_EXP), BF16),
        compiler_params=_params(1),
    )(pt_flat, q3, gn3, sknew3, wknew3, win4, cosc, sinc, ab_seq3, sel_pool4)


def _post_kernel(c_ref, o_ref, gm_ref, x_ref, wc_ref, wn_ref, wo_ref, g_ref, wu_ref, wd_ref, gf_ref, y_ref, *, final):
    conv_out = _dot(c_ref[...], wc_ref[...])
    nsa_out = _dot(o_ref[...], wn_ref[...])
    gm = gm_ref[...].astype(F32)
    merged = gm[:, 0:D_MODEL] * conv_out + gm[:, D_MODEL:2 * D_MODEL] * nsa_out
    x = x_ref[...] + _dot(merged.astype(BF16), wo_ref[...])
    h = _rms_bf16(x, g_ref[...])
    acc = x
    for f in range(D_FF // FF_TILE):
        up = jnp.maximum(_dot(h, wu_ref[:, f * FF_TILE:(f + 1) * FF_TILE]), 0.0)
        acc = acc + _dot((up * up).astype(BF16), wd_ref[f * FF_TILE:(f + 1) * FF_TILE, :])
    if final:
        ms2 = jnp.mean(acc * acc, axis=-1, keepdims=True)
        acc = acc * lax.rsqrt(ms2 + EPS) * gf_ref[...]
    y_ref[...] = acc


FF_TILE = 1024


def _post(c2, o2, gm2, x2, wc, wn, wo, g, wu, wd, gf, tm, final):
    m = x2.shape[0]
    row = lambda i: (i, 0)
    const = lambda i: (0, 0)
    resident = lambda a: pl.BlockSpec(a.shape, const, pipeline_mode=pl.Buffered(1))
    return pl.pallas_call(
        functools.partial(_post_kernel, final=final),
        grid=(m // tm,),
        in_specs=[
            pl.BlockSpec((tm, c2.shape[1]), row),
            pl.BlockSpec((tm, o2.shape[1]), row),
            pl.BlockSpec((tm, 2 * D_MODEL), row),
            pl.BlockSpec((tm, D_MODEL), row),
            resident(wc), resident(wn), resident(wo),
            pl.BlockSpec((1, D_MODEL), const),
            resident(wu), resident(wd),
            pl.BlockSpec((1, D_MODEL), const),
        ],
        out_specs=pl.BlockSpec((tm, D_MODEL), row),
        out_shape=jax.ShapeDtypeStruct((m, D_MODEL), F32),
        compiler_params=_params(1),
    )(c2, o2, gm2, x2, wc, wn, wo, g, wu, wd, gf)


def _shift_append_kernel(old_ref, new_ref, o_ref, *, n_new):
    o_ref[...] = jnp.concatenate([old_ref[..., n_new:], new_ref[...]], axis=-1)


def _shift_append(old, new, rows_per_step):
    n, a, b, keep = old.shape
    n_new = new.shape[-1]
    r = _pick_tile(n, rows_per_step, 1)
    return pl.pallas_call(
        functools.partial(_shift_append_kernel, n_new=n_new),
        grid=(n // r,),
        in_specs=[pl.BlockSpec((r, a, b, keep), lambda i: (i, 0, 0, 0)),
                  pl.BlockSpec((r, a, b, n_new), lambda i: (i, 0, 0, 0))],
        out_specs=pl.BlockSpec((r, a, b, keep), lambda i: (i, 0, 0, 0)),
        out_shape=jax.ShapeDtypeStruct(old.shape, old.dtype),
        compiler_params=_params(1),
    )(old, new)


def _rope_tables(pos):
    half = HEAD_DIM // 2
    inv = jnp.power(ROPE_THETA, -jnp.arange(half, dtype=F32) / half)
    ang = pos.astype(F32)[:, None] * inv[None, :]
    cos = jnp.cos(ang)
    sin = jnp.sin(ang)
    return jnp.concatenate([cos, cos, cos, cos], axis=1), jnp.concatenate([-sin, sin, -sin, sin], axis=1)


def _pack_w_in(w):
    o0 = 2 * D_CONV
    o1 = o0 + N_HEADS * HEAD_DIM
    o4 = o1 + 3 * KV_COLS
    o5 = o4 + N_GATES
    wq = w[:, o0:o1].reshape(D_MODEL, N_HEADS, HEAD_DIM) * (HEAD_DIM ** -0.5)
    zero = jnp.zeros_like(wq)
    in_g0 = (jnp.arange(N_HEADS) < HPG)[None, :, None]
    wq_exp = jnp.concatenate([jnp.where(in_g0, wq, zero), jnp.where(in_g0, zero, wq)], axis=2)
    wgn = jnp.pad(w[:, o4:o5], ((0, 0), (0, LANES - N_GATES)))
    wq_exp = wq_exp.reshape(D_MODEL, Q_EXP)
    packed = jnp.concatenate([w[:, :o0], wq_exp, w[:, o1:o4], wgn, w[:, o5:]], axis=1)
    w_t = jnp.concatenate([wq.reshape(D_MODEL, N_HEADS * HEAD_DIM), w[:, o1:o4], wgn[:, :GATE_ROWS]], axis=1).T
    return packed.astype(BF16), w_t.astype(BF16)


def _pack_w_nsa_out(w):
    wh = w.reshape(N_HEADS, HEAD_DIM, D_MODEL)
    zero = jnp.zeros_like(wh)
    in_g0 = (jnp.arange(N_HEADS) < HPG)[:, None, None]
    return jnp.concatenate([jnp.where(in_g0, wh, zero), jnp.where(in_g0, zero, wh)], axis=1).reshape(Q_EXP, D_MODEL).astype(BF16)


def _pack_w_cmp(w_kv, pos_kv):
    lo, hi = w_kv[:CMP_STRIDE], w_kv[CMP_STRIDE:]
    eye = jnp.eye(N_KV, dtype=w_kv.dtype)
    blk = lambda part: jnp.einsum('lde,gh->lgdhe', part, eye).reshape(CMP_STRIDE * N_KV * HEAD_DIM, N_KV * HEAD_DIM)
    w2 = jnp.concatenate([blk(lo), blk(hi)], axis=1).astype(BF16)
    tile = lambda p: jnp.broadcast_to(p[:, None, :], (CMP_STRIDE, N_KV, HEAD_DIM)).reshape(1, -1)
    p2 = jnp.concatenate([tile(pos_kv[:CMP_STRIDE]), tile(pos_kv[CMP_STRIDE:]),
                          jnp.zeros((SUBLANES - 2, CMP_STRIDE * N_KV * HEAD_DIM), pos_kv.dtype)], axis=0).astype(BF16)
    return w2, p2


def _feature_major(a):
    lead = a.shape[:-4]
    n = len(lead)
    a = jnp.transpose(a, tuple(range(n)) + (n + 1, n + 2, n + 3, n))
    return a.reshape(lead + (2, N_KV * HEAD_DIM, a.shape[-1]))


def _position_major(a_t, rows):
    b = a_t.shape[0]
    return jnp.transpose(a_t.reshape(b, 2, N_KV, HEAD_DIM, rows), (0, 4, 1, 2, 3))


def kernel(x_prompt, x_sample, cache_cmp_kv, cache_sel_kv, state_win_kv, state_conv, page_table, norm_mix_g, w_in, conv_dw_w, conv_dw_b, conv_ln_g, conv_ln_b, w_conv_out, cmp_pos, w_cmp, w_nsa_out, w_out, norm_mlp_g, w_up, w_down, norm_final_g):
    depth = w_in.shape[0]
    bp, tp, _ = x_prompt.shape
    bd, tq, _ = x_sample.shape
    n_pool = cache_cmp_kv.shape[1]
    n_pages = page_table.shape[1]
    past_len = n_pages * PAGE_SIZE
    win_buf = state_win_kv.shape[2]
    assert tp % Q_TILE == 0 and tp // CMP_STRIDE == LANES and tp >= WINDOW
    assert past_len // CMP_STRIDE == LANES and win_buf == WINDOW and tq <= Q_ROWS

    mp, ms = bp * tp, bd * tq
    tm_p = _pick_tile(tp, 512)
    tm_s = _pick_tile(ms, 256)
    assert tp % tm_p == 0 and tm_s % tq == 0

    cos_p, sin_p = _rope_tables(jnp.arange(tp))
    cos_pt, sin_pt = cos_p.T, sin_p.T
    cos_s, sin_s = _rope_tables(past_len + jnp.arange(tm_s) % tq)
    n_chunk = tp // CMP_STRIDE
    cos_c, sin_c = _rope_tables(jnp.arange(n_chunk) * CMP_STRIDE + CMP_BLOCK - 1)

    pt_flat = page_table.reshape(-1).astype(jnp.int32)
    cmp_pool4 = _feature_major(cache_cmp_kv).reshape(depth * n_pool, 2, LANES, PAGE_SIZE)
    sel_pool4 = _feature_major(cache_sel_kv).reshape(depth * n_pool, 2, LANES, PAGE_SIZE)
    win4 = _feature_major(state_win_kv).reshape(depth * bd, 2, LANES, win_buf)
    conv_past = jnp.swapaxes(state_conv, 1, 2).reshape(depth * (CONV_K - 1), bd, D_CONV)

    xp = x_prompt.reshape(mp, D_MODEL)
    xs = x_sample.reshape(ms, D_MODEL)
    outs = [[] for _ in range(8)]
    for l in range(depth):
        w_packed, w_t = _pack_w_in(w_in[l])
        wn_exp = _pack_w_nsa_out(w_nsa_out[l])
        wn = w_nsa_out[l].astype(BF16)
        wc, wo = w_conv_out[l].astype(BF16), w_out[l].astype(BF16)
        wu, wd = w_up[l].astype(BF16), w_down[l].astype(BF16)
        wk2, pk2 = _pack_w_cmp(w_cmp[l, 0], cmp_pos[l, 0])
        wv2, pv2 = _pack_w_cmp(w_cmp[l, 1], cmp_pos[l, 1])
        g_mix, g_mlp = norm_mix_g[l][None], norm_mlp_g[l][None]
        dw_b, ln_g, ln_b = conv_dw_b[l][None], conv_ln_g[l][None], conv_ln_b[l][None]
        gf = norm_final_g[None]
        final = l == depth - 1

        u, q_t, ckv_t, skv_t, wkv_tm, skr, wkr, gn_t, gm, c_act = _inproj_cols(
            xp, g_mix, w_packed, w_t, cos_p, sin_p, cos_pt, sin_pt, conv_dw_w[l], dw_b, ln_g, ln_b, bp, tp, tm_p)
        ab = _chunkproj_rows(ckv_t.reshape(bp, 2, LANES, tp), wk2, wv2, pk2, pv2)
        o = _attn_prompt(q_t, gn_t, skr, skv_t, wkr, wkv_tm, ab, cos_c, sin_c, bp, tp)
        xp = _post(c_act, o, gm, xp, wc, wn, wo, g_mlp, wu, wd, gf, tm_p, final)
        keep = min(WINDOW, tp)
        outs[0].append(_position_major(ckv_t, tp))
        outs[2].append(_position_major(skv_t, tp))
        outs[4].append(_position_major(wkv_tm[:, :, tp - keep:], keep))
        outs[6].append(u.reshape(bp, tp, D_CONV)[:, tp - (CONV_K - 1):])

        u, q, ckv, skv, wkv, gn, gm = _inproj_rows(xs, g_mix, w_packed, cos_s, sin_s, tm_s)
        u3 = u.reshape(bd, tq, D_CONV)
        c_t = _conv_sample(conv_past, l, jnp.swapaxes(u3, 0, 1), conv_dw_w[l], dw_b, ln_g, ln_b)
        c_act = jnp.swapaxes(c_t, 0, 1).reshape(ms, D_CONV)
        ab_seq = _chunkproj_gather(pt_flat, cmp_pool4, wk2, wv2, pk2, pv2, l * n_pool)
        pad_q = lambda a: jnp.pad(a.reshape(bd, tq, -1), ((0, 0), (0, Q_ROWS - tq), (0, 0)))
        o = _attn_sample(pt_flat, pad_q(q), pad_q(gn), pad_q(skv), pad_q(wkv), win4, cos_c, sin_c, sel_pool4,
                         ab_seq.reshape(bd, n_pages * CHUNKS_PER_PAGE, 4 * LANES), l, n_pool, bd, past_len, tq)
        xs = _post(c_act, o[:, :tq].reshape(ms, Q_EXP), gm, xs, wc, wn_exp, wo, g_mlp, wu, wd, gf, tm_s, final)
        wkv5 = wkv.reshape(bd, tq, 2, N_KV, HEAD_DIM)
        outs[1].append(ckv.reshape(bd, tq, 2, N_KV, HEAD_DIM))
        outs[3].append(skv.reshape(bd, tq, 2, N_KV, HEAD_DIM))
        outs[5].append(wkv5)
        outs[7].append(u3)

    y_prompt = xp.reshape(bp, tp, D_MODEL)
    y_sample = xs.reshape(bd, tq, D_MODEL)
    outs = [jnp.stack(o) for o in outs]
    win_new = _shift_append(win4, _feature_major(outs[5]).reshape(depth * bd, 2, LANES, tq), 8)
    outs[5] = _position_major(win_new.reshape(depth * bd, KV_COLS, win_buf), win_buf).reshape(
        depth, bd, win_buf, 2, N_KV, HEAD_DIM)
    outs[7] = jnp.concatenate([state_conv[:, :, tq:], outs[7]], axis=2)
    return (y_prompt, y_sample) + tuple(outs)
```
